```python
import jax, jax.numpy as jnp
from jax import lax
import numpy as np

D_MODEL = 1024
BATCH = 32
SEQ = 256
DEPTH = 2
DEC_BATCH = 8
DEC_SEQ = 1024
PAST_LEN = 512

GRID_W = 64
HEAD_DIM = 128
A_HEADS = 4
A_DK = 128
A_DV = 128
SHORT_CONV = 5
DELTA_CHUNK = 64
B_Q_HEADS = 4
B_KV_HEADS = 2
WINDOW = 128
ATTN_BLOCK = 128
C_HEADS = 4
C_DK = 128
C_DV = 256
GATE_RANK = 16
GATE_NORM = 16.0
GLA_CHUNK = 16
D_FF = 2816
ROPE_THETA = 10000.0
EPS = 1e-6
N_MOD = 9
N_EVEN = (DEPTH + 1) // 2
N_ODD = DEPTH // 2
EVEN_SIZES = (2 * A_HEADS * A_DK + A_HEADS * A_DV, A_HEADS * A_DV, 2 * A_HEADS, 2 * A_HEADS,
              B_Q_HEADS * HEAD_DIM, B_KV_HEADS * HEAD_DIM, B_KV_HEADS * HEAD_DIM)
EVEN_IN = sum(EVEN_SIZES)
EVEN_MIX = A_HEADS * A_DV + B_Q_HEADS * HEAD_DIM
ODD_SIZES = (C_HEADS * C_DK, C_HEADS * C_DK, C_HEADS * C_DV, C_HEADS * C_DV, 2 * GATE_RANK)
ODD_IN = sum(ODD_SIZES)
ODD_MIX = C_HEADS * C_DV
F32 = jnp.float32

kernel_name = 'hybrid_diffusion_deltanet_swa_gla_step'


def _split(x, sizes):
    idx, acc = [], 0
    for s in sizes[:-1]:
        acc += s
        idx.append(acc)
    return jnp.split(x, idx, axis=-1)


def _rmsnorm(x, g):
    xf = x.astype(F32)
    y = xf * lax.rsqrt(jnp.mean(xf * xf, axis=-1, keepdims=True) + EPS)
    return (y * g.astype(F32)).astype(x.dtype)


def _l2norm(x):
    return x * lax.rsqrt(jnp.sum(x * x, axis=-1, keepdims=True) + EPS)


def _modnorm(x, g, shift, scale):
    return _rmsnorm(x, g) * (1.0 + scale) + shift


def _swiglu(h, w_gu, w_down):
    gt, up = jnp.split(h @ w_gu, 2, axis=-1)
    return (jax.nn.silu(gt) * up) @ w_down


def _ffn_half(x, mod, i, g, w_gu, w_down):
    h = _modnorm(x, g, mod[:, i], mod[:, i + 1])
    return x + 0.5 * mod[:, i + 2] * _swiglu(h, w_gu, w_down)


def _ada(cond, w, b):
    m = jax.nn.silu(cond) @ w + b
    return m.reshape(cond.shape[0], N_MOD, 1, D_MODEL)


def _depthwise_conv(x, w):
    pad = (SHORT_CONV - 1) // 2
    return lax.conv_general_dilated(x, w[:, None, :].astype(x.dtype), window_strides=(1,),
                                    padding=[(pad, pad)], dimension_numbers=('NWC', 'WIO', 'NWC'),
                                    feature_group_count=x.shape[-1])


def _to_chunks(x, c):
    b, t, h = x.shape[:3]
    x = x.reshape(b, t // c, c, h, *x.shape[3:])
    return x.transpose(0, 3, 1, 2, *range(4, x.ndim))


def _from_chunks(o):
    n, b, h, c, d = o.shape
    return o.transpose(1, 0, 3, 2, 4).reshape(b, n * c, h, d)


def _chunk_first(x):
    return jnp.moveaxis(x, 2, 0)


def _delta_rule(q, k, v, g, beta, s0):
    dt = v.dtype
    b, t, h, dk = q.shape
    dv = v.shape[-1]
    c = DELTA_CHUNK
    q = _l2norm(q.astype(F32)) * (dk ** -0.5)
    k = _l2norm(k.astype(F32))
    qc, kc, vc = _to_chunks(q, c), _to_chunks(k, c), _to_chunks(v.astype(F32), c)
    gc = jnp.cumsum(_to_chunks(g.astype(F32), c), axis=-1)
    bc = _to_chunks(beta.astype(F32), c)[..., None]
    tril = jnp.tril(jnp.ones((c, c), bool))
    strict = jnp.tril(jnp.ones((c, c), bool), -1)
    decay = jnp.exp(jnp.where(tril, gc[..., :, None] - gc[..., None, :], -jnp.inf))
    kk = jnp.einsum('bhnid,bhnjd->bhnij', kc * bc, kc) * decay
    lower = jnp.where(strict, kk, 0.0) + jnp.eye(c, dtype=F32)
    rhs = jnp.concatenate([vc * bc, kc * bc * jnp.exp(gc)[..., None]], axis=-1)
    sol = jax.lax.linalg.triangular_solve(lower, rhs, left_side=True, lower=True, unit_diagonal=True)
    u, w = sol[..., :dv], sol[..., dv:]
    qk = jnp.einsum('bhnid,bhnjd->bhnij', qc, kc) * decay
    q_dec = qc * jnp.exp(gc)[..., None]
    k_dec = kc * jnp.exp(gc[..., -1:] - gc)[..., None]
    g_last = jnp.exp(gc[..., -1])

    def step(s, xs):
        qd, kd, u_c, w_c, a_c, gl = xs
        v_new = u_c - jnp.einsum('bhck,bhkv->bhcv', w_c, s)
        o = jnp.einsum('bhck,bhkv->bhcv', qd, s) + jnp.einsum('bhij,bhjv->bhiv', a_c, v_new)
        s = s * gl[..., None, None] + jnp.einsum('bhck,bhcv->bhkv', kd, v_new)
        return s, o

    s_fin, o = lax.scan(step, s0.astype(F32), (_chunk_first(q_dec), _chunk_first(k_dec), _chunk_first(u),
                                              _chunk_first(w), _chunk_first(qk), _chunk_first(g_last)))
    return _from_chunks(o).astype(dt), s_fin


def _gla(q, k, v, gk, s0):
    dt = v.dtype
    b, t, h, dk = q.shape
    c = GLA_CHUNK
    qc = _to_chunks(q.astype(F32) * (dk ** -0.5), c)
    kc = _to_chunks(k.astype(F32), c)
    vc = _to_chunks(v.astype(F32), c)
    gcum = jnp.cumsum(_to_chunks(gk.astype(F32), c), axis=-2)
    tril = jnp.tril(jnp.ones((c, c), bool))
    rel = jnp.exp(jnp.where(tril[..., None], gcum[..., :, None, :] - gcum[..., None, :, :], -jnp.inf))
    a = jnp.einsum('bhnid,bhnjd,bhnijd->bhnij', qc, kc, rel)
    q_dec = qc * jnp.exp(gcum)
    k_dec = kc * jnp.exp(gcum[..., -1:, :] - gcum)
    g_last = jnp.exp(gcum[..., -1, :])

    def step(s, xs):
        qd, kd, v_c, a_c, gl = xs
        o = jnp.einsum('bhck,bhkv->bhcv', qd, s) + jnp.einsum('bhij,bhjv->bhiv', a_c, v_c)
        s = s * gl[..., None] + jnp.einsum('bhck,bhcv->bhkv', kd, v_c)
        return s, o

    s_fin, o = lax.scan(step, s0.astype(F32), (_chunk_first(q_dec), _chunk_first(k_dec), _chunk_first(vc),
                                              _chunk_first(a), _chunk_first(g_last)))
    return _from_chunks(o).astype(dt), s_fin


def _flip(x):
    return jnp.flip(x, axis=1)


def _delta_bidir(q, k, v, g, beta, s0):
    o_f, s_f = _delta_rule(q, k, v, g[:, :, 0], beta[:, :, 0], s0[:, 0])
    o_b, s_b = _delta_rule(_flip(q), _flip(k), _flip(v), _flip(g[:, :, 1]), _flip(beta[:, :, 1]), s0[:, 1])
    return o_f + _flip(o_b), jnp.stack([s_f, s_b], axis=1)


def _gla_bidir(q, k, v, gk, s0):
    o_f, s_f = _gla(q, k, v, gk[:, :, 0], s0[:, 0])
    o_b, s_b = _gla(_flip(q), _flip(k), _flip(v), _flip(gk[:, :, 1]), s0[:, 1])
    return o_f + _flip(o_b), jnp.stack([s_f, s_b], axis=1)


def _axial_rope(x):
    t, dh = x.shape[1], x.shape[-1]
    rows = t // GRID_W
    row = jnp.repeat(jnp.arange(rows), GRID_W).astype(F32)
    col = (jnp.arange(t) % GRID_W).astype(F32)
    half = dh // 2
    quarter = half // 2
    inv = ROPE_THETA ** (-jnp.arange(quarter, dtype=F32) / quarter)

    def rot(xh, pos):
        ang = pos[:, None] * inv[None, :]
        cos = jnp.cos(ang)[None, :, None, :]
        sin = jnp.sin(ang)[None, :, None, :]
        x1, x2 = xh[..., :quarter], xh[..., quarter:]
        return jnp.concatenate([x1 * cos - x2 * sin, x2 * cos + x1 * sin], axis=-1)

    xf = x.astype(F32)
    return jnp.concatenate([rot(xf[..., :half], row), rot(xf[..., half:], col)], axis=-1).astype(x.dtype)


def _sink_probs(s, sink):
    sk = sink.astype(F32)[None, :, :, None, None]
    m = jnp.maximum(jnp.max(s, axis=-1, keepdims=True), sk)
    p = jnp.exp(s - m)
    return p / (jnp.sum(p, axis=-1, keepdims=True) + jnp.exp(sk - m))


def _context_attention(q, k, v, sink):
    b, s, hq, dh = q.shape
    grp = hq // B_KV_HEADS
    qg = (q.astype(F32) * dh ** -0.5).reshape(b, s, B_KV_HEADS, grp, dh)
    kf, vf = k.astype(F32), v.astype(F32)
    sink = sink.reshape(B_KV_HEADS, grp)

    def block(start):
        qb = lax.dynamic_slice_in_dim(qg, start, ATTN_BLOCK, axis=1)
        p = _sink_probs(jnp.einsum('bqhgd,bkhd->bhgqk', qb, kf), sink)
        return jnp.einsum('bhgqk,bkhd->bqhgd', p, vf)

    o = lax.map(block, jnp.arange(s // ATTN_BLOCK) * ATTN_BLOCK)
    return o.transpose(1, 0, 2, 3, 4, 5).reshape(b, s, hq * dh).astype(v.dtype)


def _latent_attention(q_rot, q_plain, k, v, k_ctx, v_ctx, sink):
    b, t, hq, dh = q_rot.shape
    grp = hq // B_KV_HEADS
    span = ATTN_BLOCK + 2 * WINDOW
    scale = dh ** -0.5
    qr = (q_rot.astype(F32) * scale).reshape(b, t, B_KV_HEADS, grp, dh)
    qp = (q_plain.astype(F32) * scale).reshape(b, t, B_KV_HEADS, grp, dh)
    pad = ((0, 0), (WINDOW, WINDOW), (0, 0), (0, 0))
    kp = jnp.pad(k.astype(F32), pad)
    vp = jnp.pad(v.astype(F32), pad)
    kc, vc = k_ctx.astype(F32), v_ctx.astype(F32)
    sink = sink.reshape(B_KV_HEADS, grp)

    def block(start):
        q_loc = lax.dynamic_slice_in_dim(qr, start, ATTN_BLOCK, axis=1)
        q_cb = lax.dynamic_slice_in_dim(qp, start, ATTN_BLOCK, axis=1)
        k_loc = lax.dynamic_slice_in_dim(kp, start, span, axis=1)
        v_loc = lax.dynamic_slice_in_dim(vp, start, span, axis=1)
        qpos = start + jnp.arange(ATTN_BLOCK)
        kpos = start - WINDOW + jnp.arange(span)
        valid = (jnp.abs(qpos[:, None] - kpos[None, :]) <= WINDOW) & (kpos >= 0) & (kpos < t)
        s_loc = jnp.where(valid, jnp.einsum('bqhgd,bkhd->bhgqk', q_loc, k_loc), -jnp.inf)
        s_ctx = jnp.einsum('bqhgd,bkhd->bhgqk', q_cb, kc)
        p = _sink_probs(jnp.concatenate([s_loc, s_ctx], axis=-1), sink)
        return (jnp.einsum('bhgqk,bkhd->bqhgd', p[..., :span], v_loc)
                + jnp.einsum('bhgqk,bkhd->bqhgd', p[..., span:], vc))

    o = lax.map(block, jnp.arange(t // ATTN_BLOCK) * ATTN_BLOCK)
    return o.transpose(1, 0, 2, 3, 4, 5).reshape(b, t, hq * dh).astype(v.dtype)


def _even_inputs(h, w_in, conv_w, a_log, dt_bias):
    b, t, _ = h.shape
    qkv, gate, b_raw, a_raw, qb, kb, vb = _split(h @ w_in, EVEN_SIZES)
    qkv = jax.nn.silu(_depthwise_conv(qkv, conv_w))
    qa, ka, va = _split(qkv, (A_HEADS * A_DK, A_HEADS * A_DK, A_HEADS * A_DV))
    beta = jax.nn.sigmoid(b_raw.astype(F32)).reshape(b, t, 2, A_HEADS)
    g = -jnp.exp(a_log.astype(F32)) * jax.nn.softplus(a_raw.astype(F32).reshape(b, t, 2, A_HEADS)
                                                       + dt_bias.astype(F32))
    return (qa.reshape(b, t, A_HEADS, A_DK), ka.reshape(b, t, A_HEADS, A_DK), va.reshape(b, t, A_HEADS, A_DV),
            g, beta, gate.reshape(b, t, A_HEADS, A_DV), qb.reshape(b, t, B_Q_HEADS, HEAD_DIM),
            kb.reshape(b, t, B_KV_HEADS, HEAD_DIM), vb.reshape(b, t, B_KV_HEADS, HEAD_DIM))


def _even_merge(o_delta, gate, o_attn, onorm, w_out):
    b, t = gate.shape[:2]
    o_a = _rmsnorm(o_delta, onorm) * jax.nn.silu(gate)
    mix = jnp.concatenate([o_a.reshape(b, t, A_HEADS * A_DV).astype(o_attn.dtype), o_attn], axis=-1)
    return mix @ w_out


def _even_context(h, w_in, conv_w, a_log, dt_bias, onorm, sink, w_out):
    qa, ka, va, g, beta, gate, qb, kb, vb = _even_inputs(h, w_in, conv_w, a_log, dt_bias)
    s0 = jnp.zeros((h.shape[0], 2, A_HEADS, A_DK, A_DV), F32)
    o_d, state = _delta_bidir(qa, ka, va, g, beta, s0)
    o_att = _context_attention(qb, kb, vb, sink)
    return _even_merge(o_d, gate, o_att, onorm, w_out), state, kb, vb


def _even_latent(h, s_ctx, k_ctx, v_ctx, w_in, conv_w, a_log, dt_bias, onorm, sink, w_out):
    qa, ka, va, g, beta, gate, qb, kb, vb = _even_inputs(h, w_in, conv_w, a_log, dt_bias)
    o_d, _ = _delta_bidir(qa, ka, va, g, beta, s_ctx)
    o_att = _latent_attention(_axial_rope(qb), qb, _axial_rope(kb), vb, k_ctx, v_ctx, sink)
    return _even_merge(o_d, gate, o_att, onorm, w_out)


def _odd_inputs(h, w_in, w_gate, gate_bias):
    b, t, _ = h.shape
    q, k, v, g_out, lr = _split(h @ w_in, ODD_SIZES)
    lr = lr.astype(F32).reshape(b, t, 2, GATE_RANK)
    gk = jax.nn.log_sigmoid(jnp.einsum('btzr,zrk->btzk', lr, w_gate.astype(F32))
                            + gate_bias.astype(F32)) / GATE_NORM
    return (q.reshape(b, t, C_HEADS, C_DK), k.reshape(b, t, C_HEADS, C_DK), v.reshape(b, t, C_HEADS, C_DV),
            gk.reshape(b, t, 2, C_HEADS, C_DK), g_out.reshape(b, t, C_HEADS, C_DV))


def _odd_merge(o, g_out, onorm, w_out):
    b, t = g_out.shape[:2]
    o = _rmsnorm(o, onorm) * jax.nn.silu(g_out)
    return o.reshape(b, t, ODD_MIX).astype(g_out.dtype) @ w_out


def _odd_context(h, w_in, w_gate, gate_bias, onorm, w_out):
    q, k, v, gk, g_out = _odd_inputs(h, w_in, w_gate, gate_bias)
    s0 = jnp.zeros((h.shape[0], 2, C_HEADS, C_DK, C_DV), F32)
    o, state = _gla_bidir(q, k, v, gk, s0)
    return _odd_merge(o, g_out, onorm, w_out), state


def _odd_latent(h, s_ctx, w_in, w_gate, gate_bias, onorm, w_out):
    q, k, v, gk, g_out = _odd_inputs(h, w_in, w_gate, gate_bias)
    o, _ = _gla_bidir(q, k, v, gk, s_ctx)
    return _odd_merge(o, g_out, onorm, w_out)


def setup_inputs(seed: int = 0) -> dict:
    key = jax.random.key(seed)
    ks = jax.random.split(key, 26)

    def nrm(i, shape, s=1.0):
        return jax.random.normal(ks[i], shape, F32) * s

    a_log = jnp.log(jax.random.uniform(ks[15], (N_EVEN, 2, A_HEADS), F32, 1.0, 16.0))
    dt = jnp.exp(jax.random.uniform(ks[16], (N_EVEN, 2, A_HEADS), F32,
                                    float(np.log(1e-3)), float(np.log(1e-1))))
    return {
        'x_prompt': nrm(0, (BATCH, SEQ, D_MODEL)),
        'x_sample': nrm(1, (DEC_BATCH, DEC_SEQ, D_MODEL)),
        'state_delta': nrm(2, (DEC_BATCH, N_EVEN, 2, A_HEADS, A_DK, A_DV), A_DK ** -0.5),
        'cache_k': nrm(3, (DEC_BATCH, N_EVEN, PAST_LEN, B_KV_HEADS, HEAD_DIM)),
        'cache_v': nrm(4, (DEC_BATCH, N_EVEN, PAST_LEN, B_KV_HEADS, HEAD_DIM)),
        'state_gla': nrm(5, (DEC_BATCH, N_ODD, 2, C_HEADS, C_DK, C_DV), 0.5),
        'c': nrm(6, (DEC_BATCH, D_MODEL)),
        'c_ctx': nrm(7, (D_MODEL,)),
        'norm_g': 1.0 + nrm(8, (DEPTH, 3, D_MODEL), 0.02),
        'ada_w': nrm(9, (DEPTH, D_MODEL, N_MOD * D_MODEL), 0.5 * D_MODEL ** -0.5),
        'ada_b': nrm(10, (DEPTH, N_MOD * D_MODEL), 0.01),
        'ffn_w_gu': nrm(11, (DEPTH, 2, D_MODEL, 2 * D_FF), D_MODEL ** -0.5),
        'ffn_w_down': nrm(12, (DEPTH, 2, D_FF, D_MODEL), D_FF ** -0.5),
        'even_w_in': nrm(13, (N_EVEN, D_MODEL, EVEN_IN), D_MODEL ** -0.5),
        'even_conv': nrm(14, (N_EVEN, SHORT_CONV, EVEN_SIZES[0]), SHORT_CONV ** -0.5),
        'even_a_log': a_log,
        'even_dt_bias': dt + jnp.log(-jnp.expm1(-dt)),
        'even_onorm': 1.0 + nrm(17, (N_EVEN, A_DV), 0.02),
        'even_sink': nrm(18, (N_EVEN, B_Q_HEADS), 0.5),
        'even_w_out': nrm(19, (N_EVEN, EVEN_MIX, D_MODEL), EVEN_MIX ** -0.5),
        'odd_w_in': nrm(20, (N_ODD, D_MODEL, ODD_IN), D_MODEL ** -0.5),
        'odd_w_gate': nrm(21, (N_ODD, 2, GATE_RANK, C_HEADS * C_DK), GATE_RANK ** -0.5),
        'odd_gate_bias': nrm(22, (N_ODD, 2, C_HEADS * C_DK), 0.1),
        'odd_onorm': 1.0 + nrm(23, (N_ODD, C_DV), 0.02),
        'odd_w_out': nrm(24, (N_ODD, ODD_MIX, D_MODEL), ODD_MIX ** -0.5),
        'final_g': 1.0 + nrm(25, (D_MODEL,), 0.02),
    }


def reference(x_prompt, x_sample, state_delta, cache_k, cache_v, state_gla, c, c_ctx,
              norm_g, ada_w, ada_b, ffn_w_gu, ffn_w_down,
              even_w_in, even_conv, even_a_log, even_dt_bias, even_onorm, even_sink, even_w_out,
              odd_w_in, odd_w_gate, odd_gate_bias, odd_onorm, odd_w_out, final_g):
    xp, xs = x_prompt, x_sample
    new_delta, new_k, new_v, new_gla = [], [], [], []
    for l in range(DEPTH):
        j = l // 2
        mp = _ada(c_ctx[None, :], ada_w[l], ada_b[l])
        ms = _ada(c, ada_w[l], ada_b[l])
        xp = _ffn_half(xp, mp, 0, norm_g[l, 0], ffn_w_gu[l, 0], ffn_w_down[l, 0])
        xs = _ffn_half(xs, ms, 0, norm_g[l, 0], ffn_w_gu[l, 0], ffn_w_down[l, 0])
        hp = _modnorm(xp, norm_g[l, 1], mp[:, 3], mp[:, 4])
        hs = _modnorm(xs, norm_g[l, 1], ms[:, 3], ms[:, 4])
        if l % 2 == 0:
            even = (even_w_in[j], even_conv[j], even_a_log[j], even_dt_bias[j], even_onorm[j],
                    even_sink[j], even_w_out[j])
            op, st, kc, vc = _even_context(hp, *even)
            os_ = _even_latent(hs, state_delta[:, j], cache_k[:, j], cache_v[:, j], *even)
            new_delta.append(st)
            new_k.append(kc)
            new_v.append(vc)
        else:
            odd = (odd_w_in[j], odd_w_gate[j], odd_gate_bias[j], odd_onorm[j], odd_w_out[j])
            op, st = _odd_context(hp, *odd)
            os_ = _odd_latent(hs, state_gla[:, j], *odd)
            new_gla.append(st)
        xp = xp + mp[:, 5] * op
        xs = xs + ms[:, 5] * os_
        xp = _ffn_half(xp, mp, 6, norm_g[l, 2], ffn_w_gu[l, 1], ffn_w_down[l, 1])
        xs = _ffn_half(xs, ms, 6, norm_g[l, 2], ffn_w_gu[l, 1], ffn_w_down[l, 1])
    y_prompt = _rmsnorm(xp, final_g)
    y_sample = _rmsnorm(xs, final_g)
    new_state_delta = jnp.stack(new_delta, axis=1)
    new_cache_k = jnp.stack(new_k, axis=1)
    new_cache_v = jnp.stack(new_v, axis=1)
    new_state_gla = jnp.stack(new_gla, axis=1)
    return (y_prompt, y_sample, new_state_delta, new_cache_k, new_cache_v, new_state_gla)
```

```python
import functools

import jax
import jax.numpy as jnp
from jax import lax
from jax.experimental import pallas as pl
from jax.experimental.pallas import tpu as pltpu

F32 = jnp.float32
BF16 = jnp.bfloat16

EPS = 1e-6
N_MOD = 9
GRID_W = 64
HEAD_DIM = 128
A_HEADS = 4
A_DK = 128
A_DV = 128
SHORT_CONV = 5
B_Q_HEADS = 4
B_KV_HEADS = 2
WINDOW = 128
C_HEADS = 4
C_DK = 128
C_DV = 256
GATE_RANK = 16
GATE_NORM = 16.0
ROPE_THETA = 10000.0

LANES = 128
SUBLANES = 8
ROW_TILE = 512
CHUNK = 256
GLA_ROW_BLOCK = 64
VMEM_LIMIT = 56 * 1024 * 1024


def _cparams(sem, vmem=VMEM_LIMIT):
    return pltpu.CompilerParams(dimension_semantics=sem, vmem_limit_bytes=vmem)


def _resident(block_shape, index_map):
    return pl.BlockSpec(block_shape, index_map, pipeline_mode=pl.Buffered(1))


def _dot(a, b):
    return jnp.dot(a, b, preferred_element_type=F32)


def _dot_nt(a, b):
    return lax.dot_general(a, b, (((1,), (1,)), ((), ())), preferred_element_type=F32)


def _dot_tn(a, b):
    return lax.dot_general(a, b, (((0,), (0,)), ((), ())), preferred_element_type=F32)


def _bf(x):
    return x.astype(BF16)


def _sigmoid(x):
    return 1.0 / (1.0 + jnp.exp(-x))


def _silu(x):
    return x * _sigmoid(x)


def _softplus(x):
    return jnp.maximum(x, 0.0) + jnp.log1p(jnp.exp(-jnp.abs(x)))


def _rms(x):
    return x * lax.rsqrt(jnp.mean(x * x, axis=-1, keepdims=True) + EPS)


def _modnorm(x, g, shift, scale):
    return (_rms(x) * g) * (1.0 + scale) + shift


def _split3(x):
    hi = _bf(x)
    r1 = x - hi.astype(F32)
    mid = _bf(r1)
    lo = _bf(r1 - mid.astype(F32))
    return hi, mid, lo


def _cumsum_rows(tri_bf, x):
    hi, mid, lo = _split3(x)
    return (_dot(tri_bf, hi) + _dot(tri_bf, mid)) + _dot(tri_bf, lo)


def _cumsum_cols(x, trit_bf):
    hi, mid, lo = _split3(x)
    return (_dot(hi, trit_bf) + _dot(mid, trit_bf)) + _dot(lo, trit_bf)


def _order_masks(c, reverse):
    ri = lax.broadcasted_iota(jnp.int32, (c, c), 0)
    ci = lax.broadcasted_iota(jnp.int32, (c, c), 1)
    if reverse:
        return ri <= ci, ri < ci, ri >= ci
    return ri >= ci, ri > ci, ri <= ci


class _Tiles:
    def __init__(self, n_p, t_p, n_s, t_s, c, reverse=False):
        self.n_p, self.n_s, self.c, self.reverse = n_p, n_s, c, reverse
        self.per_p, self.per_s = t_p // c, t_s // c
        self.np_tiles = n_p * self.per_p
        self.n = self.np_tiles + n_s * self.per_s

    def row_block(self, i):
        return self.n - 1 - i if self.reverse else i

    def is_ctx(self, r):
        return r < self.np_tiles

    def seq(self, r):
        return jnp.where(r < self.np_tiles, r // self.per_p, self.n_p + (r - self.np_tiles) // self.per_s)

    def pos(self, r):
        return jnp.where(r < self.np_tiles, r % self.per_p, (r - self.np_tiles) % self.per_s)

    def length(self, r):
        return jnp.where(r < self.np_tiles, self.per_p, self.per_s)

    def first_in_scan(self, r):
        return self.pos(r) == (self.length(r) - 1 if self.reverse else 0)


def _ada_kernel(c_ref, w_ref, b_ref, o_ref):
    s = _bf(_silu(c_ref[...]))
    o_ref[0] = _dot(s, _bf(w_ref[0])) + b_ref[0]


def _ada(cond, ada_w, ada_b):
    depth, d, n = ada_w.shape
    rows = cond.shape[0]
    tn = n // 4
    return pl.pallas_call(
        _ada_kernel,
        out_shape=jax.ShapeDtypeStruct((depth, rows, n), F32),
        grid=(depth, n // tn),
        in_specs=[pl.BlockSpec((rows, d), lambda l, j: (0, 0)),
                  pl.BlockSpec((1, d, tn), lambda l, j: (l, 0, j)),
                  pl.BlockSpec((1, 1, tn), lambda l, j: (l, 0, j))],
        out_specs=pl.BlockSpec((1, rows, tn), lambda l, j: (l, 0, j)),
        compiler_params=_cparams(("parallel", "parallel")),
        name="ada",
    )(cond, ada_w, ada_b.reshape(depth, 1, n))


def _cond_index(n_prompt_rows, dec_seq, tm):
    npt = n_prompt_rows // tm

    def cond(i):
        return jnp.where(i < npt, 0, 1 + ((i - npt) * tm) // dec_seq)

    return cond


def _ffn_kernel(x_ref, mod_ref, g_ref, wgu_ref, wd_ref, o_ref, *, i0, d_ff, n_chunks):
    x = x_ref[...]
    mod = mod_ref[0]
    h = _bf(_modnorm(x, g_ref[...], mod[i0:i0 + 1], mod[i0 + 1:i0 + 2]))
    ch = d_ff // n_chunks
    y = None
    for c in range(n_chunks):
        gt = _dot(h, wgu_ref[:, c * ch:(c + 1) * ch])
        up = _dot(h, wgu_ref[:, d_ff + c * ch:d_ff + (c + 1) * ch])
        part = _dot(_bf(_silu(gt) * up), wd_ref[c * ch:(c + 1) * ch, :])
        y = part if y is None else y + part
    o_ref[...] = x + (0.5 * mod[i0 + 2:i0 + 3]) * y


def _ffn_half(x, mod, i0, g, w_gu, w_down, cond, tm):
    nt, d = x.shape
    d_ff = w_down.shape[0]
    kern = functools.partial(_ffn_kernel, i0=i0, d_ff=d_ff, n_chunks=2)
    return pl.pallas_call(
        kern,
        out_shape=jax.ShapeDtypeStruct((nt, d), F32),
        grid=(nt // tm,),
        in_specs=[pl.BlockSpec((tm, d), lambda i: (i, 0)),
                  pl.BlockSpec((1, N_MOD, d), lambda i: (cond(i), 0, 0)),
                  pl.BlockSpec((1, d), lambda i: (0, 0)),
                  _resident((d, 2 * d_ff), lambda i: (0, 0)),
                  _resident((d_ff, d), lambda i: (0, 0))],
        out_specs=pl.BlockSpec((tm, d), lambda i: (i, 0)),
        compiler_params=_cparams(("parallel",)),
        name="ffn_half",
    )(x, mod, g.reshape(1, d), w_gu, w_down)


def _proj_kernel(x_ref, mod_ref, g_ref, w_ref, *o_refs, i0, widths):
    mod = mod_ref[0]
    h = _bf(_modnorm(x_ref[...], g_ref[...], mod[i0:i0 + 1], mod[i0 + 1:i0 + 2]))
    off = 0
    for o_ref, wd in zip(o_refs, widths):
        o_ref[...] = _dot(h, w_ref[:, off:off + wd])
        off += wd


def _mixer_in(x, mod, g, w, widths, cond, tm):
    nt, d = x.shape
    n = sum(widths)
    kern = functools.partial(_proj_kernel, i0=3, widths=tuple(widths))
    return pl.pallas_call(
        kern,
        out_shape=[jax.ShapeDtypeStruct((nt, wd), F32) for wd in widths],
        grid=(nt // tm,),
        in_specs=[pl.BlockSpec((tm, d), lambda i: (i, 0)),
                  pl.BlockSpec((1, N_MOD, d), lambda i: (cond(i), 0, 0)),
                  pl.BlockSpec((1, d), lambda i: (0, 0)),
                  _resident((d, n), lambda i: (0, 0))],
        out_specs=[pl.BlockSpec((tm, wd), lambda i: (i, 0)) for wd in widths],
        compiler_params=_cparams(("parallel",)),
        name="mixer_in",
    )(x, mod, g.reshape(1, d), w)


def _mixer_out_kernel(x_ref, mod_ref, of_ref, ob_ref, gate_ref, on_ref, w_ref, *rest, nh, dv):
    o_ref = rest[-1]
    mod = mod_ref[0]
    od = of_ref[...] + ob_ref[...]
    gate = gate_ref[...]
    y = None
    if len(rest) == 2:
        y = _dot(_bf(rest[0][...]), w_ref[nh * dv:, :])
    for h in range(nh):
        sl = slice(h * dv, (h + 1) * dv)
        oa = (_rms(od[:, sl]) * on_ref[...]) * _silu(gate[:, sl])
        part = _dot(_bf(oa), w_ref[sl, :])
        y = part if y is None else y + part
    o_ref[...] = x_ref[...] + mod[5:6] * y


def _mixer_out(x, mod, o_f, o_b, gate, extra, onorm, w_out, nh, dv, cond, tm):
    nt, d = x.shape
    wa = nh * dv
    kern = functools.partial(_mixer_out_kernel, nh=nh, dv=dv)
    row = lambda i: (i, 0)
    in_specs = [pl.BlockSpec((tm, d), row),
                pl.BlockSpec((1, N_MOD, d), lambda i: (cond(i), 0, 0)),
                pl.BlockSpec((tm, wa), row), pl.BlockSpec((tm, wa), row), pl.BlockSpec((tm, wa), row),
                pl.BlockSpec((1, dv), lambda i: (0, 0)),
                _resident(w_out.shape, lambda i: (0, 0))]
    args = [x, mod, o_f, o_b, gate, onorm.reshape(1, dv), w_out]
    if extra is not None:
        in_specs.append(pl.BlockSpec((tm, extra.shape[1]), row))
        args.append(extra)
    return pl.pallas_call(
        kern,
        out_shape=jax.ShapeDtypeStruct((nt, d), F32),
        grid=(nt // tm,),
        in_specs=in_specs,
        out_specs=pl.BlockSpec((tm, d), row),
        compiler_params=_cparams(("parallel",)),
        name="mixer_out",
    )(*args)


def _final_norm_kernel(x_ref, g_ref, o_ref):
    o_ref[...] = _rms(x_ref[...]) * g_ref[...]


def _final_norm(x, g, row0, rows, tm):
    d = x.shape[1]
    base = row0 // tm
    return pl.pallas_call(
        _final_norm_kernel,
        out_shape=jax.ShapeDtypeStruct((rows, d), F32),
        grid=(rows // tm,),
        in_specs=[pl.BlockSpec((tm, d), lambda i: (base + i, 0)),
                  pl.BlockSpec((1, d), lambda i: (0, 0))],
        out_specs=pl.BlockSpec((tm, d), lambda i: (i, 0)),
        compiler_params=_cparams(("parallel",)),
        name="final_norm",
    )(x, g.reshape(1, d))


def _conv_kernel(prev_ref, x_ref, next_ref, w_ref, o_ref, *, tiles, dk):
    r = pl.program_id(0)
    part = pl.program_id(1)
    c, width = x_ref.shape
    pad = (SHORT_CONV - 1) // 2
    has_prev = (tiles.pos(r) > 0).astype(F32)
    has_next = (tiles.pos(r) < tiles.length(r) - 1).astype(F32)
    ext = c + 2 * SUBLANES
    for hh in range(width // LANES):
        sl = slice(hh * LANES, (hh + 1) * LANES)
        xe = jnp.concatenate([prev_ref[:, sl] * has_prev, x_ref[:, sl], next_ref[:, sl] * has_next], axis=0)
        w = w_ref[:, sl]
        acc = None
        for j in range(SHORT_CONV):
            sh = pad - j
            xs = xe if sh == 0 else pltpu.roll(xe, sh % ext, axis=0)
            term = xs[SUBLANES:SUBLANES + c] * w[j:j + 1]
            acc = term if acc is None else acc + term
        y = _silu(acc)
        nrm = lax.rsqrt(jnp.sum(y * y, axis=-1, keepdims=True) + EPS)
        scale = jnp.where(part == 0, nrm * (dk ** -0.5), jnp.where(part == 1, nrm, 1.0))
        o_ref[:, sl] = y * scale


def _conv_qkv(qkv, conv_w, tiles):
    nt, width = qkv.shape
    c = tiles.c
    pw = width // 3
    per = c // SUBLANES
    n8 = nt // SUBLANES
    kern = functools.partial(_conv_kernel, tiles=tiles, dk=A_DK)
    return pl.pallas_call(
        kern,
        out_shape=jax.ShapeDtypeStruct((nt, width), F32),
        grid=(tiles.n, 3),
        in_specs=[pl.BlockSpec((SUBLANES, pw), lambda r, p: (jnp.maximum(r * per - 1, 0), p)),
                  pl.BlockSpec((c, pw), lambda r, p: (r, p)),
                  pl.BlockSpec((SUBLANES, pw), lambda r, p: (jnp.minimum((r + 1) * per, n8 - 1), p)),
                  pl.BlockSpec((SHORT_CONV, pw), lambda r, p: (0, p))],
        out_specs=pl.BlockSpec((c, pw), lambda r, p: (r, p)),
        compiler_params=_cparams(("parallel", "parallel")),
        name="conv_qkv",
    )(qkv, qkv, qkv, conv_w)


def _unit_tri_inverse(l, ri, ci):
    c = l.shape[0]
    shift = SUBLANES.bit_length() - 1
    m = jnp.where((ri >> shift) == (ci >> shift), -l, 0.0)
    m_bf = _bf(m)
    p = _dot(m_bf, m_bf)
    n = m + p + _dot(m_bf, _bf(p))
    p_bf = _bf(p)
    p = _dot(p_bf, p_bf)
    n = n + p + _dot(_bf(n), _bf(p))
    while (1 << shift) < c:
        cl = jnp.where(((ri >> (shift + 1)) == (ci >> (shift + 1))) & ((ri >> shift) != (ci >> shift)), l, 0.0)
        n_bf = _bf(n)
        y = cl + _dot(_bf(cl), n_bf)
        n = n - (y + _dot(n_bf, _bf(y)))
        shift += 1
    return n


def _init_state(s_ref, s0_ref, tiles, r):
    first = tiles.first_in_scan(r)

    @pl.when(first & tiles.is_ctx(r))
    def _():
        s_ref[...] = jnp.zeros(s_ref.shape, F32)

    @pl.when(first & jnp.logical_not(tiles.is_ctx(r)))
    def _():
        s_ref[...] = s0_ref[...]


def _delta_kernel(q_ref, k_ref, v_ref, ba_ref, bat_ref, alc_ref, dtc_ref, alr_ref, dtr_ref, s0_ref,
                  o_ref, s_ref, *, tiles, z, nh, dk, dv):
    reverse = tiles.reverse
    _init_state(s_ref, s0_ref, tiles, tiles.row_block(pl.program_id(0)))
    c = q_ref.shape[0]
    incl, strict, incl_t = _order_masks(c, reverse)
    tri = _bf(incl.astype(F32))
    trit = _bf(incl_t.astype(F32))
    ba = ba_ref[...]
    g_col = -jnp.exp(alc_ref[...]) * _softplus(ba + dtc_ref[...])
    beta_col = _sigmoid(ba)
    gc_col = _cumsum_rows(tri, g_col)
    g_row = -jnp.exp(alr_ref[...]) * _softplus(bat_ref[...] + dtr_ref[...])
    gc_row = _cumsum_cols(g_row, trit)
    last = 0 if reverse else c - 1
    ri = lax.broadcasted_iota(jnp.int32, (c, c), 0)
    ci = lax.broadcasted_iota(jnp.int32, (c, c), 1)
    for h in range(nh):
        cb = z * nh + h
        cg = 2 * nh + cb
        gcc = gc_col[:, cg:cg + 1]
        gcr = gc_row[cg:cg + 1, :]
        beta = beta_col[:, cb:cb + 1]
        decay = jnp.where(incl, jnp.exp(jnp.where(incl, gcc - gcr, 0.0)), 0.0)
        egc = jnp.exp(gcc)
        gl = gcc[last:last + 1]
        q = q_ref[:, h * dk:(h + 1) * dk]
        k = k_ref[:, h * dk:(h + 1) * dk]
        v = v_ref[:, h * dv:(h + 1) * dv]
        kb = k * beta
        k_bf = _bf(k)
        kk = _dot_nt(_bf(kb), k_bf)
        qk = _dot_nt(_bf(q), k_bf) * decay
        n_inv = _unit_tri_inverse(jnp.where(strict, kk * decay, 0.0), ri, ci)
        r = jnp.concatenate([v * beta, kb * egc], axis=1)
        r = r + _dot(_bf(n_inv), _bf(r))
        u = r[:, :dv]
        w = r[:, dv:]
        s = s_ref[0, h]
        s_bf = _bf(s)
        v_new = u - _dot(_bf(w), s_bf)
        o_ref[:, h * dv:(h + 1) * dv] = _dot(_bf(q * egc), s_bf) + _dot(_bf(qk), _bf(v_new))
        kd = k * jnp.exp(gl - gcc)
        s_ref[0, h] = s * jnp.exp(gl) + _dot_tn(_bf(kd), _bf(v_new))


def _delta_dir(qkv, ba, bat, a_log, dt_bias, s0, z, tiles):
    nt = qkv.shape[0]
    nh, dk, dv = A_HEADS, A_DK, A_DV
    c = tiles.c
    n_seq = tiles.n_p + tiles.n_s
    rb = tiles.row_block
    pad = LANES - 4 * nh
    al = jnp.concatenate([jnp.zeros((2 * nh,), F32), a_log.reshape(-1), jnp.zeros((pad,), F32)])
    dt = jnp.concatenate([jnp.zeros((2 * nh,), F32), dt_bias.reshape(-1), jnp.zeros((pad,), F32)])
    nr = bat.shape[0]
    kern = functools.partial(_delta_kernel, tiles=tiles, z=z, nh=nh, dk=dk, dv=dv)
    wq = nh * dk
    return pl.pallas_call(
        kern,
        out_shape=[jax.ShapeDtypeStruct((nt, nh * dv), F32),
                   jax.ShapeDtypeStruct((n_seq, nh, dk, dv), F32)],
        grid=(tiles.n,),
        in_specs=[pl.BlockSpec((c, wq), lambda i: (rb(i), 0)),
                  pl.BlockSpec((c, wq), lambda i: (rb(i), 1)),
                  pl.BlockSpec((c, nh * dv), lambda i: (rb(i), 2)),
                  pl.BlockSpec((c, LANES), lambda i: (rb(i), 0)),
                  pl.BlockSpec((nr, c), lambda i: (0, rb(i))),
                  pl.BlockSpec((1, LANES), lambda i: (0, 0)),
                  pl.BlockSpec((1, LANES), lambda i: (0, 0)),
                  pl.BlockSpec((nr, 1), lambda i: (0, 0)),
                  pl.BlockSpec((nr, 1), lambda i: (0, 0)),
                  pl.BlockSpec((1, nh, dk, dv), lambda i: (jnp.maximum(tiles.seq(rb(i)) - tiles.n_p, 0), 0, 0, 0))],
        out_specs=[pl.BlockSpec((c, nh * dv), lambda i: (rb(i), 0)),
                   pl.BlockSpec((1, nh, dk, dv), lambda i: (tiles.seq(rb(i)), 0, 0, 0))],
        compiler_params=_cparams(("arbitrary",)),
        name="delta_rule",
    )(qkv, qkv, qkv, ba, bat, al.reshape(1, LANES), dt.reshape(1, LANES),
      al[:nr].reshape(nr, 1), dt[:nr].reshape(nr, 1), s0)


def _ctx_attn_kernel(sink_ref, q_ref, k_ref, v_ref, o_ref, *, n_kv, grp, dh):
    scale = dh ** -0.5
    for hk in range(n_kv):
        k = _bf(k_ref[:, hk * dh:(hk + 1) * dh])
        v = _bf(v_ref[:, hk * dh:(hk + 1) * dh])
        for g in range(grp):
            h = hk * grp + g
            sk = sink_ref[h]
            s = _dot_nt(_bf(q_ref[:, h * dh:(h + 1) * dh] * scale), k)
            m = jnp.maximum(jnp.max(s, axis=-1, keepdims=True), sk)
            p = jnp.exp(s - m)
            den = jnp.sum(p, axis=-1, keepdims=True) + jnp.exp(sk - m)
            o_ref[:, h * dh:(h + 1) * dh] = _dot(_bf(p), v) / den


def _ctx_attention(qb, kb, vb, sink, n_seq, t):
    nt, wq = qb.shape
    wk = kb.shape[1]
    kern = functools.partial(_ctx_attn_kernel, n_kv=B_KV_HEADS, grp=B_Q_HEADS // B_KV_HEADS, dh=HEAD_DIM)
    return pl.pallas_call(
        kern,
        out_shape=jax.ShapeDtypeStruct((nt, wq), F32),
        grid=(n_seq,),
        in_specs=[pl.BlockSpec(memory_space=pltpu.SMEM),
                  pl.BlockSpec((t, wq), lambda b: (b, 0)),
                  pl.BlockSpec((t, wk), lambda b: (b, 0)),
                  pl.BlockSpec((t, wk), lambda b: (b, 0))],
        out_specs=pl.BlockSpec((t, wq), lambda b: (b, 0)),
        compiler_params=_cparams(("parallel",)),
        name="ctx_attention",
    )(sink, qb, kb, vb)


def _rope(x, cos, sin_signed):
    lane = lax.broadcasted_iota(jnp.int32, x.shape, 1)
    quarter = HEAD_DIM // 4
    partner = jnp.where((lane % (2 * quarter)) < quarter,
                        pltpu.roll(x, HEAD_DIM - quarter, axis=1), pltpu.roll(x, quarter, axis=1))
    return x * cos + partner * sin_signed


def _lat_attn_kernel(sink_ref, q_ref, k_ref, v_ref, kc_ref, vc_ref, cq_ref, sq_ref, ck_ref, sk_ref, prev_ref,
                     o_ref, *, n_kv, grp, dh, window):
    del prev_ref
    scale = dh ** -0.5
    tq = q_ref.shape[0]
    t = k_ref.shape[0]
    qpos = pl.program_id(1) * tq + lax.broadcasted_iota(jnp.int32, (tq, t), 0)
    kpos = lax.broadcasted_iota(jnp.int32, (tq, t), 1)
    valid = jnp.abs(qpos - kpos) <= window
    cq, sq = cq_ref[...], sq_ref[...]
    ck, sk_t = ck_ref[...], sk_ref[...]
    for hk in range(n_kv):
        hs = slice(hk * dh, (hk + 1) * dh)
        k_rot = _bf(_rope(k_ref[:, hs], ck, sk_t))
        v = _bf(v_ref[:, hs])
        kc = _bf(kc_ref[0, :, hs])
        vc = _bf(vc_ref[0, :, hs])
        for g in range(grp):
            h = hk * grp + g
            sink = sink_ref[h]
            q = q_ref[:, h * dh:(h + 1) * dh] * scale
            s_loc = jnp.where(valid, _dot_nt(_bf(_rope(q, cq, sq)), k_rot), -jnp.inf)
            s_ctx = _dot_nt(_bf(q), kc)
            m = jnp.maximum(jnp.maximum(jnp.max(s_loc, axis=-1, keepdims=True),
                                        jnp.max(s_ctx, axis=-1, keepdims=True)), sink)
            p_loc = jnp.exp(s_loc - m)
            p_ctx = jnp.exp(s_ctx - m)
            den = (jnp.sum(p_loc, axis=-1, keepdims=True) + jnp.sum(p_ctx, axis=-1, keepdims=True)
                   + jnp.exp(sink - m))
            o_ref[:, h * dh:(h + 1) * dh] = (_dot(_bf(p_loc), v) + _dot(_bf(p_ctx), vc)) / den


def _lat_attention(qb, kb, vb, k_ctx, v_ctx, sink, cos, sin_signed, out, row0, n_seq, t):
    nt, wq = qb.shape
    wk = kb.shape[1]
    tq = CHUNK
    nq = t // tq
    base_q = row0 // tq
    base_t = row0 // t
    past = k_ctx.shape[1]
    kern = functools.partial(_lat_attn_kernel, n_kv=B_KV_HEADS, grp=B_Q_HEADS // B_KV_HEADS, dh=HEAD_DIM,
                             window=WINDOW)
    return pl.pallas_call(
        kern,
        out_shape=jax.ShapeDtypeStruct((nt, wq), F32),
        grid=(n_seq, nq),
        in_specs=[pl.BlockSpec(memory_space=pltpu.SMEM),
                  pl.BlockSpec((tq, wq), lambda b, i: (base_q + b * nq + i, 0)),
                  pl.BlockSpec((t, wk), lambda b, i: (base_t + b, 0)),
                  pl.BlockSpec((t, wk), lambda b, i: (base_t + b, 0)),
                  pl.BlockSpec((1, past, wk), lambda b, i: (b, 0, 0)),
                  pl.BlockSpec((1, past, wk), lambda b, i: (b, 0, 0)),
                  pl.BlockSpec((tq, HEAD_DIM), lambda b, i: (i, 0)),
                  pl.BlockSpec((tq, HEAD_DIM), lambda b, i: (i, 0)),
                  pl.BlockSpec((t, HEAD_DIM), lambda b, i: (0, 0)),
                  pl.BlockSpec((t, HEAD_DIM), lambda b, i: (0, 0)),
                  pl.BlockSpec(memory_space=pl.ANY)],
        out_specs=pl.BlockSpec((tq, wq), lambda b, i: (base_q + b * nq + i, 0)),
        input_output_aliases={10: 0},
        compiler_params=_cparams(("parallel", "parallel")),
        name="latent_attention",
    )(sink, qb, kb, vb, k_ctx, v_ctx, cos, sin_signed, cos, sin_signed, out)


def _rope_tables(t):
    half = HEAD_DIM // 2
    quarter = half // 2
    pos = jnp.arange(t)
    row = (pos // GRID_W).astype(F32)
    col = (pos % GRID_W).astype(F32)
    inv = ROPE_THETA ** (-jnp.arange(quarter, dtype=F32) / quarter)
    ang_r = row[:, None] * inv[None, :]
    ang_c = col[:, None] * inv[None, :]
    cos = jnp.concatenate([jnp.cos(ang_r), jnp.cos(ang_r), jnp.cos(ang_c), jnp.cos(ang_c)], axis=-1)
    sin = jnp.concatenate([-jnp.sin(ang_r), jnp.sin(ang_r), -jnp.sin(ang_c), jnp.sin(ang_c)], axis=-1)
    return cos, sin


def _gla_kernel(q_ref, k_ref, v_ref, lr_ref, wg_ref, bias_ref, s0_ref, o_ref, s_ref, *, tiles, nh, dk, dv, rb):
    reverse = tiles.reverse
    _init_state(s_ref, s0_ref, tiles, tiles.row_block(pl.program_id(0)))
    c = q_ref.shape[0]
    incl, _, _ = _order_masks(c, reverse)
    tri = _bf(incl.astype(F32))
    x = _dot(_bf(lr_ref[...]), _bf(wg_ref[...])) + bias_ref[...]
    gk = (jnp.minimum(x, 0.0) - jnp.log1p(jnp.exp(-jnp.abs(x)))) * (1.0 / GATE_NORM)
    gcum = _cumsum_rows(tri, gk)
    eye = (lax.broadcasted_iota(jnp.int32, (dk, dk), 0) == lax.broadcasted_iota(jnp.int32, (dk, dk), 1))
    last = 0 if reverse else c - 1
    zero_row = jnp.zeros((1, dk), F32)
    for h in range(nh):
        g = gcum[:, h * dk:(h + 1) * dk]
        gl_row = g[last:last + 1]
        q = q_ref[:, h * dk:(h + 1) * dk] * (dk ** -0.5)
        k = k_ref[:, h * dk:(h + 1) * dk]
        v_bf = _bf(v_ref[:, h * dv:(h + 1) * dv])
        s = s_ref[0, h]
        blocks = []
        for blk in range(c // rb):
            r0, r1 = blk * rb, (blk + 1) * rb
            if reverse:
                c0, c1 = r0, c
                ref = g[r1:r1 + 1] if r1 < c else zero_row
            else:
                c0, c1 = 0, r1
                ref = g[r0 - 1:r0] if r0 > 0 else zero_row
            qe = q[r0:r1] * jnp.exp(g[r0:r1] - ref)
            ke = k[c0:c1] * jnp.exp(ref - g[c0:c1])
            a = jnp.where(incl[r0:r1, c0:c1], _dot_nt(_bf(qe), _bf(ke)), 0.0)
            blocks.append(_dot(_bf(a), v_bf[c0:c1]))
        o_ref[:, h * dv:(h + 1) * dv] = _dot(_bf(q * jnp.exp(g)), _bf(s)) + jnp.concatenate(blocks, axis=0)
        kd = k * jnp.exp(gl_row - g)
        gl_col = jnp.sum(jnp.where(eye, jnp.broadcast_to(gl_row, (dk, dk)), 0.0), axis=1, keepdims=True)
        s_ref[0, h] = s * jnp.exp(gl_col) + _dot_tn(_bf(kd), v_bf)


def _gla_dir(q, k, v, lr, w_gate, gate_bias, s0, z, tiles):
    nt = q.shape[0]
    nh, dk, dv = C_HEADS, C_DK, C_DV
    c = tiles.c
    n_seq = tiles.n_p + tiles.n_s
    rb = tiles.row_block
    wg = jnp.zeros((LANES, nh * dk), F32).at[z * GATE_RANK:(z + 1) * GATE_RANK].set(w_gate[z])
    kern = functools.partial(_gla_kernel, tiles=tiles, nh=nh, dk=dk, dv=dv, rb=GLA_ROW_BLOCK)
    return pl.pallas_call(
        kern,
        out_shape=[jax.ShapeDtypeStruct((nt, nh * dv), F32),
                   jax.ShapeDtypeStruct((n_seq, nh, dk, dv), F32)],
        grid=(tiles.n,),
        in_specs=[pl.BlockSpec((c, nh * dk), lambda i: (rb(i), 0)),
                  pl.BlockSpec((c, nh * dk), lambda i: (rb(i), 0)),
                  pl.BlockSpec((c, nh * dv), lambda i: (rb(i), 0)),
                  pl.BlockSpec((c, LANES), lambda i: (rb(i), 0)),
                  pl.BlockSpec((LANES, nh * dk), lambda i: (0, 0)),
                  pl.BlockSpec((1, nh * dk), lambda i: (0, 0)),
                  pl.BlockSpec((1, nh, dk, dv), lambda i: (jnp.maximum(tiles.seq(rb(i)) - tiles.n_p, 0), 0, 0, 0))],
        out_specs=[pl.BlockSpec((c, nh * dv), lambda i: (rb(i), 0)),
                   pl.BlockSpec((1, nh, dk, dv), lambda i: (tiles.seq(rb(i)), 0, 0, 0))],
        compiler_params=_cparams(("arbitrary",)),
        name="gla",
    )(q, k, v, lr, wg, gate_bias[z].reshape(1, nh * dk), s0)


def kernel(x_prompt, x_sample, state_delta, cache_k, cache_v, state_gla, c, c_ctx, norm_g, ada_w, ada_b,
           ffn_w_gu, ffn_w_down, even_w_in, even_conv, even_a_log, even_dt_bias, even_onorm, even_sink,
           even_w_out, odd_w_in, odd_w_gate, odd_gate_bias, odd_onorm, odd_w_out, final_g):
    n_p, t_p, d = x_prompt.shape
    n_s, t_s, _ = x_sample.shape
    depth = norm_g.shape[0]
    np_rows, ns_rows = n_p * t_p, n_s * t_s
    tm = ROW_TILE
    assert np_rows % tm == 0 and t_s % tm == 0 and np_rows % t_s == 0
    assert t_p % CHUNK == 0 and t_s % CHUNK == 0 and t_s % GRID_W == 0
    cond = _cond_index(np_rows, t_s, tm)
    fwd = _Tiles(n_p, t_p, n_s, t_s, CHUNK, reverse=False)
    bwd = _Tiles(n_p, t_p, n_s, t_s, CHUNK, reverse=True)

    x = jnp.concatenate([x_prompt.reshape(np_rows, d), x_sample.reshape(ns_rows, d)], axis=0)
    n_cond = 1 + n_s
    cond_rows = -(-n_cond // SUBLANES) * SUBLANES
    conds = jnp.concatenate([c_ctx[None, :], c, jnp.zeros((cond_rows - n_cond, d), F32)], axis=0)
    mods = _ada(conds, ada_w, ada_b)[:, :n_cond].reshape(depth, n_cond, N_MOD, d)

    w_gu = ffn_w_gu.astype(BF16)
    w_down = ffn_w_down.astype(BF16)

    new_delta, new_k, new_v, new_gla = [], [], [], []
    for l in range(depth):
        j = l // 2
        mod = mods[l]
        x = _ffn_half(x, mod, 0, norm_g[l, 0], w_gu[l, 0], w_down[l, 0], cond, tm)
        if l % 2 == 0:
            nh = A_HEADS
            w = even_w_in[j]
            o_qkv = 2 * nh * A_DK + nh * A_DV
            o_gate = o_qkv + nh * A_DV
            o_ba = o_gate + 4 * nh
            o_qb = o_ba + B_Q_HEADS * HEAD_DIM
            o_kb = o_qb + B_KV_HEADS * HEAD_DIM
            w_cat = jnp.concatenate([w[:, :o_gate], w[:, o_ba:], w[:, o_gate:o_ba],
                                     jnp.zeros((d, LANES - 4 * nh), F32)], axis=1).astype(BF16)
            widths = (o_qkv, nh * A_DV, B_Q_HEADS * HEAD_DIM, B_KV_HEADS * HEAD_DIM, B_KV_HEADS * HEAD_DIM, LANES)
            assert o_kb + B_KV_HEADS * HEAD_DIM == w.shape[1]
            qkv, gate, qb, kb, vb, ba = _mixer_in(x, mod, norm_g[l, 1], w_cat, widths, cond, tm)
            bat = jnp.transpose(ba[:, :4 * nh])
            qkv_n = _conv_qkv(qkv, even_conv[j], fwd)
            o_dirs, states = [], []
            for z, tiles in ((0, fwd), (1, bwd)):
                o_z, st = _delta_dir(qkv_n, ba, bat, even_a_log[j], even_dt_bias[j], state_delta[:, j, z], z, tiles)
                o_dirs.append(o_z)
                states.append(st[:n_p])
            cos, sin_signed = _rope_tables(t_s)
            att = _ctx_attention(qb, kb, vb, even_sink[j], n_p, t_p)
            att = _lat_attention(qb, kb, vb,
                                 cache_k[:, j].reshape(n_s, -1, B_KV_HEADS * HEAD_DIM),
                                 cache_v[:, j].reshape(n_s, -1, B_KV_HEADS * HEAD_DIM),
                                 even_sink[j], cos, sin_signed, att, np_rows, n_s, t_s)
            x = _mixer_out(x, mod, o_dirs[0], o_dirs[1], gate, att, even_onorm[j], even_w_out[j].astype(BF16),
                           nh, A_DV, cond, tm)
            new_delta.append(jnp.stack(states, axis=1))
            new_k.append(kb[:np_rows].reshape(n_p, t_p, B_KV_HEADS, HEAD_DIM))
            new_v.append(vb[:np_rows].reshape(n_p, t_p, B_KV_HEADS, HEAD_DIM))
        else:
            nh = C_HEADS
            w = odd_w_in[j]
            w_cat = jnp.concatenate([w, jnp.zeros((d, LANES - 2 * GATE_RANK), F32)], axis=1).astype(BF16)
            widths = (nh * C_DK, nh * C_DK, nh * C_DV, nh * C_DV, LANES)
            assert sum(widths) == w_cat.shape[1]
            q, k, v, g_out, lr = _mixer_in(x, mod, norm_g[l, 1], w_cat, widths, cond, tm)
            o_dirs, states = [], []
            for z, tiles in ((0, fwd), (1, bwd)):
                o_z, st = _gla_dir(q, k, v, lr, odd_w_gate[j], odd_gate_bias[j], state_gla[:, j, z], z, tiles)
                o_dirs.append(o_z)
                states.append(st[:n_p])
            x = _mixer_out(x, mod, o_dirs[0], o_dirs[1], g_out, None, odd_onorm[j], odd_w_out[j].astype(BF16),
                           nh, C_DV, cond, tm)
            new_gla.append(jnp.stack(states, axis=1))
        x = _ffn_half(x, mod, 6, norm_g[l, 2], w_gu[l, 1], w_down[l, 1], cond, tm)

    y_prompt = _final_norm(x, final_g, 0, np_rows, tm).reshape(n_p, t_p, d)
    y_sample = _final_norm(x, final_g, np_rows, ns_rows, tm).reshape(n_s, t_s, d)
    return (y_prompt, y_sample, jnp.stack(new_delta, axis=1), jnp.stack(new_k, axis=1),
            jnp.stack(new_v, axis=1), jnp.stack(new_gla, axis=1))
```

```python
import functools

import jax
import jax.numpy as jnp
from jax import lax
from jax.experimental import pallas as pl
from jax.experimental.pallas import tpu as pltpu

F32 = jnp.float32
BF16 = jnp.bfloat16

EPS = 1e-6
N_MOD = 9
GRID_W = 64
HEAD_DIM = 128
A_HEADS = 4
A_DK = 128
A_DV = 128
SHORT_CONV = 5
B_Q_HEADS = 4
B_KV_HEADS = 2
WINDOW = 128
C_HEADS = 4
C_DK = 128
C_DV = 256
GATE_RANK = 16
GATE_NORM = 16.0
ROPE_THETA = 10000.0

LANES = 128
SUBLANES = 8
ROW_TILE = 512
CHUNK = 256
GLA_ROW_BLOCK = 64
VMEM_LIMIT = 56 * 1024 * 1024


def _cparams(sem, vmem=VMEM_LIMIT):
    return pltpu.CompilerParams(dimension_semantics=sem, vmem_limit_bytes=vmem)


def _resident(block_shape, index_map):
    return pl.BlockSpec(block_shape, index_map, pipeline_mode=pl.Buffered(1))


def _dot(a, b):
    return jnp.dot(a, b, preferred_element_type=F32)


def _dot_nt(a, b):
    return lax.dot_general(a, b, (((1,), (1,)), ((), ())), preferred_element_type=F32)


def _dot_tn(a, b):
    return lax.dot_general(a, b, (((0,), (0,)), ((), ())), preferred_element_type=F32)


def _bf(x):
    return x.astype(BF16)


def _sigmoid(x):
    return 1.0 / (1.0 + jnp.exp(-x))


def _silu(x):
    return x * _sigmoid(x)


def _softplus(x):
    return jnp.maximum(x, 0.0) + jnp.log1p(jnp.exp(-jnp.abs(x)))


def _rms(x):
    return x * lax.rsqrt(jnp.mean(x * x, axis=-1, keepdims=True) + EPS)


def _modnorm(x, g, shift, scale):
    return (_rms(x) * g) * (1.0 + scale) + shift


def _split3(x):
    hi = _bf(x)
    r1 = x - hi.astype(F32)
    mid = _bf(r1)
    lo = _bf(r1 - mid.astype(F32))
    return hi, mid, lo


def _cumsum_rows(tri_bf, x):
    hi, mid, lo = _split3(x)
    return (_dot(tri_bf, hi) + _dot(tri_bf, mid)) + _dot(tri_bf, lo)


def _cumsum_cols(x, trit_bf):
    hi, mid, lo = _split3(x)
    return (_dot(hi, trit_bf) + _dot(mid, trit_bf)) + _dot(lo, trit_bf)


def _order_masks(c, reverse):
    ri = lax.broadcasted_iota(jnp.int32, (c, c), 0)
    ci = lax.broadcasted_iota(jnp.int32, (c, c), 1)
    if reverse:
        return ri <= ci, ri < ci, ri >= ci
    return ri >= ci, ri > ci, ri <= ci


class _Tiles:
    def __init__(self, n_p, t_p, n_s, t_s, c, reverse=False):
        self.n_p, self.n_s, self.c, self.reverse = n_p, n_s, c, reverse
        self.per_p, self.per_s = t_p // c, t_s // c
        self.np_tiles = n_p * self.per_p
        self.n = self.np_tiles + n_s * self.per_s

    def row_block(self, i):
        return self.n - 1 - i if self.reverse else i

    def is_ctx(self, r):
        return r < self.np_tiles

    def seq(self, r):
        return jnp.where(r < self.np_tiles, r // self.per_p, self.n_p + (r - self.np_tiles) // self.per_s)

    def pos(self, r):
        return jnp.where(r < self.np_tiles, r % self.per_p, (r - self.np_tiles) % self.per_s)

    def length(self, r):
        return jnp.where(r < self.np_tiles, self.per_p, self.per_s)

    def first_in_scan(self, r):
        return self.pos(r) == (self.length(r) - 1 if self.reverse else 0)


def _ada_kernel(c_ref, w_ref, b_ref, o_ref):
    s = _bf(_silu(c_ref[...]))
    o_ref[0] = _dot(s, _bf(w_ref[0])) + b_ref[0]


def _ada(cond, ada_w, ada_b):
    depth, d, n = ada_w.shape
    rows = cond.shape[0]
    tn = n // 4
    return pl.pallas_call(
        _ada_kernel,
        out_shape=jax.ShapeDtypeStruct((depth, rows, n), F32),
        grid=(depth, n // tn),
        in_specs=[pl.BlockSpec((rows, d), lambda l, j: (0, 0)),
                  pl.BlockSpec((1, d, tn), lambda l, j: (l, 0, j)),
                  pl.BlockSpec((1, 1, tn), lambda l, j: (l, 0, j))],
        out_specs=pl.BlockSpec((1, rows, tn), lambda l, j: (l, 0, j)),
        compiler_params=_cparams(("parallel", "parallel")),
        name="ada",
    )(cond, ada_w, ada_b.reshape(depth, 1, n))


def _cond_index(n_prompt_rows, dec_seq, tm):
    npt = n_prompt_rows // tm

    def cond(i):
        return jnp.where(i < npt, 0, 1 + ((i - npt) * tm) // dec_seq)

    return cond


def _ffn_kernel(x_ref, mod_ref, g_ref, wgu_ref, wd_ref, o_ref, *, i0, d_ff, n_chunks):
    x = x_ref[...]
    mod = mod_ref[0]
    h = _bf(_modnorm(x, g_ref[...], mod[i0:i0 + 1], mod[i0 + 1:i0 + 2]))
    ch = d_ff // n_chunks
    y = None
    for c in range(n_chunks):
        gt = _dot(h, wgu_ref[:, c * ch:(c + 1) * ch])
        up = _dot(h, wgu_ref[:, d_ff + c * ch:d_ff + (c + 1) * ch])
        part = _dot(_bf(_silu(gt) * up), wd_ref[c * ch:(c + 1) * ch, :])
        y = part if y is None else y + part
    o_ref[...] = x + (0.5 * mod[i0 + 2:i0 + 3]) * y


def _ffn_half(x, mod, i0, g, w_gu, w_down, cond, tm):
    nt, d = x.shape
    d_ff = w_down.shape[0]
    kern = functools.partial(_ffn_kernel, i0=i0, d_ff=d_ff, n_chunks=2)
    return pl.pallas_call(
        kern,
        out_shape=jax.ShapeDtypeStruct((nt, d), F32),
        grid=(nt // tm,),
        in_specs=[pl.BlockSpec((tm, d), lambda i: (i, 0)),
                  pl.BlockSpec((1, N_MOD, d), lambda i: (cond(i), 0, 0)),
                  pl.BlockSpec((1, d), lambda i: (0, 0)),
                  _resident((d, 2 * d_ff), lambda i: (0, 0)),
                  _resident((d_ff, d), lambda i: (0, 0))],
        out_specs=pl.BlockSpec((tm, d), lambda i: (i, 0)),
        compiler_params=_cparams(("parallel",)),
        name="ffn_half",
    )(x, mod, g.reshape(1, d), w_gu, w_down)


def _proj_kernel(x_ref, mod_ref, g_ref, w_ref, *o_refs, i0, widths):
    mod = mod_ref[0]
    h = _bf(_modnorm(x_ref[...], g_ref[...], mod[i0:i0 + 1], mod[i0 + 1:i0 + 2]))
    off = 0
    for o_ref, wd in zip(o_refs, widths):
        o_ref[...] = _dot(h, w_ref[:, off:off + wd])
        off += wd


def _mixer_in(x, mod, g, w, widths, cond, tm):
    nt, d = x.shape
    n = sum(widths)
    kern = functools.partial(_proj_kernel, i0=3, widths=tuple(widths))
    return pl.pallas_call(
        kern,
        out_shape=[jax.ShapeDtypeStruct((nt, wd), F32) for wd in widths],
        grid=(nt // tm,),
        in_specs=[pl.BlockSpec((tm, d), lambda i: (i, 0)),
                  pl.BlockSpec((1, N_MOD, d), lambda i: (cond(i), 0, 0)),
                  pl.BlockSpec((1, d), lambda i: (0, 0)),
                  _resident((d, n), lambda i: (0, 0))],
        out_specs=[pl.BlockSpec((tm, wd), lambda i: (i, 0)) for wd in widths],
        compiler_params=_cparams(("parallel",)),
        name="mixer_in",
    )(x, mod, g.reshape(1, d), w)


def _mixer_out_kernel(x_ref, mod_ref, of_ref, ob_ref, gate_ref, on_ref, w_ref, *rest, nh, dv):
    o_ref = rest[-1]
    mod = mod_ref[0]
    od = of_ref[...] + ob_ref[...]
    gate = gate_ref[...]
    y = None
    if len(rest) == 2:
        y = _dot(_bf(rest[0][...]), w_ref[nh * dv:, :])
    for h in range(nh):
        sl = slice(h * dv, (h + 1) * dv)
        oa = (_rms(od[:, sl]) * on_ref[...]) * _silu(gate[:, sl])
        part = _dot(_bf(oa), w_ref[sl, :])
        y = part if y is None else y + part
    o_ref[...] = x_ref[...] + mod[5:6] * y


def _mixer_out(x, mod, o_f, o_b, gate, extra, onorm, w_out, nh, dv, cond, tm):
    nt, d = x.shape
    wa = nh * dv
    kern = functools.partial(_mixer_out_kernel, nh=nh, dv=dv)
    row = lambda i: (i, 0)
    in_specs = [pl.BlockSpec((tm, d), row),
                pl.BlockSpec((1, N_MOD, d), lambda i: (cond(i), 0, 0)),
                pl.BlockSpec((tm, wa), row), pl.BlockSpec((tm, wa), row), pl.BlockSpec((tm, wa), row),
                pl.BlockSpec((1, dv), lambda i: (0, 0)),
                _resident(w_out.shape, lambda i: (0, 0))]
    args = [x, mod, o_f, o_b, gate, onorm.reshape(1, dv), w_out]
    if extra is not None:
        in_specs.append(pl.BlockSpec((tm, extra.shape[1]), row))
        args.append(extra)
    return pl.pallas_call(
        kern,
        out_shape=jax.ShapeDtypeStruct((nt, d), F32),
        grid=(nt // tm,),
        in_specs=in_specs,
        out_specs=pl.BlockSpec((tm, d), row),
        compiler_params=_cparams(("parallel",)),
        name="mixer_out",
    )(*args)


def _final_norm_kernel(x_ref, g_ref, o_ref):
    o_ref[...] = _rms(x_ref[...]) * g_ref[...]


def _final_norm(x, g, row0, rows, tm):
    d = x.shape[1]
    base = row0 // tm
    return pl.pallas_call(
        _final_norm_kernel,
        out_shape=jax.ShapeDtypeStruct((rows, d), F32),
        grid=(rows // tm,),
        in_specs=[pl.BlockSpec((tm, d), lambda i: (base + i, 0)),
                  pl.BlockSpec((1, d), lambda i: (0, 0))],
        out_specs=pl.BlockSpec((tm, d), lambda i: (i, 0)),
        compiler_params=_cparams(("parallel",)),
        name="final_norm",
    )(x, g.reshape(1, d))


def _conv_kernel(prev_ref, x_ref, next_ref, w_ref, o_ref, *, tiles, dk):
    r = pl.program_id(0)
    part = pl.program_id(1)
    c, width = x_ref.shape
    pad = (SHORT_CONV - 1) // 2
    has_prev = (tiles.pos(r) > 0).astype(F32)
    has_next = (tiles.pos(r) < tiles.length(r) - 1).astype(F32)
    ext = c + 2 * SUBLANES
    for hh in range(width // LANES):
        sl = slice(hh * LANES, (hh + 1) * LANES)
        xe = jnp.concatenate([prev_ref[:, sl] * has_prev, x_ref[:, sl], next_ref[:, sl] * has_next], axis=0)
        w = w_ref[:, sl]
        acc = None
        for j in range(SHORT_CONV):
            sh = pad - j
            xs = xe if sh == 0 else pltpu.roll(xe, sh % ext, axis=0)
            term = xs[SUBLANES:SUBLANES + c] * w[j:j + 1]
            acc = term if acc is None else acc + term
        y = _silu(acc)
        nrm = lax.rsqrt(jnp.sum(y * y, axis=-1, keepdims=True) + EPS)
        scale = jnp.where(part == 0, nrm * (dk ** -0.5), jnp.where(part == 1, nrm, 1.0))
        o_ref[:, sl] = y * scale


def _conv_qkv(qkv, conv_w, tiles):
    nt, width = qkv.shape
    c = tiles.c
    pw = width // 3
    per = c // SUBLANES
    n8 = nt // SUBLANES
    kern = functools.partial(_conv_kernel, tiles=tiles, dk=A_DK)
    return pl.pallas_call(
        kern,
        out_shape=jax.ShapeDtypeStruct((nt, width), F32),
        grid=(tiles.n, 3),
        in_specs=[pl.BlockSpec((SUBLANES, pw), lambda r, p: (jnp.maximum(r * per - 1, 0), p)),
                  pl.BlockSpec((c, pw), lambda r, p: (r, p)),
                  pl.BlockSpec((SUBLANES, pw), lambda r, p: (jnp.minimum((r + 1) * per, n8 - 1), p)),
                  pl.BlockSpec((SHORT_CONV, pw), lambda r, p: (0, p))],
        out_specs=pl.BlockSpec((c, pw), lambda r, p: (r, p)),
        compiler_params=_cparams(("parallel", "parallel")),
        name="conv_qkv",
    )(qkv, qkv, qkv, conv_w)


def _unit_tri_inverses(ls, ri, ci):
    c = ls[0].shape[0]
    shift = SUBLANES.bit_length() - 1
    same = (ri >> shift) == (ci >> shift)
    ms = [jnp.where(same, -l, 0.0) for l in ls]
    ms_bf = [_bf(m) for m in ms]
    ps = [_dot(mb, mb) for mb in ms_bf]
    ns = [m + p + _dot(mb, _bf(p)) for m, mb, p in zip(ms, ms_bf, ps)]
    ps = [_dot(pb, pb) for pb in [_bf(p) for p in ps]]
    ns = [n + p + _dot(_bf(n), _bf(p)) for n, p in zip(ns, ps)]
    while (1 << shift) < c:
        lvl = ((ri >> (shift + 1)) == (ci >> (shift + 1))) & ((ri >> shift) != (ci >> shift))
        cls = [jnp.where(lvl, l, 0.0) for l in ls]
        ns_bf = [_bf(n) for n in ns]
        ys = [cl + _dot(_bf(cl), nb) for cl, nb in zip(cls, ns_bf)]
        ns = [n - (y + _dot(nb, _bf(y))) for n, nb, y in zip(ns, ns_bf, ys)]
        shift += 1
    return ns


def _init_state(s_ref, s0_ref, tiles, r):
    first = tiles.first_in_scan(r)

    @pl.when(first & tiles.is_ctx(r))
    def _():
        s_ref[...] = jnp.zeros(s_ref.shape, F32)

    @pl.when(first & jnp.logical_not(tiles.is_ctx(r)))
    def _():
        s_ref[...] = s0_ref[...]


def _delta_kernel(*refs, dirs, nh, dk, dv):
    nd = len(dirs)
    ins = [refs[6 * d:6 * d + 6] for d in range(nd)]
    alc_ref, dtc_ref, alr_ref, dtr_ref = refs[6 * nd:6 * nd + 4]
    outs = [refs[6 * nd + 4 + 2 * d:6 * nd + 6 + 2 * d] for d in range(nd)]
    step = pl.program_id(0)
    c = ins[0][0].shape[0]
    ri = lax.broadcasted_iota(jnp.int32, (c, c), 0)
    ci = lax.broadcasted_iota(jnp.int32, (c, c), 1)
    units = []
    for (z, tiles), (q_ref, k_ref, v_ref, ba_ref, bat_ref, s0_ref), (o_ref, s_ref) in zip(dirs, ins, outs):
        reverse = tiles.reverse
        _init_state(s_ref, s0_ref, tiles, tiles.row_block(step))
        incl, strict, incl_t = _order_masks(c, reverse)
        ba = ba_ref[...]
        g_col = -jnp.exp(alc_ref[...]) * _softplus(ba + dtc_ref[...])
        beta_col = _sigmoid(ba)
        gc_col = _cumsum_rows(_bf(incl.astype(F32)), g_col)
        g_row = -jnp.exp(alr_ref[...]) * _softplus(bat_ref[...] + dtr_ref[...])
        gc_row = _cumsum_cols(g_row, _bf(incl_t.astype(F32)))
        last = 0 if reverse else c - 1
        for h in range(nh):
            cb = z * nh + h
            cg = 2 * nh + cb
            gcc = gc_col[:, cg:cg + 1]
            gcr = gc_row[cg:cg + 1, :]
            units.append(dict(
                h=h, o_ref=o_ref, s_ref=s_ref, incl=incl, strict=strict, gcc=gcc, gl=gcc[last:last + 1],
                beta=beta_col[:, cb:cb + 1], egc=jnp.exp(gcc),
                decay=jnp.where(incl, jnp.exp(jnp.where(incl, gcc - gcr, 0.0)), 0.0),
                q=q_ref[:, h * dk:(h + 1) * dk], k=k_ref[:, h * dk:(h + 1) * dk], v=v_ref[:, h * dv:(h + 1) * dv]))
    for u in units:
        u["kb"] = u["k"] * u["beta"]
        u["k_bf"] = _bf(u["k"])
    kks = [_dot_nt(_bf(u["kb"]), u["k_bf"]) for u in units]
    qks = [_dot_nt(_bf(u["q"]), u["k_bf"]) * u["decay"] for u in units]
    n_invs = _unit_tri_inverses([jnp.where(u["strict"], kk * u["decay"], 0.0) for u, kk in zip(units, kks)], ri, ci)
    rs = [jnp.concatenate([u["v"] * u["beta"], u["kb"] * u["egc"]], axis=1) for u in units]
    rs = [r + _dot(_bf(n), _bf(r)) for r, n in zip(rs, n_invs)]
    ss = [u["s_ref"][0, u["h"]] for u in units]
    ss_bf = [_bf(s) for s in ss]
    v_news = [r[:, :dv] - _dot(_bf(r[:, dv:]), sb) for r, sb in zip(rs, ss_bf)]
    v_news_bf = [_bf(vn) for vn in v_news]
    for u, sb, qk, vnb in zip(units, ss_bf, qks, v_news_bf):
        h = u["h"]
        u["o_ref"][:, h * dv:(h + 1) * dv] = _dot(_bf(u["q"] * u["egc"]), sb) + _dot(_bf(qk), vnb)
    for u, s, vnb in zip(units, ss, v_news_bf):
        kd = u["k"] * jnp.exp(u["gl"] - u["gcc"])
        u["s_ref"][0, u["h"]] = s * jnp.exp(u["gl"]) + _dot_tn(_bf(kd), vnb)


def _delta_rule(qkv, ba, bat, a_log, dt_bias, s0, dirs):
    nt = qkv.shape[0]
    nh, dk, dv = A_HEADS, A_DK, A_DV
    c = dirs[0][1].c
    n_tiles = dirs[0][1].n
    n_seq = dirs[0][1].n_p + dirs[0][1].n_s
    pad = LANES - 4 * nh
    al = jnp.concatenate([jnp.zeros((2 * nh,), F32), a_log.reshape(-1), jnp.zeros((pad,), F32)])
    dt = jnp.concatenate([jnp.zeros((2 * nh,), F32), dt_bias.reshape(-1), jnp.zeros((pad,), F32)])
    nr = bat.shape[0]
    wq = nh * dk
    in_specs, args, out_specs, out_shape = [], [], [], []
    for z, tiles in dirs:
        rb = tiles.row_block
        latent_seq = functools.partial(lambda t, i: jnp.maximum(t.seq(t.row_block(i)) - t.n_p, 0), tiles)
        in_specs += [pl.BlockSpec((c, wq), functools.partial(lambda f, i: (f(i), 0), rb)),
                     pl.BlockSpec((c, wq), functools.partial(lambda f, i: (f(i), 1), rb)),
                     pl.BlockSpec((c, nh * dv), functools.partial(lambda f, i: (f(i), 2), rb)),
                     pl.BlockSpec((c, LANES), functools.partial(lambda f, i: (f(i), 0), rb)),
                     pl.BlockSpec((nr, c), functools.partial(lambda f, i: (0, f(i)), rb)),
                     pl.BlockSpec((1, None, nh, dk, dv),
                                  functools.partial(lambda f, zz, i: (f(i), zz, 0, 0, 0), latent_seq, z))]
        args += [qkv, qkv, qkv, ba, bat, s0]
        out_specs += [pl.BlockSpec((c, nh * dv), functools.partial(lambda f, i: (f(i), 0), rb)),
                      pl.BlockSpec((1, nh, dk, dv),
                                   functools.partial(lambda t, i: (t.seq(t.row_block(i)), 0, 0, 0), tiles))]
        out_shape += [jax.ShapeDtypeStruct((nt, nh * dv), F32), jax.ShapeDtypeStruct((n_seq, nh, dk, dv), F32)]
    in_specs += [pl.BlockSpec((1, LANES), lambda i: (0, 0)), pl.BlockSpec((1, LANES), lambda i: (0, 0)),
                 pl.BlockSpec((nr, 1), lambda i: (0, 0)), pl.BlockSpec((nr, 1), lambda i: (0, 0))]
    args += [al.reshape(1, LANES), dt.reshape(1, LANES), al[:nr].reshape(nr, 1), dt[:nr].reshape(nr, 1)]
    kern = functools.partial(_delta_kernel, dirs=tuple(dirs), nh=nh, dk=dk, dv=dv)
    return pl.pallas_call(
        kern,
        out_shape=out_shape,
        grid=(n_tiles,),
        in_specs=in_specs,
        out_specs=out_specs,
        compiler_params=_cparams(("arbitrary",)),
        name="delta_rule",
    )(*args)


def _ctx_attn_kernel(sink_ref, q_ref, k_ref, v_ref, o_ref, *, n_kv, grp, dh):
    scale = dh ** -0.5
    for hk in range(n_kv):
        k = _bf(k_ref[:, hk * dh:(hk + 1) * dh])
        v = _bf(v_ref[:, hk * dh:(hk + 1) * dh])
        for g in range(grp):
            h = hk * grp + g
            sk = sink_ref[h]
            s = _dot_nt(_bf(q_ref[:, h * dh:(h + 1) * dh] * scale), k)
            m = jnp.maximum(jnp.max(s, axis=-1, keepdims=True), sk)
            p = jnp.exp(s - m)
            den = jnp.sum(p, axis=-1, keepdims=True) + jnp.exp(sk - m)
            o_ref[:, h * dh:(h + 1) * dh] = _dot(_bf(p), v) / den


def _ctx_attention(qb, kb, vb, sink, n_seq, t):
    nt, wq = qb.shape
    wk = kb.shape[1]
    kern = functools.partial(_ctx_attn_kernel, n_kv=B_KV_HEADS, grp=B_Q_HEADS // B_KV_HEADS, dh=HEAD_DIM)
    return pl.pallas_call(
        kern,
        out_shape=jax.ShapeDtypeStruct((nt, wq), F32),
        grid=(n_seq,),
        in_specs=[pl.BlockSpec(memory_space=pltpu.SMEM),
                  pl.BlockSpec((t, wq), lambda b: (b, 0)),
                  pl.BlockSpec((t, wk), lambda b: (b, 0)),
                  pl.BlockSpec((t, wk), lambda b: (b, 0))],
        out_specs=pl.BlockSpec((t, wq), lambda b: (b, 0)),
        compiler_params=_cparams(("parallel",)),
        name="ctx_attention",
    )(sink, qb, kb, vb)


def _rope(x, cos, sin_signed):
    lane = lax.broadcasted_iota(jnp.int32, x.shape, 1)
    quarter = HEAD_DIM // 4
    partner = jnp.where((lane % (2 * quarter)) < quarter,
                        pltpu.roll(x, HEAD_DIM - quarter, axis=1), pltpu.roll(x, quarter, axis=1))
    return x * cos + partner * sin_signed


def _lat_attn_kernel(sink_ref, q_ref, k_ref, v_ref, kc_ref, vc_ref, cq_ref, sq_ref, ck_ref, sk_ref, prev_ref,
                     o_ref, *, n_kv, grp, dh, window):
    del prev_ref
    scale = dh ** -0.5
    tq = q_ref.shape[0]
    t = k_ref.shape[0]
    qpos = pl.program_id(1) * tq + lax.broadcasted_iota(jnp.int32, (tq, t), 0)
    kpos = lax.broadcasted_iota(jnp.int32, (tq, t), 1)
    valid = jnp.abs(qpos - kpos) <= window
    cq, sq = cq_ref[...], sq_ref[...]
    ck, sk_t = ck_ref[...], sk_ref[...]
    for hk in range(n_kv):
        hs = slice(hk * dh, (hk + 1) * dh)
        k_rot = _bf(_rope(k_ref[:, hs], ck, sk_t))
        v = _bf(v_ref[:, hs])
        kc = _bf(kc_ref[0, :, hs])
        vc = _bf(vc_ref[0, :, hs])
        for g in range(grp):
            h = hk * grp + g
            sink = sink_ref[h]
            q = q_ref[:, h * dh:(h + 1) * dh] * scale
            s_loc = jnp.where(valid, _dot_nt(_bf(_rope(q, cq, sq)), k_rot), -jnp.inf)
            s_ctx = _dot_nt(_bf(q), kc)
            m = jnp.maximum(jnp.maximum(jnp.max(s_loc, axis=-1, keepdims=True),
                                        jnp.max(s_ctx, axis=-1, keepdims=True)), sink)
            p_loc = jnp.exp(s_loc - m)
            p_ctx = jnp.exp(s_ctx - m)
            den = (jnp.sum(p_loc, axis=-1, keepdims=True) + jnp.sum(p_ctx, axis=-1, keepdims=True)
                   + jnp.exp(sink - m))
            o_ref[:, h * dh:(h + 1) * dh] = (_dot(_bf(p_loc), v) + _dot(_bf(p_ctx), vc)) / den


def _lat_attention(qb, kb, vb, k_ctx, v_ctx, sink, cos, sin_signed, out, row0, n_seq, t):
    nt, wq = qb.shape
    wk = kb.shape[1]
    tq = CHUNK
    nq = t // tq
    base_q = row0 // tq
    base_t = row0 // t
    past = k_ctx.shape[1]
    kern = functools.partial(_lat_attn_kernel, n_kv=B_KV_HEADS, grp=B_Q_HEADS // B_KV_HEADS, dh=HEAD_DIM,
                             window=WINDOW)
    return pl.pallas_call(
        kern,
        out_shape=jax.ShapeDtypeStruct((nt, wq), F32),
        grid=(n_seq, nq),
        in_specs=[pl.BlockSpec(memory_space=pltpu.SMEM),
                  pl.BlockSpec((tq, wq), lambda b, i: (base_q + b * nq + i, 0)),
                  pl.BlockSpec((t, wk), lambda b, i: (base_t + b, 0)),
                  pl.BlockSpec((t, wk), lambda b, i: (base_t + b, 0)),
                  pl.BlockSpec((1, past, wk), lambda b, i: (b, 0, 0)),
                  pl.BlockSpec((1, past, wk), lambda b, i: (b, 0, 0)),
                  pl.BlockSpec((tq, HEAD_DIM), lambda b, i: (i, 0)),
                  pl.BlockSpec((tq, HEAD_DIM), lambda b, i: (i, 0)),
                  pl.BlockSpec((t, HEAD_DIM), lambda b, i: (0, 0)),
                  pl.BlockSpec((t, HEAD_DIM), lambda b, i: (0, 0)),
                  pl.BlockSpec(memory_space=pl.ANY)],
        out_specs=pl.BlockSpec((tq, wq), lambda b, i: (base_q + b * nq + i, 0)),
        input_output_aliases={10: 0},
        compiler_params=_cparams(("parallel", "parallel")),
        name="latent_attention",
    )(sink, qb, kb, vb, k_ctx, v_ctx, cos, sin_signed, cos, sin_signed, out)


def _rope_tables(t):
    half = HEAD_DIM // 2
    quarter = half // 2
    pos = jnp.arange(t)
    row = (pos // GRID_W).astype(F32)
    col = (pos % GRID_W).astype(F32)
    inv = ROPE_THETA ** (-jnp.arange(quarter, dtype=F32) / quarter)
    ang_r = row[:, None] * inv[None, :]
    ang_c = col[:, None] * inv[None, :]
    cos = jnp.concatenate([jnp.cos(ang_r), jnp.cos(ang_r), jnp.cos(ang_c), jnp.cos(ang_c)], axis=-1)
    sin = jnp.concatenate([-jnp.sin(ang_r), jnp.sin(ang_r), -jnp.sin(ang_c), jnp.sin(ang_c)], axis=-1)
    return cos, sin


def _gla_kernel(q_ref, k_ref, v_ref, lr_ref, wg_ref, bias_ref, s0_ref, o_ref, s_ref, *, tiles, nh, dk, dv, rb):
    reverse = tiles.reverse
    _init_state(s_ref, s0_ref, tiles, tiles.row_block(pl.program_id(0)))
    c = q_ref.shape[0]
    incl, _, _ = _order_masks(c, reverse)
    tri = _bf(incl.astype(F32))
    x = _dot(_bf(lr_ref[...]), _bf(wg_ref[...])) + bias_ref[...]
    gk = (jnp.minimum(x, 0.0) - jnp.log1p(jnp.exp(-jnp.abs(x)))) * (1.0 / GATE_NORM)
    gcum = _cumsum_rows(tri, gk)
    eye = (lax.broadcasted_iota(jnp.int32, (dk, dk), 0) == lax.broadcasted_iota(jnp.int32, (dk, dk), 1))
    last = 0 if reverse else c - 1
    zero_row = jnp.zeros((1, dk), F32)
    for h in range(nh):
        g = gcum[:, h * dk:(h + 1) * dk]
        gl_row = g[last:last + 1]
        q = q_ref[:, h * dk:(h + 1) * dk] * (dk ** -0.5)
        k = k_ref[:, h * dk:(h + 1) * dk]
        v_bf = _bf(v_ref[:, h * dv:(h + 1) * dv])
        s = s_ref[0, h]
        blocks = []
        for blk in range(c // rb):
            r0, r1 = blk * rb, (blk + 1) * rb
            if reverse:
                c0, c1 = r0, c
                ref = g[r1:r1 + 1] if r1 < c else zero_row
            else:
                c0, c1 = 0, r1
                ref = g[r0 - 1:r0] if r0 > 0 else zero_row
            qe = q[r0:r1] * jnp.exp(g[r0:r1] - ref)
            ke = k[c0:c1] * jnp.exp(ref - g[c0:c1])
            a = jnp.where(incl[r0:r1, c0:c1], _dot_nt(_bf(qe), _bf(ke)), 0.0)
            blocks.append(_dot(_bf(a), v_bf[c0:c1]))
        o_ref[:, h * dv:(h + 1) * dv] = _dot(_bf(q * jnp.exp(g)), _bf(s)) + jnp.concatenate(blocks, axis=0)
        kd = k * jnp.exp(gl_row - g)
        gl_col = jnp.sum(jnp.where(eye, jnp.broadcast_to(gl_row, (dk, dk)), 0.0), axis=1, keepdims=True)
        s_ref[0, h] = s * jnp.exp(gl_col) + _dot_tn(_bf(kd), v_bf)


def _gla_dir(q, k, v, lr, w_gate, gate_bias, s0, z, tiles):
    nt = q.shape[0]
    nh, dk, dv = C_HEADS, C_DK, C_DV
    c = tiles.c
    n_seq = tiles.n_p + tiles.n_s
    rb = tiles.row_block
    wg = jnp.zeros((LANES, nh * dk), F32).at[z * GATE_RANK:(z + 1) * GATE_RANK].set(w_gate[z])
    kern = functools.partial(_gla_kernel, tiles=tiles, nh=nh, dk=dk, dv=dv, rb=GLA_ROW_BLOCK)
    return pl.pallas_call(
        kern,
        out_shape=[jax.ShapeDtypeStruct((nt, nh * dv), F32),
                   jax.ShapeDtypeStruct((n_seq, nh, dk, dv), F32)],
        grid=(tiles.n,),
        in_specs=[pl.BlockSpec((c, nh * dk), lambda i: (rb(i), 0)),
                  pl.BlockSpec((c, nh * dk), lambda i: (rb(i), 0)),
                  pl.BlockSpec((c, nh * dv), lambda i: (rb(i), 0)),
                  pl.BlockSpec((c, LANES), lambda i: (rb(i), 0)),
                  pl.BlockSpec((LANES, nh * dk), lambda i: (0, 0)),
                  pl.BlockSpec((1, nh * dk), lambda i: (0, 0)),
                  pl.BlockSpec((1, nh, dk, dv), lambda i: (jnp.maximum(tiles.seq(rb(i)) - tiles.n_p, 0), 0, 0, 0))],
        out_specs=[pl.BlockSpec((c, nh * dv), lambda i: (rb(i), 0)),
                   pl.BlockSpec((1, nh, dk, dv), lambda i: (tiles.seq(rb(i)), 0, 0, 0))],
        compiler_params=_cparams(("arbitrary",)),
        name="gla",
    )(q, k, v, lr, wg, gate_bias[z].reshape(1, nh * dk), s0)


def kernel(x_prompt, x_sample, state_delta, cache_k, cache_v, state_gla, c, c_ctx, norm_g, ada_w, ada_b,
           ffn_w_gu, ffn_w_down, even_w_in, even_conv, even_a_log, even_dt_bias, even_onorm, even_sink,
           even_w_out, odd_w_in, odd_w_gate, odd_gate_bias, odd_onorm, odd_w_out, final_g):
    n_p, t_p, d = x_prompt.shape
    n_s, t_s, _ = x_sample.shape
    depth = norm_g.shape[0]
    np_rows, ns_rows = n_p * t_p, n_s * t_s
    tm = ROW_TILE
    assert np_rows % tm == 0 and t_s % tm == 0 and np_rows % t_s == 0
    assert t_p % CHUNK == 0 and t_s % CHUNK == 0 and t_s % GRID_W == 0
    cond = _cond_index(np_rows, t_s, tm)
    fwd = _Tiles(n_p, t_p, n_s, t_s, CHUNK, reverse=False)
    bwd = _Tiles(n_p, t_p, n_s, t_s, CHUNK, reverse=True)

    x = jnp.concatenate([x_prompt.reshape(np_rows, d), x_sample.reshape(ns_rows, d)], axis=0)
    n_cond = 1 + n_s
    cond_rows = -(-n_cond // SUBLANES) * SUBLANES
    conds = jnp.concatenate([c_ctx[None, :], c, jnp.zeros((cond_rows - n_cond, d), F32)], axis=0)
    mods = _ada(conds, ada_w, ada_b)[:, :n_cond].reshape(depth, n_cond, N_MOD, d)

    w_gu = ffn_w_gu.astype(BF16)
    w_down = ffn_w_down.astype(BF16)

    new_delta, new_k, new_v, new_gla = [], [], [], []
    for l in range(depth):
        j = l // 2
        mod = mods[l]
        x = _ffn_half(x, mod, 0, norm_g[l, 0], w_gu[l, 0], w_down[l, 0], cond, tm)
        if l % 2 == 0:
            nh = A_HEADS
            w = even_w_in[j]
            o_qkv = 2 * nh * A_DK + nh * A_DV
            o_gate = o_qkv + nh * A_DV
            o_ba = o_gate + 4 * nh
            o_qb = o_ba + B_Q_HEADS * HEAD_DIM
            o_kb = o_qb + B_KV_HEADS * HEAD_DIM
            w_cat = jnp.concatenate([w[:, :o_gate], w[:, o_ba:], w[:, o_gate:o_ba],
                                     jnp.zeros((d, LANES - 4 * nh), F32)], axis=1).astype(BF16)
            widths = (o_qkv, nh * A_DV, B_Q_HEADS * HEAD_DIM, B_KV_HEADS * HEAD_DIM, B_KV_HEADS * HEAD_DIM, LANES)
            assert o_kb + B_KV_HEADS * HEAD_DIM == w.shape[1]
            qkv, gate, qb, kb, vb, ba = _mixer_in(x, mod, norm_g[l, 1], w_cat, widths, cond, tm)
            bat = jnp.transpose(ba[:, :4 * nh])
            qkv_n = _conv_qkv(qkv, even_conv[j], fwd)
            o_f, st_f, o_b, st_b = _delta_rule(qkv_n, ba, bat, even_a_log[j], even_dt_bias[j], state_delta[:, j],
                                               [(0, fwd), (1, bwd)])
            o_dirs, states = [o_f, o_b], [st_f[:n_p], st_b[:n_p]]
            cos, sin_signed = _rope_tables(t_s)
            att = _ctx_attention(qb, kb, vb, even_sink[j], n_p, t_p)
            att = _lat_attention(qb, kb, vb,
                                 cache_k[:, j].reshape(n_s, -1, B_KV_HEADS * HEAD_DIM),
                                 cache_v[:, j].reshape(n_s, -1, B_KV_HEADS * HEAD_DIM),
                                 even_sink[j], cos, sin_signed, att, np_rows, n_s, t_s)
            x = _mixer_out(x, mod, o_dirs[0], o_dirs[1], gate, att, even_onorm[j], even_w_out[j].astype(BF16),
                           nh, A_DV, cond, tm)
            new_delta.append(jnp.stack(states, axis=1))
            new_k.append(kb[:np_rows].reshape(n_p, t_p, B_KV_HEADS, HEAD_DIM))
            new_v.append(vb[:np_rows].reshape(n_p, t_p, B_KV_HEADS, HEAD_DIM))
        else:
            nh = C_HEADS
            w = odd_w_in[j]
            w_cat = jnp.concatenate([w, jnp.zeros((d, LANES - 2 * GATE_RANK), F32)], axis=1).astype(BF16)
            widths = (nh * C_DK, nh * C_DK, nh * C_DV, nh * C_DV, LANES)
            assert sum(widths) == w_cat.shape[1]
            q, k, v, g_out, lr = _mixer_in(x, mod, norm_g[l, 1], w_cat, widths, cond, tm)
            o_dirs, states = [], []
            for z, tiles in ((0, fwd), (1, bwd)):
                o_z, st = _gla_dir(q, k, v, lr, odd_w_gate[j], odd_gate_bias[j], state_gla[:, j, z], z, tiles)
                o_dirs.append(o_z)
                states.append(st[:n_p])
            x = _mixer_out(x, mod, o_dirs[0], o_dirs[1], g_out, None, odd_onorm[j], odd_w_out[j].astype(BF16),
                           nh, C_DV, cond, tm)
            new_gla.append(jnp.stack(states, axis=1))
        x = _ffn_half(x, mod, 6, norm_g[l, 2], w_gu[l, 1], w_down[l, 1], cond, tm)

    y_prompt = _final_norm(x, final_g, 0, np_rows, tm).reshape(n_p, t_p, d)
    y_sample = _final_norm(x, final_g, np_rows, ns_rows, tm).reshape(n_s, t_s, d)
    return (y_prompt, y_sample, jnp.stack(new_delta, axis=1), jnp.stack(new_k, axis=1),
            jnp.stack(new_v, axis=1), jnp.stack(new_gla, axis=1))
```

```python
import functools

import jax
import jax.numpy as jnp
from jax import lax
from jax.experimental import pallas as pl
from jax.experimental.pallas import tpu as pltpu

F32 = jnp.float32
BF16 = jnp.bfloat16

EPS = 1e-6
N_MOD = 9
GRID_W = 64
HEAD_DIM = 128
A_HEADS = 4
A_DK = 128
A_DV = 128
SHORT_CONV = 5
B_Q_HEADS = 4
B_KV_HEADS = 2
WINDOW = 128
C_HEADS = 4
C_DK = 128
C_DV = 256
GATE_RANK = 16
GATE_NORM = 16.0
ROPE_THETA = 10000.0

LANES = 128
SUBLANES = 8
ROW_TILE = 512
CHUNK = 256
GLA_ROW_BLOCK = 64
VMEM_LIMIT = 56 * 1024 * 1024


def _cparams(sem, vmem=VMEM_LIMIT):
    return pltpu.CompilerParams(dimension_semantics=sem, vmem_limit_bytes=vmem)


def _resident(block_shape, index_map):
    return pl.BlockSpec(block_shape, index_map, pipeline_mode=pl.Buffered(1))


def _dot(a, b):
    return jnp.dot(a, b, preferred_element_type=F32)


def _dot_nt(a, b):
    return lax.dot_general(a, b, (((1,), (1,)), ((), ())), preferred_element_type=F32)


def _dot_tn(a, b):
    return lax.dot_general(a, b, (((0,), (0,)), ((), ())), preferred_element_type=F32)


def _bf(x):
    return x.astype(BF16)


def _sigmoid(x):
    return 1.0 / (1.0 + jnp.exp(-x))


def _silu(x):
    return x * _sigmoid(x)


def _softplus(x):
    return jnp.maximum(x, 0.0) + jnp.log1p(jnp.exp(-jnp.abs(x)))


def _rms(x):
    return x * lax.rsqrt(jnp.mean(x * x, axis=-1, keepdims=True) + EPS)


def _modnorm(x, g, shift, scale):
    return (_rms(x) * g) * (1.0 + scale) + shift


def _split3(x):
    hi = _bf(x)
    r1 = x - hi.astype(F32)
    mid = _bf(r1)
    lo = _bf(r1 - mid.astype(F32))
    return hi, mid, lo


def _cumsum_rows(tri_bf, x):
    hi, mid, lo = _split3(x)
    return (_dot(tri_bf, hi) + _dot(tri_bf, mid)) + _dot(tri_bf, lo)


def _cumsum_cols(x, trit_bf):
    hi, mid, lo = _split3(x)
    return (_dot(hi, trit_bf) + _dot(mid, trit_bf)) + _dot(lo, trit_bf)


def _order_masks(ri, ci, reverse):
    if reverse:
        return ri <= ci, ri < ci, ri >= ci
    return ri >= ci, ri > ci, ri <= ci


class _Tiles:
    def __init__(self, n_p, t_p, n_s, t_s, c):
        self.n_p, self.n_s, self.c = n_p, n_s, c
        self.per_p, self.per_s = t_p // c, t_s // c
        self.np_tiles = n_p * self.per_p
        self.n = self.np_tiles + n_s * self.per_s

    def is_ctx(self, i):
        return i < self.np_tiles

    def seq(self, i):
        return jnp.where(i < self.np_tiles, i // self.per_p, self.n_p + (i - self.np_tiles) // self.per_s)

    def pos(self, i):
        return jnp.where(i < self.np_tiles, i % self.per_p, (i - self.np_tiles) % self.per_s)

    def length(self, i):
        return jnp.where(i < self.np_tiles, self.per_p, self.per_s)

    def row_block(self, i, reverse):
        return i + self.length(i) - 1 - 2 * self.pos(i) if reverse else i

    def ctx_seq(self, i):
        return jnp.minimum(self.seq(i), self.n_p - 1)

    def latent_seq(self, i):
        return jnp.maximum(self.seq(i) - self.n_p, 0)


def _ada_kernel(c_ref, w_ref, b_ref, o_ref):
    s = _bf(_silu(c_ref[...]))
    o_ref[0] = _dot(s, _bf(w_ref[0])) + b_ref[0]


def _ada(cond, ada_w, ada_b):
    depth, d, n = ada_w.shape
    rows = cond.shape[0]
    tn = n // 4
    return pl.pallas_call(
        _ada_kernel,
        out_shape=jax.ShapeDtypeStruct((depth, rows, n), F32),
        grid=(depth, n // tn),
        in_specs=[pl.BlockSpec((rows, d), lambda l, j: (0, 0)),
                  pl.BlockSpec((1, d, tn), lambda l, j: (l, 0, j)),
                  pl.BlockSpec((1, 1, tn), lambda l, j: (l, 0, j))],
        out_specs=pl.BlockSpec((1, rows, tn), lambda l, j: (l, 0, j)),
        compiler_params=_cparams(("parallel", "parallel")),
        name="ada",
    )(cond, ada_w, ada_b.reshape(depth, 1, n))


def _cond_index(n_prompt_rows, dec_seq, tm):
    npt = n_prompt_rows // tm

    def cond(i):
        return jnp.where(i < npt, 0, 1 + ((i - npt) * tm) // dec_seq)

    return cond


def _ffn_kernel(*refs, i0, d_ff, n_chunks, npt, split_in, final):
    n_x = 2 if split_in else 1
    x_refs, (mod_ref, g_ref, wgu_ref, wd_ref) = refs[:n_x], refs[n_x:n_x + 4]
    rest = refs[n_x + 4:]
    i = pl.program_id(0)
    if split_in:
        x = jnp.where(i < npt, x_refs[0][...], x_refs[1][...])
    else:
        x = x_refs[0][...]
    mod = mod_ref[0]
    h = _bf(_modnorm(x, g_ref[...], mod[i0:i0 + 1], mod[i0 + 1:i0 + 2]))
    ch = d_ff // n_chunks
    y = None
    for c in range(n_chunks):
        gt = _dot(h, wgu_ref[:, c * ch:(c + 1) * ch])
        up = _dot(h, wgu_ref[:, d_ff + c * ch:d_ff + (c + 1) * ch])
        part = _dot(_bf(_silu(gt) * up), wd_ref[c * ch:(c + 1) * ch, :])
        y = part if y is None else y + part
    out = x + (0.5 * mod[i0 + 2:i0 + 3]) * y
    if final:
        fg_ref, yp_ref, ys_ref = rest
        out = _rms(out) * fg_ref[...]

        @pl.when(i < npt)
        def _():
            yp_ref[...] = out

        @pl.when(i >= npt)
        def _():
            ys_ref[...] = out
    else:
        rest[0][...] = out


def _ffn_half(xs, mod, i0, g, w_gu, w_down, widx, cond, tm, np_rows, final_g=None):
    split_in = len(xs) == 2
    d = xs[0].shape[1]
    nt = sum(x.shape[0] for x in xs)
    d_ff = w_down.shape[-2]
    npt = np_rows // tm
    final = final_g is not None
    kern = functools.partial(_ffn_kernel, i0=i0, d_ff=d_ff, n_chunks=2, npt=npt, split_in=split_in, final=final)
    ctx_map = lambda i: (jnp.minimum(i, npt - 1), 0)
    lat_map = lambda i: (jnp.maximum(i - npt, 0), 0)
    if split_in:
        x_specs = [pl.BlockSpec((tm, d), ctx_map), pl.BlockSpec((tm, d), lat_map)]
    else:
        x_specs = [pl.BlockSpec((tm, d), lambda i: (i, 0))]
    in_specs = x_specs + [pl.BlockSpec((1, N_MOD, d), lambda i: (cond(i), 0, 0)),
                          pl.BlockSpec((1, d), lambda i: (0, 0)),
                          _resident((None, None, d, 2 * d_ff), lambda i: widx + (0, 0)),
                          _resident((None, None, d_ff, d), lambda i: widx + (0, 0))]
    args = list(xs) + [mod, g.reshape(1, d), w_gu, w_down]
    if final:
        in_specs.append(pl.BlockSpec((1, d), lambda i: (0, 0)))
        args.append(final_g.reshape(1, d))
        out_shape = [jax.ShapeDtypeStruct((np_rows, d), F32), jax.ShapeDtypeStruct((nt - np_rows, d), F32)]
        out_specs = [pl.BlockSpec((tm, d), ctx_map), pl.BlockSpec((tm, d), lat_map)]
    else:
        out_shape = jax.ShapeDtypeStruct((nt, d), F32)
        out_specs = pl.BlockSpec((tm, d), lambda i: (i, 0))
    return pl.pallas_call(
        kern,
        out_shape=out_shape,
        grid=(nt // tm,),
        in_specs=in_specs,
        out_specs=out_specs,
        compiler_params=_cparams(("arbitrary",)),
        name="ffn_half",
    )(*args)


def _proj_kernel(x_ref, mod_ref, g_ref, w_ref, *o_refs, i0, widths):
    mod = mod_ref[0]
    h = _bf(_modnorm(x_ref[...], g_ref[...], mod[i0:i0 + 1], mod[i0 + 1:i0 + 2]))
    off = 0
    for o_ref, wd in zip(o_refs, widths):
        o_ref[...] = _dot(h, w_ref[:, off:off + wd])
        off += wd


def _mixer_in(x, mod, g, w, widths, cond, tm):
    nt, d = x.shape
    n = sum(widths)
    kern = functools.partial(_proj_kernel, i0=3, widths=tuple(widths))
    return pl.pallas_call(
        kern,
        out_shape=[jax.ShapeDtypeStruct((nt, wd), F32) for wd in widths],
        grid=(nt // tm,),
        in_specs=[pl.BlockSpec((tm, d), lambda i: (i, 0)),
                  pl.BlockSpec((1, N_MOD, d), lambda i: (cond(i), 0, 0)),
                  pl.BlockSpec((1, d), lambda i: (0, 0)),
                  _resident((d, n), lambda i: (0, 0))],
        out_specs=[pl.BlockSpec((tm, wd), lambda i: (i, 0)) for wd in widths],
        compiler_params=_cparams(("parallel",)),
        name="mixer_in",
    )(x, mod, g.reshape(1, d), w)


def _mixer_out_kernel(x_ref, mod_ref, of_ref, ob_ref, gate_ref, on_ref, w_ref, *rest, nh, dv):
    o_ref = rest[-1]
    mod = mod_ref[0]
    od = of_ref[...] + ob_ref[...]
    gate = gate_ref[...]
    y = None
    if len(rest) == 2:
        y = _dot(_bf(rest[0][...]), w_ref[nh * dv:, :])
    for h in range(nh):
        sl = slice(h * dv, (h + 1) * dv)
        oa = (_rms(od[:, sl]) * on_ref[...]) * _silu(gate[:, sl])
        part = _dot(_bf(oa), w_ref[sl, :])
        y = part if y is None else y + part
    o_ref[...] = x_ref[...] + mod[5:6] * y


def _mixer_out(x, mod, o_f, o_b, gate, extra, onorm, w_out, nh, dv, cond, tm):
    nt, d = x.shape
    wa = nh * dv
    kern = functools.partial(_mixer_out_kernel, nh=nh, dv=dv)
    row = lambda i: (i, 0)
    in_specs = [pl.BlockSpec((tm, d), row),
                pl.BlockSpec((1, N_MOD, d), lambda i: (cond(i), 0, 0)),
                pl.BlockSpec((tm, wa), row), pl.BlockSpec((tm, wa), row), pl.BlockSpec((tm, wa), row),
                pl.BlockSpec((1, dv), lambda i: (0, 0)),
                _resident(w_out.shape, lambda i: (0, 0))]
    args = [x, mod, o_f, o_b, gate, onorm.reshape(1, dv), w_out]
    if extra is not None:
        in_specs.append(pl.BlockSpec((tm, extra.shape[1]), row))
        args.append(extra)
    return pl.pallas_call(
        kern,
        out_shape=jax.ShapeDtypeStruct((nt, d), F32),
        grid=(nt // tm,),
        in_specs=in_specs,
        out_specs=pl.BlockSpec((tm, d), row),
        compiler_params=_cparams(("parallel",)),
        name="mixer_out",
    )(*args)


def _conv_kernel(prev_ref, x_ref, next_ref, w_ref, o_ref, *, tiles, dk):
    r = pl.program_id(0)
    part = pl.program_id(1)
    c, width = x_ref.shape
    pad = (SHORT_CONV - 1) // 2
    has_prev = jnp.where(tiles.pos(r) > 0, 1.0, 0.0)
    has_next = jnp.where(tiles.pos(r) < tiles.length(r) - 1, 1.0, 0.0)
    ext = c + 2 * SUBLANES
    for hh in range(width // LANES):
        sl = slice(hh * LANES, (hh + 1) * LANES)
        xe = jnp.concatenate([prev_ref[:, sl] * has_prev, x_ref[:, sl], next_ref[:, sl] * has_next], axis=0)
        w = w_ref[:, sl]
        acc = None
        for j in range(SHORT_CONV):
            sh = pad - j
            xs = xe if sh == 0 else pltpu.roll(xe, sh % ext, axis=0)
            term = xs[SUBLANES:SUBLANES + c] * w[j:j + 1]
            acc = term if acc is None else acc + term
        y = _silu(acc)
        nrm = lax.rsqrt(jnp.sum(y * y, axis=-1, keepdims=True) + EPS)
        scale = jnp.where(part == 0, nrm * (dk ** -0.5), jnp.where(part == 1, nrm, 1.0))
        o_ref[:, sl] = y * scale


def _conv_qkv(qkv, conv_w, tiles):
    nt, width = qkv.shape
    c = tiles.c
    pw = width // 3
    per = c // SUBLANES
    n8 = nt // SUBLANES
    kern = functools.partial(_conv_kernel, tiles=tiles, dk=A_DK)
    return pl.pallas_call(
        kern,
        out_shape=jax.ShapeDtypeStruct((nt, width), F32),
        grid=(tiles.n, 3),
        in_specs=[pl.BlockSpec((SUBLANES, pw), lambda r, p: (jnp.maximum(r * per - 1, 0), p)),
                  pl.BlockSpec((c, pw), lambda r, p: (r, p)),
                  pl.BlockSpec((SUBLANES, pw), lambda r, p: (jnp.minimum((r + 1) * per, n8 - 1), p)),
                  pl.BlockSpec((SHORT_CONV, pw), lambda r, p: (0, p))],
        out_specs=pl.BlockSpec((c, pw), lambda r, p: (r, p)),
        compiler_params=_cparams(("parallel", "parallel")),
        name="conv_qkv",
    )(qkv, qkv, qkv, conv_w)


def _load_state(state, s0_ref, tiles, i):
    first = tiles.pos(i) == 0

    @pl.when(first & tiles.is_ctx(i))
    def _():
        state[...] = jnp.zeros(state.shape, F32)

    @pl.when(first & jnp.logical_not(tiles.is_ctx(i)))
    def _():
        state[...] = s0_ref[0]


def _store_state(state, out_ref, tiles, i):
    @pl.when((tiles.pos(i) == tiles.length(i) - 1) & tiles.is_ctx(i))
    def _():
        out_ref[0] = state[...]


def _dir_specs(tiles, c, cols):
    specs = []
    for reverse in (False, True):
        for width, col in cols:
            specs.append(pl.BlockSpec((c, width), functools.partial(
                lambda rev, cc, i: (tiles.row_block(i, rev), cc), reverse, col)))
    return specs


def _unit_tri_inverses(ls, ri, ci):
    c = ls[0].shape[0]
    shift = SUBLANES.bit_length() - 1
    same = (ri >> shift) == (ci >> shift)
    ms = [jnp.where(same, -l, 0.0) for l in ls]
    ms_bf = [_bf(m) for m in ms]
    ps = [_dot(mb, mb) for mb in ms_bf]
    ns = [m + p + _dot(mb, _bf(p)) for m, mb, p in zip(ms, ms_bf, ps)]
    ps = [_dot(pb, pb) for pb in [_bf(p) for p in ps]]
    ns = [n + p + _dot(_bf(n), _bf(p)) for n, p in zip(ns, ps)]
    while (1 << shift) < c:
        lvl = ((ri >> (shift + 1)) == (ci >> (shift + 1))) & ((ri >> shift) != (ci >> shift))
        cls = [jnp.where(lvl, l, 0.0) for l in ls]
        ns_bf = [_bf(n) for n in ns]
        ys = [cl + _dot(_bf(cl), nb) for cl, nb in zip(cls, ns_bf)]
        ns = [n - (y + _dot(nb, _bf(y))) for n, nb, y in zip(ns, ns_bf, ys)]
        shift += 1
    return ns


def _delta_kernel(qf_ref, kf_ref, vf_ref, baf_ref, qb_ref, kb_ref, vb_ref, bab_ref, batf_ref, batb_ref,
                  alc_ref, dtc_ref, alr_ref, dtr_ref, s0_ref, of_ref, ob_ref, sout_ref, state, *, tiles, nh, dk, dv):
    step = pl.program_id(0)
    _load_state(state, s0_ref, tiles, step)
    c = qf_ref.shape[0]
    ri = lax.broadcasted_iota(jnp.int32, (c, c), 0)
    ci = lax.broadcasted_iota(jnp.int32, (c, c), 1)
    units = []
    for z, (q_ref, k_ref, v_ref, ba_ref, bat_ref, o_ref) in enumerate(
            ((qf_ref, kf_ref, vf_ref, baf_ref, batf_ref, of_ref), (qb_ref, kb_ref, vb_ref, bab_ref, batb_ref, ob_ref))):
        reverse = z == 1
        incl, strict, incl_t = _order_masks(ri, ci, reverse)
        ba = ba_ref[...]
        g_col = -jnp.exp(alc_ref[...]) * _softplus(ba + dtc_ref[...])
        beta_col = _sigmoid(ba)
        gc_col = _cumsum_rows(_bf(incl.astype(F32)), g_col)
        g_row = -jnp.exp(alr_ref[...]) * _softplus(bat_ref[...] + dtr_ref[...])
        gc_row = _cumsum_cols(g_row, _bf(incl_t.astype(F32)))
        last = 0 if reverse else c - 1
        for h in range(nh):
            cb = z * nh + h
            cg = 2 * nh + cb
            gcc = gc_col[:, cg:cg + 1]
            gcr = gc_row[cg:cg + 1, :]
            units.append(dict(
                z=z, h=h, o_ref=o_ref, strict=strict, gcc=gcc, gl=gcc[last:last + 1],
                beta=beta_col[:, cb:cb + 1], egc=jnp.exp(gcc),
                decay=jnp.where(incl, jnp.exp(jnp.where(incl, gcc - gcr, 0.0)), 0.0),
                q=q_ref[:, h * dk:(h + 1) * dk], k=k_ref[:, h * dk:(h + 1) * dk], v=v_ref[:, h * dv:(h + 1) * dv]))
    for u in units:
        u["kb"] = u["k"] * u["beta"]
        u["k_bf"] = _bf(u["k"])
    kks = [_dot_nt(_bf(u["kb"]), u["k_bf"]) for u in units]
    qks = [_dot_nt(_bf(u["q"]), u["k_bf"]) * u["decay"] for u in units]
    n_invs = _unit_tri_inverses([jnp.where(u["strict"], kk * u["decay"], 0.0) for u, kk in zip(units, kks)], ri, ci)
    rs = [jnp.concatenate([u["v"] * u["beta"], u["kb"] * u["egc"]], axis=1) for u in units]
    rs = [r + _dot(_bf(n), _bf(r)) for r, n in zip(rs, n_invs)]
    ss = [state[u["z"], u["h"]] for u in units]
    ss_bf = [_bf(s) for s in ss]
    v_news_bf = [_bf(r[:, :dv] - _dot(_bf(r[:, dv:]), sb)) for r, sb in zip(rs, ss_bf)]
    for u, sb, qk, vnb in zip(units, ss_bf, qks, v_news_bf):
        h = u["h"]
        u["o_ref"][:, h * dv:(h + 1) * dv] = _dot(_bf(u["q"] * u["egc"]), sb) + _dot(_bf(qk), vnb)
    for u, s, vnb in zip(units, ss, v_news_bf):
        kd = u["k"] * jnp.exp(u["gl"] - u["gcc"])
        state[u["z"], u["h"]] = s * jnp.exp(u["gl"]) + _dot_tn(_bf(kd), vnb)
    _store_state(state, sout_ref, tiles, step)


def _delta_rule(qkv, ba, bat, a_log, dt_bias, s0, tiles):
    nt = qkv.shape[0]
    nh, dk, dv = A_HEADS, A_DK, A_DV
    c = tiles.c
    pad = LANES - 4 * nh
    al = jnp.concatenate([jnp.zeros((2 * nh,), F32), a_log.reshape(-1), jnp.zeros((pad,), F32)])
    dt = jnp.concatenate([jnp.zeros((2 * nh,), F32), dt_bias.reshape(-1), jnp.zeros((pad,), F32)])
    nr = bat.shape[0]
    const = lambda i: (0, 0)
    in_specs = (_dir_specs(tiles, c, ((nh * dk, 0), (nh * dk, 1), (nh * dv, 2), (LANES, 0)))
                + [pl.BlockSpec((nr, c), lambda i: (0, tiles.row_block(i, False))),
                   pl.BlockSpec((nr, c), lambda i: (0, tiles.row_block(i, True))),
                   pl.BlockSpec((1, LANES), const), pl.BlockSpec((1, LANES), const),
                   pl.BlockSpec((nr, 1), const), pl.BlockSpec((nr, 1), const),
                   pl.BlockSpec((1, 2, nh, dk, dv), lambda i: (tiles.latent_seq(i), 0, 0, 0, 0))])
    kern = functools.partial(_delta_kernel, tiles=tiles, nh=nh, dk=dk, dv=dv)
    return pl.pallas_call(
        kern,
        out_shape=[jax.ShapeDtypeStruct((nt, nh * dv), F32), jax.ShapeDtypeStruct((nt, nh * dv), F32),
                   jax.ShapeDtypeStruct((tiles.n_p, 2, nh, dk, dv), F32)],
        grid=(tiles.n,),
        in_specs=in_specs,
        out_specs=[pl.BlockSpec((c, nh * dv), lambda i: (tiles.row_block(i, False), 0)),
                   pl.BlockSpec((c, nh * dv), lambda i: (tiles.row_block(i, True), 0)),
                   pl.BlockSpec((1, 2, nh, dk, dv), lambda i: (tiles.ctx_seq(i), 0, 0, 0, 0))],
        scratch_shapes=[pltpu.VMEM((2, nh, dk, dv), F32)],
        compiler_params=_cparams(("arbitrary",)),
        name="delta_rule",
    )(qkv, qkv, qkv, ba, qkv, qkv, qkv, ba, bat, bat, al.reshape(1, LANES), dt.reshape(1, LANES),
      al[:nr].reshape(nr, 1), dt[:nr].reshape(nr, 1), s0)


def _ctx_attn_kernel(sink_ref, q_ref, k_ref, v_ref, o_ref, *, n_kv, grp, dh):
    scale = dh ** -0.5
    for hk in range(n_kv):
        k = _bf(k_ref[:, hk * dh:(hk + 1) * dh])
        v = _bf(v_ref[:, hk * dh:(hk + 1) * dh])
        for g in range(grp):
            h = hk * grp + g
            sk = sink_ref[h]
            s = _dot_nt(_bf(q_ref[:, h * dh:(h + 1) * dh] * scale), k)
            m = jnp.maximum(jnp.max(s, axis=-1, keepdims=True), sk)
            p = jnp.exp(s - m)
            den = jnp.sum(p, axis=-1, keepdims=True) + jnp.exp(sk - m)
            o_ref[:, h * dh:(h + 1) * dh] = _dot(_bf(p), v) / den


def _ctx_attention(qb, kb, vb, sink, n_seq, t):
    nt, wq = qb.shape
    wk = kb.shape[1]
    kern = functools.partial(_ctx_attn_kernel, n_kv=B_KV_HEADS, grp=B_Q_HEADS // B_KV_HEADS, dh=HEAD_DIM)
    return pl.pallas_call(
        kern,
        out_shape=jax.ShapeDtypeStruct((nt, wq), F32),
        grid=(n_seq,),
        in_specs=[pl.BlockSpec(memory_space=pltpu.SMEM),
                  pl.BlockSpec((t, wq), lambda b: (b, 0)),
                  pl.BlockSpec((t, wk), lambda b: (b, 0)),
                  pl.BlockSpec((t, wk), lambda b: (b, 0))],
        out_specs=pl.BlockSpec((t, wq), lambda b: (b, 0)),
        compiler_params=_cparams(("parallel",)),
        name="ctx_attention",
    )(sink, qb, kb, vb)


def _rope(x, cos, sin_signed):
    lane = lax.broadcasted_iota(jnp.int32, x.shape, 1)
    quarter = HEAD_DIM // 4
    partner = jnp.where((lane % (2 * quarter)) < quarter,
                        pltpu.roll(x, HEAD_DIM - quarter, axis=1), pltpu.roll(x, quarter, axis=1))
    return x * cos + partner * sin_signed


def _lat_attn_kernel(sink_ref, q_ref, k_ref, v_ref, kc_ref, vc_ref, cq_ref, sq_ref, ck_ref, sk_ref, prev_ref,
                     o_ref, *, n_kv, grp, dh, window):
    del prev_ref
    scale = dh ** -0.5
    tq = q_ref.shape[0]
    t = k_ref.shape[0]
    qpos = pl.program_id(1) * tq + lax.broadcasted_iota(jnp.int32, (tq, t), 0)
    kpos = lax.broadcasted_iota(jnp.int32, (tq, t), 1)
    valid = jnp.abs(qpos - kpos) <= window
    cq, sq = cq_ref[...], sq_ref[...]
    ck, sk_t = ck_ref[...], sk_ref[...]
    for hk in range(n_kv):
        hs = slice(hk * dh, (hk + 1) * dh)
        k_rot = _bf(_rope(k_ref[:, hs], ck, sk_t))
        v = _bf(v_ref[:, hs])
        kc = _bf(kc_ref[0, :, hs])
        vc = _bf(vc_ref[0, :, hs])
        for g in range(grp):
            h = hk * grp + g
            sink = sink_ref[h]
            q = q_ref[:, h * dh:(h + 1) * dh] * scale
            s_loc = jnp.where(valid, _dot_nt(_bf(_rope(q, cq, sq)), k_rot), -jnp.inf)
            s_ctx = _dot_nt(_bf(q), kc)
            m = jnp.maximum(jnp.maximum(jnp.max(s_loc, axis=-1, keepdims=True),
                                        jnp.max(s_ctx, axis=-1, keepdims=True)), sink)
            p_loc = jnp.exp(s_loc - m)
            p_ctx = jnp.exp(s_ctx - m)
            den = (jnp.sum(p_loc, axis=-1, keepdims=True) + jnp.sum(p_ctx, axis=-1, keepdims=True)
                   + jnp.exp(sink - m))
            o_ref[:, h * dh:(h + 1) * dh] = (_dot(_bf(p_loc), v) + _dot(_bf(p_ctx), vc)) / den


def _lat_attention(qb, kb, vb, k_ctx, v_ctx, sink, cos, sin_signed, out, row0, n_seq, t):
    nt, wq = qb.shape
    wk = kb.shape[1]
    tq = CHUNK
    nq = t // tq
    base_q = row0 // tq
    base_t = row0 // t
    past = k_ctx.shape[1]
    kern = functools.partial(_lat_attn_kernel, n_kv=B_KV_HEADS, grp=B_Q_HEADS // B_KV_HEADS, dh=HEAD_DIM,
                             window=WINDOW)
    return pl.pallas_call(
        kern,
        out_shape=jax.ShapeDtypeStruct((nt, wq), F32),
        grid=(n_seq, nq),
        in_specs=[pl.BlockSpec(memory_space=pltpu.SMEM),
                  pl.BlockSpec((tq, wq), lambda b, i: (base_q + b * nq + i, 0)),
                  pl.BlockSpec((t, wk), lambda b, i: (base_t + b, 0)),
                  pl.BlockSpec((t, wk), lambda b, i: (base_t + b, 0)),
                  pl.BlockSpec((1, past, wk), lambda b, i: (b, 0, 0)),
                  pl.BlockSpec((1, past, wk), lambda b, i: (b, 0, 0)),
                  pl.BlockSpec((tq, HEAD_DIM), lambda b, i: (i, 0)),
                  pl.BlockSpec((tq, HEAD_DIM), lambda b, i: (i, 0)),
                  pl.BlockSpec((t, HEAD_DIM), lambda b, i: (0, 0)),
                  pl.BlockSpec((t, HEAD_DIM), lambda b, i: (0, 0)),
                  pl.BlockSpec(memory_space=pl.ANY)],
        out_specs=pl.BlockSpec((tq, wq), lambda b, i: (base_q + b * nq + i, 0)),
        input_output_aliases={10: 0},
        compiler_params=_cparams(("parallel", "parallel")),
        name="latent_attention",
    )(sink, qb, kb, vb, k_ctx, v_ctx, cos, sin_signed, cos, sin_signed, out)


def _rope_tables(t):
    half = HEAD_DIM // 2
    quarter = half // 2
    pos = jnp.arange(t)
    row = (pos // GRID_W).astype(F32)
    col = (pos % GRID_W).astype(F32)
    inv = ROPE_THETA ** (-jnp.arange(quarter, dtype=F32) / quarter)
    ang_r = row[:, None] * inv[None, :]
    ang_c = col[:, None] * inv[None, :]
    cos = jnp.concatenate([jnp.cos(ang_r), jnp.cos(ang_r), jnp.cos(ang_c), jnp.cos(ang_c)], axis=-1)
    sin = jnp.concatenate([-jnp.sin(ang_r), jnp.sin(ang_r), -jnp.sin(ang_c), jnp.sin(ang_c)], axis=-1)
    return cos, sin


def _gla_kernel(qf_ref, kf_ref, vf_ref, lrf_ref, qb_ref, kb_ref, vb_ref, lrb_ref, wg_ref, bias_ref, s0_ref,
                of_ref, ob_ref, sout_ref, state, *, tiles, nh, dk, dv, rb):
    step = pl.program_id(0)
    _load_state(state, s0_ref, tiles, step)
    c = qf_ref.shape[0]
    ri = lax.broadcasted_iota(jnp.int32, (c, c), 0)
    ci = lax.broadcasted_iota(jnp.int32, (c, c), 1)
    eye = (lax.broadcasted_iota(jnp.int32, (dk, dk), 0) == lax.broadcasted_iota(jnp.int32, (dk, dk), 1))
    zero_row = jnp.zeros((1, dk), F32)
    units = []
    for z, (q_ref, k_ref, v_ref, lr_ref, o_ref) in enumerate(
            ((qf_ref, kf_ref, vf_ref, lrf_ref, of_ref), (qb_ref, kb_ref, vb_ref, lrb_ref, ob_ref))):
        reverse = z == 1
        incl, _, _ = _order_masks(ri, ci, reverse)
        x = _dot(_bf(lr_ref[...]), _bf(wg_ref[z])) + bias_ref[z]
        gk = (jnp.minimum(x, 0.0) - jnp.log1p(jnp.exp(-jnp.abs(x)))) * (1.0 / GATE_NORM)
        gcum = _cumsum_rows(_bf(incl.astype(F32)), gk)
        last = 0 if reverse else c - 1
        for h in range(nh):
            g = gcum[:, h * dk:(h + 1) * dk]
            units.append(dict(
                z=z, h=h, o_ref=o_ref, reverse=reverse, incl=incl, g=g, gl_row=g[last:last + 1],
                q=q_ref[:, h * dk:(h + 1) * dk] * (dk ** -0.5), k=k_ref[:, h * dk:(h + 1) * dk],
                v_bf=_bf(v_ref[:, h * dv:(h + 1) * dv]), s=state[z, h]))
    intra = [[] for _ in units]
    for blk in range(c // rb):
        r0, r1 = blk * rb, (blk + 1) * rb
        scores, cols = [], []
        for u in units:
            g = u["g"]
            if u["reverse"]:
                c0, c1 = r0, c
                ref = g[r1:r1 + 1] if r1 < c else zero_row
            else:
                c0, c1 = 0, r1
                ref = g[r0 - 1:r0] if r0 > 0 else zero_row
            qe = u["q"][r0:r1] * jnp.exp(g[r0:r1] - ref)
            ke = u["k"][c0:c1] * jnp.exp(ref - g[c0:c1])
            scores.append(jnp.where(u["incl"][r0:r1, c0:c1], _dot_nt(_bf(qe), _bf(ke)), 0.0))
            cols.append((c0, c1))
        for parts, u, a, (c0, c1) in zip(intra, units, scores, cols):
            parts.append(_dot(_bf(a), u["v_bf"][c0:c1]))
    inter = [_dot(_bf(u["q"] * jnp.exp(u["g"])), _bf(u["s"])) for u in units]
    for u, o_inter, parts in zip(units, inter, intra):
        h = u["h"]
        u["o_ref"][:, h * dv:(h + 1) * dv] = o_inter + jnp.concatenate(parts, axis=0)
    for u in units:
        kd = u["k"] * jnp.exp(u["gl_row"] - u["g"])
        gl_col = jnp.sum(jnp.where(eye, jnp.broadcast_to(u["gl_row"], (dk, dk)), 0.0), axis=1, keepdims=True)
        state[u["z"], u["h"]] = u["s"] * jnp.exp(gl_col) + _dot_tn(_bf(kd), u["v_bf"])
    _store_state(state, sout_ref, tiles, step)


def _gla(q, k, v, lr, w_gate, gate_bias, s0, tiles):
    nt = q.shape[0]
    nh, dk, dv = C_HEADS, C_DK, C_DV
    c = tiles.c
    wg = jnp.zeros((2, LANES, nh * dk), F32)
    for z in range(2):
        wg = wg.at[z, z * GATE_RANK:(z + 1) * GATE_RANK].set(w_gate[z])
    in_specs = (_dir_specs(tiles, c, ((nh * dk, 0), (nh * dk, 0), (nh * dv, 0), (LANES, 0)))
                + [pl.BlockSpec((2, LANES, nh * dk), lambda i: (0, 0, 0)),
                   pl.BlockSpec((2, 1, nh * dk), lambda i: (0, 0, 0)),
                   pl.BlockSpec((1, 2, nh, dk, dv), lambda i: (tiles.latent_seq(i), 0, 0, 0, 0))])
    kern = functools.partial(_gla_kernel, tiles=tiles, nh=nh, dk=dk, dv=dv, rb=GLA_ROW_BLOCK)
    return pl.pallas_call(
        kern,
        out_shape=[jax.ShapeDtypeStruct((nt, nh * dv), F32), jax.ShapeDtypeStruct((nt, nh * dv), F32),
                   jax.ShapeDtypeStruct((tiles.n_p, 2, nh, dk, dv), F32)],
        grid=(tiles.n,),
        in_specs=in_specs,
        out_specs=[pl.BlockSpec((c, nh * dv), lambda i: (tiles.row_block(i, False), 0)),
                   pl.BlockSpec((c, nh * dv), lambda i: (tiles.row_block(i, True), 0)),
                   pl.BlockSpec((1, 2, nh, dk, dv), lambda i: (tiles.ctx_seq(i), 0, 0, 0, 0))],
        scratch_shapes=[pltpu.VMEM((2, nh, dk, dv), F32)],
        compiler_params=_cparams(("arbitrary",)),
        name="gla",
    )(q, k, v, lr, q, k, v, lr, wg, gate_bias.reshape(2, 1, nh * dk), s0)


def kernel(x_prompt, x_sample, state_delta, cache_k, cache_v, state_gla, c, c_ctx, norm_g, ada_w, ada_b,
           ffn_w_gu, ffn_w_down, even_w_in, even_conv, even_a_log, even_dt_bias, even_onorm, even_sink,
           even_w_out, odd_w_in, odd_w_gate, odd_gate_bias, odd_onorm, odd_w_out, final_g):
    n_p, t_p, d = x_prompt.shape
    n_s, t_s, _ = x_sample.shape
    depth = norm_g.shape[0]
    np_rows, ns_rows = n_p * t_p, n_s * t_s
    tm = ROW_TILE
    assert np_rows % tm == 0 and t_s % tm == 0 and np_rows % t_s == 0
    assert t_p % CHUNK == 0 and t_s % CHUNK == 0 and t_s % GRID_W == 0
    cond = _cond_index(np_rows, t_s, tm)
    tiles = _Tiles(n_p, t_p, n_s, t_s, CHUNK)

    n_cond = 1 + n_s
    cond_rows = -(-n_cond // SUBLANES) * SUBLANES
    conds = jnp.concatenate([c_ctx[None, :], c, jnp.zeros((cond_rows - n_cond, d), F32)], axis=0)
    mods = _ada(conds, ada_w, ada_b)[:, :n_cond].reshape(depth, n_cond, N_MOD, d)

    w_gu = ffn_w_gu.astype(BF16)
    w_down = ffn_w_down.astype(BF16)

    xs = (x_prompt.reshape(np_rows, d), x_sample.reshape(ns_rows, d))
    new_delta, new_k, new_v, new_gla = [], [], [], []
    for l in range(depth):
        j = l // 2
        mod = mods[l]
        x = _ffn_half(xs, mod, 0, norm_g[l, 0], w_gu, w_down, (l, 0), cond, tm, np_rows)
        if l % 2 == 0:
            nh = A_HEADS
            w = even_w_in[j]
            o_qkv = 2 * nh * A_DK + nh * A_DV
            o_gate = o_qkv + nh * A_DV
            o_ba = o_gate + 4 * nh
            w_cat = jnp.concatenate([w[:, :o_gate], w[:, o_ba:], w[:, o_gate:o_ba],
                                     jnp.zeros((d, LANES - 4 * nh), F32)], axis=1).astype(BF16)
            widths = (o_qkv, nh * A_DV, B_Q_HEADS * HEAD_DIM, B_KV_HEADS * HEAD_DIM, B_KV_HEADS * HEAD_DIM, LANES)
            assert sum(widths) == w_cat.shape[1]
            qkv, gate, qb, kb, vb, ba = _mixer_in(x, mod, norm_g[l, 1], w_cat, widths, cond, tm)
            bat = jnp.transpose(ba[:, :4 * nh])
            qkv_n = _conv_qkv(qkv, even_conv[j], tiles)
            o_f, o_b, st = _delta_rule(qkv_n, ba, bat, even_a_log[j], even_dt_bias[j], state_delta[:, j], tiles)
            cos, sin_signed = _rope_tables(t_s)
            att = _ctx_attention(qb, kb, vb, even_sink[j], n_p, t_p)
            att = _lat_attention(qb, kb, vb,
                                 cache_k[:, j].reshape(n_s, -1, B_KV_HEADS * HEAD_DIM),
                                 cache_v[:, j].reshape(n_s, -1, B_KV_HEADS * HEAD_DIM),
                                 even_sink[j], cos, sin_signed, att, np_rows, n_s, t_s)
            x = _mixer_out(x, mod, o_f, o_b, gate, att, even_onorm[j], even_w_out[j].astype(BF16),
                           nh, A_DV, cond, tm)
            new_delta.append(st)
            new_k.append(kb[:np_rows].reshape(n_p, t_p, B_KV_HEADS, HEAD_DIM))
            new_v.append(vb[:np_rows].reshape(n_p, t_p, B_KV_HEADS, HEAD_DIM))
        else:
            nh = C_HEADS
            w_cat = jnp.concatenate([odd_w_in[j], jnp.zeros((d, LANES - 2 * GATE_RANK), F32)], axis=1).astype(BF16)
            widths = (nh * C_DK, nh * C_DK, nh * C_DV, nh * C_DV, LANES)
            assert sum(widths) == w_cat.shape[1]
            q, k, v, g_out, lr = _mixer_in(x, mod, norm_g[l, 1], w_cat, widths, cond, tm)
            o_f, o_b, st = _gla(q, k, v, lr, odd_w_gate[j], odd_gate_bias[j], state_gla[:, j], tiles)
            x = _mixer_out(x, mod, o_f, o_b, g_out, None, odd_onorm[j], odd_w_out[j].astype(BF16),
                           nh, C_DV, cond, tm)
            new_gla.append(st)
        last = l == depth - 1
        xs = _ffn_half((x,), mod, 6, norm_g[l, 2], w_gu, w_down, (l, 1), cond, tm, np_rows,
                       final_g=final_g if last else None)
        if not last:
            xs = (xs,)

    y_prompt, y_sample = xs
    return (y_prompt.reshape(n_p, t_p, d), y_sample.reshape(n_s, t_s, d), jnp.stack(new_delta, axis=1),
            jnp.stack(new_k, axis=1), jnp.stack(new_v, axis=1), jnp.stack(new_gla, axis=1))
```

```python
import functools

import jax
import jax.numpy as jnp
from jax import lax
from jax.experimental import pallas as pl
from jax.experimental.pallas import tpu as pltpu

F32 = jnp.float32
BF16 = jnp.bfloat16

EPS = 1e-6
N_MOD = 9
GRID_W = 64
HEAD_DIM = 128
A_HEADS = 4
A_DK = 128
A_DV = 128
SHORT_CONV = 5
B_Q_HEADS = 4
B_KV_HEADS = 2
WINDOW = 128
C_HEADS = 4
C_DK = 128
C_DV = 256
GATE_RANK = 16
GATE_NORM = 16.0
ROPE_THETA = 10000.0

LANES = 128
SUBLANES = 8
ROW_TILE = 512
FFN_ROW_TILE = 512
FFN_CHUNKS = 11
CHUNK = 256
GLA_ROW_BLOCK = 64
VMEM_LIMIT = 56 * 1024 * 1024


def _cparams(sem, vmem=VMEM_LIMIT):
    return pltpu.CompilerParams(dimension_semantics=sem, vmem_limit_bytes=vmem)


def _resident(block_shape, index_map):
    return pl.BlockSpec(block_shape, index_map, pipeline_mode=pl.Buffered(1))


def _dot(a, b):
    return jnp.dot(a, b, preferred_element_type=F32)


def _dot_nt(a, b):
    return lax.dot_general(a, b, (((1,), (1,)), ((), ())), preferred_element_type=F32)


def _dot_tn(a, b):
    return lax.dot_general(a, b, (((0,), (0,)), ((), ())), preferred_element_type=F32)


def _bf(x):
    return x.astype(BF16)


def _sigmoid(x):
    return 1.0 / (1.0 + jnp.exp(-x))


def _silu(x):
    return x * _sigmoid(x)


def _softplus(x):
    return jnp.maximum(x, 0.0) + jnp.log1p(jnp.exp(-jnp.abs(x)))


def _rms(x):
    return x * lax.rsqrt(jnp.mean(x * x, axis=-1, keepdims=True) + EPS)


def _modnorm(x, g, shift, scale):
    return (_rms(x) * g) * (1.0 + scale) + shift


def _split3(x):
    hi = _bf(x)
    r1 = x - hi.astype(F32)
    mid = _bf(r1)
    lo = _bf(r1 - mid.astype(F32))
    return hi, mid, lo


def _cumsum_rows(tri_bf, x):
    hi, mid, lo = _split3(x)
    return (_dot(tri_bf, hi) + _dot(tri_bf, mid)) + _dot(tri_bf, lo)


def _cumsum_cols(x, trit_bf):
    hi, mid, lo = _split3(x)
    return (_dot(hi, trit_bf) + _dot(mid, trit_bf)) + _dot(lo, trit_bf)


def _order_masks(ri, ci, reverse):
    if reverse:
        return ri <= ci, ri < ci, ri >= ci
    return ri >= ci, ri > ci, ri <= ci


class _Tiles:
    def __init__(self, n_p, t_p, n_s, t_s, c):
        self.n_p, self.n_s, self.c = n_p, n_s, c
        self.per_p, self.per_s = t_p // c, t_s // c
        self.np_tiles = n_p * self.per_p
        self.n = self.np_tiles + n_s * self.per_s

    def is_ctx(self, i):
        return i < self.np_tiles

    def seq(self, i):
        return jnp.where(i < self.np_tiles, i // self.per_p, self.n_p + (i - self.np_tiles) // self.per_s)

    def pos(self, i):
        return jnp.where(i < self.np_tiles, i % self.per_p, (i - self.np_tiles) % self.per_s)

    def length(self, i):
        return jnp.where(i < self.np_tiles, self.per_p, self.per_s)

    def row_block(self, i, reverse):
        return i + self.length(i) - 1 - 2 * self.pos(i) if reverse else i

    def ctx_seq(self, i):
        return jnp.minimum(self.seq(i), self.n_p - 1)

    def latent_seq(self, i):
        return jnp.maximum(self.seq(i) - self.n_p, 0)


def _ada_kernel(c_ref, w_ref, b_ref, o_ref):
    s = _bf(_silu(c_ref[...]))
    o_ref[0] = _dot(s, _bf(w_ref[0])) + b_ref[0]


def _ada(cond, ada_w, ada_b):
    depth, d, n = ada_w.shape
    rows = cond.shape[0]
    tn = n // 4
    return pl.pallas_call(
        _ada_kernel,
        out_shape=jax.ShapeDtypeStruct((depth, rows, n), F32),
        grid=(depth, n // tn),
        in_specs=[pl.BlockSpec((rows, d), lambda l, j: (0, 0)),
                  pl.BlockSpec((1, d, tn), lambda l, j: (l, 0, j)),
                  pl.BlockSpec((1, 1, tn), lambda l, j: (l, 0, j))],
        out_specs=pl.BlockSpec((1, rows, tn), lambda l, j: (l, 0, j)),
        compiler_params=_cparams(("parallel", "parallel")),
        name="ada",
    )(cond, ada_w, ada_b.reshape(depth, 1, n))


def _cond_index(n_prompt_rows, dec_seq, tm):
    npt = n_prompt_rows // tm

    def cond(i):
        return jnp.where(i < npt, 0, 1 + ((i - npt) * tm) // dec_seq)

    return cond


def _ffn_kernel(*refs, i0, d_ff, n_chunks, npt, split_in, final):
    n_x = 2 if split_in else 1
    x_refs, (mod_ref, g_ref, wgu_ref, wd_ref) = refs[:n_x], refs[n_x:n_x + 4]
    rest = refs[n_x + 4:]
    i = pl.program_id(0)
    if split_in:
        x = jnp.where(i < npt, x_refs[0][...], x_refs[1][...])
    else:
        x = x_refs[0][...]
    mod = mod_ref[0]
    h = _bf(_modnorm(x, g_ref[...], mod[i0:i0 + 1], mod[i0 + 1:i0 + 2]))
    ch = d_ff // n_chunks
    y = None
    for c in range(n_chunks):
        gt = _dot(h, wgu_ref[:, c * ch:(c + 1) * ch])
        up = _dot(h, wgu_ref[:, d_ff + c * ch:d_ff + (c + 1) * ch])
        part = _dot(_bf(_silu(gt) * up), wd_ref[c * ch:(c + 1) * ch, :])
        y = part if y is None else y + part
    out = x + (0.5 * mod[i0 + 2:i0 + 3]) * y
    if final:
        fg_ref, yp_ref, ys_ref = rest
        out = _rms(out) * fg_ref[...]

        @pl.when(i < npt)
        def _():
            yp_ref[...] = out

        @pl.when(i >= npt)
        def _():
            ys_ref[...] = out
    else:
        rest[0][...] = out


def _ffn_half(xs, mod, i0, g, w_gu, w_down, widx, rows, final_g=None):
    split_in = len(xs) == 2
    d = xs[0].shape[1]
    nt = sum(x.shape[0] for x in xs)
    d_ff = w_down.shape[-2]
    tm = FFN_ROW_TILE
    np_rows = rows[0]
    assert np_rows % tm == 0 and rows[1] % tm == 0 and d_ff % (FFN_CHUNKS * LANES) == 0
    cond = _cond_index(*rows, tm)
    npt = np_rows // tm
    final = final_g is not None
    kern = functools.partial(_ffn_kernel, i0=i0, d_ff=d_ff, n_chunks=FFN_CHUNKS, npt=npt, split_in=split_in,
                             final=final)
    ctx_map = lambda i: (jnp.minimum(i, npt - 1), 0)
    lat_map = lambda i: (jnp.maximum(i - npt, 0), 0)
    if split_in:
        x_specs = [pl.BlockSpec((tm, d), ctx_map), pl.BlockSpec((tm, d), lat_map)]
    else:
        x_specs = [pl.BlockSpec((tm, d), lambda i: (i, 0))]
    in_specs = x_specs + [pl.BlockSpec((1, N_MOD, d), lambda i: (cond(i), 0, 0)),
                          pl.BlockSpec((1, d), lambda i: (0, 0)),
                          _resident((None, None, d, 2 * d_ff), lambda i: widx + (0, 0)),
                          _resident((None, None, d_ff, d), lambda i: widx + (0, 0))]
    args = list(xs) + [mod, g.reshape(1, d), w_gu, w_down]
    if final:
        in_specs.append(pl.BlockSpec((1, d), lambda i: (0, 0)))
        args.append(final_g.reshape(1, d))
        out_shape = [jax.ShapeDtypeStruct((np_rows, d), F32), jax.ShapeDtypeStruct((nt - np_rows, d), F32)]
        out_specs = [pl.BlockSpec((tm, d), ctx_map), pl.BlockSpec((tm, d), lat_map)]
    else:
        out_shape = jax.ShapeDtypeStruct((nt, d), F32)
        out_specs = pl.BlockSpec((tm, d), lambda i: (i, 0))
    return pl.pallas_call(
        kern,
        out_shape=out_shape,
        grid=(nt // tm,),
        in_specs=in_specs,
        out_specs=out_specs,
        compiler_params=_cparams(("arbitrary",)),
        name="ffn_half",
    )(*args)


def _proj_kernel(x_ref, mod_ref, g_ref, w_ref, *o_refs, i0, widths):
    mod = mod_ref[0]
    h = _bf(_modnorm(x_ref[...], g_ref[...], mod[i0:i0 + 1], mod[i0 + 1:i0 + 2]))
    off = 0
    for o_ref, wd in zip(o_refs, widths):
        o_ref[...] = _dot(h, w_ref[:, off:off + wd])
        off += wd


def _mixer_in(x, mod, g, w, widths, rows):
    nt, d = x.shape
    n = sum(widths)
    tm = ROW_TILE
    cond = _cond_index(*rows, tm)
    kern = functools.partial(_proj_kernel, i0=3, widths=tuple(widths))
    return pl.pallas_call(
        kern,
        out_shape=[jax.ShapeDtypeStruct((nt, wd), F32) for wd in widths],
        grid=(nt // tm,),
        in_specs=[pl.BlockSpec((tm, d), lambda i: (i, 0)),
                  pl.BlockSpec((1, N_MOD, d), lambda i: (cond(i), 0, 0)),
                  pl.BlockSpec((1, d), lambda i: (0, 0)),
                  _resident((d, n), lambda i: (0, 0))],
        out_specs=[pl.BlockSpec((tm, wd), lambda i: (i, 0)) for wd in widths],
        compiler_params=_cparams(("parallel",)),
        name="mixer_in",
    )(x, mod, g.reshape(1, d), w)


def _mixer_out_kernel(x_ref, mod_ref, of_ref, ob_ref, gate_ref, on_ref, w_ref, *rest, nh, dv, npt):
    o_ref = rest[-1]
    mod = mod_ref[0]
    od = of_ref[...] + ob_ref[...]
    gate = gate_ref[...]
    y = None
    if len(rest) == 3:
        extra = jnp.where(pl.program_id(0) < npt, rest[0][...], rest[1][...])
        y = _dot(_bf(extra), w_ref[nh * dv:, :])
    for h in range(nh):
        sl = slice(h * dv, (h + 1) * dv)
        oa = (_rms(od[:, sl]) * on_ref[...]) * _silu(gate[:, sl])
        part = _dot(_bf(oa), w_ref[sl, :])
        y = part if y is None else y + part
    o_ref[...] = x_ref[...] + mod[5:6] * y


def _mixer_out(x, mod, o_f, o_b, gate, extra, onorm, w_out, nh, dv, rows):
    nt, d = x.shape
    wa = nh * dv
    tm = ROW_TILE
    cond = _cond_index(*rows, tm)
    npt = rows[0] // tm
    kern = functools.partial(_mixer_out_kernel, nh=nh, dv=dv, npt=npt)
    row = lambda i: (i, 0)
    in_specs = [pl.BlockSpec((tm, d), row),
                pl.BlockSpec((1, N_MOD, d), lambda i: (cond(i), 0, 0)),
                pl.BlockSpec((tm, wa), row), pl.BlockSpec((tm, wa), row), pl.BlockSpec((tm, wa), row),
                pl.BlockSpec((1, dv), lambda i: (0, 0)),
                _resident(w_out.shape, lambda i: (0, 0))]
    args = [x, mod, o_f, o_b, gate, onorm.reshape(1, dv), w_out]
    if extra is not None:
        we = extra[0].shape[1]
        in_specs += [pl.BlockSpec((tm, we), lambda i: (jnp.minimum(i, npt - 1), 0)),
                     pl.BlockSpec((tm, we), lambda i: (jnp.maximum(i - npt, 0), 0))]
        args += list(extra)
    return pl.pallas_call(
        kern,
        out_shape=jax.ShapeDtypeStruct((nt, d), F32),
        grid=(nt // tm,),
        in_specs=in_specs,
        out_specs=pl.BlockSpec((tm, d), row),
        compiler_params=_cparams(("arbitrary",)),
        name="mixer_out",
    )(*args)


def _conv_kernel(prev_ref, x_ref, next_ref, w_ref, o_ref, *, tiles, dk):
    r = pl.program_id(0)
    c, width = x_ref.shape
    pad = (SHORT_CONV - 1) // 2
    has_prev = jnp.where(tiles.pos(r) > 0, 1.0, 0.0)
    has_next = jnp.where(tiles.pos(r) < tiles.length(r) - 1, 1.0, 0.0)
    ext = c + 2 * SUBLANES
    for hh in range(width // dk):
        part = hh // (width // (3 * dk))
        sl = slice(hh * dk, (hh + 1) * dk)
        xe = jnp.concatenate([prev_ref[:, sl] * has_prev, x_ref[:, sl], next_ref[:, sl] * has_next], axis=0)
        w = w_ref[:, sl]
        acc = None
        for j in range(SHORT_CONV):
            sh = pad - j
            xs = xe if sh == 0 else pltpu.roll(xe, sh % ext, axis=0)
            term = xs[SUBLANES:SUBLANES + c] * w[j:j + 1]
            acc = term if acc is None else acc + term
        y = _silu(acc)
        if part < 2:
            nrm = lax.rsqrt(jnp.sum(y * y, axis=-1, keepdims=True) + EPS)
            y = y * (nrm * (dk ** -0.5) if part == 0 else nrm)
        o_ref[:, sl] = y


def _conv_qkv(qkv, conv_w, tiles):
    nt, width = qkv.shape
    c = tiles.c
    per = c // SUBLANES
    n8 = nt // SUBLANES
    kern = functools.partial(_conv_kernel, tiles=tiles, dk=A_DK)
    return pl.pallas_call(
        kern,
        out_shape=jax.ShapeDtypeStruct((nt, width), F32),
        grid=(tiles.n,),
        in_specs=[pl.BlockSpec((SUBLANES, width), lambda r: (jnp.maximum(r * per - 1, 0), 0)),
                  pl.BlockSpec((c, width), lambda r: (r, 0)),
                  pl.BlockSpec((SUBLANES, width), lambda r: (jnp.minimum((r + 1) * per, n8 - 1), 0)),
                  pl.BlockSpec((SHORT_CONV, width), lambda r: (0, 0))],
        out_specs=pl.BlockSpec((c, width), lambda r: (r, 0)),
        compiler_params=_cparams(("parallel",)),
        name="conv_qkv",
    )(qkv, qkv, qkv, conv_w)


def _load_state(state, s0_ref, tiles, i):
    first = tiles.pos(i) == 0

    @pl.when(first & tiles.is_ctx(i))
    def _():
        state[...] = jnp.zeros(state.shape, F32)

    @pl.when(first & jnp.logical_not(tiles.is_ctx(i)))
    def _():
        state[...] = s0_ref[0]


def _store_state(state, out_ref, tiles, i):
    @pl.when((tiles.pos(i) == tiles.length(i) - 1) & tiles.is_ctx(i))
    def _():
        out_ref[0] = state[...]


def _dir_specs(tiles, c, cols):
    specs = []
    for reverse in (False, True):
        for width, col in cols:
            specs.append(pl.BlockSpec((c, width), functools.partial(
                lambda rev, cc, i: (tiles.row_block(i, rev), cc), reverse, col)))
    return specs


def _pair_dot(a, b):
    n = a.shape[0]
    a_bf, b_bf = _bf(a), _bf(b)
    z = jnp.zeros((n, n), BF16)
    b_diag = jnp.concatenate([jnp.concatenate([b_bf[:, :n], z], axis=1),
                              jnp.concatenate([z, b_bf[:, n:]], axis=1)], axis=0)
    return _dot(a_bf, b_diag)


def _unit_tri_solves(ls, rs, reverse_flags):
    c = ls[0].shape[0]
    n = c // 2
    ri = lax.broadcasted_iota(jnp.int32, (n, c), 0)
    ci = lax.broadcasted_iota(jnp.int32, (n, c), 1) & (n - 1)
    pairs = [jnp.concatenate([l[:n, :n], l[n:, n:]], axis=1) for l in ls]
    shift = SUBLANES.bit_length() - 1
    same = (ri >> shift) == (ci >> shift)
    ms = [jnp.where(same, -lp, 0.0) for lp in pairs]
    ps = [_pair_dot(m, m) for m in ms]
    ns = [m + p + _pair_dot(m, p) for m, p in zip(ms, ps)]
    ps = [_pair_dot(p, p) for p in ps]
    ns = [nv + p + _pair_dot(nv, p) for nv, p in zip(ns, ps)]
    while (1 << shift) < n:
        lvl = ((ri >> (shift + 1)) == (ci >> (shift + 1))) & ((ri >> shift) != (ci >> shift))
        cls = [jnp.where(lvl, lp, 0.0) for lp in pairs]
        ys = [cl + _pair_dot(cl, nv) for cl, nv in zip(cls, ns)]
        ns = [nv - (y + _pair_dot(nv, y)) for nv, y in zip(ns, ys)]
        shift += 1
    firsts, seconds = [], []
    for l, r, nv, rev in zip(ls, rs, ns, reverse_flags):
        if rev:
            firsts.append((r[n:], _bf(nv[:, n:])))
            seconds.append((r[:n], _bf(nv[:, :n]), _bf(l[:n, n:])))
        else:
            firsts.append((r[:n], _bf(nv[:, :n])))
            seconds.append((r[n:], _bf(nv[:, n:]), _bf(l[n:, :n])))
    xas = [ra + _dot(na, _bf(ra)) for ra, na in firsts]
    ts = [rb - _dot(lba, _bf(xa)) for (rb, _, lba), xa in zip(seconds, xas)]
    xbs = [t + _dot(nb, _bf(t)) for (_, nb, _), t in zip(seconds, ts)]
    return [jnp.concatenate([xb, xa] if rev else [xa, xb], axis=0) for xa, xb, rev in zip(xas, xbs, reverse_flags)]


def _delta_kernel(qf_ref, kf_ref, vf_ref, baf_ref, qb_ref, kb_ref, vb_ref, bab_ref, batf_ref, batb_ref,
                  alc_ref, dtc_ref, alr_ref, dtr_ref, s0_ref, of_ref, ob_ref, sout_ref, state, *, tiles, nh, dk, dv):
    step = pl.program_id(0)
    _load_state(state, s0_ref, tiles, step)
    c = qf_ref.shape[0]
    ri = lax.broadcasted_iota(jnp.int32, (c, c), 0)
    ci = lax.broadcasted_iota(jnp.int32, (c, c), 1)
    units = []
    for z, (q_ref, k_ref, v_ref, ba_ref, bat_ref, o_ref) in enumerate(
            ((qf_ref, kf_ref, vf_ref, baf_ref, batf_ref, of_ref), (qb_ref, kb_ref, vb_ref, bab_ref, batb_ref, ob_ref))):
        reverse = z == 1
        incl, strict, incl_t = _order_masks(ri, ci, reverse)
        ba = ba_ref[...]
        g_col = -jnp.exp(alc_ref[...]) * _softplus(ba + dtc_ref[...])
        beta_col = _sigmoid(ba)
        gc_col = _cumsum_rows(_bf(incl.astype(F32)), g_col)
        g_row = -jnp.exp(alr_ref[...]) * _softplus(bat_ref[...] + dtr_ref[...])
        gc_row = _cumsum_cols(g_row, _bf(incl_t.astype(F32)))
        last = 0 if reverse else c - 1
        for h in range(nh):
            cb = z * nh + h
            cg = 2 * nh + cb
            gcc = gc_col[:, cg:cg + 1]
            gcr = gc_row[cg:cg + 1, :]
            units.append(dict(
                z=z, h=h, o_ref=o_ref, strict=strict, gcc=gcc, gl=gcc[last:last + 1],
                beta=beta_col[:, cb:cb + 1], egc=jnp.exp(gcc),
                decay=jnp.where(incl, jnp.exp(jnp.where(incl, gcc - gcr, 0.0)), 0.0),
                q=q_ref[:, h * dk:(h + 1) * dk], k=k_ref[:, h * dk:(h + 1) * dk], v=v_ref[:, h * dv:(h + 1) * dv]))
    for u in units:
        u["kb"] = u["k"] * u["beta"]
        u["k_bf"] = _bf(u["k"])
    kks = [_dot_nt(_bf(u["kb"]), u["k_bf"]) for u in units]
    qks = [_dot_nt(_bf(u["q"]), u["k_bf"]) * u["decay"] for u in units]
    rs = _unit_tri_solves([jnp.where(u["strict"], kk * u["decay"], 0.0) for u, kk in zip(units, kks)],
                          [jnp.concatenate([u["v"] * u["beta"], u["kb"] * u["egc"]], axis=1) for u in units],
                          [u["z"] == 1 for u in units])
    ss = [state[u["z"], u["h"]] for u in units]
    ss_bf = [_bf(s) for s in ss]
    v_news_bf = [_bf(r[:, :dv] - _dot(_bf(r[:, dv:]), sb)) for r, sb in zip(rs, ss_bf)]
    for u, sb, qk, vnb in zip(units, ss_bf, qks, v_news_bf):
        h = u["h"]
        u["o_ref"][:, h * dv:(h + 1) * dv] = _dot(_bf(u["q"] * u["egc"]), sb) + _dot(_bf(qk), vnb)
    for u, s, vnb in zip(units, ss, v_news_bf):
        kd = u["k"] * jnp.exp(u["gl"] - u["gcc"])
        state[u["z"], u["h"]] = s * jnp.exp(u["gl"]) + _dot_tn(_bf(kd), vnb)
    _store_state(state, sout_ref, tiles, step)


def _delta_rule(qkv, ba, bat, a_log, dt_bias, s0, tiles):
    nt = qkv.shape[0]
    nh, dk, dv = A_HEADS, A_DK, A_DV
    c = tiles.c
    pad = LANES - 4 * nh
    al = jnp.concatenate([jnp.zeros((2 * nh,), F32), a_log.reshape(-1), jnp.zeros((pad,), F32)])
    dt = jnp.concatenate([jnp.zeros((2 * nh,), F32), dt_bias.reshape(-1), jnp.zeros((pad,), F32)])
    nr = bat.shape[0]
    const = lambda i: (0, 0)
    in_specs = (_dir_specs(tiles, c, ((nh * dk, 0), (nh * dk, 1), (nh * dv, 2), (LANES, 0)))
                + [pl.BlockSpec((nr, c), lambda i: (0, tiles.row_block(i, False))),
                   pl.BlockSpec((nr, c), lambda i: (0, tiles.row_block(i, True))),
                   pl.BlockSpec((1, LANES), const), pl.BlockSpec((1, LANES), const),
                   pl.BlockSpec((nr, 1), const), pl.BlockSpec((nr, 1), const),
                   pl.BlockSpec((1, 2, nh, dk, dv), lambda i: (tiles.latent_seq(i), 0, 0, 0, 0))])
    kern = functools.partial(_delta_kernel, tiles=tiles, nh=nh, dk=dk, dv=dv)
    return pl.pallas_call(
        kern,
        out_shape=[jax.ShapeDtypeStruct((nt, nh * dv), F32), jax.ShapeDtypeStruct((nt, nh * dv), F32),
                   jax.ShapeDtypeStruct((tiles.n_p, 2, nh, dk, dv), F32)],
        grid=(tiles.n,),
        in_specs=in_specs,
        out_specs=[pl.BlockSpec((c, nh * dv), lambda i: (tiles.row_block(i, False), 0)),
                   pl.BlockSpec((c, nh * dv), lambda i: (tiles.row_block(i, True), 0)),
                   pl.BlockSpec((1, 2, nh, dk, dv), lambda i: (tiles.ctx_seq(i), 0, 0, 0, 0))],
        scratch_shapes=[pltpu.VMEM((2, nh, dk, dv), F32)],
        compiler_params=_cparams(("arbitrary",)),
        name="delta_rule",
    )(qkv, qkv, qkv, ba, qkv, qkv, qkv, ba, bat, bat, al.reshape(1, LANES), dt.reshape(1, LANES),
      al[:nr].reshape(nr, 1), dt[:nr].reshape(nr, 1), s0)


def _ctx_attn_kernel(sink_ref, q_ref, k_ref, v_ref, o_ref, *, n_kv, grp, dh):
    scale = dh ** -0.5
    for hk in range(n_kv):
        k = _bf(k_ref[:, hk * dh:(hk + 1) * dh])
        v = _bf(v_ref[:, hk * dh:(hk + 1) * dh])
        for g in range(grp):
            h = hk * grp + g
            sk = sink_ref[h]
            s = _dot_nt(_bf(q_ref[:, h * dh:(h + 1) * dh] * scale), k)
            m = jnp.maximum(jnp.max(s, axis=-1, keepdims=True), sk)
            p = jnp.exp(s - m)
            den = jnp.sum(p, axis=-1, keepdims=True) + jnp.exp(sk - m)
            o_ref[:, h * dh:(h + 1) * dh] = _dot(_bf(p), v) / den


def _ctx_attention(qb, kb, vb, sink, n_seq, t):
    wq = qb.shape[1]
    wk = kb.shape[1]
    kern = functools.partial(_ctx_attn_kernel, n_kv=B_KV_HEADS, grp=B_Q_HEADS // B_KV_HEADS, dh=HEAD_DIM)
    return pl.pallas_call(
        kern,
        out_shape=jax.ShapeDtypeStruct((n_seq * t, wq), F32),
        grid=(n_seq,),
        in_specs=[pl.BlockSpec(memory_space=pltpu.SMEM),
                  pl.BlockSpec((t, wq), lambda b: (b, 0)),
                  pl.BlockSpec((t, wk), lambda b: (b, 0)),
                  pl.BlockSpec((t, wk), lambda b: (b, 0))],
        out_specs=pl.BlockSpec((t, wq), lambda b: (b, 0)),
        compiler_params=_cparams(("parallel",)),
        name="ctx_attention",
    )(sink, qb, kb, vb)


def _rope(x, cos, sin_signed):
    lane = lax.broadcasted_iota(jnp.int32, x.shape, 1)
    quarter = HEAD_DIM // 4
    partner = jnp.where((lane % (2 * quarter)) < quarter,
                        pltpu.roll(x, HEAD_DIM - quarter, axis=1), pltpu.roll(x, quarter, axis=1))
    return x * cos + partner * sin_signed


def _lat_attn_kernel(sink_ref, q_ref, k_ref, v_ref, kc_ref, vc_ref, cq_ref, sq_ref, ck_ref, sk_ref,
                     o_ref, *, n_kv, grp, dh, window):
    scale = dh ** -0.5
    tq = q_ref.shape[0]
    t = k_ref.shape[0]
    qpos = pl.program_id(1) * tq + lax.broadcasted_iota(jnp.int32, (tq, t), 0)
    kpos = lax.broadcasted_iota(jnp.int32, (tq, t), 1)
    valid = jnp.abs(qpos - kpos) <= window
    cq, sq = cq_ref[...], sq_ref[...]
    ck, sk_t = ck_ref[...], sk_ref[...]
    for hk in range(n_kv):
        hs = slice(hk * dh, (hk + 1) * dh)
        k_rot = _bf(_rope(k_ref[:, hs], ck, sk_t))
        v = _bf(v_ref[:, hs])
        kc = _bf(kc_ref[0, :, hs])
        vc = _bf(vc_ref[0, :, hs])
        for g in range(grp):
            h = hk * grp + g
            sink = sink_ref[h]
            q = q_ref[:, h * dh:(h + 1) * dh] * scale
            s_loc = jnp.where(valid, _dot_nt(_bf(_rope(q, cq, sq)), k_rot), -jnp.inf)
            s_ctx = _dot_nt(_bf(q), kc)
            m = jnp.maximum(jnp.maximum(jnp.max(s_loc, axis=-1, keepdims=True),
                                        jnp.max(s_ctx, axis=-1, keepdims=True)), sink)
            p_loc = jnp.exp(s_loc - m)
            p_ctx = jnp.exp(s_ctx - m)
            den = (jnp.sum(p_loc, axis=-1, keepdims=True) + jnp.sum(p_ctx, axis=-1, keepdims=True)
                   + jnp.exp(sink - m))
            o_ref[:, h * dh:(h + 1) * dh] = (_dot(_bf(p_loc), v) + _dot(_bf(p_ctx), vc)) / den


def _lat_attention(qb, kb, vb, k_ctx, v_ctx, sink, cos, sin_signed, row0, n_seq, t):
    wq = qb.shape[1]
    wk = kb.shape[1]
    tq = CHUNK
    nq = t // tq
    base_q = row0 // tq
    base_t = row0 // t
    past = k_ctx.shape[1]
    kern = functools.partial(_lat_attn_kernel, n_kv=B_KV_HEADS, grp=B_Q_HEADS // B_KV_HEADS, dh=HEAD_DIM,
                             window=WINDOW)
    return pl.pallas_call(
        kern,
        out_shape=jax.ShapeDtypeStruct((n_seq * t, wq), F32),
        grid=(n_seq, nq),
        in_specs=[pl.BlockSpec(memory_space=pltpu.SMEM),
                  pl.BlockSpec((tq, wq), lambda b, i: (base_q + b * nq + i, 0)),
                  pl.BlockSpec((t, wk), lambda b, i: (base_t + b, 0)),
                  pl.BlockSpec((t, wk), lambda b, i: (base_t + b, 0)),
                  pl.BlockSpec((1, past, wk), lambda b, i: (b, 0, 0)),
                  pl.BlockSpec((1, past, wk), lambda b, i: (b, 0, 0)),
                  pl.BlockSpec((tq, HEAD_DIM), lambda b, i: (i, 0)),
                  pl.BlockSpec((tq, HEAD_DIM), lambda b, i: (i, 0)),
                  pl.BlockSpec((t, HEAD_DIM), lambda b, i: (0, 0)),
                  pl.BlockSpec((t, HEAD_DIM), lambda b, i: (0, 0))],
        out_specs=pl.BlockSpec((tq, wq), lambda b, i: (b * nq + i, 0)),
        compiler_params=_cparams(("parallel", "parallel")),
        name="latent_attention",
    )(sink, qb, kb, vb, k_ctx, v_ctx, cos, sin_signed, cos, sin_signed)


def _rope_tables(t):
    half = HEAD_DIM // 2
    quarter = half // 2
    pos = jnp.arange(t)
    row = (pos // GRID_W).astype(F32)
    col = (pos % GRID_W).astype(F32)
    inv = ROPE_THETA ** (-jnp.arange(quarter, dtype=F32) / quarter)
    ang_r = row[:, None] * inv[None, :]
    ang_c = col[:, None] * inv[None, :]
    cos = jnp.concatenate([jnp.cos(ang_r), jnp.cos(ang_r), jnp.cos(ang_c), jnp.cos(ang_c)], axis=-1)
    sin = jnp.concatenate([-jnp.sin(ang_r), jnp.sin(ang_r), -jnp.sin(ang_c), jnp.sin(ang_c)], axis=-1)
    return cos, sin


def _gla_kernel(qf_ref, kf_ref, vf_ref, lrf_ref, qb_ref, kb_ref, vb_ref, lrb_ref, wg_ref, bias_ref, s0_ref,
                of_ref, ob_ref, sout_ref, state, *, tiles, nh, dk, dv, rb):
    step = pl.program_id(0)
    _load_state(state, s0_ref, tiles, step)
    c = qf_ref.shape[0]
    ri = lax.broadcasted_iota(jnp.int32, (c, c), 0)
    ci = lax.broadcasted_iota(jnp.int32, (c, c), 1)
    eye = (lax.broadcasted_iota(jnp.int32, (dk, dk), 0) == lax.broadcasted_iota(jnp.int32, (dk, dk), 1))
    zero_row = jnp.zeros((1, dk), F32)
    units = []
    for z, (q_ref, k_ref, v_ref, lr_ref, o_ref) in enumerate(
            ((qf_ref, kf_ref, vf_ref, lrf_ref, of_ref), (qb_ref, kb_ref, vb_ref, lrb_ref, ob_ref))):
        reverse = z == 1
        incl, _, _ = _order_masks(ri, ci, reverse)
        x = _dot(_bf(lr_ref[...]), _bf(wg_ref[z])) + bias_ref[z]
        gk = (jnp.minimum(x, 0.0) - jnp.log1p(jnp.exp(-jnp.abs(x)))) * (1.0 / GATE_NORM)
        gcum = _cumsum_rows(_bf(incl.astype(F32)), gk)
        last = 0 if reverse else c - 1
        for h in range(nh):
            g = gcum[:, h * dk:(h + 1) * dk]
            units.append(dict(
                z=z, h=h, o_ref=o_ref, reverse=reverse, incl=incl, g=g, gl_row=g[last:last + 1],
                q=q_ref[:, h * dk:(h + 1) * dk] * (dk ** -0.5), k=k_ref[:, h * dk:(h + 1) * dk],
                v_bf=_bf(v_ref[:, h * dv:(h + 1) * dv]), s=state[z, h]))
    intra = [[] for _ in units]
    for blk in range(c // rb):
        r0, r1 = blk * rb, (blk + 1) * rb
        scores, cols = [], []
        for u in units:
            g = u["g"]
            if u["reverse"]:
                c0, c1 = r0, c
                ref = g[r1:r1 + 1] if r1 < c else zero_row
            else:
                c0, c1 = 0, r1
                ref = g[r0 - 1:r0] if r0 > 0 else zero_row
            qe = u["q"][r0:r1] * jnp.exp(g[r0:r1] - ref)
            ke = u["k"][c0:c1] * jnp.exp(ref - g[c0:c1])
            scores.append(jnp.where(u["incl"][r0:r1, c0:c1], _dot_nt(_bf(qe), _bf(ke)), 0.0))
            cols.append((c0, c1))
        for parts, u, a, (c0, c1) in zip(intra, units, scores, cols):
            parts.append(_dot(_bf(a), u["v_bf"][c0:c1]))
    inter = [_dot(_bf(u["q"] * jnp.exp(u["g"])), _bf(u["s"])) for u in units]
    for u, o_inter, parts in zip(units, inter, intra):
        h = u["h"]
        u["o_ref"][:, h * dv:(h + 1) * dv] = o_inter + jnp.concatenate(parts, axis=0)
    for u in units:
        kd = u["k"] * jnp.exp(u["gl_row"] - u["g"])
        gl_col = jnp.sum(jnp.where(eye, jnp.broadcast_to(u["gl_row"], (dk, dk)), 0.0), axis=1, keepdims=True)
        state[u["z"], u["h"]] = u["s"] * jnp.exp(gl_col) + _dot_tn(_bf(kd), u["v_bf"])
    _store_state(state, sout_ref, tiles, step)


def _gla(q, k, v, lr, w_gate, gate_bias, s0, tiles):
    nt = q.shape[0]
    nh, dk, dv = C_HEADS, C_DK, C_DV
    c = tiles.c
    wg = jnp.zeros((2, LANES, nh * dk), F32)
    for z in range(2):
        wg = wg.at[z, z * GATE_RANK:(z + 1) * GATE_RANK].set(w_gate[z])
    in_specs = (_dir_specs(tiles, c, ((nh * dk, 0), (nh * dk, 0), (nh * dv, 0), (LANES, 0)))
                + [pl.BlockSpec((2, LANES, nh * dk), lambda i: (0, 0, 0)),
                   pl.BlockSpec((2, 1, nh * dk), lambda i: (0, 0, 0)),
                   pl.BlockSpec((1, 2, nh, dk, dv), lambda i: (tiles.latent_seq(i), 0, 0, 0, 0))])
    kern = functools.partial(_gla_kernel, tiles=tiles, nh=nh, dk=dk, dv=dv, rb=GLA_ROW_BLOCK)
    return pl.pallas_call(
        kern,
        out_shape=[jax.ShapeDtypeStruct((nt, nh * dv), F32), jax.ShapeDtypeStruct((nt, nh * dv), F32),
                   jax.ShapeDtypeStruct((tiles.n_p, 2, nh, dk, dv), F32)],
        grid=(tiles.n,),
        in_specs=in_specs,
        out_specs=[pl.BlockSpec((c, nh * dv), lambda i: (tiles.row_block(i, False), 0)),
                   pl.BlockSpec((c, nh * dv), lambda i: (tiles.row_block(i, True), 0)),
                   pl.BlockSpec((1, 2, nh, dk, dv), lambda i: (tiles.ctx_seq(i), 0, 0, 0, 0))],
        scratch_shapes=[pltpu.VMEM((2, nh, dk, dv), F32)],
        compiler_params=_cparams(("arbitrary",)),
        name="gla",
    )(q, k, v, lr, q, k, v, lr, wg, gate_bias.reshape(2, 1, nh * dk), s0)


def kernel(x_prompt, x_sample, state_delta, cache_k, cache_v, state_gla, c, c_ctx, norm_g, ada_w, ada_b,
           ffn_w_gu, ffn_w_down, even_w_in, even_conv, even_a_log, even_dt_bias, even_onorm, even_sink,
           even_w_out, odd_w_in, odd_w_gate, odd_gate_bias, odd_onorm, odd_w_out, final_g):
    n_p, t_p, d = x_prompt.shape
    n_s, t_s, _ = x_sample.shape
    depth = norm_g.shape[0]
    np_rows, ns_rows = n_p * t_p, n_s * t_s
    assert np_rows % ROW_TILE == 0 and t_s % ROW_TILE == 0 and np_rows % t_s == 0
    assert t_p % CHUNK == 0 and t_s % CHUNK == 0 and t_s % GRID_W == 0
    rows = (np_rows, t_s)
    tiles = _Tiles(n_p, t_p, n_s, t_s, CHUNK)

    n_cond = 1 + n_s
    cond_rows = -(-n_cond // SUBLANES) * SUBLANES
    conds = jnp.concatenate([c_ctx[None, :], c, jnp.zeros((cond_rows - n_cond, d), F32)], axis=0)
    mods = _ada(conds, ada_w, ada_b)[:, :n_cond].reshape(depth, n_cond, N_MOD, d)

    w_gu = ffn_w_gu.astype(BF16)
    w_down = ffn_w_down.astype(BF16)

    xs = (x_prompt.reshape(np_rows, d), x_sample.reshape(ns_rows, d))
    new_delta, new_k, new_v, new_gla = [], [], [], []
    for l in range(depth):
        j = l // 2
        mod = mods[l]
        x = _ffn_half(xs, mod, 0, norm_g[l, 0], w_gu, w_down, (l, 0), rows)
        if l % 2 == 0:
            nh = A_HEADS
            w = even_w_in[j]
            o_qkv = 2 * nh * A_DK + nh * A_DV
            o_gate = o_qkv + nh * A_DV
            o_ba = o_gate + 4 * nh
            w_cat = jnp.concatenate([w[:, :o_gate], w[:, o_ba:], w[:, o_gate:o_ba],
                                     jnp.zeros((d, LANES - 4 * nh), F32)], axis=1).astype(BF16)
            widths = (o_qkv, nh * A_DV, B_Q_HEADS * HEAD_DIM, B_KV_HEADS * HEAD_DIM, B_KV_HEADS * HEAD_DIM, LANES)
            assert sum(widths) == w_cat.shape[1]
            qkv, gate, qb, kb, vb, ba = _mixer_in(x, mod, norm_g[l, 1], w_cat, widths, rows)
            bat = jnp.transpose(ba[:, :4 * nh])
            qkv_n = _conv_qkv(qkv, even_conv[j], tiles)
            o_f, o_b, st = _delta_rule(qkv_n, ba, bat, even_a_log[j], even_dt_bias[j], state_delta[:, j], tiles)
            cos, sin_signed = _rope_tables(t_s)
            att_p = _ctx_attention(qb, kb, vb, even_sink[j], n_p, t_p)
            att_s = _lat_attention(qb, kb, vb,
                                   cache_k[:, j].reshape(n_s, -1, B_KV_HEADS * HEAD_DIM),
                                   cache_v[:, j].reshape(n_s, -1, B_KV_HEADS * HEAD_DIM),
                                   even_sink[j], cos, sin_signed, np_rows, n_s, t_s)
            x = _mixer_out(x, mod, o_f, o_b, gate, (att_p, att_s), even_onorm[j], even_w_out[j].astype(BF16),
                           nh, A_DV, rows)
            new_delta.append(st)
            new_k.append(kb[:np_rows].reshape(n_p, t_p, B_KV_HEADS, HEAD_DIM))
            new_v.append(vb[:np_rows].reshape(n_p, t_p, B_KV_HEADS, HEAD_DIM))
        else:
            nh = C_HEADS
            w_cat = jnp.concatenate([odd_w_in[j], jnp.zeros((d, LANES - 2 * GATE_RANK), F32)], axis=1).astype(BF16)
            widths = (nh * C_DK, nh * C_DK, nh * C_DV, nh * C_DV, LANES)
            assert sum(widths) == w_cat.shape[1]
            q, k, v, g_out, lr = _mixer_in(x, mod, norm_g[l, 1], w_cat, widths, rows)
            o_f, o_b, st = _gla(q, k, v, lr, odd_w_gate[j], odd_gate_bias[j], state_gla[:, j], tiles)
            x = _mixer_out(x, mod, o_f, o_b, g_out, None, odd_onorm[j], odd_w_out[j].astype(BF16),
                           nh, C_DV, rows)
            new_gla.append(st)
        last = l == depth - 1
        xs = _ffn_half((x,), mod, 6, norm_g[l, 2], w_gu, w_down, (l, 1), rows,
                       final_g=final_g if last else None)
        if not last:
            xs = (xs,)

    y_prompt, y_sample = xs
    return (y_prompt.reshape(n_p, t_p, d), y_sample.reshape(n_s, t_s, d), jnp.stack(new_delta, axis=1),
            jnp.stack(new_k, axis=1), jnp.stack(new_v, axis=1), jnp.stack(new_gla, axis=1))
```

```python
import functools
import math

import jax
import jax.numpy as jnp
from jax import lax
from jax.experimental import pallas as pl
from jax.experimental.pallas import tpu as pltpu

F32 = jnp.float32
BF16 = jnp.bfloat16

EPS = 1e-6
N_MOD = 9
GRID_W = 64
HEAD_DIM = 128
A_HEADS = 4
A_DK = 128
A_DV = 128
SHORT_CONV = 5
B_Q_HEADS = 4
B_KV_HEADS = 2
WINDOW = 128
C_HEADS = 4
C_DK = 128
C_DV = 256
GATE_RANK = 16
GATE_NORM = 16.0
ROPE_THETA = 10000.0

LANES = 128
SUBLANES = 8
ROW_TILE = 512
PROJ_ROW_TILE = 1024
PROJ_ROW_PARTS = 2
FFN_ROW_TILE = 1024
FFN_ROW_PARTS = 2
FFN_CHUNKS = 11
CHUNK = 256
ATTN_ROW_TILE = 256
GLA_ROW_BLOCK = 128
VMEM_LIMIT = 56 * 1024 * 1024


def _cparams(sem, vmem=VMEM_LIMIT):
    return pltpu.CompilerParams(dimension_semantics=sem, vmem_limit_bytes=vmem)


def _resident(block_shape, index_map):
    return pl.BlockSpec(block_shape, index_map, pipeline_mode=pl.Buffered(1))


def _dot(a, b):
    return jnp.dot(a, b, preferred_element_type=F32)


def _dot_nt(a, b):
    return lax.dot_general(a, b, (((1,), (1,)), ((), ())), preferred_element_type=F32)


def _dot_tn(a, b):
    return lax.dot_general(a, b, (((0,), (0,)), ((), ())), preferred_element_type=F32)


def _bf(x):
    return x.astype(BF16)


def _sigmoid(x):
    return 0.5 * jnp.tanh(0.5 * x) + 0.5


def _silu(x):
    return x * _sigmoid(x)


def _softplus(x):
    return jnp.maximum(x, 0.0) + jnp.log1p(jnp.exp(-jnp.abs(x)))


def _rms(x):
    return x * lax.rsqrt(jnp.mean(x * x, axis=-1, keepdims=True) + EPS)


def _modnorm(x, g, shift, scale):
    return (_rms(x) * g) * (1.0 + scale) + shift


def _split3(x):
    hi = _bf(x)
    r1 = x - hi.astype(F32)
    mid = _bf(r1)
    lo = _bf(r1 - mid.astype(F32))
    return hi, mid, lo


def _cumsum_rows(tri_bf, x):
    hi, mid, lo = _split3(x)
    return (_dot(tri_bf, hi) + _dot(tri_bf, mid)) + _dot(tri_bf, lo)


def _order_masks(ri, ci, reverse):
    if reverse:
        return ri <= ci, ri < ci
    return ri >= ci, ri > ci


class _Tiles:
    def __init__(self, n_p, t_p, n_s, t_s, c):
        self.n_p, self.n_s, self.c = n_p, n_s, c
        self.per_p, self.per_s = t_p // c, t_s // c
        self.np_tiles = n_p * self.per_p
        self.n = self.np_tiles + n_s * self.per_s

    def is_ctx(self, i):
        return i < self.np_tiles

    def seq(self, i):
        return jnp.where(i < self.np_tiles, i // self.per_p, self.n_p + (i - self.np_tiles) // self.per_s)

    def pos(self, i):
        return jnp.where(i < self.np_tiles, i % self.per_p, (i - self.np_tiles) % self.per_s)

    def length(self, i):
        return jnp.where(i < self.np_tiles, self.per_p, self.per_s)

    def row_block(self, i, reverse):
        return i + self.length(i) - 1 - 2 * self.pos(i) if reverse else i

    def ctx_seq(self, i):
        return jnp.minimum(self.seq(i), self.n_p - 1)

    def latent_seq(self, i):
        return jnp.maximum(self.seq(i) - self.n_p, 0)


def _ada_kernel(c_ref, w_ref, b_ref, o_ref):
    s = _bf(_silu(c_ref[...]))
    o_ref[0] = _dot(s, _bf(w_ref[0])) + b_ref[0]


def _ada(cond, ada_w, ada_b):
    depth, d, n = ada_w.shape
    rows = cond.shape[0]
    tn = n // 4
    return pl.pallas_call(
        _ada_kernel,
        out_shape=jax.ShapeDtypeStruct((depth, rows, n), F32),
        grid=(depth, n // tn),
        in_specs=[pl.BlockSpec((rows, d), lambda l, j: (0, 0)),
                  pl.BlockSpec((1, d, tn), lambda l, j: (l, 0, j)),
                  pl.BlockSpec((1, 1, tn), lambda l, j: (l, 0, j))],
        out_specs=pl.BlockSpec((1, rows, tn), lambda l, j: (l, 0, j)),
        compiler_params=_cparams(("parallel", "parallel")),
        name="ada",
    )(cond, ada_w, ada_b.reshape(depth, 1, n))


def _cond_index(n_prompt_rows, dec_seq, tm):
    npt = n_prompt_rows // tm

    def cond(i):
        return jnp.where(i < npt, 0, 1 + ((i - npt) * tm) // dec_seq)

    return cond


def _ffn_kernel(*refs, i0, d_ff, n_chunks, parts, npt, split_in, final):
    n_x = 2 if split_in else 1
    x_refs, (mod_ref, g_ref, wgu_ref, wd_ref) = refs[:n_x], refs[n_x:n_x + 4]
    rest = refs[n_x + 4:]
    i = pl.program_id(0)
    mod = mod_ref[0]
    ch = d_ff // n_chunks
    rp = x_refs[0].shape[0] // parts
    outs = []
    for p in range(parts):
        rs = slice(p * rp, (p + 1) * rp)
        if split_in:
            x = jnp.where(i < npt, x_refs[0][rs, :], x_refs[1][rs, :])
        else:
            x = x_refs[0][rs, :]
        h = _bf(_modnorm(x, g_ref[...], mod[i0:i0 + 1], mod[i0 + 1:i0 + 2]))
        acts = []
        for c in range(n_chunks):
            gt = _dot(h, wgu_ref[:, c * ch:(c + 1) * ch])
            up = _dot(h, wgu_ref[:, d_ff + c * ch:d_ff + (c + 1) * ch])
            acts.append(_bf(_silu(gt) * up))
        out = x + (0.5 * mod[i0 + 2:i0 + 3]) * _dot(jnp.concatenate(acts, axis=1), wd_ref[...])
        if final:
            outs.append((rs, _rms(out) * rest[0][...]))
        else:
            rest[0][rs, :] = out
    if final:
        _, yp_ref, ys_ref = rest

        @pl.when(i < npt)
        def _():
            for rs, out in outs:
                yp_ref[rs, :] = out

        @pl.when(i >= npt)
        def _():
            for rs, out in outs:
                ys_ref[rs, :] = out


def _ffn_half(xs, mod, i0, g, w_gu, w_down, widx, rows, final_g=None):
    split_in = len(xs) == 2
    d = xs[0].shape[1]
    nt = sum(x.shape[0] for x in xs)
    d_ff = w_down.shape[-2]
    tm = FFN_ROW_TILE
    np_rows = rows[0]
    assert np_rows % tm == 0 and rows[1] % tm == 0 and d_ff % (FFN_CHUNKS * LANES) == 0
    cond = _cond_index(*rows, tm)
    npt = np_rows // tm
    final = final_g is not None
    kern = functools.partial(_ffn_kernel, i0=i0, d_ff=d_ff, n_chunks=FFN_CHUNKS, parts=FFN_ROW_PARTS, npt=npt, split_in=split_in,
                             final=final)
    ctx_map = lambda i: (jnp.minimum(i, npt - 1), 0)
    lat_map = lambda i: (jnp.maximum(i - npt, 0), 0)
    if split_in:
        x_specs = [pl.BlockSpec((tm, d), ctx_map), pl.BlockSpec((tm, d), lat_map)]
    else:
        x_specs = [pl.BlockSpec((tm, d), lambda i: (i, 0))]
    in_specs = x_specs + [pl.BlockSpec((1, N_MOD, d), lambda i: (cond(i), 0, 0)),
                          pl.BlockSpec((1, d), lambda i: (0, 0)),
                          _resident((None, None, d, 2 * d_ff), lambda i: widx + (0, 0)),
                          _resident((None, None, d_ff, d), lambda i: widx + (0, 0))]
    args = list(xs) + [mod, g.reshape(1, d), w_gu, w_down]
    if final:
        in_specs.append(pl.BlockSpec((1, d), lambda i: (0, 0)))
        args.append(final_g.reshape(1, d))
        out_shape = [jax.ShapeDtypeStruct((np_rows, d), F32), jax.ShapeDtypeStruct((nt - np_rows, d), F32)]
        out_specs = [pl.BlockSpec((tm, d), ctx_map), pl.BlockSpec((tm, d), lat_map)]
    else:
        out_shape = jax.ShapeDtypeStruct((nt, d), F32)
        out_specs = pl.BlockSpec((tm, d), lambda i: (i, 0))
    return pl.pallas_call(
        kern,
        out_shape=out_shape,
        grid=(nt // tm,),
        in_specs=in_specs,
        out_specs=out_specs,
        compiler_params=_cparams(("arbitrary",)),
        name="ffn_half",
    )(*args)


def _proj_kernel(x_ref, mod_ref, g_ref, w_ref, *o_refs, i0, widths, parts):
    mod = mod_ref[0]
    rp = x_ref.shape[0] // parts
    for p in range(parts):
        rs = slice(p * rp, (p + 1) * rp)
        h = _bf(_modnorm(x_ref[rs, :], g_ref[...], mod[i0:i0 + 1], mod[i0 + 1:i0 + 2]))
        off = 0
        for o_ref, wd in zip(o_refs, widths):
            o_ref[rs, :] = _dot(h, w_ref[:, off:off + wd])
            off += wd


def _mixer_in(x, mod, g, w, widths, rows):
    nt, d = x.shape
    n = sum(widths)
    tm = PROJ_ROW_TILE
    assert rows[0] % tm == 0 and rows[1] % tm == 0
    cond = _cond_index(*rows, tm)
    kern = functools.partial(_proj_kernel, i0=3, widths=tuple(widths), parts=PROJ_ROW_PARTS)
    return pl.pallas_call(
        kern,
        out_shape=[jax.ShapeDtypeStruct((nt, wd), F32) for wd in widths],
        grid=(nt // tm,),
        in_specs=[pl.BlockSpec((tm, d), lambda i: (i, 0)),
                  pl.BlockSpec((1, N_MOD, d), lambda i: (cond(i), 0, 0)),
                  pl.BlockSpec((1, d), lambda i: (0, 0)),
                  _resident((d, n), lambda i: (0, 0))],
        out_specs=[pl.BlockSpec((tm, wd), lambda i: (i, 0)) for wd in widths],
        compiler_params=_cparams(("parallel",)),
        name="mixer_in",
    )(x, mod, g.reshape(1, d), w)


def _mixer_out_kernel(x_ref, mod_ref, of_ref, ob_ref, gate_ref, on_ref, w_ref, *rest, nh, dv, npt):
    o_ref = rest[-1]
    mod = mod_ref[0]
    od = of_ref[...] + ob_ref[...]
    gate = gate_ref[...]
    y = None
    if len(rest) == 3:
        extra = jnp.where(pl.program_id(0) < npt, rest[0][...], rest[1][...])
        y = _dot(_bf(extra), w_ref[nh * dv:, :])
    per = max(1, (2 * LANES) // dv)
    for h0 in range(0, nh, per):
        mix = [_bf((_rms(od[:, h * dv:(h + 1) * dv]) * on_ref[...]) * _silu(gate[:, h * dv:(h + 1) * dv]))
               for h in range(h0, min(h0 + per, nh))]
        part = _dot(jnp.concatenate(mix, axis=1), w_ref[h0 * dv:min(h0 + per, nh) * dv, :])
        y = part if y is None else y + part
    o_ref[...] = x_ref[...] + mod[5:6] * y


def _mixer_out(x, mod, o_f, o_b, gate, extra, onorm, w_out, nh, dv, rows):
    nt, d = x.shape
    wa = nh * dv
    tm = ROW_TILE
    cond = _cond_index(*rows, tm)
    npt = rows[0] // tm
    kern = functools.partial(_mixer_out_kernel, nh=nh, dv=dv, npt=npt)
    row = lambda i: (i, 0)
    in_specs = [pl.BlockSpec((tm, d), row),
                pl.BlockSpec((1, N_MOD, d), lambda i: (cond(i), 0, 0)),
                pl.BlockSpec((tm, wa), row), pl.BlockSpec((tm, wa), row), pl.BlockSpec((tm, wa), row),
                pl.BlockSpec((1, dv), lambda i: (0, 0)),
                _resident(w_out.shape, lambda i: (0, 0))]
    args = [x, mod, o_f, o_b, gate, onorm.reshape(1, dv), w_out]
    if extra is not None:
        we = extra[0].shape[1]
        in_specs += [pl.BlockSpec((tm, we), lambda i: (jnp.minimum(i, npt - 1), 0)),
                     pl.BlockSpec((tm, we), lambda i: (jnp.maximum(i - npt, 0), 0))]
        args += list(extra)
    return pl.pallas_call(
        kern,
        out_shape=jax.ShapeDtypeStruct((nt, d), F32),
        grid=(nt // tm,),
        in_specs=in_specs,
        out_specs=pl.BlockSpec((tm, d), row),
        compiler_params=_cparams(("arbitrary",)),
        name="mixer_out",
    )(*args)


def _conv_kernel(prev_ref, x_ref, next_ref, w_ref, o_ref, *, tiles, dk):
    r = pl.program_id(0)
    c, width = x_ref.shape
    pad = (SHORT_CONV - 1) // 2
    has_prev = jnp.where(tiles.pos(r) > 0, 1.0, 0.0)
    has_next = jnp.where(tiles.pos(r) < tiles.length(r) - 1, 1.0, 0.0)
    ext = c + 2 * SUBLANES
    for hh in range(width // dk):
        part = hh // (width // (3 * dk))
        sl = slice(hh * dk, (hh + 1) * dk)
        xe = jnp.concatenate([prev_ref[:, sl] * has_prev, x_ref[:, sl], next_ref[:, sl] * has_next], axis=0)
        w = w_ref[:, sl]
        acc = None
        for j in range(SHORT_CONV):
            sh = pad - j
            xs = xe if sh == 0 else pltpu.roll(xe, sh % ext, axis=0)
            term = xs[SUBLANES:SUBLANES + c] * w[j:j + 1]
            acc = term if acc is None else acc + term
        y = _silu(acc)
        if part < 2:
            nrm = lax.rsqrt(jnp.sum(y * y, axis=-1, keepdims=True) + EPS)
            y = y * (nrm * (dk ** -0.5) if part == 0 else nrm)
        o_ref[:, sl] = y


def _conv_qkv(qkv, conv_w, tiles):
    nt, width = qkv.shape
    c = tiles.c
    per = c // SUBLANES
    n8 = nt // SUBLANES
    kern = functools.partial(_conv_kernel, tiles=tiles, dk=A_DK)
    return pl.pallas_call(
        kern,
        out_shape=jax.ShapeDtypeStruct((nt, width), F32),
        grid=(tiles.n,),
        in_specs=[pl.BlockSpec((SUBLANES, width), lambda r: (jnp.maximum(r * per - 1, 0), 0)),
                  pl.BlockSpec((c, width), lambda r: (r, 0)),
                  pl.BlockSpec((SUBLANES, width), lambda r: (jnp.minimum((r + 1) * per, n8 - 1), 0)),
                  pl.BlockSpec((SHORT_CONV, width), lambda r: (0, 0))],
        out_specs=pl.BlockSpec((c, width), lambda r: (r, 0)),
        compiler_params=_cparams(("parallel",)),
        name="conv_qkv",
    )(qkv, qkv, qkv, conv_w)


def _load_state(state, s0_ref, tiles, i):
    first = tiles.pos(i) == 0

    @pl.when(first & tiles.is_ctx(i))
    def _():
        state[...] = jnp.zeros(state.shape, F32)

    @pl.when(first & jnp.logical_not(tiles.is_ctx(i)))
    def _():
        state[...] = s0_ref[0]


def _store_state(state, out_ref, tiles, i):
    @pl.when((tiles.pos(i) == tiles.length(i) - 1) & tiles.is_ctx(i))
    def _():
        out_ref[0] = state[...]


def _dir_specs(tiles, c, cols):
    specs = []
    for reverse in (False, True):
        for width, col in cols:
            specs.append(pl.BlockSpec((c, width), functools.partial(
                lambda rev, cc, i: (tiles.row_block(i, rev), cc), reverse, col)))
    return specs


def _pair_dot(a, b):
    n = a.shape[0]
    a_bf, b_bf = _bf(a), _bf(b)
    z = jnp.zeros((n, n), BF16)
    b_diag = jnp.concatenate([jnp.concatenate([b_bf[:, :n], z], axis=1),
                              jnp.concatenate([z, b_bf[:, n:]], axis=1)], axis=0)
    return _dot(a_bf, b_diag)


def _unit_tri_solves(ls, rs, reverse_flags):
    c = ls[0].shape[0]
    n = c // 2
    ri = lax.broadcasted_iota(jnp.int32, (n, c), 0)
    ci = lax.broadcasted_iota(jnp.int32, (n, c), 1) & (n - 1)
    pairs = [jnp.concatenate([l[:n, :n], l[n:, n:]], axis=1) for l in ls]
    shift = SUBLANES.bit_length() - 1
    same = (ri >> shift) == (ci >> shift)
    ms = [jnp.where(same, -lp, 0.0) for lp in pairs]
    ps = [_pair_dot(m, m) for m in ms]
    ns = [m + p + _pair_dot(m, p) for m, p in zip(ms, ps)]
    ps = [_pair_dot(p, p) for p in ps]
    ns = [nv + p + _pair_dot(nv, p) for nv, p in zip(ns, ps)]
    while (1 << shift) < n:
        lvl = ((ri >> (shift + 1)) == (ci >> (shift + 1))) & ((ri >> shift) != (ci >> shift))
        cls = [jnp.where(lvl, lp, 0.0) for lp in pairs]
        ys = [cl + _pair_dot(cl, nv) for cl, nv in zip(cls, ns)]
        ns = [nv - (y + _pair_dot(nv, y)) for nv, y in zip(ns, ys)]
        shift += 1
    firsts, seconds = [], []
    for l, r, nv, rev in zip(ls, rs, ns, reverse_flags):
        if rev:
            firsts.append((r[n:], _bf(nv[:, n:])))
            seconds.append((r[:n], _bf(nv[:, :n]), _bf(l[:n, n:])))
        else:
            firsts.append((r[:n], _bf(nv[:, :n])))
            seconds.append((r[n:], _bf(nv[:, n:]), _bf(l[n:, :n])))
    xas = [ra + _dot(na, _bf(ra)) for ra, na in firsts]
    ts = [rb - _dot(lba, _bf(xa)) for (rb, _, lba), xa in zip(seconds, xas)]
    xbs = [t + _dot(nb, _bf(t)) for (_, nb, _), t in zip(seconds, ts)]
    return [jnp.concatenate([xb, xa] if rev else [xa, xb], axis=0) for xa, xb, rev in zip(xas, xbs, reverse_flags)]


def _delta_kernel(qf_ref, kf_ref, vf_ref, baf_ref, qb_ref, kb_ref, vb_ref, bab_ref, al_ref, dt_ref, s0_ref,
                  of_ref, ob_ref, sout_ref, state, *, tiles, nh, dk, dv):
    step = pl.program_id(0)
    _load_state(state, s0_ref, tiles, step)
    c = qf_ref.shape[0]
    ri = lax.broadcasted_iota(jnp.int32, (c, c), 0)
    ci = lax.broadcasted_iota(jnp.int32, (c, c), 1)
    units = []
    for z, (q_ref, k_ref, v_ref, ba_ref, o_ref) in enumerate(
            ((qf_ref, kf_ref, vf_ref, baf_ref, of_ref), (qb_ref, kb_ref, vb_ref, bab_ref, ob_ref))):
        reverse = z == 1
        incl, strict = _order_masks(ri, ci, reverse)
        ba = ba_ref[...]
        g_col = -jnp.exp(al_ref[...]) * _softplus(ba + dt_ref[...])
        beta_col = _sigmoid(ba)
        gc_col = _cumsum_rows(_bf(incl.astype(F32)), g_col)
        gc_row = gc_col.T
        last = 0 if reverse else c - 1
        for h in range(nh):
            cb = z * nh + h
            cg = 2 * nh + cb
            gcc = gc_col[:, cg:cg + 1]
            gcr = gc_row[cg:cg + 1, :]
            units.append(dict(
                z=z, h=h, o_ref=o_ref, strict=strict, gcc=gcc, gl=gcc[last:last + 1],
                beta=beta_col[:, cb:cb + 1], egc=jnp.exp(gcc),
                decay=jnp.where(incl, jnp.exp(jnp.where(incl, gcc - gcr, 0.0)), 0.0),
                q=q_ref[:, h * dk:(h + 1) * dk], k=k_ref[:, h * dk:(h + 1) * dk], v=v_ref[:, h * dv:(h + 1) * dv]))
    for u in units:
        u["kb"] = u["k"] * u["beta"]
        u["k_bf"] = _bf(u["k"])
    kks = [_dot_nt(_bf(u["kb"]), u["k_bf"]) for u in units]
    qks = [_dot_nt(_bf(u["q"]), u["k_bf"]) * u["decay"] for u in units]
    rs = _unit_tri_solves([jnp.where(u["strict"], kk * u["decay"], 0.0) for u, kk in zip(units, kks)],
                          [jnp.concatenate([u["v"] * u["beta"], u["kb"] * u["egc"]], axis=1) for u in units],
                          [u["z"] == 1 for u in units])
    ss = [state[u["z"], u["h"]] for u in units]
    ss_bf = [_bf(s) for s in ss]
    v_news_bf = [_bf(r[:, :dv] - _dot(_bf(r[:, dv:]), sb)) for r, sb in zip(rs, ss_bf)]
    for u, sb, qk, vnb in zip(units, ss_bf, qks, v_news_bf):
        h = u["h"]
        u["o_ref"][:, h * dv:(h + 1) * dv] = _dot(_bf(u["q"] * u["egc"]), sb) + _dot(_bf(qk), vnb)
    for u, s, vnb in zip(units, ss, v_news_bf):
        kd = u["k"] * jnp.exp(u["gl"] - u["gcc"])
        state[u["z"], u["h"]] = s * jnp.exp(u["gl"]) + _dot_tn(_bf(kd), vnb)
    _store_state(state, sout_ref, tiles, step)


def _delta_rule(qkv, ba, a_log, dt_bias, s0, tiles):
    nt = qkv.shape[0]
    nh, dk, dv = A_HEADS, A_DK, A_DV
    c = tiles.c
    pad = LANES - 4 * nh
    al = jnp.concatenate([jnp.zeros((2 * nh,), F32), a_log.reshape(-1), jnp.zeros((pad,), F32)])
    dt = jnp.concatenate([jnp.zeros((2 * nh,), F32), dt_bias.reshape(-1), jnp.zeros((pad,), F32)])
    const = lambda i: (0, 0)
    in_specs = (_dir_specs(tiles, c, ((nh * dk, 0), (nh * dk, 1), (nh * dv, 2), (LANES, 0)))
                + [pl.BlockSpec((1, LANES), const), pl.BlockSpec((1, LANES), const),
                   pl.BlockSpec((1, 2, nh, dk, dv), lambda i: (tiles.latent_seq(i), 0, 0, 0, 0))])
    kern = functools.partial(_delta_kernel, tiles=tiles, nh=nh, dk=dk, dv=dv)
    return pl.pallas_call(
        kern,
        out_shape=[jax.ShapeDtypeStruct((nt, nh * dv), F32), jax.ShapeDtypeStruct((nt, nh * dv), F32),
                   jax.ShapeDtypeStruct((tiles.n_p, 2, nh, dk, dv), F32)],
        grid=(tiles.n,),
        in_specs=in_specs,
        out_specs=[pl.BlockSpec((c, nh * dv), lambda i: (tiles.row_block(i, False), 0)),
                   pl.BlockSpec((c, nh * dv), lambda i: (tiles.row_block(i, True), 0)),
                   pl.BlockSpec((1, 2, nh, dk, dv), lambda i: (tiles.ctx_seq(i), 0, 0, 0, 0))],
        scratch_shapes=[pltpu.VMEM((2, nh, dk, dv), F32)],
        compiler_params=_cparams(("arbitrary",)),
        name="delta_rule",
    )(qkv, qkv, qkv, ba, qkv, qkv, qkv, ba, al.reshape(1, LANES), dt.reshape(1, LANES), s0)


def _ctx_attn_kernel(sink_ref, q_ref, k_ref, v_ref, o_ref, *, n_kv, grp, dh):
    scale = dh ** -0.5
    for hk in range(n_kv):
        k = _bf(k_ref[:, hk * dh:(hk + 1) * dh])
        v = _bf(v_ref[:, hk * dh:(hk + 1) * dh])
        for g in range(grp):
            h = hk * grp + g
            sk = sink_ref[h]
            s = _dot_nt(_bf(q_ref[:, h * dh:(h + 1) * dh] * scale), k)
            m = jnp.maximum(jnp.max(s, axis=-1, keepdims=True), sk)
            p = jnp.exp(s - m)
            den = jnp.sum(p, axis=-1, keepdims=True) + jnp.exp(sk - m)
            o_ref[:, h * dh:(h + 1) * dh] = _dot(_bf(p), v) / den


def _ctx_attention(qb, kb, vb, sink, n_seq, t):
    wq = qb.shape[1]
    wk = kb.shape[1]
    kern = functools.partial(_ctx_attn_kernel, n_kv=B_KV_HEADS, grp=B_Q_HEADS // B_KV_HEADS, dh=HEAD_DIM)
    return pl.pallas_call(
        kern,
        out_shape=jax.ShapeDtypeStruct((n_seq * t, wq), F32),
        grid=(n_seq,),
        in_specs=[pl.BlockSpec(memory_space=pltpu.SMEM),
                  pl.BlockSpec((t, wq), lambda b: (b, 0)),
                  pl.BlockSpec((t, wk), lambda b: (b, 0)),
                  pl.BlockSpec((t, wk), lambda b: (b, 0))],
        out_specs=pl.BlockSpec((t, wq), lambda b: (b, 0)),
        compiler_params=_cparams(("parallel",)),
        name="ctx_attention",
    )(sink, qb, kb, vb)


def _rope(x, cos, sin_signed):
    lane = lax.broadcasted_iota(jnp.int32, x.shape, 1)
    quarter = HEAD_DIM // 4
    partner = jnp.where((lane % (2 * quarter)) < quarter,
                        pltpu.roll(x, HEAD_DIM - quarter, axis=1), pltpu.roll(x, quarter, axis=1))
    return x * cos + partner * sin_signed


def _lat_attn_kernel(sink_ref, q_ref, k_ref, v_ref, kc_ref, vc_ref, cq_ref, sq_ref, ck_ref, sk_ref,
                     o_ref, *, n_kv, grp, dh, window):
    scale = dh ** -0.5
    tq = q_ref.shape[0]
    t = k_ref.shape[0]
    span = min(t, tq + 2 * window)
    q0 = pl.program_id(1) * tq
    start = pl.multiple_of(jnp.clip(q0 - window, 0, t - span), math.gcd(tq, window))
    rows = pl.ds(start, span)
    qpos = q0 + lax.broadcasted_iota(jnp.int32, (tq, span), 0)
    kpos = start + lax.broadcasted_iota(jnp.int32, (tq, span), 1)
    valid = jnp.abs(qpos - kpos) <= window
    cq, sq = cq_ref[...], sq_ref[...]
    ck, sk_t = ck_ref[rows, :], sk_ref[rows, :]
    kv = []
    for hk in range(n_kv):
        hs = slice(hk * dh, (hk + 1) * dh)
        kv.append((_bf(_rope(k_ref[rows, hs], ck, sk_t)), _bf(v_ref[rows, hs]), _bf(kc_ref[0, :, hs]),
                   _bf(vc_ref[0, :, hs])))
    heads = range(n_kv * grp)
    sinks = [sink_ref[h] for h in heads]
    qs = [q_ref[:, h * dh:(h + 1) * dh] * scale for h in heads]
    s_locs = [jnp.where(valid, _dot_nt(_bf(_rope(q, cq, sq)), kv[h // grp][0]), -jnp.inf) for h, q in zip(heads, qs)]
    s_ctxs = [_dot_nt(_bf(q), kv[h // grp][2]) for h, q in zip(heads, qs)]
    ms = [jnp.maximum(jnp.maximum(jnp.max(sl, axis=-1, keepdims=True), jnp.max(sc, axis=-1, keepdims=True)), sk)
          for sl, sc, sk in zip(s_locs, s_ctxs, sinks)]
    p_locs = [jnp.exp(sl - m) for sl, m in zip(s_locs, ms)]
    p_ctxs = [jnp.exp(sc - m) for sc, m in zip(s_ctxs, ms)]
    dens = [jnp.sum(pl_, axis=-1, keepdims=True) + jnp.sum(pc, axis=-1, keepdims=True) + jnp.exp(sk - m)
            for pl_, pc, sk, m in zip(p_locs, p_ctxs, sinks, ms)]
    for h, pl_, pc, den in zip(heads, p_locs, p_ctxs, dens):
        o_ref[:, h * dh:(h + 1) * dh] = (_dot(_bf(pl_), kv[h // grp][1]) + _dot(_bf(pc), kv[h // grp][3])) / den


def _lat_attention(qb, kb, vb, k_ctx, v_ctx, sink, cos, sin_signed, row0, n_seq, t):
    wq = qb.shape[1]
    wk = kb.shape[1]
    tq = ATTN_ROW_TILE
    assert t % tq == 0 and row0 % t == 0
    nq = t // tq
    base_q = row0 // tq
    base_t = row0 // t
    past = k_ctx.shape[1]
    kern = functools.partial(_lat_attn_kernel, n_kv=B_KV_HEADS, grp=B_Q_HEADS // B_KV_HEADS, dh=HEAD_DIM,
                             window=WINDOW)
    return pl.pallas_call(
        kern,
        out_shape=jax.ShapeDtypeStruct((n_seq * t, wq), F32),
        grid=(n_seq, nq),
        in_specs=[pl.BlockSpec(memory_space=pltpu.SMEM),
                  pl.BlockSpec((tq, wq), lambda b, i: (base_q + b * nq + i, 0)),
                  pl.BlockSpec((t, wk), lambda b, i: (base_t + b, 0)),
                  pl.BlockSpec((t, wk), lambda b, i: (base_t + b, 0)),
                  pl.BlockSpec((1, past, wk), lambda b, i: (b, 0, 0)),
                  pl.BlockSpec((1, past, wk), lambda b, i: (b, 0, 0)),
                  pl.BlockSpec((tq, HEAD_DIM), lambda b, i: (i, 0)),
                  pl.BlockSpec((tq, HEAD_DIM), lambda b, i: (i, 0)),
                  pl.BlockSpec((t, HEAD_DIM), lambda b, i: (0, 0)),
                  pl.BlockSpec((t, HEAD_DIM), lambda b, i: (0, 0))],
        out_specs=pl.BlockSpec((tq, wq), lambda b, i: (b * nq + i, 0)),
        compiler_params=_cparams(("parallel", "parallel")),
        name="latent_attention",
    )(sink, qb, kb, vb, k_ctx, v_ctx, cos, sin_signed, cos, sin_signed)


def _rope_tables(t):
    half = HEAD_DIM // 2
    quarter = half // 2
    pos = jnp.arange(t)
    row = (pos // GRID_W).astype(F32)
    col = (pos % GRID_W).astype(F32)
    inv = ROPE_THETA ** (-jnp.arange(quarter, dtype=F32) / quarter)
    ang_r = row[:, None] * inv[None, :]
    ang_c = col[:, None] * inv[None, :]
    cos = jnp.concatenate([jnp.cos(ang_r), jnp.cos(ang_r), jnp.cos(ang_c), jnp.cos(ang_c)], axis=-1)
    sin = jnp.concatenate([-jnp.sin(ang_r), jnp.sin(ang_r), -jnp.sin(ang_c), jnp.sin(ang_c)], axis=-1)
    return cos, sin


def _gla_kernel(qf_ref, kf_ref, vf_ref, lrf_ref, qb_ref, kb_ref, vb_ref, lrb_ref, wg_ref, bias_ref, s0_ref,
                of_ref, ob_ref, sout_ref, state, *, tiles, nh, dk, dv, rb):
    step = pl.program_id(0)
    _load_state(state, s0_ref, tiles, step)
    c = qf_ref.shape[0]
    ri = lax.broadcasted_iota(jnp.int32, (c, c), 0)
    ci = lax.broadcasted_iota(jnp.int32, (c, c), 1)
    eye = (lax.broadcasted_iota(jnp.int32, (dk, dk), 0) == lax.broadcasted_iota(jnp.int32, (dk, dk), 1))
    units = []
    for z, (q_ref, k_ref, v_ref, lr_ref, o_ref) in enumerate(
            ((qf_ref, kf_ref, vf_ref, lrf_ref, of_ref), (qb_ref, kb_ref, vb_ref, lrb_ref, ob_ref))):
        reverse = z == 1
        incl, _ = _order_masks(ri, ci, reverse)
        x = _dot(_bf(lr_ref[...]), _bf(wg_ref[z])) + bias_ref[z]
        gk = (jnp.minimum(x, 0.0) - jnp.log1p(jnp.exp(-jnp.abs(x)))) * (1.0 / GATE_NORM)
        gcum = _cumsum_rows(_bf(incl.astype(F32)), gk)
        last = 0 if reverse else c - 1
        for h in range(nh):
            g = gcum[:, h * dk:(h + 1) * dk]
            units.append(dict(
                z=z, h=h, o_ref=o_ref, reverse=reverse, incl=incl, g=g, gl_row=g[last:last + 1],
                q=q_ref[:, h * dk:(h + 1) * dk] * (dk ** -0.5), k=k_ref[:, h * dk:(h + 1) * dk],
                v_bf=_bf(v_ref[:, h * dv:(h + 1) * dv]), s=state[z, h]))
    intra = [[] for _ in units]
    for blk in range(c // rb):
        r0, r1 = blk * rb, (blk + 1) * rb
        scores, cols = [], []
        for u in units:
            g = u["g"]
            mid = r0 + rb // 2
            if u["reverse"]:
                c0, c1 = r0, c
                ref = g[mid:mid + 1]
            else:
                c0, c1 = 0, r1
                ref = g[mid - 1:mid]
            qe = u["q"][r0:r1] * jnp.exp(g[r0:r1] - ref)
            ke = u["k"][c0:c1] * jnp.exp(ref - g[c0:c1])
            scores.append(jnp.where(u["incl"][r0:r1, c0:c1], _dot_nt(_bf(qe), _bf(ke)), 0.0))
            cols.append((c0, c1))
        for parts, u, a, (c0, c1) in zip(intra, units, scores, cols):
            parts.append(_dot(_bf(a), u["v_bf"][c0:c1]))
    inter = [_dot(_bf(u["q"] * jnp.exp(u["g"])), _bf(u["s"])) for u in units]
    for u, o_inter, parts in zip(units, inter, intra):
        h = u["h"]
        u["o_ref"][:, h * dv:(h + 1) * dv] = o_inter + jnp.concatenate(parts, axis=0)
    for u in units:
        kd = u["k"] * jnp.exp(u["gl_row"] - u["g"])
        gl_col = jnp.sum(jnp.where(eye, jnp.broadcast_to(u["gl_row"], (dk, dk)), 0.0), axis=1, keepdims=True)
        state[u["z"], u["h"]] = u["s"] * jnp.exp(gl_col) + _dot_tn(_bf(kd), u["v_bf"])
    _store_state(state, sout_ref, tiles, step)


def _gla(q, k, v, lr, w_gate, gate_bias, s0, tiles):
    nt = q.shape[0]
    nh, dk, dv = C_HEADS, C_DK, C_DV
    c = tiles.c
    wg = jnp.zeros((2, LANES, nh * dk), F32)
    for z in range(2):
        wg = wg.at[z, z * GATE_RANK:(z + 1) * GATE_RANK].set(w_gate[z])
    in_specs = (_dir_specs(tiles, c, ((nh * dk, 0), (nh * dk, 0), (nh * dv, 0), (LANES, 0)))
                + [pl.BlockSpec((2, LANES, nh * dk), lambda i: (0, 0, 0)),
                   pl.BlockSpec((2, 1, nh * dk), lambda i: (0, 0, 0)),
                   pl.BlockSpec((1, 2, nh, dk, dv), lambda i: (tiles.latent_seq(i), 0, 0, 0, 0))])
    kern = functools.partial(_gla_kernel, tiles=tiles, nh=nh, dk=dk, dv=dv, rb=GLA_ROW_BLOCK)
    return pl.pallas_call(
        kern,
        out_shape=[jax.ShapeDtypeStruct((nt, nh * dv), F32), jax.ShapeDtypeStruct((nt, nh * dv), F32),
                   jax.ShapeDtypeStruct((tiles.n_p, 2, nh, dk, dv), F32)],
        grid=(tiles.n,),
        in_specs=in_specs,
        out_specs=[pl.BlockSpec((c, nh * dv), lambda i: (tiles.row_block(i, False), 0)),
                   pl.BlockSpec((c, nh * dv), lambda i: (tiles.row_block(i, True), 0)),
                   pl.BlockSpec((1, 2, nh, dk, dv), lambda i: (tiles.ctx_seq(i), 0, 0, 0, 0))],
        scratch_shapes=[pltpu.VMEM((2, nh, dk, dv), F32)],
        compiler_params=_cparams(("arbitrary",)),
        name="gla",
    )(q, k, v, lr, q, k, v, lr, wg, gate_bias.reshape(2, 1, nh * dk), s0)


def kernel(x_prompt, x_sample, state_delta, cache_k, cache_v, state_gla, c, c_ctx, norm_g, ada_w, ada_b,
           ffn_w_gu, ffn_w_down, even_w_in, even_conv, even_a_log, even_dt_bias, even_onorm, even_sink,
           even_w_out, odd_w_in, odd_w_gate, odd_gate_bias, odd_onorm, odd_w_out, final_g):
    n_p, t_p, d = x_prompt.shape
    n_s, t_s, _ = x_sample.shape
    depth = norm_g.shape[0]
    np_rows, ns_rows = n_p * t_p, n_s * t_s
    assert np_rows % ROW_TILE == 0 and t_s % ROW_TILE == 0 and np_rows % t_s == 0
    assert t_p % CHUNK == 0 and t_s % CHUNK == 0 and t_s % GRID_W == 0
    rows = (np_rows, t_s)
    tiles = _Tiles(n_p, t_p, n_s, t_s, CHUNK)

    n_cond = 1 + n_s
    cond_rows = -(-n_cond // SUBLANES) * SUBLANES
    conds = jnp.concatenate([c_ctx[None, :], c, jnp.zeros((cond_rows - n_cond, d), F32)], axis=0)
    mods = _ada(conds, ada_w, ada_b)[:, :n_cond].reshape(depth, n_cond, N_MOD, d)

    w_gu = ffn_w_gu.astype(BF16)
    w_down = ffn_w_down.astype(BF16)

    xs = (x_prompt.reshape(np_rows, d), x_sample.reshape(ns_rows, d))
    new_delta, new_k, new_v, new_gla = [], [], [], []
    for l in range(depth):
        j = l // 2
        mod = mods[l]
        x = _ffn_half(xs, mod, 0, norm_g[l, 0], w_gu, w_down, (l, 0), rows)
        if l % 2 == 0:
            nh = A_HEADS
            w = even_w_in[j]
            o_qkv = 2 * nh * A_DK + nh * A_DV
            o_gate = o_qkv + nh * A_DV
            o_ba = o_gate + 4 * nh
            w_cat = jnp.concatenate([w[:, :o_gate], w[:, o_ba:], w[:, o_gate:o_ba],
                                     jnp.zeros((d, LANES - 4 * nh), F32)], axis=1).astype(BF16)
            widths = (o_qkv, nh * A_DV, B_Q_HEADS * HEAD_DIM, B_KV_HEADS * HEAD_DIM, B_KV_HEADS * HEAD_DIM, LANES)
            assert sum(widths) == w_cat.shape[1]
            qkv, gate, qb, kb, vb, ba = _mixer_in(x, mod, norm_g[l, 1], w_cat, widths, rows)
            qkv_n = _conv_qkv(qkv, even_conv[j], tiles)
            o_f, o_b, st = _delta_rule(qkv_n, ba, even_a_log[j], even_dt_bias[j], state_delta[:, j], tiles)
            cos, sin_signed = _rope_tables(t_s)
            att_p = _ctx_attention(qb, kb, vb, even_sink[j], n_p, t_p)
            att_s = _lat_attention(qb, kb, vb,
                                   cache_k[:, j].reshape(n_s, -1, B_KV_HEADS * HEAD_DIM),
                                   cache_v[:, j].reshape(n_s, -1, B_KV_HEADS * HEAD_DIM),
                                   even_sink[j], cos, sin_signed, np_rows, n_s, t_s)
            x = _mixer_out(x, mod, o_f, o_b, gate, (att_p, att_s), even_onorm[j], even_w_out[j].astype(BF16),
                           nh, A_DV, rows)
            new_delta.append(st)
            new_k.append(kb[:np_rows].reshape(n_p, t_p, B_KV_HEADS, HEAD_DIM))
            new_v.append(vb[:np_rows].reshape(n_p, t_p, B_KV_HEADS, HEAD_DIM))
        else:
            nh = C_HEADS
            w_cat = jnp.concatenate([odd_w_in[j], jnp.zeros((d, LANES - 2 * GATE_RANK), F32)], axis=1).astype(BF16)
            widths = (nh * C_DK, nh * C_DK, nh * C_DV, nh * C_DV, LANES)
            assert sum(widths) == w_cat.shape[1]
            q, k, v, g_out, lr = _mixer_in(x, mod, norm_g[l, 1], w_cat, widths, rows)
            o_f, o_b, st = _gla(q, k, v, lr, odd_w_gate[j], odd_gate_bias[j], state_gla[:, j], tiles)
            x = _mixer_out(x, mod, o_f, o_b, g_out, None, odd_onorm[j], odd_w_out[j].astype(BF16),
                           nh, C_DV, rows)
            new_gla.append(st)
        last = l == depth - 1
        xs = _ffn_half((x,), mod, 6, norm_g[l, 2], w_gu, w_down, (l, 1), rows,
                       final_g=final_g if last else None)
        if not last:
            xs = (xs,)

    y_prompt, y_sample = xs
    return (y_prompt.reshape(n_p, t_p, d), y_sample.reshape(n_s, t_s, d), jnp.stack(new_delta, axis=1),
            jnp.stack(new_k, axis=1), jnp.stack(new_v, axis=1), jnp.stack(new_gla, axis=1))
```

```python
import functools
import math

import jax
import jax.numpy as jnp
from jax import lax
from jax.experimental import pallas as pl
from jax.experimental.pallas import tpu as pltpu

F32 = jnp.float32
BF16 = jnp.bfloat16

EPS = 1e-6
N_MOD = 9
GRID_W = 64
HEAD_DIM = 128
A_HEADS = 4
A_DK = 128
A_DV = 128
SHORT_CONV = 5
B_Q_HEADS = 4
B_KV_HEADS = 2
WINDOW = 128
C_HEADS = 4
C_DK = 128
C_DV = 256
GATE_RANK = 16
GATE_NORM = 16.0
ROPE_THETA = 10000.0

LANES = 128
SUBLANES = 8
PROJ_ROW_TILE = 1024
PROJ_ROW_PARTS = 2
FFN_ROW_TILE = 512
FFN_ROW_PARTS = 1
FFN_CHUNKS = 11
CHUNK = 256
ATTN_ROW_TILE = 256
GLA_ROW_BLOCK = 128
VMEM_LIMIT = 56 * 1024 * 1024


def _cparams(sem, vmem=VMEM_LIMIT):
    return pltpu.CompilerParams(dimension_semantics=sem, vmem_limit_bytes=vmem)


def _resident(block_shape, index_map):
    return pl.BlockSpec(block_shape, index_map, pipeline_mode=pl.Buffered(1))


def _dot(a, b):
    return jnp.dot(a, b, preferred_element_type=F32)


def _dot_nt(a, b):
    return lax.dot_general(a, b, (((1,), (1,)), ((), ())), preferred_element_type=F32)


def _dot_tn(a, b):
    return lax.dot_general(a, b, (((0,), (0,)), ((), ())), preferred_element_type=F32)


def _bf(x):
    return x.astype(BF16)


def _sigmoid(x):
    return 0.5 * jnp.tanh(0.5 * x) + 0.5


def _silu(x):
    return x * _sigmoid(x)


def _softplus(x):
    return jnp.maximum(x, 0.0) + jnp.log1p(jnp.exp(-jnp.abs(x)))


def _rms(x):
    return x * lax.rsqrt(jnp.mean(x * x, axis=-1, keepdims=True) + EPS)


def _modnorm(x, g, shift, scale):
    return (_rms(x) * g) * (1.0 + scale) + shift


def _split3(x):
    hi = _bf(x)
    r1 = x - hi.astype(F32)
    mid = _bf(r1)
    lo = _bf(r1 - mid.astype(F32))
    return hi, mid, lo


def _cumsum_rows(tri_bf, x):
    hi, mid, lo = _split3(x)
    return (_dot(tri_bf, hi) + _dot(tri_bf, mid)) + _dot(tri_bf, lo)


def _order_masks(ri, ci, reverse):
    if reverse:
        return ri <= ci, ri < ci
    return ri >= ci, ri > ci


class _Tiles:
    def __init__(self, n_p, t_p, n_s, t_s, c):
        self.n_p, self.n_s, self.c = n_p, n_s, c
        self.per_p, self.per_s = t_p // c, t_s // c
        self.np_tiles = n_p * self.per_p
        self.n = self.np_tiles + n_s * self.per_s

    def is_ctx(self, i):
        return i < self.np_tiles

    def seq(self, i):
        return jnp.where(i < self.np_tiles, i // self.per_p, self.n_p + (i - self.np_tiles) // self.per_s)

    def pos(self, i):
        return jnp.where(i < self.np_tiles, i % self.per_p, (i - self.np_tiles) % self.per_s)

    def length(self, i):
        return jnp.where(i < self.np_tiles, self.per_p, self.per_s)

    def row_block(self, i, reverse):
        return i + self.length(i) - 1 - 2 * self.pos(i) if reverse else i

    def ctx_seq(self, i):
        return jnp.minimum(self.seq(i), self.n_p - 1)

    def latent_seq(self, i):
        return jnp.maximum(self.seq(i) - self.n_p, 0)


def _ada_kernel(c_ref, w_ref, b_ref, o_ref):
    s = _bf(_silu(c_ref[...]))
    o_ref[0] = _dot(s, _bf(w_ref[0])) + b_ref[0]


def _ada(cond, ada_w, ada_b):
    depth, d, n = ada_w.shape
    rows = cond.shape[0]
    tn = n // 4
    return pl.pallas_call(
        _ada_kernel,
        out_shape=jax.ShapeDtypeStruct((depth, rows, n), F32),
        grid=(depth, n // tn),
        in_specs=[pl.BlockSpec((rows, d), lambda l, j: (0, 0)),
                  pl.BlockSpec((1, d, tn), lambda l, j: (l, 0, j)),
                  pl.BlockSpec((1, 1, tn), lambda l, j: (l, 0, j))],
        out_specs=pl.BlockSpec((1, rows, tn), lambda l, j: (l, 0, j)),
        compiler_params=_cparams(("parallel", "parallel")),
        name="ada",
    )(cond, ada_w, ada_b.reshape(depth, 1, n))


def _cond_index(n_prompt_rows, dec_seq, tm):
    npt = n_prompt_rows // tm

    def cond(i):
        return jnp.where(i < npt, 0, 1 + ((i - npt) * tm) // dec_seq)

    return cond


def _mixer_residual(x, gate_mod, of_ref, ob_ref, gate_ref, on_ref, w_ref, extra, rs, nh, dv):
    od = of_ref[rs, :].astype(F32) + ob_ref[rs, :].astype(F32)
    gate = gate_ref[rs, :].astype(F32)
    y = None if extra is None else _dot(extra, w_ref[nh * dv:, :])
    per = max(1, (2 * LANES) // dv)
    for h0 in range(0, nh, per):
        mix = [_bf((_rms(od[:, h * dv:(h + 1) * dv]) * on_ref[...]) * _silu(gate[:, h * dv:(h + 1) * dv]))
               for h in range(h0, min(h0 + per, nh))]
        part = _dot(jnp.concatenate(mix, axis=1), w_ref[h0 * dv:min(h0 + per, nh) * dv, :])
        y = part if y is None else y + part
    return x + gate_mod * y


def _ffn_kernel(*refs, i0, d_ff, n_chunks, parts, npt, split_in, mixer, final):
    refs = list(refs)
    x_refs = [refs.pop(0) for _ in range(2 if split_in else 1)]
    mod_ref, g_ref, wgu_ref, wd_ref = (refs.pop(0) for _ in range(4))
    if mixer is not None:
        nh, dv, has_extra = mixer
        of_ref, ob_ref, gate_ref, on_ref, wout_ref = (refs.pop(0) for _ in range(5))
        extra_refs = [refs.pop(0) for _ in range(2 if has_extra else 0)]
    rest = refs
    i = pl.program_id(0)
    mod = mod_ref[0]
    ch = d_ff // n_chunks
    rp = x_refs[0].shape[0] // parts
    outs = []
    for p in range(parts):
        rs = slice(p * rp, (p + 1) * rp)
        if split_in:
            x = jnp.where(i < npt, x_refs[0][rs, :], x_refs[1][rs, :])
        else:
            x = x_refs[0][rs, :]
        if mixer is not None:
            extra = jnp.where(i < npt, extra_refs[0][rs, :], extra_refs[1][rs, :]) if has_extra else None
            x = _mixer_residual(x, mod[i0 - 1:i0], of_ref, ob_ref, gate_ref, on_ref, wout_ref, extra, rs, nh, dv)
        h = _bf(_modnorm(x, g_ref[...], mod[i0:i0 + 1], mod[i0 + 1:i0 + 2]))
        acts = []
        for c in range(n_chunks):
            gt = _dot(h, wgu_ref[:, c * ch:(c + 1) * ch])
            up = _dot(h, wgu_ref[:, d_ff + c * ch:d_ff + (c + 1) * ch])
            acts.append(_bf(_silu(gt) * up))
        out = x + (0.5 * mod[i0 + 2:i0 + 3]) * _dot(jnp.concatenate(acts, axis=1), wd_ref[...])
        if final:
            outs.append((rs, _rms(out) * rest[0][...]))
        else:
            rest[0][rs, :] = out
    if final:
        _, yp_ref, ys_ref = rest

        @pl.when(i < npt)
        def _():
            for rs, out in outs:
                yp_ref[rs, :] = out

        @pl.when(i >= npt)
        def _():
            for rs, out in outs:
                ys_ref[rs, :] = out


def _ffn_half(xs, mod, i0, g, w_gu, w_down, widx, rows, mixer=None, final_g=None):
    split_in = len(xs) == 2
    d = xs[0].shape[1]
    nt = sum(x.shape[0] for x in xs)
    d_ff = w_down.shape[-2]
    tm = FFN_ROW_TILE
    np_rows = rows[0]
    assert np_rows % tm == 0 and rows[1] % tm == 0 and d_ff % (FFN_CHUNKS * LANES) == 0
    cond = _cond_index(*rows, tm)
    npt = np_rows // tm
    final = final_g is not None
    ctx_map = lambda i: (jnp.minimum(i, npt - 1), 0)
    lat_map = lambda i: (jnp.maximum(i - npt, 0), 0)
    mix_specs, mix_args, mix_cfg = [], [], None
    if mixer is not None:
        o_f, o_b, gate, onorm, w_out, nh, dv, extra = mixer
        assert i0 >= 1
        wa = nh * dv
        row = lambda i: (i, 0)
        mix_specs = [pl.BlockSpec((tm, wa), row), pl.BlockSpec((tm, wa), row), pl.BlockSpec((tm, wa), row),
                     pl.BlockSpec((1, dv), lambda i: (0, 0)), _resident(w_out.shape, lambda i: (0, 0))]
        mix_args = [o_f, o_b, gate, onorm.reshape(1, dv), w_out]
        if extra is not None:
            we = extra[0].shape[1]
            mix_specs += [pl.BlockSpec((tm, we), ctx_map), pl.BlockSpec((tm, we), lat_map)]
            mix_args += list(extra)
        mix_cfg = (nh, dv, extra is not None)
    kern = functools.partial(_ffn_kernel, i0=i0, d_ff=d_ff, n_chunks=FFN_CHUNKS, parts=FFN_ROW_PARTS, npt=npt,
                             split_in=split_in, mixer=mix_cfg, final=final)
    if split_in:
        x_specs = [pl.BlockSpec((tm, d), ctx_map), pl.BlockSpec((tm, d), lat_map)]
    else:
        x_specs = [pl.BlockSpec((tm, d), lambda i: (i, 0))]
    in_specs = x_specs + [pl.BlockSpec((1, N_MOD, d), lambda i: (cond(i), 0, 0)),
                          pl.BlockSpec((1, d), lambda i: (0, 0)),
                          _resident((None, None, d, 2 * d_ff), lambda i: widx + (0, 0)),
                          _resident((None, None, d_ff, d), lambda i: widx + (0, 0))]
    args = list(xs) + [mod, g.reshape(1, d), w_gu, w_down] + mix_args
    in_specs += mix_specs
    if final:
        in_specs.append(pl.BlockSpec((1, d), lambda i: (0, 0)))
        args.append(final_g.reshape(1, d))
        out_shape = [jax.ShapeDtypeStruct((np_rows, d), F32), jax.ShapeDtypeStruct((nt - np_rows, d), F32)]
        out_specs = [pl.BlockSpec((tm, d), ctx_map), pl.BlockSpec((tm, d), lat_map)]
    else:
        out_shape = jax.ShapeDtypeStruct((nt, d), F32)
        out_specs = pl.BlockSpec((tm, d), lambda i: (i, 0))
    return pl.pallas_call(
        kern,
        out_shape=out_shape,
        grid=(nt // tm,),
        in_specs=in_specs,
        out_specs=out_specs,
        compiler_params=_cparams(("arbitrary",)),
        name="ffn_half",
    )(*args)


def _proj_kernel(x_ref, mod_ref, g_ref, w_ref, *o_refs, i0, widths, parts):
    mod = mod_ref[0]
    rp = x_ref.shape[0] // parts
    for p in range(parts):
        rs = slice(p * rp, (p + 1) * rp)
        h = _bf(_modnorm(x_ref[rs, :], g_ref[...], mod[i0:i0 + 1], mod[i0 + 1:i0 + 2]))
        off = 0
        for o_ref, wd in zip(o_refs, widths):
            o_ref[rs, :] = _dot(h, w_ref[:, off:off + wd]).astype(o_ref.dtype)
            off += wd


def _mixer_in(x, mod, g, w, widths, dtypes, rows):
    nt, d = x.shape
    n = sum(widths)
    tm = PROJ_ROW_TILE
    assert rows[0] % tm == 0 and rows[1] % tm == 0
    cond = _cond_index(*rows, tm)
    kern = functools.partial(_proj_kernel, i0=3, widths=tuple(widths), parts=PROJ_ROW_PARTS)
    return pl.pallas_call(
        kern,
        out_shape=[jax.ShapeDtypeStruct((nt, wd), dt) for wd, dt in zip(widths, dtypes)],
        grid=(nt // tm,),
        in_specs=[pl.BlockSpec((tm, d), lambda i: (i, 0)),
                  pl.BlockSpec((1, N_MOD, d), lambda i: (cond(i), 0, 0)),
                  pl.BlockSpec((1, d), lambda i: (0, 0)),
                  _resident((d, n), lambda i: (0, 0))],
        out_specs=[pl.BlockSpec((tm, wd), lambda i: (i, 0)) for wd in widths],
        compiler_params=_cparams(("parallel",)),
        name="mixer_in",
    )(x, mod, g.reshape(1, d), w)


def _conv_kernel(prev_ref, x_ref, next_ref, w_ref, o_ref, *, tiles, dk):
    r = pl.program_id(0)
    c, width = x_ref.shape
    pad = (SHORT_CONV - 1) // 2
    has_prev = jnp.where(tiles.pos(r) > 0, 1.0, 0.0)
    has_next = jnp.where(tiles.pos(r) < tiles.length(r) - 1, 1.0, 0.0)
    ext = c + 2 * SUBLANES
    for hh in range(width // dk):
        part = hh // (width // (3 * dk))
        sl = slice(hh * dk, (hh + 1) * dk)
        xe = jnp.concatenate([prev_ref[:, sl] * has_prev, x_ref[:, sl], next_ref[:, sl] * has_next], axis=0)
        w = w_ref[:, sl]
        acc = None
        for j in range(SHORT_CONV):
            sh = pad - j
            xs = xe if sh == 0 else pltpu.roll(xe, sh % ext, axis=0)
            term = xs[SUBLANES:SUBLANES + c] * w[j:j + 1]
            acc = term if acc is None else acc + term
        y = _silu(acc)
        if part < 2:
            nrm = lax.rsqrt(jnp.sum(y * y, axis=-1, keepdims=True) + EPS)
            y = y * (nrm * (dk ** -0.5) if part == 0 else nrm)
        o_ref[:, sl] = y


def _conv_qkv(qkv, conv_w, tiles):
    nt, width = qkv.shape
    c = tiles.c
    per = c // SUBLANES
    n8 = nt // SUBLANES
    kern = functools.partial(_conv_kernel, tiles=tiles, dk=A_DK)
    return pl.pallas_call(
        kern,
        out_shape=jax.ShapeDtypeStruct((nt, width), F32),
        grid=(tiles.n,),
        in_specs=[pl.BlockSpec((SUBLANES, width), lambda r: (jnp.maximum(r * per - 1, 0), 0)),
                  pl.BlockSpec((c, width), lambda r: (r, 0)),
                  pl.BlockSpec((SUBLANES, width), lambda r: (jnp.minimum((r + 1) * per, n8 - 1), 0)),
                  pl.BlockSpec((SHORT_CONV, width), lambda r: (0, 0))],
        out_specs=pl.BlockSpec((c, width), lambda r: (r, 0)),
        compiler_params=_cparams(("parallel",)),
        name="conv_qkv",
    )(qkv, qkv, qkv, conv_w)


def _load_state(state, s0_ref, tiles, i):
    first = tiles.pos(i) == 0

    @pl.when(first & tiles.is_ctx(i))
    def _():
        state[...] = jnp.zeros(state.shape, F32)

    @pl.when(first & jnp.logical_not(tiles.is_ctx(i)))
    def _():
        state[...] = s0_ref[0]


def _store_state(state, out_ref, tiles, i):
    @pl.when((tiles.pos(i) == tiles.length(i) - 1) & tiles.is_ctx(i))
    def _():
        out_ref[0] = state[...]


def _dir_specs(tiles, c, cols):
    specs = []
    for reverse in (False, True):
        for width, col in cols:
            specs.append(pl.BlockSpec((c, width), functools.partial(
                lambda rev, cc, i: (tiles.row_block(i, rev), cc), reverse, col)))
    return specs


def _pair_dot(a, b):
    n = a.shape[0]
    a_bf, b_bf = _bf(a), _bf(b)
    z = jnp.zeros((n, n), BF16)
    b_diag = jnp.concatenate([jnp.concatenate([b_bf[:, :n], z], axis=1),
                              jnp.concatenate([z, b_bf[:, n:]], axis=1)], axis=0)
    return _dot(a_bf, b_diag)


def _unit_tri_solves(ls, rs, reverse_flags):
    c = ls[0].shape[0]
    n = c // 2
    ri = lax.broadcasted_iota(jnp.int32, (n, c), 0)
    ci = lax.broadcasted_iota(jnp.int32, (n, c), 1) & (n - 1)
    pairs = [jnp.concatenate([l[:n, :n], l[n:, n:]], axis=1) for l in ls]
    shift = SUBLANES.bit_length() - 1
    same = (ri >> shift) == (ci >> shift)
    ms = [jnp.where(same, -lp, 0.0) for lp in pairs]
    ps = [_pair_dot(m, m) for m in ms]
    ns = [m + p + _pair_dot(m, p) for m, p in zip(ms, ps)]
    ps = [_pair_dot(p, p) for p in ps]
    ns = [nv + p + _pair_dot(nv, p) for nv, p in zip(ns, ps)]
    while (1 << shift) < n:
        lvl = ((ri >> (shift + 1)) == (ci >> (shift + 1))) & ((ri >> shift) != (ci >> shift))
        cls = [jnp.where(lvl, lp, 0.0) for lp in pairs]
        ys = [cl + _pair_dot(cl, nv) for cl, nv in zip(cls, ns)]
        ns = [nv - (y + _pair_dot(nv, y)) for nv, y in zip(ns, ys)]
        shift += 1
    firsts, seconds = [], []
    for l, r, nv, rev in zip(ls, rs, ns, reverse_flags):
        if rev:
            firsts.append((r[n:], _bf(nv[:, n:])))
            seconds.append((r[:n], _bf(nv[:, :n]), _bf(l[:n, n:])))
        else:
            firsts.append((r[:n], _bf(nv[:, :n])))
            seconds.append((r[n:], _bf(nv[:, n:]), _bf(l[n:, :n])))
    xas = [ra + _dot(na, _bf(ra)) for ra, na in firsts]
    ts = [rb - _dot(lba, _bf(xa)) for (rb, _, lba), xa in zip(seconds, xas)]
    xbs = [t + _dot(nb, _bf(t)) for (_, nb, _), t in zip(seconds, ts)]
    return [jnp.concatenate([xb, xa] if rev else [xa, xb], axis=0) for xa, xb, rev in zip(xas, xbs, reverse_flags)]


def _delta_kernel(qf_ref, kf_ref, vf_ref, baf_ref, qb_ref, kb_ref, vb_ref, bab_ref, al_ref, dt_ref, s0_ref,
                  of_ref, ob_ref, sout_ref, state, *, tiles, nh, dk, dv):
    step = pl.program_id(0)
    _load_state(state, s0_ref, tiles, step)
    c = qf_ref.shape[0]
    ri = lax.broadcasted_iota(jnp.int32, (c, c), 0)
    ci = lax.broadcasted_iota(jnp.int32, (c, c), 1)
    units = []
    for z, (q_ref, k_ref, v_ref, ba_ref, o_ref) in enumerate(
            ((qf_ref, kf_ref, vf_ref, baf_ref, of_ref), (qb_ref, kb_ref, vb_ref, bab_ref, ob_ref))):
        reverse = z == 1
        incl, strict = _order_masks(ri, ci, reverse)
        ba = ba_ref[...]
        g_col = -jnp.exp(al_ref[...]) * _softplus(ba + dt_ref[...])
        beta_col = _sigmoid(ba)
        gc_col = _cumsum_rows(_bf(incl.astype(F32)), g_col)
        gc_row = gc_col.T
        last = 0 if reverse else c - 1
        for h in range(nh):
            cb = z * nh + h
            cg = 2 * nh + cb
            gcc = gc_col[:, cg:cg + 1]
            gcr = gc_row[cg:cg + 1, :]
            units.append(dict(
                z=z, h=h, o_ref=o_ref, strict=strict, gcc=gcc, gl=gcc[last:last + 1],
                beta=beta_col[:, cb:cb + 1], egc=jnp.exp(gcc),
                decay=jnp.where(incl, jnp.exp(jnp.where(incl, gcc - gcr, 0.0)), 0.0),
                q=q_ref[:, h * dk:(h + 1) * dk], k=k_ref[:, h * dk:(h + 1) * dk], v=v_ref[:, h * dv:(h + 1) * dv]))
    for u in units:
        u["kb"] = u["k"] * u["beta"]
        u["k_bf"] = _bf(u["k"])
    kks = [_dot_nt(_bf(u["kb"]), u["k_bf"]) for u in units]
    qks = [_dot_nt(_bf(u["q"]), u["k_bf"]) * u["decay"] for u in units]
    rs = _unit_tri_solves([jnp.where(u["strict"], kk * u["decay"], 0.0) for u, kk in zip(units, kks)],
                          [jnp.concatenate([u["v"] * u["beta"], u["kb"] * u["egc"]], axis=1) for u in units],
                          [u["z"] == 1 for u in units])
    ss = [state[u["z"], u["h"]] for u in units]
    ss_bf = [_bf(s) for s in ss]
    v_news_bf = [_bf(r[:, :dv] - _dot(_bf(r[:, dv:]), sb)) for r, sb in zip(rs, ss_bf)]
    for u, sb, qk, vnb in zip(units, ss_bf, qks, v_news_bf):
        h = u["h"]
        u["o_ref"][:, h * dv:(h + 1) * dv] = _bf(_dot(_bf(u["q"] * u["egc"]), sb) + _dot(_bf(qk), vnb))
    for u, s, vnb in zip(units, ss, v_news_bf):
        kd = u["k"] * jnp.exp(u["gl"] - u["gcc"])
        state[u["z"], u["h"]] = s * jnp.exp(u["gl"]) + _dot_tn(_bf(kd), vnb)
    _store_state(state, sout_ref, tiles, step)


def _delta_rule(qkv, ba, a_log, dt_bias, s0, tiles):
    nt = qkv.shape[0]
    nh, dk, dv = A_HEADS, A_DK, A_DV
    c = tiles.c
    pad = LANES - 4 * nh
    al = jnp.concatenate([jnp.zeros((2 * nh,), F32), a_log.reshape(-1), jnp.zeros((pad,), F32)])
    dt = jnp.concatenate([jnp.zeros((2 * nh,), F32), dt_bias.reshape(-1), jnp.zeros((pad,), F32)])
    const = lambda i: (0, 0)
    in_specs = (_dir_specs(tiles, c, ((nh * dk, 0), (nh * dk, 1), (nh * dv, 2), (LANES, 0)))
                + [pl.BlockSpec((1, LANES), const), pl.BlockSpec((1, LANES), const),
                   pl.BlockSpec((1, 2, nh, dk, dv), lambda i: (tiles.latent_seq(i), 0, 0, 0, 0))])
    kern = functools.partial(_delta_kernel, tiles=tiles, nh=nh, dk=dk, dv=dv)
    return pl.pallas_call(
        kern,
        out_shape=[jax.ShapeDtypeStruct((nt, nh * dv), BF16), jax.ShapeDtypeStruct((nt, nh * dv), BF16),
                   jax.ShapeDtypeStruct((tiles.n_p, 2, nh, dk, dv), F32)],
        grid=(tiles.n,),
        in_specs=in_specs,
        out_specs=[pl.BlockSpec((c, nh * dv), lambda i: (tiles.row_block(i, False), 0)),
                   pl.BlockSpec((c, nh * dv), lambda i: (tiles.row_block(i, True), 0)),
                   pl.BlockSpec((1, 2, nh, dk, dv), lambda i: (tiles.ctx_seq(i), 0, 0, 0, 0))],
        scratch_shapes=[pltpu.VMEM((2, nh, dk, dv), F32)],
        compiler_params=_cparams(("arbitrary",)),
        name="delta_rule",
    )(qkv, qkv, qkv, ba, qkv, qkv, qkv, ba, al.reshape(1, LANES), dt.reshape(1, LANES), s0)


def _ctx_attn_kernel(sink_ref, q_ref, k_ref, v_ref, o_ref, *, n_kv, grp, dh):
    scale = dh ** -0.5
    for hk in range(n_kv):
        k = _bf(k_ref[:, hk * dh:(hk + 1) * dh])
        v = _bf(v_ref[:, hk * dh:(hk + 1) * dh])
        for g in range(grp):
            h = hk * grp + g
            sk = sink_ref[h]
            s = _dot_nt(_bf(q_ref[:, h * dh:(h + 1) * dh] * scale), k)
            m = jnp.maximum(jnp.max(s, axis=-1, keepdims=True), sk)
            p = jnp.exp(s - m)
            den = jnp.sum(p, axis=-1, keepdims=True) + jnp.exp(sk - m)
            o_ref[:, h * dh:(h + 1) * dh] = _bf(_dot(_bf(p), v) / den)


def _ctx_attention(qb, kb, vb, sink, n_seq, t):
    wq = qb.shape[1]
    wk = kb.shape[1]
    kern = functools.partial(_ctx_attn_kernel, n_kv=B_KV_HEADS, grp=B_Q_HEADS // B_KV_HEADS, dh=HEAD_DIM)
    return pl.pallas_call(
        kern,
        out_shape=jax.ShapeDtypeStruct((n_seq * t, wq), BF16),
        grid=(n_seq,),
        in_specs=[pl.BlockSpec(memory_space=pltpu.SMEM),
                  pl.BlockSpec((t, wq), lambda b: (b, 0)),
                  pl.BlockSpec((t, wk), lambda b: (b, 0)),
                  pl.BlockSpec((t, wk), lambda b: (b, 0))],
        out_specs=pl.BlockSpec((t, wq), lambda b: (b, 0)),
        compiler_params=_cparams(("parallel",)),
        name="ctx_attention",
    )(sink, qb, kb, vb)


def _rope(x, cos, sin_signed):
    lane = lax.broadcasted_iota(jnp.int32, x.shape, 1)
    quarter = HEAD_DIM // 4
    partner = jnp.where((lane % (2 * quarter)) < quarter,
                        pltpu.roll(x, HEAD_DIM - quarter, axis=1), pltpu.roll(x, quarter, axis=1))
    return x * cos + partner * sin_signed


def _lat_attn_kernel(sink_ref, q_ref, k_ref, v_ref, kc_ref, vc_ref, cq_ref, sq_ref, ck_ref, sk_ref,
                     o_ref, *, n_kv, grp, dh, window):
    scale = dh ** -0.5
    tq = q_ref.shape[0]
    t = k_ref.shape[0]
    span = min(t, tq + 2 * window)
    q0 = pl.program_id(1) * tq
    start = pl.multiple_of(jnp.clip(q0 - window, 0, t - span), math.gcd(tq, window))
    rows = pl.ds(start, span)
    qpos = q0 + lax.broadcasted_iota(jnp.int32, (tq, span), 0)
    kpos = start + lax.broadcasted_iota(jnp.int32, (tq, span), 1)
    valid = jnp.abs(qpos - kpos) <= window
    cq, sq = cq_ref[...], sq_ref[...]
    ck, sk_t = ck_ref[rows, :], sk_ref[rows, :]
    kv = []
    for hk in range(n_kv):
        hs = slice(hk * dh, (hk + 1) * dh)
        kv.append((_bf(_rope(k_ref[rows, hs], ck, sk_t)), _bf(v_ref[rows, hs]), _bf(kc_ref[0, :, hs]),
                   _bf(vc_ref[0, :, hs])))
    heads = range(n_kv * grp)
    sinks = [sink_ref[h] for h in heads]
    qs = [q_ref[:, h * dh:(h + 1) * dh] * scale for h in heads]
    s_locs = [jnp.where(valid, _dot_nt(_bf(_rope(q, cq, sq)), kv[h // grp][0]), -jnp.inf) for h, q in zip(heads, qs)]
    s_ctxs = [_dot_nt(_bf(q), kv[h // grp][2]) for h, q in zip(heads, qs)]
    ms = [jnp.maximum(jnp.maximum(jnp.max(sl, axis=-1, keepdims=True), jnp.max(sc, axis=-1, keepdims=True)), sk)
          for sl, sc, sk in zip(s_locs, s_ctxs, sinks)]
    p_locs = [jnp.exp(sl - m) for sl, m in zip(s_locs, ms)]
    p_ctxs = [jnp.exp(sc - m) for sc, m in zip(s_ctxs, ms)]
    dens = [jnp.sum(pl_, axis=-1, keepdims=True) + jnp.sum(pc, axis=-1, keepdims=True) + jnp.exp(sk - m)
            for pl_, pc, sk, m in zip(p_locs, p_ctxs, sinks, ms)]
    for h, pl_, pc, den in zip(heads, p_locs, p_ctxs, dens):
        o_ref[:, h * dh:(h + 1) * dh] = _bf((_dot(_bf(pl_), kv[h // grp][1]) + _dot(_bf(pc), kv[h // grp][3])) / den)


def _lat_attention(qb, kb, vb, k_ctx, v_ctx, sink, cos, sin_signed, row0, n_seq, t):
    wq = qb.shape[1]
    wk = kb.shape[1]
    tq = ATTN_ROW_TILE
    assert t % tq == 0 and row0 % t == 0
    nq = t // tq
    base_q = row0 // tq
    base_t = row0 // t
    past = k_ctx.shape[1]
    kern = functools.partial(_lat_attn_kernel, n_kv=B_KV_HEADS, grp=B_Q_HEADS // B_KV_HEADS, dh=HEAD_DIM,
                             window=WINDOW)
    return pl.pallas_call(
        kern,
        out_shape=jax.ShapeDtypeStruct((n_seq * t, wq), BF16),
        grid=(n_seq, nq),
        in_specs=[pl.BlockSpec(memory_space=pltpu.SMEM),
                  pl.BlockSpec((tq, wq), lambda b, i: (base_q + b * nq + i, 0)),
                  pl.BlockSpec((t, wk), lambda b, i: (base_t + b, 0)),
                  pl.BlockSpec((t, wk), lambda b, i: (base_t + b, 0)),
                  pl.BlockSpec((1, past, wk), lambda b, i: (b, 0, 0)),
                  pl.BlockSpec((1, past, wk), lambda b, i: (b, 0, 0)),
                  pl.BlockSpec((tq, HEAD_DIM), lambda b, i: (i, 0)),
                  pl.BlockSpec((tq, HEAD_DIM), lambda b, i: (i, 0)),
                  pl.BlockSpec((t, HEAD_DIM), lambda b, i: (0, 0)),
                  pl.BlockSpec((t, HEAD_DIM), lambda b, i: (0, 0))],
        out_specs=pl.BlockSpec((tq, wq), lambda b, i: (b * nq + i, 0)),
        compiler_params=_cparams(("parallel", "parallel")),
        name="latent_attention",
    )(sink, qb, kb, vb, k_ctx, v_ctx, cos, sin_signed, cos, sin_signed)


def _rope_tables(t):
    half = HEAD_DIM // 2
    quarter = half // 2
    pos = jnp.arange(t)
    row = (pos // GRID_W).astype(F32)
    col = (pos % GRID_W).astype(F32)
    inv = ROPE_THETA ** (-jnp.arange(quarter, dtype=F32) / quarter)
    ang_r = row[:, None] * inv[None, :]
    ang_c = col[:, None] * inv[None, :]
    cos = jnp.concatenate([jnp.cos(ang_r), jnp.cos(ang_r), jnp.cos(ang_c), jnp.cos(ang_c)], axis=-1)
    sin = jnp.concatenate([-jnp.sin(ang_r), jnp.sin(ang_r), -jnp.sin(ang_c), jnp.sin(ang_c)], axis=-1)
    return cos, sin


def _gla_kernel(qf_ref, kf_ref, vf_ref, lrf_ref, qb_ref, kb_ref, vb_ref, lrb_ref, wg_ref, bias_ref, s0_ref,
                of_ref, ob_ref, sout_ref, state, *, tiles, nh, dk, dv, rb):
    step = pl.program_id(0)
    _load_state(state, s0_ref, tiles, step)
    c = qf_ref.shape[0]
    ri = lax.broadcasted_iota(jnp.int32, (c, c), 0)
    ci = lax.broadcasted_iota(jnp.int32, (c, c), 1)
    eye = (lax.broadcasted_iota(jnp.int32, (dk, dk), 0) == lax.broadcasted_iota(jnp.int32, (dk, dk), 1))
    units = []
    for z, (q_ref, k_ref, v_ref, lr_ref, o_ref) in enumerate(
            ((qf_ref, kf_ref, vf_ref, lrf_ref, of_ref), (qb_ref, kb_ref, vb_ref, lrb_ref, ob_ref))):
        reverse = z == 1
        incl, _ = _order_masks(ri, ci, reverse)
        x = _dot(_bf(lr_ref[...]), _bf(wg_ref[z])) + bias_ref[z]
        gk = (jnp.minimum(x, 0.0) - jnp.log1p(jnp.exp(-jnp.abs(x)))) * (1.0 / GATE_NORM)
        gcum = _cumsum_rows(_bf(incl.astype(F32)), gk)
        last = 0 if reverse else c - 1
        for h in range(nh):
            g = gcum[:, h * dk:(h + 1) * dk]
            units.append(dict(
                z=z, h=h, o_ref=o_ref, reverse=reverse, incl=incl, g=g, gl_row=g[last:last + 1],
                q=q_ref[:, h * dk:(h + 1) * dk] * (dk ** -0.5), k=k_ref[:, h * dk:(h + 1) * dk],
                v_bf=v_ref[:, h * dv:(h + 1) * dv], s=state[z, h]))
    intra = [[] for _ in units]
    for blk in range(c // rb):
        r0, r1 = blk * rb, (blk + 1) * rb
        scores, cols = [], []
        for u in units:
            g = u["g"]
            mid = r0 + rb // 2
            if u["reverse"]:
                c0, c1 = r0, c
                ref = g[mid:mid + 1]
            else:
                c0, c1 = 0, r1
                ref = g[mid - 1:mid]
            qe = u["q"][r0:r1] * jnp.exp(g[r0:r1] - ref)
            ke = u["k"][c0:c1] * jnp.exp(ref - g[c0:c1])
            scores.append(jnp.where(u["incl"][r0:r1, c0:c1], _dot_nt(_bf(qe), _bf(ke)), 0.0))
            cols.append((c0, c1))
        for parts, u, a, (c0, c1) in zip(intra, units, scores, cols):
            parts.append(_dot(_bf(a), u["v_bf"][c0:c1]))
    inter = [_dot(_bf(u["q"] * jnp.exp(u["g"])), _bf(u["s"])) for u in units]
    for u, o_inter, parts in zip(units, inter, intra):
        h = u["h"]
        u["o_ref"][:, h * dv:(h + 1) * dv] = _bf(o_inter + jnp.concatenate(parts, axis=0))
    for u in units:
        kd = u["k"] * jnp.exp(u["gl_row"] - u["g"])
        gl_col = jnp.sum(jnp.where(eye, jnp.broadcast_to(u["gl_row"], (dk, dk)), 0.0), axis=1, keepdims=True)
        state[u["z"], u["h"]] = u["s"] * jnp.exp(gl_col) + _dot_tn(_bf(kd), u["v_bf"])
    _store_state(state, sout_ref, tiles, step)


def _gla(q, k, v, lr, w_gate, gate_bias, s0, tiles):
    nt = q.shape[0]
    nh, dk, dv = C_HEADS, C_DK, C_DV
    c = tiles.c
    wg = jnp.zeros((2, LANES, nh * dk), F32)
    for z in range(2):
        wg = wg.at[z, z * GATE_RANK:(z + 1) * GATE_RANK].set(w_gate[z])
    in_specs = (_dir_specs(tiles, c, ((nh * dk, 0), (nh * dk, 0), (nh * dv, 0), (LANES, 0)))
                + [pl.BlockSpec((2, LANES, nh * dk), lambda i: (0, 0, 0)),
                   pl.BlockSpec((2, 1, nh * dk), lambda i: (0, 0, 0)),
                   pl.BlockSpec((1, 2, nh, dk, dv), lambda i: (tiles.latent_seq(i), 0, 0, 0, 0))])
    kern = functools.partial(_gla_kernel, tiles=tiles, nh=nh, dk=dk, dv=dv, rb=GLA_ROW_BLOCK)
    return pl.pallas_call(
        kern,
        out_shape=[jax.ShapeDtypeStruct((nt, nh * dv), BF16), jax.ShapeDtypeStruct((nt, nh * dv), BF16),
                   jax.ShapeDtypeStruct((tiles.n_p, 2, nh, dk, dv), F32)],
        grid=(tiles.n,),
        in_specs=in_specs,
        out_specs=[pl.BlockSpec((c, nh * dv), lambda i: (tiles.row_block(i, False), 0)),
                   pl.BlockSpec((c, nh * dv), lambda i: (tiles.row_block(i, True), 0)),
                   pl.BlockSpec((1, 2, nh, dk, dv), lambda i: (tiles.ctx_seq(i), 0, 0, 0, 0))],
        scratch_shapes=[pltpu.VMEM((2, nh, dk, dv), F32)],
        compiler_params=_cparams(("arbitrary",)),
        name="gla",
    )(q, k, v, lr, q, k, v, lr, wg, gate_bias.reshape(2, 1, nh * dk), s0)


def kernel(x_prompt, x_sample, state_delta, cache_k, cache_v, state_gla, c, c_ctx, norm_g, ada_w, ada_b,
           ffn_w_gu, ffn_w_down, even_w_in, even_conv, even_a_log, even_dt_bias, even_onorm, even_sink,
           even_w_out, odd_w_in, odd_w_gate, odd_gate_bias, odd_onorm, odd_w_out, final_g):
    n_p, t_p, d = x_prompt.shape
    n_s, t_s, _ = x_sample.shape
    depth = norm_g.shape[0]
    np_rows, ns_rows = n_p * t_p, n_s * t_s
    assert np_rows % t_s == 0
    assert t_p % CHUNK == 0 and t_s % CHUNK == 0 and t_s % GRID_W == 0
    rows = (np_rows, t_s)
    tiles = _Tiles(n_p, t_p, n_s, t_s, CHUNK)

    n_cond = 1 + n_s
    cond_rows = -(-n_cond // SUBLANES) * SUBLANES
    conds = jnp.concatenate([c_ctx[None, :], c, jnp.zeros((cond_rows - n_cond, d), F32)], axis=0)
    mods = _ada(conds, ada_w, ada_b)[:, :n_cond].reshape(depth, n_cond, N_MOD, d)

    w_gu = ffn_w_gu.astype(BF16)
    w_down = ffn_w_down.astype(BF16)

    xs = (x_prompt.reshape(np_rows, d), x_sample.reshape(ns_rows, d))
    new_delta, new_k, new_v, new_gla = [], [], [], []
    for l in range(depth):
        j = l // 2
        mod = mods[l]
        x = _ffn_half(xs, mod, 0, norm_g[l, 0], w_gu, w_down, (l, 0), rows)
        if l % 2 == 0:
            nh = A_HEADS
            w = even_w_in[j]
            o_qkv = 2 * nh * A_DK + nh * A_DV
            o_gate = o_qkv + nh * A_DV
            o_ba = o_gate + 4 * nh
            w_cat = jnp.concatenate([w[:, :o_gate], w[:, o_ba:], w[:, o_gate:o_ba],
                                     jnp.zeros((d, LANES - 4 * nh), F32)], axis=1).astype(BF16)
            widths = (o_qkv, nh * A_DV, B_Q_HEADS * HEAD_DIM, B_KV_HEADS * HEAD_DIM, B_KV_HEADS * HEAD_DIM, LANES)
            assert sum(widths) == w_cat.shape[1]
            dtypes = (F32, BF16, F32, F32, F32, F32)
            qkv, gate, qb, kb, vb, ba = _mixer_in(x, mod, norm_g[l, 1], w_cat, widths, dtypes, rows)
            qkv_n = _conv_qkv(qkv, even_conv[j], tiles)
            o_f, o_b, st = _delta_rule(qkv_n, ba, even_a_log[j], even_dt_bias[j], state_delta[:, j], tiles)
            cos, sin_signed = _rope_tables(t_s)
            att_p = _ctx_attention(qb, kb, vb, even_sink[j], n_p, t_p)
            att_s = _lat_attention(qb, kb, vb,
                                   cache_k[:, j].reshape(n_s, -1, B_KV_HEADS * HEAD_DIM),
                                   cache_v[:, j].reshape(n_s, -1, B_KV_HEADS * HEAD_DIM),
                                   even_sink[j], cos, sin_signed, np_rows, n_s, t_s)
            mixer = (o_f, o_b, gate, even_onorm[j], even_w_out[j].astype(BF16), nh, A_DV, (att_p, att_s))
            new_delta.append(st)
            new_k.append(kb[:np_rows].reshape(n_p, t_p, B_KV_HEADS, HEAD_DIM))
            new_v.append(vb[:np_rows].reshape(n_p, t_p, B_KV_HEADS, HEAD_DIM))
        else:
            nh = C_HEADS
            w_cat = jnp.concatenate([odd_w_in[j], jnp.zeros((d, LANES - 2 * GATE_RANK), F32)], axis=1).astype(BF16)
            widths = (nh * C_DK, nh * C_DK, nh * C_DV, nh * C_DV, LANES)
            assert sum(widths) == w_cat.shape[1]
            dtypes = (F32, F32, BF16, BF16, F32)
            q, k, v, g_out, lr = _mixer_in(x, mod, norm_g[l, 1], w_cat, widths, dtypes, rows)
            o_f, o_b, st = _gla(q, k, v, lr, odd_w_gate[j], odd_gate_bias[j], state_gla[:, j], tiles)
            mixer = (o_f, o_b, g_out, odd_onorm[j], odd_w_out[j].astype(BF16), nh, C_DV, None)
            new_gla.append(st)
        last = l == depth - 1
        xs = _ffn_half((x,), mod, 6, norm_g[l, 2], w_gu, w_down, (l, 1), rows, mixer=mixer,
                       final_g=final_g if last else None)
        if not last:
            xs = (xs,)

    y_prompt, y_sample = xs
    return (y_prompt.reshape(n_p, t_p, d), y_sample.reshape(n_s, t_s, d), jnp.stack(new_delta, axis=1),
            jnp.stack(new_k, axis=1), jnp.stack(new_v, axis=1), jnp.stack(new_gla, axis=1))
```

```python
import functools
import math

import jax
import jax.numpy as jnp
from jax import lax
from jax.experimental import pallas as pl
from jax.experimental.pallas import tpu as pltpu

F32 = jnp.float32
BF16 = jnp.bfloat16

EPS = 1e-6
N_MOD = 9
GRID_W = 64
HEAD_DIM = 128
A_HEADS = 4
A_DK = 128
A_DV = 128
SHORT_CONV = 5
B_Q_HEADS = 4
B_KV_HEADS = 2
WINDOW = 128
C_HEADS = 4
C_DK = 128
C_DV = 256
GATE_RANK = 16
GATE_NORM = 16.0
ROPE_THETA = 10000.0

LANES = 128
SUBLANES = 8
PROJ_ROW_TILE = 1024
PROJ_ROW_PARTS = 2
FFN_ROW_TILE = 512
FFN_ROW_PARTS = 1
FFN_CHUNKS = 11
CHUNK = 256
ATTN_ROW_TILE = 256
GLA_ROW_BLOCK = 128
VMEM_LIMIT = 56 * 1024 * 1024


def _cparams(sem, vmem=VMEM_LIMIT):
    return pltpu.CompilerParams(dimension_semantics=sem, vmem_limit_bytes=vmem)


def _resident(block_shape, index_map):
    return pl.BlockSpec(block_shape, index_map, pipeline_mode=pl.Buffered(1))


def _dot(a, b):
    return jnp.dot(a, b, preferred_element_type=F32)


def _dot_nt(a, b):
    return lax.dot_general(a, b, (((1,), (1,)), ((), ())), preferred_element_type=F32)


def _dot_tn(a, b):
    return lax.dot_general(a, b, (((0,), (0,)), ((), ())), preferred_element_type=F32)


def _bf(x):
    return x.astype(BF16)


def _sigmoid(x):
    return 0.5 * jnp.tanh(0.5 * x) + 0.5


def _silu(x):
    return x * _sigmoid(x)


def _softplus(x):
    return jnp.maximum(x, 0.0) + jnp.log(1.0 + jnp.exp(-jnp.abs(x)))


def _rms(x):
    return x * lax.rsqrt(jnp.mean(x * x, axis=-1, keepdims=True) + EPS)


def _modnorm(x, g, shift, scale):
    return (_rms(x) * g) * (1.0 + scale) + shift


def _cumsum_rows(tri_bf, x):
    hi = _bf(x)
    lo = _bf(x - hi.astype(F32))
    return _dot(tri_bf, hi) + _dot(tri_bf, lo)


def _order_masks(ri, ci, reverse):
    if reverse:
        return ri <= ci, ri < ci
    return ri >= ci, ri > ci


class _Tiles:
    def __init__(self, n_p, t_p, n_s, t_s, c):
        self.n_p, self.n_s, self.c = n_p, n_s, c
        self.per_p, self.per_s = t_p // c, t_s // c
        self.np_tiles = n_p * self.per_p
        self.n = self.np_tiles + n_s * self.per_s

    def is_ctx(self, i):
        return i < self.np_tiles

    def seq(self, i):
        return jnp.where(i < self.np_tiles, i // self.per_p, self.n_p + (i - self.np_tiles) // self.per_s)

    def pos(self, i):
        return jnp.where(i < self.np_tiles, i % self.per_p, (i - self.np_tiles) % self.per_s)

    def length(self, i):
        return jnp.where(i < self.np_tiles, self.per_p, self.per_s)

    def row_block(self, i, reverse):
        return i + self.length(i) - 1 - 2 * self.pos(i) if reverse else i

    def ctx_seq(self, i):
        return jnp.minimum(self.seq(i), self.n_p - 1)

    def latent_seq(self, i):
        return jnp.maximum(self.seq(i) - self.n_p, 0)


def _ada_kernel(c_ref, w_ref, b_ref, o_ref):
    s = _bf(_silu(c_ref[...]))
    o_ref[0] = _dot(s, _bf(w_ref[0])) + b_ref[0]


def _ada(cond, ada_w, ada_b):
    depth, d, n = ada_w.shape
    rows = cond.shape[0]
    tn = n // 4
    return pl.pallas_call(
        _ada_kernel,
        out_shape=jax.ShapeDtypeStruct((depth, rows, n), F32),
        grid=(depth, n // tn),
        in_specs=[pl.BlockSpec((rows, d), lambda l, j: (0, 0)),
                  pl.BlockSpec((1, d, tn), lambda l, j: (l, 0, j)),
                  pl.BlockSpec((1, 1, tn), lambda l, j: (l, 0, j))],
        out_specs=pl.BlockSpec((1, rows, tn), lambda l, j: (l, 0, j)),
        compiler_params=_cparams(("parallel", "parallel")),
        name="ada",
    )(cond, ada_w, ada_b.reshape(depth, 1, n))


def _cond_index(n_prompt_rows, dec_seq, tm):
    npt = n_prompt_rows // tm

    def cond(i):
        return jnp.where(i < npt, 0, 1 + ((i - npt) * tm) // dec_seq)

    return cond


def _mixer_residual(x, gate_mod, of_ref, ob_ref, gate_ref, on_ref, w_ref, extra, rs, nh, dv):
    od = of_ref[rs, :].astype(F32) + ob_ref[rs, :].astype(F32)
    gate = gate_ref[rs, :].astype(F32)
    y = None if extra is None else _dot(extra, w_ref[nh * dv:, :])
    per = max(1, (2 * LANES) // dv)
    for h0 in range(0, nh, per):
        mix = [_bf((_rms(od[:, h * dv:(h + 1) * dv]) * on_ref[...]) * _silu(gate[:, h * dv:(h + 1) * dv]))
               for h in range(h0, min(h0 + per, nh))]
        part = _dot(jnp.concatenate(mix, axis=1), w_ref[h0 * dv:min(h0 + per, nh) * dv, :])
        y = part if y is None else y + part
    return x + gate_mod * y


def _ffn_kernel(*refs, i0, d_ff, n_chunks, parts, npt, split_in, mixer, final):
    refs = list(refs)
    x_refs = [refs.pop(0) for _ in range(2 if split_in else 1)]
    mod_ref, g_ref, wgu_ref, wd_ref = (refs.pop(0) for _ in range(4))
    if mixer is not None:
        nh, dv, has_extra = mixer
        of_ref, ob_ref, gate_ref, on_ref, wout_ref = (refs.pop(0) for _ in range(5))
        extra_refs = [refs.pop(0) for _ in range(2 if has_extra else 0)]
    rest = refs
    i = pl.program_id(0)
    mod = mod_ref[0]
    ch = d_ff // n_chunks
    rp = x_refs[0].shape[0] // parts
    outs = []
    for p in range(parts):
        rs = slice(p * rp, (p + 1) * rp)
        if split_in:
            x = jnp.where(i < npt, x_refs[0][rs, :], x_refs[1][rs, :])
        else:
            x = x_refs[0][rs, :]
        if mixer is not None:
            extra = jnp.where(i < npt, extra_refs[0][rs, :], extra_refs[1][rs, :]) if has_extra else None
            x = _mixer_residual(x, mod[i0 - 1:i0], of_ref, ob_ref, gate_ref, on_ref, wout_ref, extra, rs, nh, dv)
        h = _bf(_modnorm(x, g_ref[...], mod[i0:i0 + 1], mod[i0 + 1:i0 + 2]))
        acts = []
        for c in range(n_chunks):
            gt = _dot(h, wgu_ref[:, c * ch:(c + 1) * ch])
            up = _dot(h, wgu_ref[:, d_ff + c * ch:d_ff + (c + 1) * ch])
            acts.append(_bf(_silu(gt) * up))
        out = x + (0.5 * mod[i0 + 2:i0 + 3]) * _dot(jnp.concatenate(acts, axis=1), wd_ref[...])
        if final:
            outs.append((rs, _rms(out) * rest[0][...]))
        else:
            rest[0][rs, :] = out
    if final:
        _, yp_ref, ys_ref = rest

        @pl.when(i < npt)
        def _():
            for rs, out in outs:
                yp_ref[rs, :] = out

        @pl.when(i >= npt)
        def _():
            for rs, out in outs:
                ys_ref[rs, :] = out


def _ffn_half(xs, mod, i0, g, w_gu, w_down, widx, rows, mixer=None, final_g=None):
    split_in = len(xs) == 2
    d = xs[0].shape[1]
    nt = sum(x.shape[0] for x in xs)
    d_ff = w_down.shape[-2]
    tm = FFN_ROW_TILE
    np_rows = rows[0]
    assert np_rows % tm == 0 and rows[1] % tm == 0 and d_ff % (FFN_CHUNKS * LANES) == 0
    cond = _cond_index(*rows, tm)
    npt = np_rows // tm
    final = final_g is not None
    ctx_map = lambda i: (jnp.minimum(i, npt - 1), 0)
    lat_map = lambda i: (jnp.maximum(i - npt, 0), 0)
    mix_specs, mix_args, mix_cfg = [], [], None
    if mixer is not None:
        o_f, o_b, gate, onorm, w_out, nh, dv, extra = mixer
        assert i0 >= 1
        wa = nh * dv
        row = lambda i: (i, 0)
        mix_specs = [pl.BlockSpec((tm, wa), row), pl.BlockSpec((tm, wa), row), pl.BlockSpec((tm, wa), row),
                     pl.BlockSpec((1, dv), lambda i: (0, 0)), _resident(w_out.shape, lambda i: (0, 0))]
        mix_args = [o_f, o_b, gate, onorm.reshape(1, dv), w_out]
        if extra is not None:
            we = extra[0].shape[1]
            mix_specs += [pl.BlockSpec((tm, we), ctx_map), pl.BlockSpec((tm, we), lat_map)]
            mix_args += list(extra)
        mix_cfg = (nh, dv, extra is not None)
    kern = functools.partial(_ffn_kernel, i0=i0, d_ff=d_ff, n_chunks=FFN_CHUNKS, parts=FFN_ROW_PARTS, npt=npt,
                             split_in=split_in, mixer=mix_cfg, final=final)
    if split_in:
        x_specs = [pl.BlockSpec((tm, d), ctx_map), pl.BlockSpec((tm, d), lat_map)]
    else:
        x_specs = [pl.BlockSpec((tm, d), lambda i: (i, 0))]
    in_specs = x_specs + [pl.BlockSpec((1, N_MOD, d), lambda i: (cond(i), 0, 0)),
                          pl.BlockSpec((1, d), lambda i: (0, 0)),
                          _resident((None, None, d, 2 * d_ff), lambda i: widx + (0, 0)),
                          _resident((None, None, d_ff, d), lambda i: widx + (0, 0))]
    args = list(xs) + [mod, g.reshape(1, d), w_gu, w_down] + mix_args
    in_specs += mix_specs
    if final:
        in_specs.append(pl.BlockSpec((1, d), lambda i: (0, 0)))
        args.append(final_g.reshape(1, d))
        out_shape = [jax.ShapeDtypeStruct((np_rows, d), F32), jax.ShapeDtypeStruct((nt - np_rows, d), F32)]
        out_specs = [pl.BlockSpec((tm, d), ctx_map), pl.BlockSpec((tm, d), lat_map)]
    else:
        out_shape = jax.ShapeDtypeStruct((nt, d), F32)
        out_specs = pl.BlockSpec((tm, d), lambda i: (i, 0))
    return pl.pallas_call(
        kern,
        out_shape=out_shape,
        grid=(nt // tm,),
        in_specs=in_specs,
        out_specs=out_specs,
        compiler_params=_cparams(("arbitrary",)),
        name="ffn_half",
    )(*args)


def _proj_kernel(x_ref, mod_ref, g_ref, w_ref, *o_refs, i0, widths, parts):
    mod = mod_ref[0]
    rp = x_ref.shape[0] // parts
    for p in range(parts):
        rs = slice(p * rp, (p + 1) * rp)
        h = _bf(_modnorm(x_ref[rs, :], g_ref[...], mod[i0:i0 + 1], mod[i0 + 1:i0 + 2]))
        off = 0
        for o_ref, wd in zip(o_refs, widths):
            o_ref[rs, :] = _dot(h, w_ref[:, off:off + wd]).astype(o_ref.dtype)
            off += wd


def _mixer_in(x, mod, g, w, widths, dtypes, rows):
    nt, d = x.shape
    n = sum(widths)
    tm = PROJ_ROW_TILE
    assert rows[0] % tm == 0 and rows[1] % tm == 0
    cond = _cond_index(*rows, tm)
    kern = functools.partial(_proj_kernel, i0=3, widths=tuple(widths), parts=PROJ_ROW_PARTS)
    return pl.pallas_call(
        kern,
        out_shape=[jax.ShapeDtypeStruct((nt, wd), dt) for wd, dt in zip(widths, dtypes)],
        grid=(nt // tm,),
        in_specs=[pl.BlockSpec((tm, d), lambda i: (i, 0)),
                  pl.BlockSpec((1, N_MOD, d), lambda i: (cond(i), 0, 0)),
                  pl.BlockSpec((1, d), lambda i: (0, 0)),
                  _resident((d, n), lambda i: (0, 0))],
        out_specs=[pl.BlockSpec((tm, wd), lambda i: (i, 0)) for wd in widths],
        compiler_params=_cparams(("parallel",)),
        name="mixer_in",
    )(x, mod, g.reshape(1, d), w)


def _conv_kernel(prev_ref, x_ref, next_ref, w_ref, o_ref, *, tiles, dk):
    r = pl.program_id(0)
    c, width = x_ref.shape
    pad = (SHORT_CONV - 1) // 2
    has_prev = jnp.where(tiles.pos(r) > 0, 1.0, 0.0)
    has_next = jnp.where(tiles.pos(r) < tiles.length(r) - 1, 1.0, 0.0)
    ext = c + 2 * SUBLANES
    for hh in range(width // dk):
        part = hh // (width // (3 * dk))
        sl = slice(hh * dk, (hh + 1) * dk)
        xe = jnp.concatenate([prev_ref[:, sl] * has_prev, x_ref[:, sl], next_ref[:, sl] * has_next], axis=0)
        w = w_ref[:, sl]
        acc = None
        for j in range(SHORT_CONV):
            sh = pad - j
            xs = xe if sh == 0 else pltpu.roll(xe, sh % ext, axis=0)
            term = xs[SUBLANES:SUBLANES + c] * w[j:j + 1]
            acc = term if acc is None else acc + term
        y = _silu(acc)
        if part < 2:
            nrm = lax.rsqrt(jnp.sum(y * y, axis=-1, keepdims=True) + EPS)
            y = y * (nrm * (dk ** -0.5) if part == 0 else nrm)
        o_ref[:, sl] = y


def _conv_qkv(qkv, conv_w, tiles):
    nt, width = qkv.shape
    c = tiles.c
    per = c // SUBLANES
    n8 = nt // SUBLANES
    kern = functools.partial(_conv_kernel, tiles=tiles, dk=A_DK)
    return pl.pallas_call(
        kern,
        out_shape=jax.ShapeDtypeStruct((nt, width), F32),
        grid=(tiles.n,),
        in_specs=[pl.BlockSpec((SUBLANES, width), lambda r: (jnp.maximum(r * per - 1, 0), 0)),
                  pl.BlockSpec((c, width), lambda r: (r, 0)),
                  pl.BlockSpec((SUBLANES, width), lambda r: (jnp.minimum((r + 1) * per, n8 - 1), 0)),
                  pl.BlockSpec((SHORT_CONV, width), lambda r: (0, 0))],
        out_specs=pl.BlockSpec((c, width), lambda r: (r, 0)),
        compiler_params=_cparams(("parallel",)),
        name="conv_qkv",
    )(qkv, qkv, qkv, conv_w)


def _load_state(state, s0_ref, tiles, i):
    first = tiles.pos(i) == 0

    @pl.when(first & tiles.is_ctx(i))
    def _():
        state[...] = jnp.zeros(state.shape, F32)

    @pl.when(first & jnp.logical_not(tiles.is_ctx(i)))
    def _():
        state[...] = s0_ref[0]


def _store_state(state, out_ref, tiles, i):
    @pl.when((tiles.pos(i) == tiles.length(i) - 1) & tiles.is_ctx(i))
    def _():
        out_ref[0] = state[...]


def _dir_specs(tiles, c, cols):
    specs = []
    for reverse in (False, True):
        for width, col in cols:
            specs.append(pl.BlockSpec((c, width), functools.partial(
                lambda rev, cc, i: (tiles.row_block(i, rev), cc), reverse, col)))
    return specs


def _pair_dot(a, b):
    n = a.shape[0]
    a_bf, b_bf = _bf(a), _bf(b)
    z = jnp.zeros((n, n), BF16)
    b_diag = jnp.concatenate([jnp.concatenate([b_bf[:, :n], z], axis=1),
                              jnp.concatenate([z, b_bf[:, n:]], axis=1)], axis=0)
    return _dot(a_bf, b_diag)


def _unit_tri_solves(ls, rs, reverse_flags):
    c = ls[0].shape[0]
    n = c // 2
    ri = lax.broadcasted_iota(jnp.int32, (n, c), 0)
    ci = lax.broadcasted_iota(jnp.int32, (n, c), 1) & (n - 1)
    pairs = [jnp.concatenate([l[:n, :n], l[n:, n:]], axis=1) for l in ls]
    shift = SUBLANES.bit_length() - 1
    same = (ri >> shift) == (ci >> shift)
    ms = [jnp.where(same, -lp, 0.0) for lp in pairs]
    ps = [_pair_dot(m, m) for m in ms]
    ns = [m + p + _pair_dot(m, p) for m, p in zip(ms, ps)]
    ps = [_pair_dot(p, p) for p in ps]
    ns = [nv + p + _pair_dot(nv, p) for nv, p in zip(ns, ps)]
    while (1 << shift) < n:
        lvl = ((ri >> (shift + 1)) == (ci >> (shift + 1))) & ((ri >> shift) != (ci >> shift))
        cls = [jnp.where(lvl, lp, 0.0) for lp in pairs]
        ys = [cl + _pair_dot(cl, nv) for cl, nv in zip(cls, ns)]
        ns = [nv - (y + _pair_dot(nv, y)) for nv, y in zip(ns, ys)]
        shift += 1
    firsts, seconds = [], []
    for l, r, nv, rev in zip(ls, rs, ns, reverse_flags):
        if rev:
            firsts.append((r[n:], _bf(nv[:, n:])))
            seconds.append((r[:n], _bf(nv[:, :n]), _bf(l[:n, n:])))
        else:
            firsts.append((r[:n], _bf(nv[:, :n])))
            seconds.append((r[n:], _bf(nv[:, n:]), _bf(l[n:, :n])))
    xas = [ra + _dot(na, _bf(ra)) for ra, na in firsts]
    ts = [rb - _dot(lba, _bf(xa)) for (rb, _, lba), xa in zip(seconds, xas)]
    xbs = [t + _dot(nb, _bf(t)) for (_, nb, _), t in zip(seconds, ts)]
    return [jnp.concatenate([xb, xa] if rev else [xa, xb], axis=0) for xa, xb, rev in zip(xas, xbs, reverse_flags)]


def _delta_kernel(qf_ref, kf_ref, vf_ref, baf_ref, qb_ref, kb_ref, vb_ref, bab_ref, al_ref, dt_ref, s0_ref,
                  of_ref, ob_ref, sout_ref, state, *, tiles, nh, dk, dv):
    step = pl.program_id(0)
    _load_state(state, s0_ref, tiles, step)
    c = qf_ref.shape[0]
    ri = lax.broadcasted_iota(jnp.int32, (c, c), 0)
    ci = lax.broadcasted_iota(jnp.int32, (c, c), 1)
    units = []
    for z, (q_ref, k_ref, v_ref, ba_ref, o_ref) in enumerate(
            ((qf_ref, kf_ref, vf_ref, baf_ref, of_ref), (qb_ref, kb_ref, vb_ref, bab_ref, ob_ref))):
        reverse = z == 1
        incl, strict = _order_masks(ri, ci, reverse)
        ba = ba_ref[...]
        g_col = -jnp.exp(al_ref[...]) * _softplus(ba + dt_ref[...])
        beta_col = _sigmoid(ba)
        gc_col = _cumsum_rows(_bf(incl.astype(F32)), g_col)
        gc_row = gc_col.T
        last = 0 if reverse else c - 1
        for h in range(nh):
            cb = z * nh + h
            cg = 2 * nh + cb
            gcc = gc_col[:, cg:cg + 1]
            gcr = gc_row[cg:cg + 1, :]
            units.append(dict(
                z=z, h=h, o_ref=o_ref, strict=strict, gcc=gcc, gl=gcc[last:last + 1],
                beta=beta_col[:, cb:cb + 1], egc=jnp.exp(gcc),
                decay=jnp.where(incl, jnp.exp(jnp.where(incl, gcc - gcr, 0.0)), 0.0),
                q=q_ref[:, h * dk:(h + 1) * dk], k=k_ref[:, h * dk:(h + 1) * dk], v=v_ref[:, h * dv:(h + 1) * dv]))
    for u in units:
        u["kb"] = u["k"] * u["beta"]
        u["k_bf"] = _bf(u["k"])
    kks = [_dot_nt(_bf(u["kb"]), u["k_bf"]) for u in units]
    qks = [_dot_nt(_bf(u["q"]), u["k_bf"]) * u["decay"] for u in units]
    rs = _unit_tri_solves([jnp.where(u["strict"], kk * u["decay"], 0.0) for u, kk in zip(units, kks)],
                          [jnp.concatenate([u["v"] * u["beta"], u["kb"] * u["egc"]], axis=1) for u in units],
                          [u["z"] == 1 for u in units])
    ss = [state[u["z"], u["h"]] for u in units]
    ss_bf = [_bf(s) for s in ss]
    v_news_bf = [_bf(r[:, :dv] - _dot(_bf(r[:, dv:]), sb)) for r, sb in zip(rs, ss_bf)]
    for u, sb, qk, vnb in zip(units, ss_bf, qks, v_news_bf):
        h = u["h"]
        u["o_ref"][:, h * dv:(h + 1) * dv] = _bf(_dot(_bf(u["q"] * u["egc"]), sb) + _dot(_bf(qk), vnb))
    for u, s, vnb in zip(units, ss, v_news_bf):
        kd = u["k"] * jnp.exp(u["gl"] - u["gcc"])
        state[u["z"], u["h"]] = s * jnp.exp(u["gl"]) + _dot_tn(_bf(kd), vnb)
    _store_state(state, sout_ref, tiles, step)


def _delta_rule(qkv, ba, a_log, dt_bias, s0, tiles):
    nt = qkv.shape[0]
    nh, dk, dv = A_HEADS, A_DK, A_DV
    c = tiles.c
    pad = LANES - 4 * nh
    al = jnp.concatenate([jnp.zeros((2 * nh,), F32), a_log.reshape(-1), jnp.zeros((pad,), F32)])
    dt = jnp.concatenate([jnp.zeros((2 * nh,), F32), dt_bias.reshape(-1), jnp.zeros((pad,), F32)])
    const = lambda i: (0, 0)
    in_specs = (_dir_specs(tiles, c, ((nh * dk, 0), (nh * dk, 1), (nh * dv, 2), (LANES, 0)))
                + [pl.BlockSpec((1, LANES), const), pl.BlockSpec((1, LANES), const),
                   pl.BlockSpec((1, 2, nh, dk, dv), lambda i: (tiles.latent_seq(i), 0, 0, 0, 0))])
    kern = functools.partial(_delta_kernel, tiles=tiles, nh=nh, dk=dk, dv=dv)
    return pl.pallas_call(
        kern,
        out_shape=[jax.ShapeDtypeStruct((nt, nh * dv), BF16), jax.ShapeDtypeStruct((nt, nh * dv), BF16),
                   jax.ShapeDtypeStruct((tiles.n_p, 2, nh, dk, dv), F32)],
        grid=(tiles.n,),
        in_specs=in_specs,
        out_specs=[pl.BlockSpec((c, nh * dv), lambda i: (tiles.row_block(i, False), 0)),
                   pl.BlockSpec((c, nh * dv), lambda i: (tiles.row_block(i, True), 0)),
                   pl.BlockSpec((1, 2, nh, dk, dv), lambda i: (tiles.ctx_seq(i), 0, 0, 0, 0))],
        scratch_shapes=[pltpu.VMEM((2, nh, dk, dv), F32)],
        compiler_params=_cparams(("arbitrary",)),
        name="delta_rule",
    )(qkv, qkv, qkv, ba, qkv, qkv, qkv, ba, al.reshape(1, LANES), dt.reshape(1, LANES), s0)


def _ctx_attn_kernel(sink_ref, q_ref, k_ref, v_ref, o_ref, *, n_kv, grp, dh):
    scale = dh ** -0.5
    heads = range(n_kv * grp)
    ks = [_bf(k_ref[:, hk * dh:(hk + 1) * dh]) for hk in range(n_kv)]
    vs = [_bf(v_ref[:, hk * dh:(hk + 1) * dh]) for hk in range(n_kv)]
    sinks = [sink_ref[h] for h in heads]
    ss = [_dot_nt(_bf(q_ref[:, h * dh:(h + 1) * dh] * scale), ks[h // grp]) for h in heads]
    ms = [jnp.maximum(jnp.max(s, axis=-1, keepdims=True), sk) for s, sk in zip(ss, sinks)]
    ps = [jnp.exp(s - m) for s, m in zip(ss, ms)]
    dens = [jnp.sum(p, axis=-1, keepdims=True) + jnp.exp(sk - m) for p, sk, m in zip(ps, sinks, ms)]
    for h, p, den in zip(heads, ps, dens):
        o_ref[:, h * dh:(h + 1) * dh] = _bf(_dot(_bf(p), vs[h // grp]) / den)


def _ctx_attention(qb, kb, vb, sink, n_seq, t):
    wq = qb.shape[1]
    wk = kb.shape[1]
    kern = functools.partial(_ctx_attn_kernel, n_kv=B_KV_HEADS, grp=B_Q_HEADS // B_KV_HEADS, dh=HEAD_DIM)
    return pl.pallas_call(
        kern,
        out_shape=jax.ShapeDtypeStruct((n_seq * t, wq), BF16),
        grid=(n_seq,),
        in_specs=[pl.BlockSpec(memory_space=pltpu.SMEM),
                  pl.BlockSpec((t, wq), lambda b: (b, 0)),
                  pl.BlockSpec((t, wk), lambda b: (b, 0)),
                  pl.BlockSpec((t, wk), lambda b: (b, 0))],
        out_specs=pl.BlockSpec((t, wq), lambda b: (b, 0)),
        compiler_params=_cparams(("parallel",)),
        name="ctx_attention",
    )(sink, qb, kb, vb)


def _rope(x, cos, sin_signed):
    lane = lax.broadcasted_iota(jnp.int32, x.shape, 1)
    quarter = HEAD_DIM // 4
    partner = jnp.where((lane % (2 * quarter)) < quarter,
                        pltpu.roll(x, HEAD_DIM - quarter, axis=1), pltpu.roll(x, quarter, axis=1))
    return x * cos + partner * sin_signed


def _lat_attn_kernel(sink_ref, q_ref, k_ref, v_ref, kc_ref, vc_ref, cq_ref, sq_ref, ck_ref, sk_ref,
                     o_ref, *, n_kv, grp, dh, window):
    scale = dh ** -0.5
    tq = q_ref.shape[0]
    t = k_ref.shape[0]
    span = min(t, tq + 2 * window)
    q0 = pl.program_id(1) * tq
    start = pl.multiple_of(jnp.clip(q0 - window, 0, t - span), math.gcd(tq, window))
    rows = pl.ds(start, span)
    qpos = q0 + lax.broadcasted_iota(jnp.int32, (tq, span), 0)
    kpos = start + lax.broadcasted_iota(jnp.int32, (tq, span), 1)
    valid = jnp.abs(qpos - kpos) <= window
    cq, sq = cq_ref[...], sq_ref[...]
    ck, sk_t = ck_ref[rows, :], sk_ref[rows, :]
    kv = []
    for hk in range(n_kv):
        hs = slice(hk * dh, (hk + 1) * dh)
        kv.append((_bf(_rope(k_ref[rows, hs], ck, sk_t)), _bf(v_ref[rows, hs]), _bf(kc_ref[0, :, hs]),
                   _bf(vc_ref[0, :, hs])))
    heads = range(n_kv * grp)
    sinks = [sink_ref[h] for h in heads]
    qs = [q_ref[:, h * dh:(h + 1) * dh] * scale for h in heads]
    s_locs = [jnp.where(valid, _dot_nt(_bf(_rope(q, cq, sq)), kv[h // grp][0]), -jnp.inf) for h, q in zip(heads, qs)]
    s_ctxs = [_dot_nt(_bf(q), kv[h // grp][2]) for h, q in zip(heads, qs)]
    ms = [jnp.maximum(jnp.maximum(jnp.max(sl, axis=-1, keepdims=True), jnp.max(sc, axis=-1, keepdims=True)), sk)
          for sl, sc, sk in zip(s_locs, s_ctxs, sinks)]
    p_locs = [jnp.exp(sl - m) for sl, m in zip(s_locs, ms)]
    p_ctxs = [jnp.exp(sc - m) for sc, m in zip(s_ctxs, ms)]
    dens = [jnp.sum(pl_, axis=-1, keepdims=True) + jnp.sum(pc, axis=-1, keepdims=True) + jnp.exp(sk - m)
            for pl_, pc, sk, m in zip(p_locs, p_ctxs, sinks, ms)]
    for h, pl_, pc, den in zip(heads, p_locs, p_ctxs, dens):
        o_ref[:, h * dh:(h + 1) * dh] = _bf((_dot(_bf(pl_), kv[h // grp][1]) + _dot(_bf(pc), kv[h // grp][3])) / den)


def _lat_attention(qb, kb, vb, k_ctx, v_ctx, sink, cos, sin_signed, row0, n_seq, t):
    wq = qb.shape[1]
    wk = kb.shape[1]
    tq = ATTN_ROW_TILE
    assert t % tq == 0 and row0 % t == 0
    nq = t // tq
    base_q = row0 // tq
    base_t = row0 // t
    past = k_ctx.shape[1]
    kern = functools.partial(_lat_attn_kernel, n_kv=B_KV_HEADS, grp=B_Q_HEADS // B_KV_HEADS, dh=HEAD_DIM,
                             window=WINDOW)
    return pl.pallas_call(
        kern,
        out_shape=jax.ShapeDtypeStruct((n_seq * t, wq), BF16),
        grid=(n_seq, nq),
        in_specs=[pl.BlockSpec(memory_space=pltpu.SMEM),
                  pl.BlockSpec((tq, wq), lambda b, i: (base_q + b * nq + i, 0)),
                  pl.BlockSpec((t, wk), lambda b, i: (base_t + b, 0)),
                  pl.BlockSpec((t, wk), lambda b, i: (base_t + b, 0)),
                  pl.BlockSpec((1, past, wk), lambda b, i: (b, 0, 0)),
                  pl.BlockSpec((1, past, wk), lambda b, i: (b, 0, 0)),
                  pl.BlockSpec((tq, HEAD_DIM), lambda b, i: (i, 0)),
                  pl.BlockSpec((tq, HEAD_DIM), lambda b, i: (i, 0)),
                  pl.BlockSpec((t, HEAD_DIM), lambda b, i: (0, 0)),
                  pl.BlockSpec((t, HEAD_DIM), lambda b, i: (0, 0))],
        out_specs=pl.BlockSpec((tq, wq), lambda b, i: (b * nq + i, 0)),
        compiler_params=_cparams(("parallel", "parallel")),
        name="latent_attention",
    )(sink, qb, kb, vb, k_ctx, v_ctx, cos, sin_signed, cos, sin_signed)


def _rope_tables(t):
    half = HEAD_DIM // 2
    quarter = half // 2
    pos = jnp.arange(t)
    row = (pos // GRID_W).astype(F32)
    col = (pos % GRID_W).astype(F32)
    inv = ROPE_THETA ** (-jnp.arange(quarter, dtype=F32) / quarter)
    ang_r = row[:, None] * inv[None, :]
    ang_c = col[:, None] * inv[None, :]
    cos = jnp.concatenate([jnp.cos(ang_r), jnp.cos(ang_r), jnp.cos(ang_c), jnp.cos(ang_c)], axis=-1)
    sin = jnp.concatenate([-jnp.sin(ang_r), jnp.sin(ang_r), -jnp.sin(ang_c), jnp.sin(ang_c)], axis=-1)
    return cos, sin


def _gla_kernel(qf_ref, kf_ref, vf_ref, lrf_ref, qb_ref, kb_ref, vb_ref, lrb_ref, wg_ref, bias_ref, s0_ref,
                of_ref, ob_ref, sout_ref, state, *, tiles, nh, dk, dv, rb):
    step = pl.program_id(0)
    _load_state(state, s0_ref, tiles, step)
    c = qf_ref.shape[0]
    ri = lax.broadcasted_iota(jnp.int32, (c, c), 0)
    ci = lax.broadcasted_iota(jnp.int32, (c, c), 1)
    eye = (lax.broadcasted_iota(jnp.int32, (dk, dk), 0) == lax.broadcasted_iota(jnp.int32, (dk, dk), 1))
    units = []
    for z, (q_ref, k_ref, v_ref, lr_ref, o_ref) in enumerate(
            ((qf_ref, kf_ref, vf_ref, lrf_ref, of_ref), (qb_ref, kb_ref, vb_ref, lrb_ref, ob_ref))):
        reverse = z == 1
        incl, _ = _order_masks(ri, ci, reverse)
        x = _dot(_bf(lr_ref[...]), _bf(wg_ref[z])) + bias_ref[z]
        gk = -_softplus(-x) * (1.0 / GATE_NORM)
        gcum = _cumsum_rows(_bf(incl.astype(F32)), gk)
        last = 0 if reverse else c - 1
        for h in range(nh):
            g = gcum[:, h * dk:(h + 1) * dk]
            units.append(dict(
                z=z, h=h, o_ref=o_ref, reverse=reverse, incl=incl, g=g, gl_row=g[last:last + 1],
                q=q_ref[:, h * dk:(h + 1) * dk] * (dk ** -0.5), k=k_ref[:, h * dk:(h + 1) * dk],
                v_bf=v_ref[:, h * dv:(h + 1) * dv], s=state[z, h]))
    intra = [[] for _ in units]
    for blk in range(c // rb):
        r0, r1 = blk * rb, (blk + 1) * rb
        scores, cols = [], []
        for u in units:
            g = u["g"]
            mid = r0 + rb // 2
            if u["reverse"]:
                c0, c1 = r0, c
                ref = g[mid:mid + 1]
            else:
                c0, c1 = 0, r1
                ref = g[mid - 1:mid]
            qe = u["q"][r0:r1] * jnp.exp(g[r0:r1] - ref)
            ke = u["k"][c0:c1] * jnp.exp(ref - g[c0:c1])
            scores.append(jnp.where(u["incl"][r0:r1, c0:c1], _dot_nt(_bf(qe), _bf(ke)), 0.0))
            cols.append((c0, c1))
        for parts, u, a, (c0, c1) in zip(intra, units, scores, cols):
            parts.append(_dot(_bf(a), u["v_bf"][c0:c1]))
    inter = [_dot(_bf(u["q"] * jnp.exp(u["g"])), _bf(u["s"])) for u in units]
    for u, o_inter, parts in zip(units, inter, intra):
        h = u["h"]
        u["o_ref"][:, h * dv:(h + 1) * dv] = _bf(o_inter + jnp.concatenate(parts, axis=0))
    for u in units:
        kd = u["k"] * jnp.exp(u["gl_row"] - u["g"])
        gl_col = jnp.sum(jnp.where(eye, jnp.broadcast_to(u["gl_row"], (dk, dk)), 0.0), axis=1, keepdims=True)
        state[u["z"], u["h"]] = u["s"] * jnp.exp(gl_col) + _dot_tn(_bf(kd), u["v_bf"])
    _store_state(state, sout_ref, tiles, step)


def _gla(q, k, v, lr, w_gate, gate_bias, s0, tiles):
    nt = q.shape[0]
    nh, dk, dv = C_HEADS, C_DK, C_DV
    c = tiles.c
    wg = jnp.zeros((2, LANES, nh * dk), F32)
    for z in range(2):
        wg = wg.at[z, z * GATE_RANK:(z + 1) * GATE_RANK].set(w_gate[z])
    in_specs = (_dir_specs(tiles, c, ((nh * dk, 0), (nh * dk, 0), (nh * dv, 0), (LANES, 0)))
                + [pl.BlockSpec((2, LANES, nh * dk), lambda i: (0, 0, 0)),
                   pl.BlockSpec((2, 1, nh * dk), lambda i: (0, 0, 0)),
                   pl.BlockSpec((1, 2, nh, dk, dv), lambda i: (tiles.latent_seq(i), 0, 0, 0, 0))])
    kern = functools.partial(_gla_kernel, tiles=tiles, nh=nh, dk=dk, dv=dv, rb=GLA_ROW_BLOCK)
    return pl.pallas_call(
        kern,
        out_shape=[jax.ShapeDtypeStruct((nt, nh * dv), BF16), jax.ShapeDtypeStruct((nt, nh * dv), BF16),
                   jax.ShapeDtypeStruct((tiles.n_p, 2, nh, dk, dv), F32)],
        grid=(tiles.n,),
        in_specs=in_specs,
        out_specs=[pl.BlockSpec((c, nh * dv), lambda i: (tiles.row_block(i, False), 0)),
                   pl.BlockSpec((c, nh * dv), lambda i: (tiles.row_block(i, True), 0)),
                   pl.BlockSpec((1, 2, nh, dk, dv), lambda i: (tiles.ctx_seq(i), 0, 0, 0, 0))],
        scratch_shapes=[pltpu.VMEM((2, nh, dk, dv), F32)],
        compiler_params=_cparams(("arbitrary",)),
        name="gla",
    )(q, k, v, lr, q, k, v, lr, wg, gate_bias.reshape(2, 1, nh * dk), s0)


def kernel(x_prompt, x_sample, state_delta, cache_k, cache_v, state_gla, c, c_ctx, norm_g, ada_w, ada_b,
           ffn_w_gu, ffn_w_down, even_w_in, even_conv, even_a_log, even_dt_bias, even_onorm, even_sink,
           even_w_out, odd_w_in, odd_w_gate, odd_gate_bias, odd_onorm, odd_w_out, final_g):
    n_p, t_p, d = x_prompt.shape
    n_s, t_s, _ = x_sample.shape
    depth = norm_g.shape[0]
    np_rows, ns_rows = n_p * t_p, n_s * t_s
    assert np_rows % t_s == 0
    assert t_p % CHUNK == 0 and t_s % CHUNK == 0 and t_s % GRID_W == 0
    rows = (np_rows, t_s)
    tiles = _Tiles(n_p, t_p, n_s, t_s, CHUNK)

    n_cond = 1 + n_s
    cond_rows = -(-n_cond // SUBLANES) * SUBLANES
    conds = jnp.concatenate([c_ctx[None, :], c, jnp.zeros((cond_rows - n_cond, d), F32)], axis=0)
    mods = _ada(conds, ada_w, ada_b)[:, :n_cond].reshape(depth, n_cond, N_MOD, d)

    w_gu = ffn_w_gu.astype(BF16)
    w_down = ffn_w_down.astype(BF16)

    xs = (x_prompt.reshape(np_rows, d), x_sample.reshape(ns_rows, d))
    new_delta, new_k, new_v, new_gla = [], [], [], []
    for l in range(depth):
        j = l // 2
        mod = mods[l]
        x = _ffn_half(xs, mod, 0, norm_g[l, 0], w_gu, w_down, (l, 0), rows)
        if l % 2 == 0:
            nh = A_HEADS
            w = even_w_in[j]
            o_qkv = 2 * nh * A_DK + nh * A_DV
            o_gate = o_qkv + nh * A_DV
            o_ba = o_gate + 4 * nh
            w_cat = jnp.concatenate([w[:, :o_gate], w[:, o_ba:], w[:, o_gate:o_ba],
                                     jnp.zeros((d, LANES - 4 * nh), F32)], axis=1).astype(BF16)
            widths = (o_qkv, nh * A_DV, B_Q_HEADS * HEAD_DIM, B_KV_HEADS * HEAD_DIM, B_KV_HEADS * HEAD_DIM, LANES)
            assert sum(widths) == w_cat.shape[1]
            dtypes = (F32, BF16, F32, F32, F32, F32)
            qkv, gate, qb, kb, vb, ba = _mixer_in(x, mod, norm_g[l, 1], w_cat, widths, dtypes, rows)
            qkv_n = _conv_qkv(qkv, even_conv[j], tiles)
            o_f, o_b, st = _delta_rule(qkv_n, ba, even_a_log[j], even_dt_bias[j], state_delta[:, j], tiles)
            cos, sin_signed = _rope_tables(t_s)
            att_p = _ctx_attention(qb, kb, vb, even_sink[j], n_p, t_p)
            att_s = _lat_attention(qb, kb, vb,
                                   cache_k[:, j].reshape(n_s, -1, B_KV_HEADS * HEAD_DIM),
                                   cache_v[:, j].reshape(n_s, -1, B_KV_HEADS * HEAD_DIM),
                                   even_sink[j], cos, sin_signed, np_rows, n_s, t_s)
            mixer = (o_f, o_b, gate, even_onorm[j], even_w_out[j].astype(BF16), nh, A_DV, (att_p, att_s))
            new_delta.append(st)
            new_k.append(kb[:np_rows].reshape(n_p, t_p, B_KV_HEADS, HEAD_DIM))
            new_v.append(vb[:np_rows].reshape(n_p, t_p, B_KV_HEADS, HEAD_DIM))
        else:
            nh = C_HEADS
            w_cat = jnp.concatenate([odd_w_in[j], jnp.zeros((d, LANES - 2 * GATE_RANK), F32)], axis=1).astype(BF16)
            widths = (nh * C_DK, nh * C_DK, nh * C_DV, nh * C_DV, LANES)
            assert sum(widths) == w_cat.shape[1]
            dtypes = (F32, F32, BF16, BF16, F32)
            q, k, v, g_out, lr = _mixer_in(x, mod, norm_g[l, 1], w_cat, widths, dtypes, rows)
            o_f, o_b, st = _gla(q, k, v, lr, odd_w_gate[j], odd_gate_bias[j], state_gla[:, j], tiles)
            mixer = (o_f, o_b, g_out, odd_onorm[j], odd_w_out[j].astype(BF16), nh, C_DV, None)
            new_gla.append(st)
        last = l == depth - 1
        xs = _ffn_half((x,), mod, 6, norm_g[l, 2], w_gu, w_down, (l, 1), rows, mixer=mixer,
                       final_g=final_g if last else None)
        if not last:
            xs = (xs,)

    y_prompt, y_sample = xs
    return (y_prompt.reshape(n_p, t_p, d), y_sample.reshape(n_s, t_s, d), jnp.stack(new_delta, axis=1),
            jnp.stack(new_k, axis=1), jnp.stack(new_v, axis=1), jnp.stack(new_gla, axis=1))
```

```python
import functools
import math

import jax
import jax.numpy as jnp
from jax import lax
from jax.experimental import pallas as pl
from jax.experimental.pallas import tpu as pltpu

F32 = jnp.float32
BF16 = jnp.bfloat16

EPS = 1e-6
N_MOD = 9
GRID_W = 64
HEAD_DIM = 128
A_HEADS = 4
A_DK = 128
A_DV = 128
SHORT_CONV = 5
B_Q_HEADS = 4
B_KV_HEADS = 2
WINDOW = 128
C_HEADS = 4
C_DK = 128
C_DV = 256
GATE_RANK = 16
GATE_NORM = 16.0
ROPE_THETA = 10000.0

LANES = 128
SUBLANES = 8
PROJ_ROW_TILE = 1024
PROJ_ROW_PARTS = 2
FFN_ROW_TILE = 512
FFN_ROW_PARTS = 1
FFN_CHUNKS = 11
CHUNK = 256
CTX_SEQS_PER_STEP = 2
ATTN_ROW_TILE = 256
GLA_ROW_BLOCK = 128
VMEM_LIMIT = 56 * 1024 * 1024


def _cparams(sem, vmem=VMEM_LIMIT):
    return pltpu.CompilerParams(dimension_semantics=sem, vmem_limit_bytes=vmem)


def _resident(block_shape, index_map):
    return pl.BlockSpec(block_shape, index_map, pipeline_mode=pl.Buffered(1))


def _dot(a, b):
    return jnp.dot(a, b, preferred_element_type=F32)


def _dot_nt(a, b):
    return lax.dot_general(a, b, (((1,), (1,)), ((), ())), preferred_element_type=F32)


def _dot_tn(a, b):
    return lax.dot_general(a, b, (((0,), (0,)), ((), ())), preferred_element_type=F32)


def _bf(x):
    return x.astype(BF16)


def _sigmoid(x):
    return 0.5 * jnp.tanh(0.5 * x) + 0.5


def _silu(x):
    return x * _sigmoid(x)


def _softplus(x):
    return jnp.maximum(x, 0.0) + jnp.log(1.0 + jnp.exp(-jnp.abs(x)))


def _rms(x):
    return x * lax.rsqrt(jnp.mean(x * x, axis=-1, keepdims=True) + EPS)


def _modnorm(x, g, shift, scale):
    return (_rms(x) * g) * (1.0 + scale) + shift


def _cumsum_rows(tri_bf, x):
    hi = _bf(x)
    lo = _bf(x - hi.astype(F32))
    return _dot(tri_bf, hi) + _dot(tri_bf, lo)


def _order_masks(ri, ci, reverse):
    if reverse:
        return ri <= ci, ri < ci
    return ri >= ci, ri > ci


class _Tiles:
    def __init__(self, n_p, t_p, n_s, t_s, c):
        self.n_p, self.n_s, self.c = n_p, n_s, c
        self.per_p, self.per_s = t_p // c, t_s // c
        self.np_tiles = n_p * self.per_p
        self.n = self.np_tiles + n_s * self.per_s

    def is_ctx(self, i):
        return i < self.np_tiles

    def seq(self, i):
        return jnp.where(i < self.np_tiles, i // self.per_p, self.n_p + (i - self.np_tiles) // self.per_s)

    def pos(self, i):
        return jnp.where(i < self.np_tiles, i % self.per_p, (i - self.np_tiles) % self.per_s)

    def length(self, i):
        return jnp.where(i < self.np_tiles, self.per_p, self.per_s)

    def row_block(self, i, reverse):
        return i + self.length(i) - 1 - 2 * self.pos(i) if reverse else i

    def ctx_seq(self, i):
        return jnp.minimum(self.seq(i), self.n_p - 1)

    def latent_seq(self, i):
        return jnp.maximum(self.seq(i) - self.n_p, 0)


def _ada_kernel(c_ref, w_ref, b_ref, o_ref):
    s = _bf(_silu(c_ref[...]))
    o_ref[0] = _dot(s, _bf(w_ref[0])) + b_ref[0]


def _ada(cond, ada_w, ada_b):
    depth, d, n = ada_w.shape
    rows = cond.shape[0]
    tn = n // 4
    return pl.pallas_call(
        _ada_kernel,
        out_shape=jax.ShapeDtypeStruct((depth, rows, n), F32),
        grid=(depth, n // tn),
        in_specs=[pl.BlockSpec((rows, d), lambda l, j: (0, 0)),
                  pl.BlockSpec((1, d, tn), lambda l, j: (l, 0, j)),
                  pl.BlockSpec((1, 1, tn), lambda l, j: (l, 0, j))],
        out_specs=pl.BlockSpec((1, rows, tn), lambda l, j: (l, 0, j)),
        compiler_params=_cparams(("parallel", "parallel")),
        name="ada",
    )(cond, ada_w, ada_b.reshape(depth, 1, n))


def _cond_index(n_prompt_rows, dec_seq, tm):
    npt = n_prompt_rows // tm

    def cond(i):
        return jnp.where(i < npt, 0, 1 + ((i - npt) * tm) // dec_seq)

    return cond


def _mixer_residual(x, gate_mod, of_ref, ob_ref, gate_ref, on_ref, w_ref, extra, rs, nh, dv):
    od = of_ref[rs, :].astype(F32) + ob_ref[rs, :].astype(F32)
    gate = gate_ref[rs, :].astype(F32)
    y = None if extra is None else _dot(extra, w_ref[nh * dv:, :])
    per = max(1, (2 * LANES) // dv)
    for h0 in range(0, nh, per):
        mix = [_bf((_rms(od[:, h * dv:(h + 1) * dv]) * on_ref[...]) * _silu(gate[:, h * dv:(h + 1) * dv]))
               for h in range(h0, min(h0 + per, nh))]
        part = _dot(jnp.concatenate(mix, axis=1), w_ref[h0 * dv:min(h0 + per, nh) * dv, :])
        y = part if y is None else y + part
    return x + gate_mod * y


def _ffn_kernel(*refs, i0, d_ff, n_chunks, parts, npt, split_in, mixer, final):
    refs = list(refs)
    x_refs = [refs.pop(0) for _ in range(2 if split_in else 1)]
    mod_ref, g_ref, wgu_ref, wd_ref = (refs.pop(0) for _ in range(4))
    if mixer is not None:
        nh, dv, has_extra = mixer
        of_ref, ob_ref, gate_ref, on_ref, wout_ref = (refs.pop(0) for _ in range(5))
        extra_refs = [refs.pop(0) for _ in range(2 if has_extra else 0)]
    rest = refs
    i = pl.program_id(0)
    mod = mod_ref[0]
    ch = d_ff // n_chunks
    rp = x_refs[0].shape[0] // parts
    outs = []
    for p in range(parts):
        rs = slice(p * rp, (p + 1) * rp)
        if split_in:
            x = jnp.where(i < npt, x_refs[0][rs, :], x_refs[1][rs, :])
        else:
            x = x_refs[0][rs, :]
        if mixer is not None:
            extra = jnp.where(i < npt, extra_refs[0][rs, :], extra_refs[1][rs, :]) if has_extra else None
            x = _mixer_residual(x, mod[i0 - 1:i0], of_ref, ob_ref, gate_ref, on_ref, wout_ref, extra, rs, nh, dv)
        h = _modnorm(x, g_ref[...], mod[i0:i0 + 1], mod[i0 + 1:i0 + 2]).astype(wgu_ref.dtype)
        acts = []
        for c in range(n_chunks):
            gt = _dot(h, wgu_ref[:, c * ch:(c + 1) * ch])
            up = _dot(h, wgu_ref[:, d_ff + c * ch:d_ff + (c + 1) * ch])
            acts.append((_silu(gt) * up).astype(wd_ref.dtype))
        out = x + (0.5 * mod[i0 + 2:i0 + 3]) * _dot(jnp.concatenate(acts, axis=1), wd_ref[...])
        if final:
            outs.append((rs, _rms(out) * rest[0][...]))
        else:
            rest[0][rs, :] = out
    if final:
        _, yp_ref, ys_ref = rest

        @pl.when(i < npt)
        def _():
            for rs, out in outs:
                yp_ref[rs, :] = out

        @pl.when(i >= npt)
        def _():
            for rs, out in outs:
                ys_ref[rs, :] = out


def _ffn_half(xs, mod, i0, g, w_gu, w_down, widx, rows, mixer=None, final_g=None):
    split_in = len(xs) == 2
    d = xs[0].shape[1]
    nt = sum(x.shape[0] for x in xs)
    d_ff = w_down.shape[-2]
    tm = FFN_ROW_TILE
    np_rows = rows[0]
    assert np_rows % tm == 0 and rows[1] % tm == 0 and d_ff % (FFN_CHUNKS * LANES) == 0
    cond = _cond_index(*rows, tm)
    npt = np_rows // tm
    final = final_g is not None
    ctx_map = lambda i: (jnp.minimum(i, npt - 1), 0)
    lat_map = lambda i: (jnp.maximum(i - npt, 0), 0)
    mix_specs, mix_args, mix_cfg = [], [], None
    if mixer is not None:
        o_f, o_b, gate, onorm, w_out, nh, dv, extra = mixer
        assert i0 >= 1
        wa = nh * dv
        row = lambda i: (i, 0)
        mix_specs = [pl.BlockSpec((tm, wa), row), pl.BlockSpec((tm, wa), row), pl.BlockSpec((tm, wa), row),
                     pl.BlockSpec((1, dv), lambda i: (0, 0)), _resident(w_out.shape, lambda i: (0, 0))]
        mix_args = [o_f, o_b, gate, onorm.reshape(1, dv), w_out]
        if extra is not None:
            we = extra[0].shape[1]
            mix_specs += [pl.BlockSpec((tm, we), ctx_map), pl.BlockSpec((tm, we), lat_map)]
            mix_args += list(extra)
        mix_cfg = (nh, dv, extra is not None)
    kern = functools.partial(_ffn_kernel, i0=i0, d_ff=d_ff, n_chunks=FFN_CHUNKS, parts=FFN_ROW_PARTS, npt=npt,
                             split_in=split_in, mixer=mix_cfg, final=final)
    if split_in:
        x_specs = [pl.BlockSpec((tm, d), ctx_map), pl.BlockSpec((tm, d), lat_map)]
    else:
        x_specs = [pl.BlockSpec((tm, d), lambda i: (i, 0))]
    in_specs = x_specs + [pl.BlockSpec((1, N_MOD, d), lambda i: (cond(i), 0, 0)),
                          pl.BlockSpec((1, d), lambda i: (0, 0)),
                          _resident((None, None, d, 2 * d_ff), lambda i: widx + (0, 0)),
                          _resident((None, None, d_ff, d), lambda i: widx + (0, 0))]
    args = list(xs) + [mod, g.reshape(1, d), w_gu, w_down] + mix_args
    in_specs += mix_specs
    if final:
        in_specs.append(pl.BlockSpec((1, d), lambda i: (0, 0)))
        args.append(final_g.reshape(1, d))
        out_shape = [jax.ShapeDtypeStruct((np_rows, d), F32), jax.ShapeDtypeStruct((nt - np_rows, d), F32)]
        out_specs = [pl.BlockSpec((tm, d), ctx_map), pl.BlockSpec((tm, d), lat_map)]
    else:
        out_shape = jax.ShapeDtypeStruct((nt, d), F32)
        out_specs = pl.BlockSpec((tm, d), lambda i: (i, 0))
    return pl.pallas_call(
        kern,
        out_shape=out_shape,
        grid=(nt // tm,),
        in_specs=in_specs,
        out_specs=out_specs,
        compiler_params=_cparams(("arbitrary",)),
        name="ffn_half",
    )(*args)


def _proj_kernel(x_ref, mod_ref, g_ref, w_ref, *o_refs, i0, widths, parts):
    mod = mod_ref[0]
    rp = x_ref.shape[0] // parts
    for p in range(parts):
        rs = slice(p * rp, (p + 1) * rp)
        h = _modnorm(x_ref[rs, :], g_ref[...], mod[i0:i0 + 1], mod[i0 + 1:i0 + 2]).astype(w_ref.dtype)
        off = 0
        for o_ref, wd in zip(o_refs, widths):
            o_ref[rs, :] = _dot(h, w_ref[:, off:off + wd]).astype(o_ref.dtype)
            off += wd


def _mixer_in(x, mod, g, w, widths, dtypes, rows):
    nt, d = x.shape
    n = sum(widths)
    tm = PROJ_ROW_TILE
    assert rows[0] % tm == 0 and rows[1] % tm == 0
    cond = _cond_index(*rows, tm)
    kern = functools.partial(_proj_kernel, i0=3, widths=tuple(widths), parts=PROJ_ROW_PARTS)
    return pl.pallas_call(
        kern,
        out_shape=[jax.ShapeDtypeStruct((nt, wd), dt) for wd, dt in zip(widths, dtypes)],
        grid=(nt // tm,),
        in_specs=[pl.BlockSpec((tm, d), lambda i: (i, 0)),
                  pl.BlockSpec((1, N_MOD, d), lambda i: (cond(i), 0, 0)),
                  pl.BlockSpec((1, d), lambda i: (0, 0)),
                  _resident((d, n), lambda i: (0, 0))],
        out_specs=[pl.BlockSpec((tm, wd), lambda i: (i, 0)) for wd in widths],
        compiler_params=_cparams(("parallel",)),
        name="mixer_in",
    )(x, mod, g.reshape(1, d), w)


def _conv_kernel(prev_ref, x_ref, next_ref, w_ref, o_ref, *, tiles, dk):
    r = pl.program_id(0)
    c, width = x_ref.shape
    pad = (SHORT_CONV - 1) // 2
    has_prev = jnp.where(tiles.pos(r) > 0, 1.0, 0.0)
    has_next = jnp.where(tiles.pos(r) < tiles.length(r) - 1, 1.0, 0.0)
    ext = c + 2 * SUBLANES
    for hh in range(width // dk):
        part = hh // (width // (3 * dk))
        sl = slice(hh * dk, (hh + 1) * dk)
        xe = jnp.concatenate([prev_ref[:, sl] * has_prev, x_ref[:, sl], next_ref[:, sl] * has_next], axis=0)
        w = w_ref[:, sl]
        acc = None
        for j in range(SHORT_CONV):
            sh = pad - j
            xs = xe if sh == 0 else pltpu.roll(xe, sh % ext, axis=0)
            term = xs[SUBLANES:SUBLANES + c] * w[j:j + 1]
            acc = term if acc is None else acc + term
        y = _silu(acc)
        if part < 2:
            nrm = lax.rsqrt(jnp.sum(y * y, axis=-1, keepdims=True) + EPS)
            y = y * (nrm * (dk ** -0.5) if part == 0 else nrm)
        o_ref[:, sl] = y


def _conv_qkv(qkv, conv_w, tiles):
    nt, width = qkv.shape
    c = tiles.c
    per = c // SUBLANES
    n8 = nt // SUBLANES
    kern = functools.partial(_conv_kernel, tiles=tiles, dk=A_DK)
    return pl.pallas_call(
        kern,
        out_shape=jax.ShapeDtypeStruct((nt, width), F32),
        grid=(tiles.n,),
        in_specs=[pl.BlockSpec((SUBLANES, width), lambda r: (jnp.maximum(r * per - 1, 0), 0)),
                  pl.BlockSpec((c, width), lambda r: (r, 0)),
                  pl.BlockSpec((SUBLANES, width), lambda r: (jnp.minimum((r + 1) * per, n8 - 1), 0)),
                  pl.BlockSpec((SHORT_CONV, width), lambda r: (0, 0))],
        out_specs=pl.BlockSpec((c, width), lambda r: (r, 0)),
        compiler_params=_cparams(("parallel",)),
        name="conv_qkv",
    )(qkv, qkv, qkv, conv_w)


def _load_state(state, s0_ref, tiles, i):
    first = tiles.pos(i) == 0

    @pl.when(first & tiles.is_ctx(i))
    def _():
        state[...] = jnp.zeros(state.shape, F32)

    @pl.when(first & jnp.logical_not(tiles.is_ctx(i)))
    def _():
        state[...] = s0_ref[0]


def _store_state(state, out_ref, tiles, i):
    @pl.when((tiles.pos(i) == tiles.length(i) - 1) & tiles.is_ctx(i))
    def _():
        out_ref[0] = state[...]


def _dir_specs(tiles, c, cols):
    specs = []
    for reverse in (False, True):
        for width, col in cols:
            specs.append(pl.BlockSpec((c, width), functools.partial(
                lambda rev, cc, i: (tiles.row_block(i, rev), cc), reverse, col)))
    return specs


def _pair_dot(a, b):
    n = a.shape[0]
    a_bf, b_bf = _bf(a), _bf(b)
    z = jnp.zeros((n, n), BF16)
    b_diag = jnp.concatenate([jnp.concatenate([b_bf[:, :n], z], axis=1),
                              jnp.concatenate([z, b_bf[:, n:]], axis=1)], axis=0)
    return _dot(a_bf, b_diag)


def _unit_tri_solves(ls, rs, reverse_flags):
    c = ls[0].shape[0]
    n = c // 2
    ri = lax.broadcasted_iota(jnp.int32, (n, c), 0)
    ci = lax.broadcasted_iota(jnp.int32, (n, c), 1) & (n - 1)
    pairs = [jnp.concatenate([l[:n, :n], l[n:, n:]], axis=1) for l in ls]
    shift = SUBLANES.bit_length() - 1
    same = (ri >> shift) == (ci >> shift)
    ms = [jnp.where(same, -lp, 0.0) for lp in pairs]
    ps = [_pair_dot(m, m) for m in ms]
    ns = [m + p + _pair_dot(m, p) for m, p in zip(ms, ps)]
    ps = [_pair_dot(p, p) for p in ps]
    ns = [nv + p + _pair_dot(nv, p) for nv, p in zip(ns, ps)]
    while (1 << shift) < n:
        lvl = ((ri >> (shift + 1)) == (ci >> (shift + 1))) & ((ri >> shift) != (ci >> shift))
        cls = [jnp.where(lvl, lp, 0.0) for lp in pairs]
        ys = [cl + _pair_dot(cl, nv) for cl, nv in zip(cls, ns)]
        ns = [nv - (y + _pair_dot(nv, y)) for nv, y in zip(ns, ys)]
        shift += 1
    firsts, seconds = [], []
    for l, r, nv, rev in zip(ls, rs, ns, reverse_flags):
        if rev:
            firsts.append((r[n:], _bf(nv[:, n:])))
            seconds.append((r[:n], _bf(nv[:, :n]), _bf(l[:n, n:])))
        else:
            firsts.append((r[:n], _bf(nv[:, :n])))
            seconds.append((r[n:], _bf(nv[:, n:]), _bf(l[n:, :n])))
    xas = [ra + _dot(na, _bf(ra)) for ra, na in firsts]
    ts = [rb - _dot(lba, _bf(xa)) for (rb, _, lba), xa in zip(seconds, xas)]
    xbs = [t + _dot(nb, _bf(t)) for (_, nb, _), t in zip(seconds, ts)]
    return [jnp.concatenate([xb, xa] if rev else [xa, xb], axis=0) for xa, xb, rev in zip(xas, xbs, reverse_flags)]


def _delta_kernel(qf_ref, kf_ref, vf_ref, baf_ref, qb_ref, kb_ref, vb_ref, bab_ref, al_ref, dt_ref, s0_ref,
                  of_ref, ob_ref, sout_ref, state, *, tiles, nh, dk, dv):
    step = pl.program_id(0)
    _load_state(state, s0_ref, tiles, step)
    c = qf_ref.shape[0]
    ri = lax.broadcasted_iota(jnp.int32, (c, c), 0)
    ci = lax.broadcasted_iota(jnp.int32, (c, c), 1)
    units = []
    for z, (q_ref, k_ref, v_ref, ba_ref, o_ref) in enumerate(
            ((qf_ref, kf_ref, vf_ref, baf_ref, of_ref), (qb_ref, kb_ref, vb_ref, bab_ref, ob_ref))):
        reverse = z == 1
        incl, strict = _order_masks(ri, ci, reverse)
        ba = ba_ref[...]
        g_col = -jnp.exp(al_ref[...]) * _softplus(ba + dt_ref[...])
        beta_col = _sigmoid(ba)
        gc_col = _cumsum_rows(_bf(incl.astype(F32)), g_col)
        gc_row = gc_col.T
        last = 0 if reverse else c - 1
        for h in range(nh):
            cb = z * nh + h
            cg = 2 * nh + cb
            gcc = gc_col[:, cg:cg + 1]
            gcr = gc_row[cg:cg + 1, :]
            units.append(dict(
                z=z, h=h, o_ref=o_ref, strict=strict, gcc=gcc, gl=gcc[last:last + 1],
                beta=beta_col[:, cb:cb + 1], egc=jnp.exp(gcc),
                decay=jnp.where(incl, jnp.exp(jnp.where(incl, gcc - gcr, 0.0)), 0.0),
                q=q_ref[:, h * dk:(h + 1) * dk], k=k_ref[:, h * dk:(h + 1) * dk], v=v_ref[:, h * dv:(h + 1) * dv]))
    for u in units:
        u["kb"] = u["k"] * u["beta"]
        u["k_bf"] = _bf(u["k"])
    kks = [_dot_nt(_bf(u["kb"]), u["k_bf"]) for u in units]
    qks = [_dot_nt(_bf(u["q"]), u["k_bf"]) * u["decay"] for u in units]
    rs = _unit_tri_solves([jnp.where(u["strict"], kk * u["decay"], 0.0) for u, kk in zip(units, kks)],
                          [jnp.concatenate([u["v"] * u["beta"], u["kb"] * u["egc"]], axis=1) for u in units],
                          [u["z"] == 1 for u in units])
    ss = [state[u["z"], u["h"]] for u in units]
    ss_bf = [_bf(s) for s in ss]
    v_news_bf = [_bf(r[:, :dv] - _dot(_bf(r[:, dv:]), sb)) for r, sb in zip(rs, ss_bf)]
    for u, sb, qk, vnb in zip(units, ss_bf, qks, v_news_bf):
        h = u["h"]
        u["o_ref"][:, h * dv:(h + 1) * dv] = _bf(_dot(_bf(u["q"] * u["egc"]), sb) + _dot(_bf(qk), vnb))
    for u, s, vnb in zip(units, ss, v_news_bf):
        kd = u["k"] * jnp.exp(u["gl"] - u["gcc"])
        state[u["z"], u["h"]] = s * jnp.exp(u["gl"]) + _dot_tn(_bf(kd), vnb)
    _store_state(state, sout_ref, tiles, step)


def _delta_rule(qkv, ba, a_log, dt_bias, s0, tiles):
    nt = qkv.shape[0]
    nh, dk, dv = A_HEADS, A_DK, A_DV
    c = tiles.c
    pad = LANES - 4 * nh
    al = jnp.concatenate([jnp.zeros((2 * nh,), F32), a_log.reshape(-1), jnp.zeros((pad,), F32)])
    dt = jnp.concatenate([jnp.zeros((2 * nh,), F32), dt_bias.reshape(-1), jnp.zeros((pad,), F32)])
    const = lambda i: (0, 0)
    in_specs = (_dir_specs(tiles, c, ((nh * dk, 0), (nh * dk, 1), (nh * dv, 2), (LANES, 0)))
                + [pl.BlockSpec((1, LANES), const), pl.BlockSpec((1, LANES), const),
                   pl.BlockSpec((1, 2, nh, dk, dv), lambda i: (tiles.latent_seq(i), 0, 0, 0, 0))])
    kern = functools.partial(_delta_kernel, tiles=tiles, nh=nh, dk=dk, dv=dv)
    return pl.pallas_call(
        kern,
        out_shape=[jax.ShapeDtypeStruct((nt, nh * dv), BF16), jax.ShapeDtypeStruct((nt, nh * dv), BF16),
                   jax.ShapeDtypeStruct((tiles.n_p, 2, nh, dk, dv), F32)],
        grid=(tiles.n,),
        in_specs=in_specs,
        out_specs=[pl.BlockSpec((c, nh * dv), lambda i: (tiles.row_block(i, False), 0)),
                   pl.BlockSpec((c, nh * dv), lambda i: (tiles.row_block(i, True), 0)),
                   pl.BlockSpec((1, 2, nh, dk, dv), lambda i: (tiles.ctx_seq(i), 0, 0, 0, 0))],
        scratch_shapes=[pltpu.VMEM((2, nh, dk, dv), F32)],
        compiler_params=_cparams(("arbitrary",)),
        name="delta_rule",
    )(qkv, qkv, qkv, ba, qkv, qkv, qkv, ba, al.reshape(1, LANES), dt.reshape(1, LANES), s0)


def _ctx_attn_kernel(sink_ref, q_ref, k_ref, v_ref, o_ref, *, n_kv, grp, dh, t):
    scale = dh ** -0.5
    seqs = [slice(b * t, (b + 1) * t) for b in range(q_ref.shape[0] // t)]
    units = [(rs, h) for rs in seqs for h in range(n_kv * grp)]
    ks = {(rs.start, hk): _bf(k_ref[rs, hk * dh:(hk + 1) * dh]) for rs in seqs for hk in range(n_kv)}
    vs = {(rs.start, hk): _bf(v_ref[rs, hk * dh:(hk + 1) * dh]) for rs in seqs for hk in range(n_kv)}
    sinks = [sink_ref[h] for _, h in units]
    ss = [_dot_nt(_bf(q_ref[rs, h * dh:(h + 1) * dh] * scale), ks[rs.start, h // grp]) for rs, h in units]
    ms = [jnp.maximum(jnp.max(s, axis=-1, keepdims=True), sk) for s, sk in zip(ss, sinks)]
    ps = [jnp.exp(s - m) for s, m in zip(ss, ms)]
    dens = [jnp.sum(p, axis=-1, keepdims=True) + jnp.exp(sk - m) for p, sk, m in zip(ps, sinks, ms)]
    for (rs, h), p, den in zip(units, ps, dens):
        o_ref[rs, h * dh:(h + 1) * dh] = _bf(_dot(_bf(p), vs[rs.start, h // grp]) / den)


def _ctx_attention(qb, kb, vb, sink, n_seq, t):
    wq = qb.shape[1]
    wk = kb.shape[1]
    per = math.gcd(n_seq, CTX_SEQS_PER_STEP)
    kern = functools.partial(_ctx_attn_kernel, n_kv=B_KV_HEADS, grp=B_Q_HEADS // B_KV_HEADS, dh=HEAD_DIM, t=t)
    return pl.pallas_call(
        kern,
        out_shape=jax.ShapeDtypeStruct((n_seq * t, wq), BF16),
        grid=(n_seq // per,),
        in_specs=[pl.BlockSpec(memory_space=pltpu.SMEM),
                  pl.BlockSpec((per * t, wq), lambda b: (b, 0)),
                  pl.BlockSpec((per * t, wk), lambda b: (b, 0)),
                  pl.BlockSpec((per * t, wk), lambda b: (b, 0))],
        out_specs=pl.BlockSpec((per * t, wq), lambda b: (b, 0)),
        compiler_params=_cparams(("parallel",)),
        name="ctx_attention",
    )(sink, qb, kb, vb)


def _rope(x, cos, sin_signed):
    lane = lax.broadcasted_iota(jnp.int32, x.shape, 1)
    quarter = HEAD_DIM // 4
    partner = jnp.where((lane % (2 * quarter)) < quarter,
                        pltpu.roll(x, HEAD_DIM - quarter, axis=1), pltpu.roll(x, quarter, axis=1))
    return x * cos + partner * sin_signed


def _lat_attn_kernel(sink_ref, q_ref, k_ref, v_ref, kc_ref, vc_ref, cq_ref, sq_ref, ck_ref, sk_ref,
                     o_ref, *, n_kv, grp, dh, window):
    scale = dh ** -0.5
    tq = q_ref.shape[0]
    t = k_ref.shape[0]
    span = min(t, tq + 2 * window)
    q0 = pl.program_id(1) * tq
    start = pl.multiple_of(jnp.clip(q0 - window, 0, t - span), math.gcd(tq, window))
    rows = pl.ds(start, span)
    qpos = q0 + lax.broadcasted_iota(jnp.int32, (tq, span), 0)
    kpos = start + lax.broadcasted_iota(jnp.int32, (tq, span), 1)
    valid = jnp.abs(qpos - kpos) <= window
    cq, sq = cq_ref[...], sq_ref[...]
    ck, sk_t = ck_ref[rows, :], sk_ref[rows, :]
    kv = []
    for hk in range(n_kv):
        hs = slice(hk * dh, (hk + 1) * dh)
        kv.append((_bf(_rope(k_ref[rows, hs], ck, sk_t)), _bf(v_ref[rows, hs]), _bf(kc_ref[0, :, hs]),
                   _bf(vc_ref[0, :, hs])))
    heads = range(n_kv * grp)
    sinks = [sink_ref[h] for h in heads]
    qs = [q_ref[:, h * dh:(h + 1) * dh] * scale for h in heads]
    s_locs = [jnp.where(valid, _dot_nt(_bf(_rope(q, cq, sq)), kv[h // grp][0]), -jnp.inf) for h, q in zip(heads, qs)]
    s_ctxs = [_dot_nt(_bf(q), kv[h // grp][2]) for h, q in zip(heads, qs)]
    ms = [jnp.maximum(jnp.maximum(jnp.max(sl, axis=-1, keepdims=True), jnp.max(sc, axis=-1, keepdims=True)), sk)
          for sl, sc, sk in zip(s_locs, s_ctxs, sinks)]
    p_locs = [jnp.exp(sl - m) for sl, m in zip(s_locs, ms)]
    p_ctxs = [jnp.exp(sc - m) for sc, m in zip(s_ctxs, ms)]
    dens = [jnp.sum(pl_, axis=-1, keepdims=True) + jnp.sum(pc, axis=-1, keepdims=True) + jnp.exp(sk - m)
            for pl_, pc, sk, m in zip(p_locs, p_ctxs, sinks, ms)]
    for h, pl_, pc, den in zip(heads, p_locs, p_ctxs, dens):
        o_ref[:, h * dh:(h + 1) * dh] = _bf((_dot(_bf(pl_), kv[h // grp][1]) + _dot(_bf(pc), kv[h // grp][3])) / den)


def _lat_attention(qb, kb, vb, k_ctx, v_ctx, sink, cos, sin_signed, row0, n_seq, t):
    wq = qb.shape[1]
    wk = kb.shape[1]
    tq = ATTN_ROW_TILE
    assert t % tq == 0 and row0 % t == 0
    nq = t // tq
    base_q = row0 // tq
    base_t = row0 // t
    past = k_ctx.shape[1]
    kern = functools.partial(_lat_attn_kernel, n_kv=B_KV_HEADS, grp=B_Q_HEADS // B_KV_HEADS, dh=HEAD_DIM,
                             window=WINDOW)
    return pl.pallas_call(
        kern,
        out_shape=jax.ShapeDtypeStruct((n_seq * t, wq), BF16),
        grid=(n_seq, nq),
        in_specs=[pl.BlockSpec(memory_space=pltpu.SMEM),
                  pl.BlockSpec((tq, wq), lambda b, i: (base_q + b * nq + i, 0)),
                  pl.BlockSpec((t, wk), lambda b, i: (base_t + b, 0)),
                  pl.BlockSpec((t, wk), lambda b, i: (base_t + b, 0)),
                  pl.BlockSpec((1, past, wk), lambda b, i: (b, 0, 0)),
                  pl.BlockSpec((1, past, wk), lambda b, i: (b, 0, 0)),
                  pl.BlockSpec((tq, HEAD_DIM), lambda b, i: (i, 0)),
                  pl.BlockSpec((tq, HEAD_DIM), lambda b, i: (i, 0)),
                  pl.BlockSpec((t, HEAD_DIM), lambda b, i: (0, 0)),
                  pl.BlockSpec((t, HEAD_DIM), lambda b, i: (0, 0))],
        out_specs=pl.BlockSpec((tq, wq), lambda b, i: (b * nq + i, 0)),
        compiler_params=_cparams(("parallel", "parallel")),
        name="latent_attention",
    )(sink, qb, kb, vb, k_ctx, v_ctx, cos, sin_signed, cos, sin_signed)


def _rope_tables(t):
    half = HEAD_DIM // 2
    quarter = half // 2
    pos = jnp.arange(t)
    row = (pos // GRID_W).astype(F32)
    col = (pos % GRID_W).astype(F32)
    inv = ROPE_THETA ** (-jnp.arange(quarter, dtype=F32) / quarter)
    ang_r = row[:, None] * inv[None, :]
    ang_c = col[:, None] * inv[None, :]
    cos = jnp.concatenate([jnp.cos(ang_r), jnp.cos(ang_r), jnp.cos(ang_c), jnp.cos(ang_c)], axis=-1)
    sin = jnp.concatenate([-jnp.sin(ang_r), jnp.sin(ang_r), -jnp.sin(ang_c), jnp.sin(ang_c)], axis=-1)
    return cos, sin


def _gla_kernel(qf_ref, kf_ref, vf_ref, lrf_ref, qb_ref, kb_ref, vb_ref, lrb_ref, wg_ref, bias_ref, s0_ref,
                of_ref, ob_ref, sout_ref, state, *, tiles, nh, dk, dv, rb):
    step = pl.program_id(0)
    _load_state(state, s0_ref, tiles, step)
    c = qf_ref.shape[0]
    ri = lax.broadcasted_iota(jnp.int32, (c, c), 0)
    ci = lax.broadcasted_iota(jnp.int32, (c, c), 1)
    eye = (lax.broadcasted_iota(jnp.int32, (dk, dk), 0) == lax.broadcasted_iota(jnp.int32, (dk, dk), 1))
    units = []
    for z, (q_ref, k_ref, v_ref, lr_ref, o_ref) in enumerate(
            ((qf_ref, kf_ref, vf_ref, lrf_ref, of_ref), (qb_ref, kb_ref, vb_ref, lrb_ref, ob_ref))):
        reverse = z == 1
        incl, _ = _order_masks(ri, ci, reverse)
        x = _dot(_bf(lr_ref[...]), _bf(wg_ref[z])) + bias_ref[z]
        gk = -_softplus(-x) * (1.0 / GATE_NORM)
        gcum = _cumsum_rows(_bf(incl.astype(F32)), gk)
        last = 0 if reverse else c - 1
        for h in range(nh):
            g = gcum[:, h * dk:(h + 1) * dk]
            units.append(dict(
                z=z, h=h, o_ref=o_ref, reverse=reverse, incl=incl, g=g, gl_row=g[last:last + 1],
                q=q_ref[:, h * dk:(h + 1) * dk] * (dk ** -0.5), k=k_ref[:, h * dk:(h + 1) * dk],
                v_bf=v_ref[:, h * dv:(h + 1) * dv], s=state[z, h]))
    intra = [[] for _ in units]
    for blk in range(c // rb):
        r0, r1 = blk * rb, (blk + 1) * rb
        scores, cols = [], []
        for u in units:
            g = u["g"]
            mid = r0 + rb // 2
            if u["reverse"]:
                c0, c1 = r0, c
                ref = g[mid:mid + 1]
            else:
                c0, c1 = 0, r1
                ref = g[mid - 1:mid]
            qe = u["q"][r0:r1] * jnp.exp(g[r0:r1] - ref)
            ke = u["k"][c0:c1] * jnp.exp(ref - g[c0:c1])
            scores.append(jnp.where(u["incl"][r0:r1, c0:c1], _dot_nt(_bf(qe), _bf(ke)), 0.0))
            cols.append((c0, c1))
        for parts, u, a, (c0, c1) in zip(intra, units, scores, cols):
            parts.append(_dot(_bf(a), u["v_bf"][c0:c1]))
    inter = [_dot(_bf(u["q"] * jnp.exp(u["g"])), _bf(u["s"])) for u in units]
    for u, o_inter, parts in zip(units, inter, intra):
        h = u["h"]
        u["o_ref"][:, h * dv:(h + 1) * dv] = _bf(o_inter + jnp.concatenate(parts, axis=0))
    for u in units:
        kd = u["k"] * jnp.exp(u["gl_row"] - u["g"])
        gl_col = jnp.sum(jnp.where(eye, jnp.broadcast_to(u["gl_row"], (dk, dk)), 0.0), axis=1, keepdims=True)
        state[u["z"], u["h"]] = u["s"] * jnp.exp(gl_col) + _dot_tn(_bf(kd), u["v_bf"])
    _store_state(state, sout_ref, tiles, step)


def _gla(q, k, v, lr, w_gate, gate_bias, s0, tiles):
    nt = q.shape[0]
    nh, dk, dv = C_HEADS, C_DK, C_DV
    c = tiles.c
    wg = jnp.zeros((2, LANES, nh * dk), F32)
    for z in range(2):
        wg = wg.at[z, z * GATE_RANK:(z + 1) * GATE_RANK].set(w_gate[z])
    in_specs = (_dir_specs(tiles, c, ((nh * dk, 0), (nh * dk, 0), (nh * dv, 0), (LANES, 0)))
                + [pl.BlockSpec((2, LANES, nh * dk), lambda i: (0, 0, 0)),
                   pl.BlockSpec((2, 1, nh * dk), lambda i: (0, 0, 0)),
                   pl.BlockSpec((1, 2, nh, dk, dv), lambda i: (tiles.latent_seq(i), 0, 0, 0, 0))])
    kern = functools.partial(_gla_kernel, tiles=tiles, nh=nh, dk=dk, dv=dv, rb=GLA_ROW_BLOCK)
    return pl.pallas_call(
        kern,
        out_shape=[jax.ShapeDtypeStruct((nt, nh * dv), BF16), jax.ShapeDtypeStruct((nt, nh * dv), BF16),
                   jax.ShapeDtypeStruct((tiles.n_p, 2, nh, dk, dv), F32)],
        grid=(tiles.n,),
        in_specs=in_specs,
        out_specs=[pl.BlockSpec((c, nh * dv), lambda i: (tiles.row_block(i, False), 0)),
                   pl.BlockSpec((c, nh * dv), lambda i: (tiles.row_block(i, True), 0)),
                   pl.BlockSpec((1, 2, nh, dk, dv), lambda i: (tiles.ctx_seq(i), 0, 0, 0, 0))],
        scratch_shapes=[pltpu.VMEM((2, nh, dk, dv), F32)],
        compiler_params=_cparams(("arbitrary",)),
        name="gla",
    )(q, k, v, lr, q, k, v, lr, wg, gate_bias.reshape(2, 1, nh * dk), s0)


def kernel(x_prompt, x_sample, state_delta, cache_k, cache_v, state_gla, c, c_ctx, norm_g, ada_w, ada_b,
           ffn_w_gu, ffn_w_down, even_w_in, even_conv, even_a_log, even_dt_bias, even_onorm, even_sink,
           even_w_out, odd_w_in, odd_w_gate, odd_gate_bias, odd_onorm, odd_w_out, final_g):
    n_p, t_p, d = x_prompt.shape
    n_s, t_s, _ = x_sample.shape
    depth = norm_g.shape[0]
    np_rows, ns_rows = n_p * t_p, n_s * t_s
    assert np_rows % t_s == 0
    assert t_p % CHUNK == 0 and t_s % CHUNK == 0 and t_s % GRID_W == 0
    rows = (np_rows, t_s)
    tiles = _Tiles(n_p, t_p, n_s, t_s, CHUNK)

    n_cond = 1 + n_s
    cond_rows = -(-n_cond // SUBLANES) * SUBLANES
    conds = jnp.concatenate([c_ctx[None, :], c, jnp.zeros((cond_rows - n_cond, d), F32)], axis=0)
    mods = _ada(conds, ada_w, ada_b)[:, :n_cond].reshape(depth, n_cond, N_MOD, d)

    w_gu_second = ffn_w_gu[:, 1:].astype(BF16)
    w_down_second = ffn_w_down[:, 1:].astype(BF16)

    xs = (x_prompt.reshape(np_rows, d), x_sample.reshape(ns_rows, d))
    new_delta, new_k, new_v, new_gla = [], [], [], []
    for l in range(depth):
        j = l // 2
        mod = mods[l]
        x = _ffn_half(xs, mod, 0, norm_g[l, 0], ffn_w_gu, ffn_w_down, (l, 0), rows)
        if l % 2 == 0:
            nh = A_HEADS
            w = even_w_in[j]
            o_qkv = 2 * nh * A_DK + nh * A_DV
            o_gate = o_qkv + nh * A_DV
            o_ba = o_gate + 4 * nh
            w_cat = jnp.concatenate([w[:, :o_gate], w[:, o_ba:], w[:, o_gate:o_ba],
                                     jnp.zeros((d, LANES - 4 * nh), F32)], axis=1)
            widths = (o_qkv, nh * A_DV, B_Q_HEADS * HEAD_DIM, B_KV_HEADS * HEAD_DIM, B_KV_HEADS * HEAD_DIM, LANES)
            assert sum(widths) == w_cat.shape[1]
            dtypes = (F32, BF16, F32, F32, F32, F32)
            qkv, gate, qb, kb, vb, ba = _mixer_in(x, mod, norm_g[l, 1], w_cat, widths, dtypes, rows)
            qkv_n = _conv_qkv(qkv, even_conv[j], tiles)
            o_f, o_b, st = _delta_rule(qkv_n, ba, even_a_log[j], even_dt_bias[j], state_delta[:, j], tiles)
            cos, sin_signed = _rope_tables(t_s)
            att_p = _ctx_attention(qb, kb, vb, even_sink[j], n_p, t_p)
            att_s = _lat_attention(qb, kb, vb,
                                   cache_k[:, j].reshape(n_s, -1, B_KV_HEADS * HEAD_DIM),
                                   cache_v[:, j].reshape(n_s, -1, B_KV_HEADS * HEAD_DIM),
                                   even_sink[j], cos, sin_signed, np_rows, n_s, t_s)
            mixer = (o_f, o_b, gate, even_onorm[j], even_w_out[j].astype(BF16), nh, A_DV, (att_p, att_s))
            new_delta.append(st)
            new_k.append(kb[:np_rows].reshape(n_p, t_p, B_KV_HEADS, HEAD_DIM))
            new_v.append(vb[:np_rows].reshape(n_p, t_p, B_KV_HEADS, HEAD_DIM))
        else:
            nh = C_HEADS
            w_cat = jnp.concatenate([odd_w_in[j], jnp.zeros((d, LANES - 2 * GATE_RANK), F32)], axis=1)
            widths = (nh * C_DK, nh * C_DK, nh * C_DV, nh * C_DV, LANES)
            assert sum(widths) == w_cat.shape[1]
            dtypes = (F32, F32, BF16, BF16, F32)
            q, k, v, g_out, lr = _mixer_in(x, mod, norm_g[l, 1], w_cat, widths, dtypes, rows)
            o_f, o_b, st = _gla(q, k, v, lr, odd_w_gate[j], odd_gate_bias[j], state_gla[:, j], tiles)
            mixer = (o_f, o_b, g_out, odd_onorm[j], odd_w_out[j].astype(BF16), nh, C_DV, None)
            new_gla.append(st)
        last = l == depth - 1
        xs = _ffn_half((x,), mod, 6, norm_g[l, 2], w_gu_second, w_down_second, (l, 0), rows, mixer=mixer,
                       final_g=final_g if last else None)
        if not last:
            xs = (xs,)

    y_prompt, y_sample = xs
    return (y_prompt.reshape(n_p, t_p, d), y_sample.reshape(n_s, t_s, d), jnp.stack(new_delta, axis=1),
            jnp.stack(new_k, axis=1), jnp.stack(new_v, axis=1), jnp.stack(new_gla, axis=1))
```

```python
import functools
import math

import jax
import jax.numpy as jnp
from jax import lax
from jax.experimental import pallas as pl
from jax.experimental.pallas import tpu as pltpu

F32 = jnp.float32
BF16 = jnp.bfloat16

EPS = 1e-6
N_MOD = 9
GRID_W = 64
HEAD_DIM = 128
A_HEADS = 4
A_DK = 128
A_DV = 128
SHORT_CONV = 5
B_Q_HEADS = 4
B_KV_HEADS = 2
WINDOW = 128
C_HEADS = 4
C_DK = 128
C_DV = 256
GATE_RANK = 16
GATE_NORM = 16.0
ROPE_THETA = 10000.0

LANES = 128
SUBLANES = 8
PROJ_ROW_TILE = 1024
PROJ_ROW_PARTS = 2
FFN_ROW_TILE = 512
FFN_ROW_PARTS = 1
FFN_CHUNKS = 11
CHUNK = 256
CTX_SEQS_PER_STEP = 2
ATTN_ROW_TILE = 256
GLA_ROW_BLOCK = 128
VMEM_LIMIT = 56 * 1024 * 1024


def _cparams(sem, vmem=VMEM_LIMIT):
    return pltpu.CompilerParams(dimension_semantics=sem, vmem_limit_bytes=vmem)


def _resident(block_shape, index_map):
    return pl.BlockSpec(block_shape, index_map, pipeline_mode=pl.Buffered(1))


def _dot(a, b):
    return jnp.dot(a, b, preferred_element_type=F32)


def _dot_nt(a, b):
    return lax.dot_general(a, b, (((1,), (1,)), ((), ())), preferred_element_type=F32)


def _dot_tn(a, b):
    return lax.dot_general(a, b, (((0,), (0,)), ((), ())), preferred_element_type=F32)


def _bf(x):
    return x.astype(BF16)


def _sigmoid(x):
    return 0.5 * jnp.tanh(0.5 * x) + 0.5


def _silu(x):
    return x * _sigmoid(x)


def _softplus(x):
    return jnp.maximum(x, 0.0) + jnp.log(1.0 + jnp.exp(-jnp.abs(x)))


def _rms(x):
    return x * lax.rsqrt(jnp.mean(x * x, axis=-1, keepdims=True) + EPS)


def _modnorm(x, g, shift, scale):
    return (_rms(x) * g) * (1.0 + scale) + shift


def _cumsum_rows(tri_bf, x):
    hi = _bf(x)
    lo = _bf(x - hi.astype(F32))
    return _dot(tri_bf, hi) + _dot(tri_bf, lo)


def _order_masks(ri, ci, reverse):
    if reverse:
        return ri <= ci, ri < ci
    return ri >= ci, ri > ci


class _Tiles:
    def __init__(self, n_p, t_p, n_s, t_s, c):
        self.n_p, self.n_s, self.c = n_p, n_s, c
        self.per_p, self.per_s = t_p // c, t_s // c
        self.np_tiles = n_p * self.per_p
        self.n = self.np_tiles + n_s * self.per_s

    def is_ctx(self, i):
        return i < self.np_tiles

    def seq(self, i):
        return jnp.where(i < self.np_tiles, i // self.per_p, self.n_p + (i - self.np_tiles) // self.per_s)

    def pos(self, i):
        return jnp.where(i < self.np_tiles, i % self.per_p, (i - self.np_tiles) % self.per_s)

    def length(self, i):
        return jnp.where(i < self.np_tiles, self.per_p, self.per_s)

    def row_block(self, i, reverse):
        return i + self.length(i) - 1 - 2 * self.pos(i) if reverse else i

    def ctx_seq(self, i):
        return jnp.minimum(self.seq(i), self.n_p - 1)

    def latent_seq(self, i):
        return jnp.maximum(self.seq(i) - self.n_p, 0)


def _ada_kernel(c_ref, w_ref, b_ref, o_ref):
    s = _bf(_silu(c_ref[...]))
    o_ref[0] = _dot(s, _bf(w_ref[0])) + b_ref[0]


def _ada(cond, ada_w, ada_b):
    depth, d, n = ada_w.shape
    rows = cond.shape[0]
    tn = n // 4
    return pl.pallas_call(
        _ada_kernel,
        out_shape=jax.ShapeDtypeStruct((depth, rows, n), F32),
        grid=(depth, n // tn),
        in_specs=[pl.BlockSpec((rows, d), lambda l, j: (0, 0)),
                  pl.BlockSpec((1, d, tn), lambda l, j: (l, 0, j)),
                  pl.BlockSpec((1, 1, tn), lambda l, j: (l, 0, j))],
        out_specs=pl.BlockSpec((1, rows, tn), lambda l, j: (l, 0, j)),
        compiler_params=_cparams(("parallel", "parallel")),
        name="ada",
    )(cond, ada_w, ada_b.reshape(depth, 1, n))


def _cond_index(n_prompt_rows, dec_seq, tm):
    npt = n_prompt_rows // tm

    def cond(i):
        return jnp.where(i < npt, 0, 1 + ((i - npt) * tm) // dec_seq)

    return cond


def _mixer_residual(x, gate_mod, of_ref, ob_ref, gate_ref, on_ref, w_ref, extra, rs, nh, dv):
    od = of_ref[rs, :].astype(F32) + ob_ref[rs, :].astype(F32)
    gate = gate_ref[rs, :].astype(F32)
    y = None if extra is None else _dot(extra, w_ref[nh * dv:, :])
    per = max(1, (2 * LANES) // dv)
    for h0 in range(0, nh, per):
        mix = [_bf((_rms(od[:, h * dv:(h + 1) * dv]) * on_ref[...]) * _silu(gate[:, h * dv:(h + 1) * dv]))
               for h in range(h0, min(h0 + per, nh))]
        part = _dot(jnp.concatenate(mix, axis=1), w_ref[h0 * dv:min(h0 + per, nh) * dv, :])
        y = part if y is None else y + part
    return x + gate_mod * y


def _ffn_kernel(*refs, i0, d_ff, n_chunks, parts, npt, split_in, mixer, final):
    refs = list(refs)
    x_refs = [refs.pop(0) for _ in range(2 if split_in else 1)]
    mod_ref, g_ref, wgu_ref, wd_ref = (refs.pop(0) for _ in range(4))
    if mixer is not None:
        nh, dv, has_extra = mixer
        of_ref, ob_ref, gate_ref, on_ref, wout_ref = (refs.pop(0) for _ in range(5))
        extra_refs = [refs.pop(0) for _ in range(2 if has_extra else 0)]
    rest = refs
    i = pl.program_id(0)
    mod = mod_ref[0]
    ch = d_ff // n_chunks
    rp = x_refs[0].shape[0] // parts
    outs = []
    for p in range(parts):
        rs = slice(p * rp, (p + 1) * rp)
        if split_in:
            x = jnp.where(i < npt, x_refs[0][rs, :], x_refs[1][rs, :])
        else:
            x = x_refs[0][rs, :]
        if mixer is not None:
            extra = jnp.where(i < npt, extra_refs[0][rs, :], extra_refs[1][rs, :]) if has_extra else None
            x = _mixer_residual(x, mod[i0 - 1:i0], of_ref, ob_ref, gate_ref, on_ref, wout_ref, extra, rs, nh, dv)
        h = _modnorm(x, g_ref[...], mod[i0:i0 + 1], mod[i0 + 1:i0 + 2]).astype(wgu_ref.dtype)
        acts = []
        for c in range(n_chunks):
            gt = _dot(h, wgu_ref[:, c * ch:(c + 1) * ch])
            up = _dot(h, wgu_ref[:, d_ff + c * ch:d_ff + (c + 1) * ch])
            acts.append((_silu(gt) * up).astype(wd_ref.dtype))
        out = x + (0.5 * mod[i0 + 2:i0 + 3]) * _dot(jnp.concatenate(acts, axis=1), wd_ref[...])
        if final:
            outs.append((rs, _rms(out) * rest[0][...]))
        else:
            rest[0][rs, :] = out
    if final:
        _, yp_ref, ys_ref = rest

        @pl.when(i < npt)
        def _():
            for rs, out in outs:
                yp_ref[rs, :] = out

        @pl.when(i >= npt)
        def _():
            for rs, out in outs:
                ys_ref[rs, :] = out


def _ffn_half(xs, mod, i0, g, w_gu, w_down, widx, rows, mixer=None, final_g=None):
    split_in = len(xs) == 2
    d = xs[0].shape[1]
    nt = sum(x.shape[0] for x in xs)
    d_ff = w_down.shape[-2]
    tm = FFN_ROW_TILE
    np_rows = rows[0]
    assert np_rows % tm == 0 and rows[1] % tm == 0 and d_ff % (FFN_CHUNKS * LANES) == 0
    cond = _cond_index(*rows, tm)
    npt = np_rows // tm
    final = final_g is not None
    ctx_map = lambda i: (jnp.minimum(i, npt - 1), 0)
    lat_map = lambda i: (jnp.maximum(i - npt, 0), 0)
    mix_specs, mix_args, mix_cfg = [], [], None
    if mixer is not None:
        o_f, o_b, gate, onorm, w_out, nh, dv, extra = mixer
        assert i0 >= 1
        wa = nh * dv
        row = lambda i: (i, 0)
        mix_specs = [pl.BlockSpec((tm, wa), row), pl.BlockSpec((tm, wa), row), pl.BlockSpec((tm, wa), row),
                     pl.BlockSpec((1, dv), lambda i: (0, 0)), _resident(w_out.shape, lambda i: (0, 0))]
        mix_args = [o_f, o_b, gate, onorm.reshape(1, dv), w_out]
        if extra is not None:
            we = extra[0].shape[1]
            mix_specs += [pl.BlockSpec((tm, we), ctx_map), pl.BlockSpec((tm, we), lat_map)]
            mix_args += list(extra)
        mix_cfg = (nh, dv, extra is not None)
    kern = functools.partial(_ffn_kernel, i0=i0, d_ff=d_ff, n_chunks=FFN_CHUNKS, parts=FFN_ROW_PARTS, npt=npt,
                             split_in=split_in, mixer=mix_cfg, final=final)
    if split_in:
        x_specs = [pl.BlockSpec((tm, d), ctx_map), pl.BlockSpec((tm, d), lat_map)]
    else:
        x_specs = [pl.BlockSpec((tm, d), lambda i: (i, 0))]
    in_specs = x_specs + [pl.BlockSpec((1, N_MOD, d), lambda i: (cond(i), 0, 0)),
                          pl.BlockSpec((1, d), lambda i: (0, 0)),
                          _resident((None, None, d, 2 * d_ff), lambda i: widx + (0, 0)),
                          _resident((None, None, d_ff, d), lambda i: widx + (0, 0))]
    args = list(xs) + [mod, g.reshape(1, d), w_gu, w_down] + mix_args
    in_specs += mix_specs
    if final:
        in_specs.append(pl.BlockSpec((1, d), lambda i: (0, 0)))
        args.append(final_g.reshape(1, d))
        out_shape = [jax.ShapeDtypeStruct((np_rows, d), F32), jax.ShapeDtypeStruct((nt - np_rows, d), F32)]
        out_specs = [pl.BlockSpec((tm, d), ctx_map), pl.BlockSpec((tm, d), lat_map)]
    else:
        out_shape = jax.ShapeDtypeStruct((nt, d), F32)
        out_specs = pl.BlockSpec((tm, d), lambda i: (i, 0))
    return pl.pallas_call(
        kern,
        out_shape=out_shape,
        grid=(nt // tm,),
        in_specs=in_specs,
        out_specs=out_specs,
        compiler_params=_cparams(("arbitrary",)),
        name="ffn_half",
    )(*args)


def _proj_kernel(x_ref, mod_ref, g_ref, *refs, i0, parts):
    n_out = len(refs) // 2
    w_refs, o_refs = refs[:n_out], refs[n_out:]
    mod = mod_ref[0]
    rp = x_ref.shape[0] // parts
    for p in range(parts):
        rs = slice(p * rp, (p + 1) * rp)
        h = _modnorm(x_ref[rs, :], g_ref[...], mod[i0:i0 + 1], mod[i0 + 1:i0 + 2])
        for w_ref, o_ref in zip(w_refs, o_refs):
            o_ref[rs, :] = _dot(h, w_ref[...]).astype(o_ref.dtype)


def _mixer_in(x, mod, g, groups, dtypes, rows):
    nt, d = x.shape
    tm = PROJ_ROW_TILE
    assert rows[0] % tm == 0 and rows[1] % tm == 0
    cond = _cond_index(*rows, tm)
    kern = functools.partial(_proj_kernel, i0=3, parts=PROJ_ROW_PARTS)
    w_specs = []
    for w, layer, off, wd in groups:
        assert off % wd == 0 and off + wd <= w.shape[-1]
        if layer is None:
            w_specs.append(_resident((d, wd), functools.partial(lambda b, i: (0, b), off // wd)))
        else:
            w_specs.append(_resident((None, d, wd), functools.partial(lambda a, b, i: (a, 0, b), layer, off // wd)))
    return pl.pallas_call(
        kern,
        out_shape=[jax.ShapeDtypeStruct((nt, grp[3]), dt) for grp, dt in zip(groups, dtypes)],
        grid=(nt // tm,),
        in_specs=[pl.BlockSpec((tm, d), lambda i: (i, 0)),
                  pl.BlockSpec((1, N_MOD, d), lambda i: (cond(i), 0, 0)),
                  pl.BlockSpec((1, d), lambda i: (0, 0))] + w_specs,
        out_specs=[pl.BlockSpec((tm, grp[3]), lambda i: (i, 0)) for grp in groups],
        compiler_params=_cparams(("parallel",)),
        name="mixer_in",
    )(x, mod, g.reshape(1, d), *[grp[0] for grp in groups])


def _conv_kernel(prev_ref, x_ref, next_ref, w_ref, o_ref, *, tiles, dk):
    r = pl.program_id(0)
    c, width = x_ref.shape
    pad = (SHORT_CONV - 1) // 2
    has_prev = jnp.where(tiles.pos(r) > 0, 1.0, 0.0)
    has_next = jnp.where(tiles.pos(r) < tiles.length(r) - 1, 1.0, 0.0)
    ext = c + 2 * SUBLANES
    for hh in range(width // dk):
        part = hh // (width // (3 * dk))
        sl = slice(hh * dk, (hh + 1) * dk)
        xe = jnp.concatenate([prev_ref[:, sl] * has_prev, x_ref[:, sl], next_ref[:, sl] * has_next], axis=0)
        w = w_ref[:, sl]
        acc = None
        for j in range(SHORT_CONV):
            sh = pad - j
            xs = xe if sh == 0 else pltpu.roll(xe, sh % ext, axis=0)
            term = xs[SUBLANES:SUBLANES + c] * w[j:j + 1]
            acc = term if acc is None else acc + term
        y = _silu(acc)
        if part < 2:
            nrm = lax.rsqrt(jnp.sum(y * y, axis=-1, keepdims=True) + EPS)
            y = y * (nrm * (dk ** -0.5) if part == 0 else nrm)
        o_ref[:, sl] = y


def _conv_qkv(qkv, conv_w, tiles):
    nt, width = qkv.shape
    c = tiles.c
    per = c // SUBLANES
    n8 = nt // SUBLANES
    kern = functools.partial(_conv_kernel, tiles=tiles, dk=A_DK)
    return pl.pallas_call(
        kern,
        out_shape=jax.ShapeDtypeStruct((nt, width), F32),
        grid=(tiles.n,),
        in_specs=[pl.BlockSpec((SUBLANES, width), lambda r: (jnp.maximum(r * per - 1, 0), 0)),
                  pl.BlockSpec((c, width), lambda r: (r, 0)),
                  pl.BlockSpec((SUBLANES, width), lambda r: (jnp.minimum((r + 1) * per, n8 - 1), 0)),
                  pl.BlockSpec((SHORT_CONV, width), lambda r: (0, 0))],
        out_specs=pl.BlockSpec((c, width), lambda r: (r, 0)),
        compiler_params=_cparams(("parallel",)),
        name="conv_qkv",
    )(qkv, qkv, qkv, conv_w)


def _load_state(state, s0_ref, tiles, i):
    first = tiles.pos(i) == 0

    @pl.when(first & tiles.is_ctx(i))
    def _():
        state[...] = jnp.zeros(state.shape, F32)

    @pl.when(first & jnp.logical_not(tiles.is_ctx(i)))
    def _():
        state[...] = s0_ref[0]


def _store_state(state, out_ref, tiles, i):
    @pl.when((tiles.pos(i) == tiles.length(i) - 1) & tiles.is_ctx(i))
    def _():
        out_ref[0] = state[...]


def _dir_specs(tiles, c, cols):
    specs = []
    for reverse in (False, True):
        for width, col in cols:
            specs.append(pl.BlockSpec((c, width), functools.partial(
                lambda rev, cc, i: (tiles.row_block(i, rev), cc), reverse, col)))
    return specs


def _pair_dot(a, b):
    n = a.shape[0]
    a_bf, b_bf = _bf(a), _bf(b)
    z = jnp.zeros((n, n), BF16)
    b_diag = jnp.concatenate([jnp.concatenate([b_bf[:, :n], z], axis=1),
                              jnp.concatenate([z, b_bf[:, n:]], axis=1)], axis=0)
    return _dot(a_bf, b_diag)


def _unit_tri_solves(ls, rs, reverse_flags):
    c = ls[0].shape[0]
    n = c // 2
    ri = lax.broadcasted_iota(jnp.int32, (n, c), 0)
    ci = lax.broadcasted_iota(jnp.int32, (n, c), 1) & (n - 1)
    pairs = [jnp.concatenate([l[:n, :n], l[n:, n:]], axis=1) for l in ls]
    shift = SUBLANES.bit_length() - 1
    same = (ri >> shift) == (ci >> shift)
    ms = [jnp.where(same, -lp, 0.0) for lp in pairs]
    ps = [_pair_dot(m, m) for m in ms]
    ns = [m + p + _pair_dot(m, p) for m, p in zip(ms, ps)]
    ps = [_pair_dot(p, p) for p in ps]
    ns = [nv + p + _pair_dot(nv, p) for nv, p in zip(ns, ps)]
    while (1 << shift) < n:
        lvl = ((ri >> (shift + 1)) == (ci >> (shift + 1))) & ((ri >> shift) != (ci >> shift))
        cls = [jnp.where(lvl, lp, 0.0) for lp in pairs]
        ys = [cl + _pair_dot(cl, nv) for cl, nv in zip(cls, ns)]
        ns = [nv - (y + _pair_dot(nv, y)) for nv, y in zip(ns, ys)]
        shift += 1
    firsts, seconds = [], []
    for l, r, nv, rev in zip(ls, rs, ns, reverse_flags):
        if rev:
            firsts.append((r[n:], _bf(nv[:, n:])))
            seconds.append((r[:n], _bf(nv[:, :n]), _bf(l[:n, n:])))
        else:
            firsts.append((r[:n], _bf(nv[:, :n])))
            seconds.append((r[n:], _bf(nv[:, n:]), _bf(l[n:, :n])))
    xas = [ra + _dot(na, _bf(ra)) for ra, na in firsts]
    ts = [rb - _dot(lba, _bf(xa)) for (rb, _, lba), xa in zip(seconds, xas)]
    xbs = [t + _dot(nb, _bf(t)) for (_, nb, _), t in zip(seconds, ts)]
    return [jnp.concatenate([xb, xa] if rev else [xa, xb], axis=0) for xa, xb, rev in zip(xas, xbs, reverse_flags)]


def _delta_kernel(qf_ref, kf_ref, vf_ref, baf_ref, qb_ref, kb_ref, vb_ref, bab_ref, al_ref, dt_ref, s0_ref,
                  of_ref, ob_ref, sout_ref, state, *, tiles, nh, dk, dv):
    step = pl.program_id(0)
    _load_state(state, s0_ref, tiles, step)
    c = qf_ref.shape[0]
    ri = lax.broadcasted_iota(jnp.int32, (c, c), 0)
    ci = lax.broadcasted_iota(jnp.int32, (c, c), 1)
    units = []
    for z, (q_ref, k_ref, v_ref, ba_ref, o_ref) in enumerate(
            ((qf_ref, kf_ref, vf_ref, baf_ref, of_ref), (qb_ref, kb_ref, vb_ref, bab_ref, ob_ref))):
        reverse = z == 1
        incl, strict = _order_masks(ri, ci, reverse)
        ba = ba_ref[...]
        g_col = -jnp.exp(al_ref[...]) * _softplus(ba + dt_ref[...])
        beta_col = _sigmoid(ba)
        gc_col = _cumsum_rows(_bf(incl.astype(F32)), g_col)
        gc_row = gc_col.T
        last = 0 if reverse else c - 1
        for h in range(nh):
            cb = z * nh + h
            cg = 2 * nh + cb
            gcc = gc_col[:, cg:cg + 1]
            gcr = gc_row[cg:cg + 1, :]
            units.append(dict(
                z=z, h=h, o_ref=o_ref, strict=strict, gcc=gcc, gl=gcc[last:last + 1],
                beta=beta_col[:, cb:cb + 1], egc=jnp.exp(gcc),
                decay=jnp.where(incl, jnp.exp(jnp.where(incl, gcc - gcr, 0.0)), 0.0),
                q=q_ref[:, h * dk:(h + 1) * dk], k=k_ref[:, h * dk:(h + 1) * dk], v=v_ref[:, h * dv:(h + 1) * dv]))
    for u in units:
        u["kb"] = u["k"] * u["beta"]
        u["k_bf"] = _bf(u["k"])
    kks = [_dot_nt(_bf(u["kb"]), u["k_bf"]) for u in units]
    qks = [_dot_nt(_bf(u["q"]), u["k_bf"]) * u["decay"] for u in units]
    rs = _unit_tri_solves([jnp.where(u["strict"], kk * u["decay"], 0.0) for u, kk in zip(units, kks)],
                          [jnp.concatenate([u["v"] * u["beta"], u["kb"] * u["egc"]], axis=1) for u in units],
                          [u["z"] == 1 for u in units])
    ss = [state[u["z"], u["h"]] for u in units]
    ss_bf = [_bf(s) for s in ss]
    v_news_bf = [_bf(r[:, :dv] - _dot(_bf(r[:, dv:]), sb)) for r, sb in zip(rs, ss_bf)]
    for u, sb, qk, vnb in zip(units, ss_bf, qks, v_news_bf):
        h = u["h"]
        u["o_ref"][:, h * dv:(h + 1) * dv] = _bf(_dot(_bf(u["q"] * u["egc"]), sb) + _dot(_bf(qk), vnb))
    for u, s, vnb in zip(units, ss, v_news_bf):
        kd = u["k"] * jnp.exp(u["gl"] - u["gcc"])
        state[u["z"], u["h"]] = s * jnp.exp(u["gl"]) + _dot_tn(_bf(kd), vnb)
    _store_state(state, sout_ref, tiles, step)


def _delta_rule(qkv, ba, a_log, dt_bias, s0, tiles):
    nt = qkv.shape[0]
    nh, dk, dv = A_HEADS, A_DK, A_DV
    c = tiles.c
    pad = LANES - 4 * nh
    al = jnp.concatenate([jnp.zeros((2 * nh,), F32), a_log.reshape(-1), jnp.zeros((pad,), F32)])
    dt = jnp.concatenate([jnp.zeros((2 * nh,), F32), dt_bias.reshape(-1), jnp.zeros((pad,), F32)])
    const = lambda i: (0, 0)
    in_specs = (_dir_specs(tiles, c, ((nh * dk, 0), (nh * dk, 1), (nh * dv, 2), (LANES, 0)))
                + [pl.BlockSpec((1, LANES), const), pl.BlockSpec((1, LANES), const),
                   pl.BlockSpec((1, 2, nh, dk, dv), lambda i: (tiles.latent_seq(i), 0, 0, 0, 0))])
    kern = functools.partial(_delta_kernel, tiles=tiles, nh=nh, dk=dk, dv=dv)
    return pl.pallas_call(
        kern,
        out_shape=[jax.ShapeDtypeStruct((nt, nh * dv), BF16), jax.ShapeDtypeStruct((nt, nh * dv), BF16),
                   jax.ShapeDtypeStruct((tiles.n_p, 2, nh, dk, dv), F32)],
        grid=(tiles.n,),
        in_specs=in_specs,
        out_specs=[pl.BlockSpec((c, nh * dv), lambda i: (tiles.row_block(i, False), 0)),
                   pl.BlockSpec((c, nh * dv), lambda i: (tiles.row_block(i, True), 0)),
                   pl.BlockSpec((1, 2, nh, dk, dv), lambda i: (tiles.ctx_seq(i), 0, 0, 0, 0))],
        scratch_shapes=[pltpu.VMEM((2, nh, dk, dv), F32)],
        compiler_params=_cparams(("arbitrary",)),
        name="delta_rule",
    )(qkv, qkv, qkv, ba, qkv, qkv, qkv, ba, al.reshape(1, LANES), dt.reshape(1, LANES), s0)


def _ctx_attn_kernel(sink_ref, q_ref, k_ref, v_ref, o_ref, *, n_kv, grp, dh, t):
    scale = dh ** -0.5
    seqs = [slice(b * t, (b + 1) * t) for b in range(q_ref.shape[0] // t)]
    units = [(rs, h) for rs in seqs for h in range(n_kv * grp)]
    ks = {(rs.start, hk): _bf(k_ref[rs, hk * dh:(hk + 1) * dh]) for rs in seqs for hk in range(n_kv)}
    vs = {(rs.start, hk): _bf(v_ref[rs, hk * dh:(hk + 1) * dh]) for rs in seqs for hk in range(n_kv)}
    sinks = [sink_ref[h] for _, h in units]
    ss = [_dot_nt(_bf(q_ref[rs, h * dh:(h + 1) * dh] * scale), ks[rs.start, h // grp]) for rs, h in units]
    ms = [jnp.maximum(jnp.max(s, axis=-1, keepdims=True), sk) for s, sk in zip(ss, sinks)]
    ps = [jnp.exp(s - m) for s, m in zip(ss, ms)]
    dens = [jnp.sum(p, axis=-1, keepdims=True) + jnp.exp(sk - m) for p, sk, m in zip(ps, sinks, ms)]
    for (rs, h), p, den in zip(units, ps, dens):
        o_ref[rs, h * dh:(h + 1) * dh] = _bf(_dot(_bf(p), vs[rs.start, h // grp]) / den)


def _ctx_attention(qb, kb, vb, sink, n_seq, t):
    wq = qb.shape[1]
    wk = kb.shape[1]
    per = math.gcd(n_seq, CTX_SEQS_PER_STEP)
    kern = functools.partial(_ctx_attn_kernel, n_kv=B_KV_HEADS, grp=B_Q_HEADS // B_KV_HEADS, dh=HEAD_DIM, t=t)
    return pl.pallas_call(
        kern,
        out_shape=jax.ShapeDtypeStruct((n_seq * t, wq), BF16),
        grid=(n_seq // per,),
        in_specs=[pl.BlockSpec(memory_space=pltpu.SMEM),
                  pl.BlockSpec((per * t, wq), lambda b: (b, 0)),
                  pl.BlockSpec((per * t, wk), lambda b: (b, 0)),
                  pl.BlockSpec((per * t, wk), lambda b: (b, 0))],
        out_specs=pl.BlockSpec((per * t, wq), lambda b: (b, 0)),
        compiler_params=_cparams(("parallel",)),
        name="ctx_attention",
    )(sink, qb, kb, vb)


def _rope(x, cos, sin_signed):
    lane = lax.broadcasted_iota(jnp.int32, x.shape, 1)
    quarter = HEAD_DIM // 4
    partner = jnp.where((lane % (2 * quarter)) < quarter,
                        pltpu.roll(x, HEAD_DIM - quarter, axis=1), pltpu.roll(x, quarter, axis=1))
    return x * cos + partner * sin_signed


def _lat_attn_kernel(sink_ref, q_ref, k_ref, v_ref, kc_ref, vc_ref, cq_ref, sq_ref, ck_ref, sk_ref,
                     o_ref, *, n_kv, grp, dh, window):
    scale = dh ** -0.5
    tq = q_ref.shape[0]
    t = k_ref.shape[0]
    span = min(t, tq + 2 * window)
    q0 = pl.program_id(1) * tq
    start = pl.multiple_of(jnp.clip(q0 - window, 0, t - span), math.gcd(tq, window))
    rows = pl.ds(start, span)
    qpos = q0 + lax.broadcasted_iota(jnp.int32, (tq, span), 0)
    kpos = start + lax.broadcasted_iota(jnp.int32, (tq, span), 1)
    valid = jnp.abs(qpos - kpos) <= window
    cq, sq = cq_ref[...], sq_ref[...]
    ck, sk_t = ck_ref[rows, :], sk_ref[rows, :]
    kv = []
    for hk in range(n_kv):
        hs = slice(hk * dh, (hk + 1) * dh)
        kv.append((_bf(_rope(k_ref[rows, hs], ck, sk_t)), _bf(v_ref[rows, hs]), _bf(kc_ref[0, :, hs]),
                   _bf(vc_ref[0, :, hs])))
    heads = range(n_kv * grp)
    sinks = [sink_ref[h] for h in heads]
    qs = [q_ref[:, h * dh:(h + 1) * dh] * scale for h in heads]
    s_locs = [jnp.where(valid, _dot_nt(_bf(_rope(q, cq, sq)), kv[h // grp][0]), -jnp.inf) for h, q in zip(heads, qs)]
    s_ctxs = [_dot_nt(_bf(q), kv[h // grp][2]) for h, q in zip(heads, qs)]
    ms = [jnp.maximum(jnp.maximum(jnp.max(sl, axis=-1, keepdims=True), jnp.max(sc, axis=-1, keepdims=True)), sk)
          for sl, sc, sk in zip(s_locs, s_ctxs, sinks)]
    p_locs = [jnp.exp(sl - m) for sl, m in zip(s_locs, ms)]
    p_ctxs = [jnp.exp(sc - m) for sc, m in zip(s_ctxs, ms)]
    dens = [jnp.sum(pl_, axis=-1, keepdims=True) + jnp.sum(pc, axis=-1, keepdims=True) + jnp.exp(sk - m)
            for pl_, pc, sk, m in zip(p_locs, p_ctxs, sinks, ms)]
    for h, pl_, pc, den in zip(heads, p_locs, p_ctxs, dens):
        o_ref[:, h * dh:(h + 1) * dh] = _bf((_dot(_bf(pl_), kv[h // grp][1]) + _dot(_bf(pc), kv[h // grp][3])) / den)


def _lat_attention(qb, kb, vb, k_ctx, v_ctx, sink, cos, sin_signed, row0, n_seq, t):
    wq = qb.shape[1]
    wk = kb.shape[1]
    tq = ATTN_ROW_TILE
    assert t % tq == 0 and row0 % t == 0
    nq = t // tq
    base_q = row0 // tq
    base_t = row0 // t
    past = k_ctx.shape[1]
    kern = functools.partial(_lat_attn_kernel, n_kv=B_KV_HEADS, grp=B_Q_HEADS // B_KV_HEADS, dh=HEAD_DIM,
                             window=WINDOW)
    return pl.pallas_call(
        kern,
        out_shape=jax.ShapeDtypeStruct((n_seq * t, wq), BF16),
        grid=(n_seq, nq),
        in_specs=[pl.BlockSpec(memory_space=pltpu.SMEM),
                  pl.BlockSpec((tq, wq), lambda b, i: (base_q + b * nq + i, 0)),
                  pl.BlockSpec((t, wk), lambda b, i: (base_t + b, 0)),
                  pl.BlockSpec((t, wk), lambda b, i: (base_t + b, 0)),
                  pl.BlockSpec((1, past, wk), lambda b, i: (b, 0, 0)),
                  pl.BlockSpec((1, past, wk), lambda b, i: (b, 0, 0)),
                  pl.BlockSpec((tq, HEAD_DIM), lambda b, i: (i, 0)),
                  pl.BlockSpec((tq, HEAD_DIM), lambda b, i: (i, 0)),
                  pl.BlockSpec((t, HEAD_DIM), lambda b, i: (0, 0)),
                  pl.BlockSpec((t, HEAD_DIM), lambda b, i: (0, 0))],
        out_specs=pl.BlockSpec((tq, wq), lambda b, i: (b * nq + i, 0)),
        compiler_params=_cparams(("parallel", "parallel")),
        name="latent_attention",
    )(sink, qb, kb, vb, k_ctx, v_ctx, cos, sin_signed, cos, sin_signed)


def _rope_tables(t):
    half = HEAD_DIM // 2
    quarter = half // 2
    pos = jnp.arange(t)
    row = (pos // GRID_W).astype(F32)
    col = (pos % GRID_W).astype(F32)
    inv = ROPE_THETA ** (-jnp.arange(quarter, dtype=F32) / quarter)
    ang_r = row[:, None] * inv[None, :]
    ang_c = col[:, None] * inv[None, :]
    cos = jnp.concatenate([jnp.cos(ang_r), jnp.cos(ang_r), jnp.cos(ang_c), jnp.cos(ang_c)], axis=-1)
    sin = jnp.concatenate([-jnp.sin(ang_r), jnp.sin(ang_r), -jnp.sin(ang_c), jnp.sin(ang_c)], axis=-1)
    return cos, sin


def _gla_kernel(qf_ref, kf_ref, vf_ref, lrf_ref, qb_ref, kb_ref, vb_ref, lrb_ref, wg_ref, bias_ref, s0_ref,
                of_ref, ob_ref, sout_ref, state, *, tiles, nh, dk, dv, rb):
    step = pl.program_id(0)
    _load_state(state, s0_ref, tiles, step)
    c = qf_ref.shape[0]
    ri = lax.broadcasted_iota(jnp.int32, (c, c), 0)
    ci = lax.broadcasted_iota(jnp.int32, (c, c), 1)
    eye = (lax.broadcasted_iota(jnp.int32, (dk, dk), 0) == lax.broadcasted_iota(jnp.int32, (dk, dk), 1))
    units = []
    for z, (q_ref, k_ref, v_ref, lr_ref, o_ref) in enumerate(
            ((qf_ref, kf_ref, vf_ref, lrf_ref, of_ref), (qb_ref, kb_ref, vb_ref, lrb_ref, ob_ref))):
        reverse = z == 1
        incl, _ = _order_masks(ri, ci, reverse)
        x = _dot(_bf(lr_ref[...]), _bf(wg_ref[z])) + bias_ref[z]
        gk = -_softplus(-x) * (1.0 / GATE_NORM)
        gcum = _cumsum_rows(_bf(incl.astype(F32)), gk)
        last = 0 if reverse else c - 1
        for h in range(nh):
            g = gcum[:, h * dk:(h + 1) * dk]
            units.append(dict(
                z=z, h=h, o_ref=o_ref, reverse=reverse, incl=incl, g=g, gl_row=g[last:last + 1],
                q=q_ref[:, h * dk:(h + 1) * dk] * (dk ** -0.5), k=k_ref[:, h * dk:(h + 1) * dk],
                v_bf=v_ref[:, h * dv:(h + 1) * dv], s=state[z, h]))
    intra = [[] for _ in units]
    for blk in range(c // rb):
        r0, r1 = blk * rb, (blk + 1) * rb
        scores, cols = [], []
        for u in units:
            g = u["g"]
            mid = r0 + rb // 2
            if u["reverse"]:
                c0, c1 = r0, c
                ref = g[mid:mid + 1]
            else:
                c0, c1 = 0, r1
                ref = g[mid - 1:mid]
            qe = u["q"][r0:r1] * jnp.exp(g[r0:r1] - ref)
            ke = u["k"][c0:c1] * jnp.exp(ref - g[c0:c1])
            scores.append(jnp.where(u["incl"][r0:r1, c0:c1], _dot_nt(_bf(qe), _bf(ke)), 0.0))
            cols.append((c0, c1))
        for parts, u, a, (c0, c1) in zip(intra, units, scores, cols):
            parts.append(_dot(_bf(a), u["v_bf"][c0:c1]))
    inter = [_dot(_bf(u["q"] * jnp.exp(u["g"])), _bf(u["s"])) for u in units]
    for u, o_inter, parts in zip(units, inter, intra):
        h = u["h"]
        u["o_ref"][:, h * dv:(h + 1) * dv] = _bf(o_inter + jnp.concatenate(parts, axis=0))
    for u in units:
        kd = u["k"] * jnp.exp(u["gl_row"] - u["g"])
        gl_col = jnp.sum(jnp.where(eye, jnp.broadcast_to(u["gl_row"], (dk, dk)), 0.0), axis=1, keepdims=True)
        state[u["z"], u["h"]] = u["s"] * jnp.exp(gl_col) + _dot_tn(_bf(kd), u["v_bf"])
    _store_state(state, sout_ref, tiles, step)


def _gla(q, k, v, lr, w_gate, gate_bias, s0, tiles):
    nt = q.shape[0]
    nh, dk, dv = C_HEADS, C_DK, C_DV
    c = tiles.c
    wg = jnp.zeros((2, LANES, nh * dk), F32)
    for z in range(2):
        wg = wg.at[z, z * GATE_RANK:(z + 1) * GATE_RANK].set(w_gate[z])
    in_specs = (_dir_specs(tiles, c, ((nh * dk, 0), (nh * dk, 0), (nh * dv, 0), (LANES, 0)))
                + [pl.BlockSpec((2, LANES, nh * dk), lambda i: (0, 0, 0)),
                   pl.BlockSpec((2, 1, nh * dk), lambda i: (0, 0, 0)),
                   pl.BlockSpec((1, 2, nh, dk, dv), lambda i: (tiles.latent_seq(i), 0, 0, 0, 0))])
    kern = functools.partial(_gla_kernel, tiles=tiles, nh=nh, dk=dk, dv=dv, rb=GLA_ROW_BLOCK)
    return pl.pallas_call(
        kern,
        out_shape=[jax.ShapeDtypeStruct((nt, nh * dv), BF16), jax.ShapeDtypeStruct((nt, nh * dv), BF16),
                   jax.ShapeDtypeStruct((tiles.n_p, 2, nh, dk, dv), F32)],
        grid=(tiles.n,),
        in_specs=in_specs,
        out_specs=[pl.BlockSpec((c, nh * dv), lambda i: (tiles.row_block(i, False), 0)),
                   pl.BlockSpec((c, nh * dv), lambda i: (tiles.row_block(i, True), 0)),
                   pl.BlockSpec((1, 2, nh, dk, dv), lambda i: (tiles.ctx_seq(i), 0, 0, 0, 0))],
        scratch_shapes=[pltpu.VMEM((2, nh, dk, dv), F32)],
        compiler_params=_cparams(("arbitrary",)),
        name="gla",
    )(q, k, v, lr, q, k, v, lr, wg, gate_bias.reshape(2, 1, nh * dk), s0)


def kernel(x_prompt, x_sample, state_delta, cache_k, cache_v, state_gla, c, c_ctx, norm_g, ada_w, ada_b,
           ffn_w_gu, ffn_w_down, even_w_in, even_conv, even_a_log, even_dt_bias, even_onorm, even_sink,
           even_w_out, odd_w_in, odd_w_gate, odd_gate_bias, odd_onorm, odd_w_out, final_g):
    n_p, t_p, d = x_prompt.shape
    n_s, t_s, _ = x_sample.shape
    depth = norm_g.shape[0]
    np_rows, ns_rows = n_p * t_p, n_s * t_s
    assert np_rows % t_s == 0
    assert t_p % CHUNK == 0 and t_s % CHUNK == 0 and t_s % GRID_W == 0
    rows = (np_rows, t_s)
    tiles = _Tiles(n_p, t_p, n_s, t_s, CHUNK)

    n_cond = 1 + n_s
    cond_rows = -(-n_cond // SUBLANES) * SUBLANES
    conds = jnp.concatenate([c_ctx[None, :], c, jnp.zeros((cond_rows - n_cond, d), F32)], axis=0)
    mods = _ada(conds, ada_w, ada_b)[:, :n_cond].reshape(depth, n_cond, N_MOD, d)

    w_gu_second = ffn_w_gu[:, 1:].astype(BF16)
    w_down_second = ffn_w_down[:, 1:].astype(BF16)

    xs = (x_prompt.reshape(np_rows, d), x_sample.reshape(ns_rows, d))
    new_delta, new_k, new_v, new_gla = [], [], [], []
    for l in range(depth):
        j = l // 2
        mod = mods[l]
        x = _ffn_half(xs, mod, 0, norm_g[l, 0], ffn_w_gu, ffn_w_down, (l, 0), rows)
        if l % 2 == 0:
            nh = A_HEADS
            o_qkv = 2 * nh * A_DK + nh * A_DV
            o_gate = o_qkv + nh * A_DV
            o_qb = o_gate + 4 * nh
            w_q, w_kv = B_Q_HEADS * HEAD_DIM, B_KV_HEADS * HEAD_DIM
            w = even_w_in[j]
            tail = jnp.concatenate([w[:, o_qb:], w[:, o_gate:o_qb], jnp.zeros((d, LANES - 4 * nh), F32)], axis=1)
            assert tail.shape[1] == w_q + 2 * w_kv + LANES and o_qkv % (nh * A_DV) == 0
            groups = ((even_w_in, j, 0, o_qkv), (even_w_in, j, o_qkv, nh * A_DV), (tail, None, 0, w_q),
                      (tail, None, w_q, w_kv), (tail, None, w_q + w_kv, w_kv), (tail, None, w_q + 2 * w_kv, LANES))
            dtypes = (F32, BF16, F32, F32, F32, F32)
            qkv, gate, qb, kb, vb, ba = _mixer_in(x, mod, norm_g[l, 1], groups, dtypes, rows)
            qkv_n = _conv_qkv(qkv, even_conv[j], tiles)
            o_f, o_b, st = _delta_rule(qkv_n, ba, even_a_log[j], even_dt_bias[j], state_delta[:, j], tiles)
            cos, sin_signed = _rope_tables(t_s)
            att_p = _ctx_attention(qb, kb, vb, even_sink[j], n_p, t_p)
            att_s = _lat_attention(qb, kb, vb,
                                   cache_k[:, j].reshape(n_s, -1, B_KV_HEADS * HEAD_DIM),
                                   cache_v[:, j].reshape(n_s, -1, B_KV_HEADS * HEAD_DIM),
                                   even_sink[j], cos, sin_signed, np_rows, n_s, t_s)
            mixer = (o_f, o_b, gate, even_onorm[j], even_w_out[j].astype(BF16), nh, A_DV, (att_p, att_s))
            new_delta.append(st)
            new_k.append(kb[:np_rows].reshape(n_p, t_p, B_KV_HEADS, HEAD_DIM))
            new_v.append(vb[:np_rows].reshape(n_p, t_p, B_KV_HEADS, HEAD_DIM))
        else:
            nh = C_HEADS
            o_v = 2 * nh * C_DK
            o_g = o_v + nh * C_DV
            o_lr = o_g + nh * C_DV
            w_lr = jnp.concatenate([odd_w_in[j][:, o_lr:], jnp.zeros((d, LANES - 2 * GATE_RANK), F32)], axis=1)
            groups = ((odd_w_in, j, 0, nh * C_DK), (odd_w_in, j, nh * C_DK, nh * C_DK), (odd_w_in, j, o_v, nh * C_DV),
                      (odd_w_in, j, o_g, nh * C_DV), (w_lr, None, 0, LANES))
            dtypes = (F32, F32, BF16, BF16, F32)
            q, k, v, g_out, lr = _mixer_in(x, mod, norm_g[l, 1], groups, dtypes, rows)
            o_f, o_b, st = _gla(q, k, v, lr, odd_w_gate[j], odd_gate_bias[j], state_gla[:, j], tiles)
            mixer = (o_f, o_b, g_out, odd_onorm[j], odd_w_out[j].astype(BF16), nh, C_DV, None)
            new_gla.append(st)
        last = l == depth - 1
        xs = _ffn_half((x,), mod, 6, norm_g[l, 2], w_gu_second, w_down_second, (l, 0), rows, mixer=mixer,
                       final_g=final_g if last else None)
        if not last:
            xs = (xs,)

    y_prompt, y_sample = xs
    return (y_prompt.reshape(n_p, t_p, d), y_sample.reshape(n_s, t_s, d), jnp.stack(new_delta, axis=1),
            jnp.stack(new_k, axis=1), jnp.stack(new_v, axis=1), jnp.stack(new_gla, axis=1))
```

```python
import functools
import math

import jax
import jax.numpy as jnp
from jax import lax
from jax.experimental import pallas as pl
from jax.experimental.pallas import tpu as pltpu

F32 = jnp.float32
BF16 = jnp.bfloat16

EPS = 1e-6
N_MOD = 9
GRID_W = 64
HEAD_DIM = 128
A_HEADS = 4
A_DK = 128
A_DV = 128
SHORT_CONV = 5
B_Q_HEADS = 4
B_KV_HEADS = 2
WINDOW = 128
C_HEADS = 4
C_DK = 128
C_DV = 256
GATE_RANK = 16
GATE_NORM = 16.0
ROPE_THETA = 10000.0

LANES = 128
SUBLANES = 8
PROJ_ROW_TILE = 1024
PROJ_ROW_PARTS = 2
FFN_ROW_TILE = 512
FFN_ROW_PARTS = 1
FFN_CHUNKS = 11
CHUNK = 256
CTX_SEQS_PER_STEP = 2
ATTN_ROW_TILE = 256
GLA_ROW_BLOCK = 128
VMEM_LIMIT = 56 * 1024 * 1024


def _cparams(sem, vmem=VMEM_LIMIT):
    return pltpu.CompilerParams(dimension_semantics=sem, vmem_limit_bytes=vmem)


def _resident(block_shape, index_map):
    return pl.BlockSpec(block_shape, index_map, pipeline_mode=pl.Buffered(1))


def _dot(a, b):
    return jnp.dot(a, b, preferred_element_type=F32)


def _dot_nt(a, b):
    return lax.dot_general(a, b, (((1,), (1,)), ((), ())), preferred_element_type=F32)


def _dot_tn(a, b):
    return lax.dot_general(a, b, (((0,), (0,)), ((), ())), preferred_element_type=F32)


def _bf(x):
    return x.astype(BF16)


def _sigmoid(x):
    return 0.5 * jnp.tanh(0.5 * x) + 0.5


def _silu(x):
    return x * _sigmoid(x)


def _softplus(x):
    return jnp.maximum(x, 0.0) + jnp.log(1.0 + jnp.exp(-jnp.abs(x)))


def _rms(x):
    return x * lax.rsqrt(jnp.mean(x * x, axis=-1, keepdims=True) + EPS)


def _modnorm(x, g, shift, scale):
    return (_rms(x) * g) * (1.0 + scale) + shift


def _cumsum_rows(tri_bf, x):
    hi = _bf(x)
    lo = _bf(x - hi.astype(F32))
    return _dot(tri_bf, hi) + _dot(tri_bf, lo)


def _order_masks(ri, ci, reverse):
    if reverse:
        return ri <= ci, ri < ci
    return ri >= ci, ri > ci


class _Tiles:
    def __init__(self, n_p, t_p, n_s, t_s, c):
        self.n_p, self.n_s, self.c = n_p, n_s, c
        self.per_p, self.per_s = t_p // c, t_s // c
        self.np_tiles = n_p * self.per_p
        self.n = self.np_tiles + n_s * self.per_s

    def is_ctx(self, i):
        return i < self.np_tiles

    def seq(self, i):
        return jnp.where(i < self.np_tiles, i // self.per_p, self.n_p + (i - self.np_tiles) // self.per_s)

    def pos(self, i):
        return jnp.where(i < self.np_tiles, i % self.per_p, (i - self.np_tiles) % self.per_s)

    def length(self, i):
        return jnp.where(i < self.np_tiles, self.per_p, self.per_s)

    def row_block(self, i, reverse):
        return i + self.length(i) - 1 - 2 * self.pos(i) if reverse else i

    def ctx_seq(self, i):
        return jnp.minimum(self.seq(i), self.n_p - 1)

    def latent_seq(self, i):
        return jnp.maximum(self.seq(i) - self.n_p, 0)


def _ada_kernel(c_ref, w_ref, b_ref, o_ref):
    s = _bf(_silu(c_ref[...]))
    o_ref[0] = _dot(s, _bf(w_ref[0])) + b_ref[0]


def _ada(cond, ada_w, ada_b):
    depth, d, n = ada_w.shape
    rows = cond.shape[0]
    tn = n // 4
    return pl.pallas_call(
        _ada_kernel,
        out_shape=jax.ShapeDtypeStruct((depth, rows, n), F32),
        grid=(depth, n // tn),
        in_specs=[pl.BlockSpec((rows, d), lambda l, j: (0, 0)),
                  pl.BlockSpec((1, d, tn), lambda l, j: (l, 0, j)),
                  pl.BlockSpec((1, 1, tn), lambda l, j: (l, 0, j))],
        out_specs=pl.BlockSpec((1, rows, tn), lambda l, j: (l, 0, j)),
        compiler_params=_cparams(("parallel", "parallel")),
        name="ada",
    )(cond, ada_w, ada_b.reshape(depth, 1, n))


def _cond_index(n_prompt_rows, dec_seq, tm):
    npt = n_prompt_rows // tm

    def cond(i):
        return jnp.where(i < npt, 0, 1 + ((i - npt) * tm) // dec_seq)

    return cond


def _mixer_residual(x, gate_mod, of_ref, ob_ref, gate_ref, on_ref, w_ref, extra, rs, nh, dv):
    od = of_ref[rs, :].astype(F32) + ob_ref[rs, :].astype(F32)
    gate = gate_ref[rs, :].astype(F32)
    y = None if extra is None else _dot(extra, w_ref[nh * dv:, :])
    per = max(1, (2 * LANES) // dv)
    for h0 in range(0, nh, per):
        mix = [_bf((_rms(od[:, h * dv:(h + 1) * dv]) * on_ref[...]) * _silu(gate[:, h * dv:(h + 1) * dv]))
               for h in range(h0, min(h0 + per, nh))]
        part = _dot(jnp.concatenate(mix, axis=1), w_ref[h0 * dv:min(h0 + per, nh) * dv, :])
        y = part if y is None else y + part
    return x + gate_mod * y


def _ffn_kernel(*refs, i0, d_ff, n_chunks, parts, npt, split_in, mixer, final):
    refs = list(refs)
    x_refs = [refs.pop(0) for _ in range(2 if split_in else 1)]
    mod_ref, g_ref, wgu_ref, wd_ref = (refs.pop(0) for _ in range(4))
    if mixer is not None:
        nh, dv, has_extra = mixer
        of_ref, ob_ref, gate_ref, on_ref, wout_ref = (refs.pop(0) for _ in range(5))
        extra_refs = [refs.pop(0) for _ in range(2 if has_extra else 0)]
    rest = refs
    i = pl.program_id(0)
    mod = mod_ref[0]
    ch = d_ff // n_chunks
    rp = x_refs[0].shape[0] // parts
    outs = []
    for p in range(parts):
        rs = slice(p * rp, (p + 1) * rp)
        if split_in:
            x = jnp.where(i < npt, x_refs[0][rs, :], x_refs[1][rs, :])
        else:
            x = x_refs[0][rs, :]
        if mixer is not None:
            extra = jnp.where(i < npt, extra_refs[0][rs, :], extra_refs[1][rs, :]) if has_extra else None
            x = _mixer_residual(x, mod[i0 - 1:i0], of_ref, ob_ref, gate_ref, on_ref, wout_ref, extra, rs, nh, dv)
        h = _modnorm(x, g_ref[...], mod[i0:i0 + 1], mod[i0 + 1:i0 + 2]).astype(wgu_ref.dtype)
        acts = []
        for c in range(n_chunks):
            gt = _dot(h, wgu_ref[:, c * ch:(c + 1) * ch])
            up = _dot(h, wgu_ref[:, d_ff + c * ch:d_ff + (c + 1) * ch])
            acts.append((_silu(gt) * up).astype(wd_ref.dtype))
        out = x + (0.5 * mod[i0 + 2:i0 + 3]) * _dot(jnp.concatenate(acts, axis=1), wd_ref[...])
        if final:
            outs.append((rs, _rms(out) * rest[0][...]))
        else:
            rest[0][rs, :] = out
    if final:
        _, yp_ref, ys_ref = rest

        @pl.when(i < npt)
        def _():
            for rs, out in outs:
                yp_ref[rs, :] = out

        @pl.when(i >= npt)
        def _():
            for rs, out in outs:
                ys_ref[rs, :] = out


def _ffn_half(xs, mod, i0, g, w_gu, w_down, widx, rows, mixer=None, final_g=None):
    split_in = len(xs) == 2
    d = xs[0].shape[1]
    nt = sum(x.shape[0] for x in xs)
    d_ff = w_down.shape[-2]
    tm = FFN_ROW_TILE
    np_rows = rows[0]
    assert np_rows % tm == 0 and rows[1] % tm == 0 and d_ff % (FFN_CHUNKS * LANES) == 0
    cond = _cond_index(*rows, tm)
    npt = np_rows // tm
    final = final_g is not None
    ctx_map = lambda i: (jnp.minimum(i, npt - 1), 0)
    lat_map = lambda i: (jnp.maximum(i - npt, 0), 0)
    mix_specs, mix_args, mix_cfg = [], [], None
    if mixer is not None:
        o_f, o_b, gate, onorm, w_out, nh, dv, extra = mixer
        assert i0 >= 1
        wa = nh * dv
        row = lambda i: (i, 0)
        mix_specs = [pl.BlockSpec((tm, wa), row), pl.BlockSpec((tm, wa), row), pl.BlockSpec((tm, wa), row),
                     pl.BlockSpec((1, dv), lambda i: (0, 0)), _resident(w_out.shape, lambda i: (0, 0))]
        mix_args = [o_f, o_b, gate, onorm.reshape(1, dv), w_out]
        if extra is not None:
            we = extra[0].shape[1]
            mix_specs += [pl.BlockSpec((tm, we), ctx_map), pl.BlockSpec((tm, we), lat_map)]
            mix_args += list(extra)
        mix_cfg = (nh, dv, extra is not None)
    kern = functools.partial(_ffn_kernel, i0=i0, d_ff=d_ff, n_chunks=FFN_CHUNKS, parts=FFN_ROW_PARTS, npt=npt,
                             split_in=split_in, mixer=mix_cfg, final=final)
    if split_in:
        x_specs = [pl.BlockSpec((tm, d), ctx_map), pl.BlockSpec((tm, d), lat_map)]
    else:
        x_specs = [pl.BlockSpec((tm, d), lambda i: (i, 0))]
    in_specs = x_specs + [pl.BlockSpec((1, N_MOD, d), lambda i: (cond(i), 0, 0)),
                          pl.BlockSpec((1, d), lambda i: (0, 0)),
                          _resident((None, None, d, 2 * d_ff), lambda i: widx + (0, 0)),
                          _resident((None, None, d_ff, d), lambda i: widx + (0, 0))]
    args = list(xs) + [mod, g.reshape(1, d), w_gu, w_down] + mix_args
    in_specs += mix_specs
    if final:
        in_specs.append(pl.BlockSpec((1, d), lambda i: (0, 0)))
        args.append(final_g.reshape(1, d))
        out_shape = [jax.ShapeDtypeStruct((np_rows, d), F32), jax.ShapeDtypeStruct((nt - np_rows, d), F32)]
        out_specs = [pl.BlockSpec((tm, d), ctx_map), pl.BlockSpec((tm, d), lat_map)]
    else:
        out_shape = jax.ShapeDtypeStruct((nt, d), F32)
        out_specs = pl.BlockSpec((tm, d), lambda i: (i, 0))
    return pl.pallas_call(
        kern,
        out_shape=out_shape,
        grid=(nt // tm,),
        in_specs=in_specs,
        out_specs=out_specs,
        compiler_params=_cparams(("arbitrary",)),
        name="ffn_half",
    )(*args)


def _proj_kernel(x_ref, mod_ref, g_ref, *refs, i0, parts):
    n_out = len(refs) // 2
    w_refs, o_refs = refs[:n_out], refs[n_out:]
    mod = mod_ref[0]
    rp = x_ref.shape[0] // parts
    for p in range(parts):
        rs = slice(p * rp, (p + 1) * rp)
        h = _modnorm(x_ref[rs, :], g_ref[...], mod[i0:i0 + 1], mod[i0 + 1:i0 + 2])
        for w_ref, o_ref in zip(w_refs, o_refs):
            o_ref[rs, :] = _dot(h, w_ref[...]).astype(o_ref.dtype)


def _mixer_in(x, mod, g, groups, dtypes, rows):
    nt, d = x.shape
    tm = PROJ_ROW_TILE
    assert rows[0] % tm == 0 and rows[1] % tm == 0
    cond = _cond_index(*rows, tm)
    kern = functools.partial(_proj_kernel, i0=3, parts=PROJ_ROW_PARTS)
    w_specs = []
    for w, off, wd in groups:
        assert off % wd == 0 and off + wd <= w.shape[1]
        w_specs.append(_resident((d, wd), functools.partial(lambda b, i: (0, b), off // wd)))
    return pl.pallas_call(
        kern,
        out_shape=[jax.ShapeDtypeStruct((nt, grp[2]), dt) for grp, dt in zip(groups, dtypes)],
        grid=(nt // tm,),
        in_specs=[pl.BlockSpec((tm, d), lambda i: (i, 0)),
                  pl.BlockSpec((1, N_MOD, d), lambda i: (cond(i), 0, 0)),
                  pl.BlockSpec((1, d), lambda i: (0, 0))] + w_specs,
        out_specs=[pl.BlockSpec((tm, grp[2]), lambda i: (i, 0)) for grp in groups],
        compiler_params=_cparams(("parallel",)),
        name="mixer_in",
    )(x, mod, g.reshape(1, d), *[grp[0] for grp in groups])


def _conv_kernel(prev_ref, x_ref, next_ref, w_ref, o_ref, *, tiles, dk):
    r = pl.program_id(0)
    c, width = x_ref.shape
    pad = (SHORT_CONV - 1) // 2
    has_prev = jnp.where(tiles.pos(r) > 0, 1.0, 0.0)
    has_next = jnp.where(tiles.pos(r) < tiles.length(r) - 1, 1.0, 0.0)
    ext = c + 2 * SUBLANES
    for hh in range(width // dk):
        part = hh // (width // (3 * dk))
        sl = slice(hh * dk, (hh + 1) * dk)
        xe = jnp.concatenate([prev_ref[:, sl] * has_prev, x_ref[:, sl], next_ref[:, sl] * has_next], axis=0)
        w = w_ref[:, sl]
        acc = None
        for j in range(SHORT_CONV):
            sh = pad - j
            xs = xe if sh == 0 else pltpu.roll(xe, sh % ext, axis=0)
            term = xs[SUBLANES:SUBLANES + c] * w[j:j + 1]
            acc = term if acc is None else acc + term
        y = _silu(acc)
        if part < 2:
            nrm = lax.rsqrt(jnp.sum(y * y, axis=-1, keepdims=True) + EPS)
            y = y * (nrm * (dk ** -0.5) if part == 0 else nrm)
        o_ref[:, sl] = y


def _conv_qkv(qkv, conv_w, tiles):
    nt, width = qkv.shape
    c = tiles.c
    per = c // SUBLANES
    n8 = nt // SUBLANES
    kern = functools.partial(_conv_kernel, tiles=tiles, dk=A_DK)
    return pl.pallas_call(
        kern,
        out_shape=jax.ShapeDtypeStruct((nt, width), F32),
        grid=(tiles.n,),
        in_specs=[pl.BlockSpec((SUBLANES, width), lambda r: (jnp.maximum(r * per - 1, 0), 0)),
                  pl.BlockSpec((c, width), lambda r: (r, 0)),
                  pl.BlockSpec((SUBLANES, width), lambda r: (jnp.minimum((r + 1) * per, n8 - 1), 0)),
                  pl.BlockSpec((SHORT_CONV, width), lambda r: (0, 0))],
        out_specs=pl.BlockSpec((c, width), lambda r: (r, 0)),
        compiler_params=_cparams(("parallel",)),
        name="conv_qkv",
    )(qkv, qkv, qkv, conv_w)


def _load_state(state, s0_ref, tiles, i):
    first = tiles.pos(i) == 0

    @pl.when(first & tiles.is_ctx(i))
    def _():
        state[...] = jnp.zeros(state.shape, F32)

    @pl.when(first & jnp.logical_not(tiles.is_ctx(i)))
    def _():
        state[...] = s0_ref[0]


def _store_state(state, out_ref, tiles, i):
    @pl.when((tiles.pos(i) == tiles.length(i) - 1) & tiles.is_ctx(i))
    def _():
        out_ref[0] = state[...]


def _dir_specs(tiles, c, cols):
    specs = []
    for reverse in (False, True):
        for width, col in cols:
            specs.append(pl.BlockSpec((c, width), functools.partial(
                lambda rev, cc, i: (tiles.row_block(i, rev), cc), reverse, col)))
    return specs


def _pair_dot(a, b):
    n = a.shape[0]
    a_bf, b_bf = _bf(a), _bf(b)
    z = jnp.zeros((n, n), BF16)
    b_diag = jnp.concatenate([jnp.concatenate([b_bf[:, :n], z], axis=1),
                              jnp.concatenate([z, b_bf[:, n:]], axis=1)], axis=0)
    return _dot(a_bf, b_diag)


def _unit_tri_solves(ls, rs, reverse_flags):
    c = ls[0].shape[0]
    n = c // 2
    ri = lax.broadcasted_iota(jnp.int32, (n, c), 0)
    ci = lax.broadcasted_iota(jnp.int32, (n, c), 1) & (n - 1)
    pairs = [jnp.concatenate([l[:n, :n], l[n:, n:]], axis=1) for l in ls]
    shift = SUBLANES.bit_length() - 1
    same = (ri >> shift) == (ci >> shift)
    ms = [jnp.where(same, -lp, 0.0) for lp in pairs]
    ps = [_pair_dot(m, m) for m in ms]
    ns = [m + p + _pair_dot(m, p) for m, p in zip(ms, ps)]
    ps = [_pair_dot(p, p) for p in ps]
    ns = [nv + p + _pair_dot(nv, p) for nv, p in zip(ns, ps)]
    while (1 << shift) < n:
        lvl = ((ri >> (shift + 1)) == (ci >> (shift + 1))) & ((ri >> shift) != (ci >> shift))
        cls = [jnp.where(lvl, lp, 0.0) for lp in pairs]
        ys = [cl + _pair_dot(cl, nv) for cl, nv in zip(cls, ns)]
        ns = [nv - (y + _pair_dot(nv, y)) for nv, y in zip(ns, ys)]
        shift += 1
    firsts, seconds = [], []
    for l, r, nv, rev in zip(ls, rs, ns, reverse_flags):
        if rev:
            firsts.append((r[n:], _bf(nv[:, n:])))
            seconds.append((r[:n], _bf(nv[:, :n]), _bf(l[:n, n:])))
        else:
            firsts.append((r[:n], _bf(nv[:, :n])))
            seconds.append((r[n:], _bf(nv[:, n:]), _bf(l[n:, :n])))
    xas = [ra + _dot(na, _bf(ra)) for ra, na in firsts]
    ts = [rb - _dot(lba, _bf(xa)) for (rb, _, lba), xa in zip(seconds, xas)]
    xbs = [t + _dot(nb, _bf(t)) for (_, nb, _), t in zip(seconds, ts)]
    return [jnp.concatenate([xb, xa] if rev else [xa, xb], axis=0) for xa, xb, rev in zip(xas, xbs, reverse_flags)]


def _delta_kernel(qf_ref, kf_ref, vf_ref, baf_ref, qb_ref, kb_ref, vb_ref, bab_ref, al_ref, dt_ref, s0_ref,
                  of_ref, ob_ref, sout_ref, state, *, tiles, nh, dk, dv):
    step = pl.program_id(0)
    _load_state(state, s0_ref, tiles, step)
    c = qf_ref.shape[0]
    ri = lax.broadcasted_iota(jnp.int32, (c, c), 0)
    ci = lax.broadcasted_iota(jnp.int32, (c, c), 1)
    units = []
    for z, (q_ref, k_ref, v_ref, ba_ref, o_ref) in enumerate(
            ((qf_ref, kf_ref, vf_ref, baf_ref, of_ref), (qb_ref, kb_ref, vb_ref, bab_ref, ob_ref))):
        reverse = z == 1
        incl, strict = _order_masks(ri, ci, reverse)
        ba = ba_ref[...]
        g_col = -jnp.exp(al_ref[...]) * _softplus(ba + dt_ref[...])
        beta_col = _sigmoid(ba)
        gc_col = _cumsum_rows(_bf(incl.astype(F32)), g_col)
        gc_row = gc_col.T
        last = 0 if reverse else c - 1
        for h in range(nh):
            cb = z * nh + h
            cg = 2 * nh + cb
            gcc = gc_col[:, cg:cg + 1]
            gcr = gc_row[cg:cg + 1, :]
            units.append(dict(
                z=z, h=h, o_ref=o_ref, strict=strict, gcc=gcc, gl=gcc[last:last + 1],
                beta=beta_col[:, cb:cb + 1], egc=jnp.exp(gcc),
                decay=jnp.where(incl, jnp.exp(gcc - gcr), 0.0),
                q=q_ref[:, h * dk:(h + 1) * dk], k=k_ref[:, h * dk:(h + 1) * dk], v=v_ref[:, h * dv:(h + 1) * dv]))
    for u in units:
        u["kb"] = u["k"] * u["beta"]
        u["k_bf"] = _bf(u["k"])
    kks = [_dot_nt(_bf(u["kb"]), u["k_bf"]) for u in units]
    qks = [_dot_nt(_bf(u["q"]), u["k_bf"]) * u["decay"] for u in units]
    rs = _unit_tri_solves([jnp.where(u["strict"], kk * u["decay"], 0.0) for u, kk in zip(units, kks)],
                          [jnp.concatenate([u["v"] * u["beta"], u["kb"] * u["egc"]], axis=1) for u in units],
                          [u["z"] == 1 for u in units])
    ss = [state[u["z"], u["h"]] for u in units]
    ss_bf = [_bf(s) for s in ss]
    v_news_bf = [_bf(r[:, :dv] - _dot(_bf(r[:, dv:]), sb)) for r, sb in zip(rs, ss_bf)]
    for u, sb, qk, vnb in zip(units, ss_bf, qks, v_news_bf):
        h = u["h"]
        u["o_ref"][:, h * dv:(h + 1) * dv] = _bf(_dot(_bf(u["q"] * u["egc"]), sb) + _dot(_bf(qk), vnb))
    for u, s, vnb in zip(units, ss, v_news_bf):
        kd = u["k"] * jnp.exp(u["gl"] - u["gcc"])
        state[u["z"], u["h"]] = s * jnp.exp(u["gl"]) + _dot_tn(_bf(kd), vnb)
    _store_state(state, sout_ref, tiles, step)


def _delta_rule(qkv, ba, a_log, dt_bias, s0, tiles):
    nt = qkv.shape[0]
    nh, dk, dv = A_HEADS, A_DK, A_DV
    c = tiles.c
    pad = LANES - 4 * nh
    al = jnp.concatenate([jnp.zeros((2 * nh,), F32), a_log.reshape(-1), jnp.zeros((pad,), F32)])
    dt = jnp.concatenate([jnp.zeros((2 * nh,), F32), dt_bias.reshape(-1), jnp.zeros((pad,), F32)])
    const = lambda i: (0, 0)
    in_specs = (_dir_specs(tiles, c, ((nh * dk, 0), (nh * dk, 1), (nh * dv, 2), (LANES, 0)))
                + [pl.BlockSpec((1, LANES), const), pl.BlockSpec((1, LANES), const),
                   pl.BlockSpec((1, 2, nh, dk, dv), lambda i: (tiles.latent_seq(i), 0, 0, 0, 0))])
    kern = functools.partial(_delta_kernel, tiles=tiles, nh=nh, dk=dk, dv=dv)
    return pl.pallas_call(
        kern,
        out_shape=[jax.ShapeDtypeStruct((nt, nh * dv), BF16), jax.ShapeDtypeStruct((nt, nh * dv), BF16),
                   jax.ShapeDtypeStruct((tiles.n_p, 2, nh, dk, dv), F32)],
        grid=(tiles.n,),
        in_specs=in_specs,
        out_specs=[pl.BlockSpec((c, nh * dv), lambda i: (tiles.row_block(i, False), 0)),
                   pl.BlockSpec((c, nh * dv), lambda i: (tiles.row_block(i, True), 0)),
                   pl.BlockSpec((1, 2, nh, dk, dv), lambda i: (tiles.ctx_seq(i), 0, 0, 0, 0))],
        scratch_shapes=[pltpu.VMEM((2, nh, dk, dv), F32)],
        compiler_params=_cparams(("arbitrary",)),
        name="delta_rule",
    )(qkv, qkv, qkv, ba, qkv, qkv, qkv, ba, al.reshape(1, LANES), dt.reshape(1, LANES), s0)


def _ctx_attn_kernel(sink_ref, q_ref, k_ref, v_ref, o_ref, *, n_kv, grp, dh, t):
    scale = dh ** -0.5
    seqs = [slice(b * t, (b + 1) * t) for b in range(q_ref.shape[0] // t)]
    units = [(rs, h) for rs in seqs for h in range(n_kv * grp)]
    ks = {(rs.start, hk): _bf(k_ref[rs, hk * dh:(hk + 1) * dh]) for rs in seqs for hk in range(n_kv)}
    vs = {(rs.start, hk): _bf(v_ref[rs, hk * dh:(hk + 1) * dh]) for rs in seqs for hk in range(n_kv)}
    sinks = [sink_ref[h] for _, h in units]
    ss = [_dot_nt(_bf(q_ref[rs, h * dh:(h + 1) * dh] * scale), ks[rs.start, h // grp]) for rs, h in units]
    ms = [jnp.maximum(jnp.max(s, axis=-1, keepdims=True), sk) for s, sk in zip(ss, sinks)]
    ps = [jnp.exp(s - m) for s, m in zip(ss, ms)]
    dens = [jnp.sum(p, axis=-1, keepdims=True) + jnp.exp(sk - m) for p, sk, m in zip(ps, sinks, ms)]
    for (rs, h), p, den in zip(units, ps, dens):
        o_ref[rs, h * dh:(h + 1) * dh] = _bf(_dot(_bf(p), vs[rs.start, h // grp]) / den)


def _ctx_attention(qb, kb, vb, sink, n_seq, t):
    wq = qb.shape[1]
    wk = kb.shape[1]
    per = math.gcd(n_seq, CTX_SEQS_PER_STEP)
    kern = functools.partial(_ctx_attn_kernel, n_kv=B_KV_HEADS, grp=B_Q_HEADS // B_KV_HEADS, dh=HEAD_DIM, t=t)
    return pl.pallas_call(
        kern,
        out_shape=jax.ShapeDtypeStruct((n_seq * t, wq), BF16),
        grid=(n_seq // per,),
        in_specs=[pl.BlockSpec(memory_space=pltpu.SMEM),
                  pl.BlockSpec((per * t, wq), lambda b: (b, 0)),
                  pl.BlockSpec((per * t, wk), lambda b: (b, 0)),
                  pl.BlockSpec((per * t, wk), lambda b: (b, 0))],
        out_specs=pl.BlockSpec((per * t, wq), lambda b: (b, 0)),
        compiler_params=_cparams(("parallel",)),
        name="ctx_attention",
    )(sink, qb, kb, vb)


def _rope(x, cos, sin_signed):
    lane = lax.broadcasted_iota(jnp.int32, x.shape, 1)
    quarter = HEAD_DIM // 4
    partner = jnp.where((lane % (2 * quarter)) < quarter,
                        pltpu.roll(x, HEAD_DIM - quarter, axis=1), pltpu.roll(x, quarter, axis=1))
    return x * cos + partner * sin_signed


def _lat_attn_kernel(sink_ref, q_ref, k_ref, v_ref, kc_ref, vc_ref, cq_ref, sq_ref, ck_ref, sk_ref,
                     o_ref, *, n_kv, grp, dh, window):
    scale = dh ** -0.5
    tq = q_ref.shape[0]
    t = k_ref.shape[0]
    span = min(t, tq + 2 * window)
    q0 = pl.program_id(1) * tq
    start = pl.multiple_of(jnp.clip(q0 - window, 0, t - span), math.gcd(tq, window))
    rows = pl.ds(start, span)
    qpos = q0 + lax.broadcasted_iota(jnp.int32, (tq, span), 0)
    kpos = start + lax.broadcasted_iota(jnp.int32, (tq, span), 1)
    valid = jnp.abs(qpos - kpos) <= window
    cq, sq = cq_ref[...], sq_ref[...]
    ck, sk_t = ck_ref[rows, :], sk_ref[rows, :]
    kv = []
    for hk in range(n_kv):
        hs = slice(hk * dh, (hk + 1) * dh)
        kv.append((_bf(_rope(k_ref[rows, hs], ck, sk_t)), _bf(v_ref[rows, hs]), _bf(kc_ref[0, :, hs]),
                   _bf(vc_ref[0, :, hs])))
    heads = range(n_kv * grp)
    sinks = [sink_ref[h] for h in heads]
    qs = [q_ref[:, h * dh:(h + 1) * dh] * scale for h in heads]
    s_locs = [jnp.where(valid, _dot_nt(_bf(_rope(q, cq, sq)), kv[h // grp][0]), -jnp.inf) for h, q in zip(heads, qs)]
    s_ctxs = [_dot_nt(_bf(q), kv[h // grp][2]) for h, q in zip(heads, qs)]
    ms = [jnp.maximum(jnp.maximum(jnp.max(sl, axis=-1, keepdims=True), jnp.max(sc, axis=-1, keepdims=True)), sk)
          for sl, sc, sk in zip(s_locs, s_ctxs, sinks)]
    p_locs = [jnp.exp(sl - m) for sl, m in zip(s_locs, ms)]
    p_ctxs = [jnp.exp(sc - m) for sc, m in zip(s_ctxs, ms)]
    dens = [jnp.sum(pl_, axis=-1, keepdims=True) + jnp.sum(pc, axis=-1, keepdims=True) + jnp.exp(sk - m)
            for pl_, pc, sk, m in zip(p_locs, p_ctxs, sinks, ms)]
    for h, pl_, pc, den in zip(heads, p_locs, p_ctxs, dens):
        o_ref[:, h * dh:(h + 1) * dh] = _bf((_dot(_bf(pl_), kv[h // grp][1]) + _dot(_bf(pc), kv[h // grp][3])) / den)


def _lat_attention(qb, kb, vb, k_ctx, v_ctx, sink, cos, sin_signed, row0, n_seq, t):
    wq = qb.shape[1]
    wk = kb.shape[1]
    tq = ATTN_ROW_TILE
    assert t % tq == 0 and row0 % t == 0
    nq = t // tq
    base_q = row0 // tq
    base_t = row0 // t
    past = k_ctx.shape[1]
    kern = functools.partial(_lat_attn_kernel, n_kv=B_KV_HEADS, grp=B_Q_HEADS // B_KV_HEADS, dh=HEAD_DIM,
                             window=WINDOW)
    return pl.pallas_call(
        kern,
        out_shape=jax.ShapeDtypeStruct((n_seq * t, wq), BF16),
        grid=(n_seq, nq),
        in_specs=[pl.BlockSpec(memory_space=pltpu.SMEM),
                  pl.BlockSpec((tq, wq), lambda b, i: (base_q + b * nq + i, 0)),
                  pl.BlockSpec((t, wk), lambda b, i: (base_t + b, 0)),
                  pl.BlockSpec((t, wk), lambda b, i: (base_t + b, 0)),
                  pl.BlockSpec((1, past, wk), lambda b, i: (b, 0, 0)),
                  pl.BlockSpec((1, past, wk), lambda b, i: (b, 0, 0)),
                  pl.BlockSpec((tq, HEAD_DIM), lambda b, i: (i, 0)),
                  pl.BlockSpec((tq, HEAD_DIM), lambda b, i: (i, 0)),
                  pl.BlockSpec((t, HEAD_DIM), lambda b, i: (0, 0)),
                  pl.BlockSpec((t, HEAD_DIM), lambda b, i: (0, 0))],
        out_specs=pl.BlockSpec((tq, wq), lambda b, i: (b * nq + i, 0)),
        compiler_params=_cparams(("parallel", "parallel")),
        name="latent_attention",
    )(sink, qb, kb, vb, k_ctx, v_ctx, cos, sin_signed, cos, sin_signed)


def _rope_tables(t):
    half = HEAD_DIM // 2
    quarter = half // 2
    pos = jnp.arange(t)
    row = (pos // GRID_W).astype(F32)
    col = (pos % GRID_W).astype(F32)
    inv = ROPE_THETA ** (-jnp.arange(quarter, dtype=F32) / quarter)
    ang_r = row[:, None] * inv[None, :]
    ang_c = col[:, None] * inv[None, :]
    cos = jnp.concatenate([jnp.cos(ang_r), jnp.cos(ang_r), jnp.cos(ang_c), jnp.cos(ang_c)], axis=-1)
    sin = jnp.concatenate([-jnp.sin(ang_r), jnp.sin(ang_r), -jnp.sin(ang_c), jnp.sin(ang_c)], axis=-1)
    return cos, sin


def _gla_kernel(qf_ref, kf_ref, vf_ref, lrf_ref, qb_ref, kb_ref, vb_ref, lrb_ref, wg_ref, bias_ref, s0_ref,
                of_ref, ob_ref, sout_ref, state, *, tiles, nh, dk, dv, rb):
    step = pl.program_id(0)
    _load_state(state, s0_ref, tiles, step)
    c = qf_ref.shape[0]
    ri = lax.broadcasted_iota(jnp.int32, (c, c), 0)
    ci = lax.broadcasted_iota(jnp.int32, (c, c), 1)
    eye = (lax.broadcasted_iota(jnp.int32, (dk, dk), 0) == lax.broadcasted_iota(jnp.int32, (dk, dk), 1))
    units = []
    for z, (q_ref, k_ref, v_ref, lr_ref, o_ref) in enumerate(
            ((qf_ref, kf_ref, vf_ref, lrf_ref, of_ref), (qb_ref, kb_ref, vb_ref, lrb_ref, ob_ref))):
        reverse = z == 1
        incl, _ = _order_masks(ri, ci, reverse)
        x = _dot(_bf(lr_ref[...]), _bf(wg_ref[z])) + bias_ref[z]
        gk = -_softplus(-x) * (1.0 / GATE_NORM)
        gcum = _cumsum_rows(_bf(incl.astype(F32)), gk)
        last = 0 if reverse else c - 1
        for h in range(nh):
            g = gcum[:, h * dk:(h + 1) * dk]
            units.append(dict(
                z=z, h=h, o_ref=o_ref, reverse=reverse, incl=incl, g=g, gl_row=g[last:last + 1],
                q=q_ref[:, h * dk:(h + 1) * dk] * (dk ** -0.5), k=k_ref[:, h * dk:(h + 1) * dk],
                v_bf=v_ref[:, h * dv:(h + 1) * dv], s=state[z, h]))
    intra = [[] for _ in units]
    for blk in range(c // rb):
        r0, r1 = blk * rb, (blk + 1) * rb
        scores, cols = [], []
        for u in units:
            g = u["g"]
            mid = r0 + rb // 2
            if u["reverse"]:
                c0, c1 = r0, c
                ref = g[mid:mid + 1]
            else:
                c0, c1 = 0, r1
                ref = g[mid - 1:mid]
            qe = u["q"][r0:r1] * jnp.exp(g[r0:r1] - ref)
            ke = u["k"][c0:c1] * jnp.exp(ref - g[c0:c1])
            scores.append(jnp.where(u["incl"][r0:r1, c0:c1], _dot_nt(_bf(qe), _bf(ke)), 0.0))
            cols.append((c0, c1))
        for parts, u, a, (c0, c1) in zip(intra, units, scores, cols):
            parts.append(_dot(_bf(a), u["v_bf"][c0:c1]))
    inter = [_dot(_bf(u["q"] * jnp.exp(u["g"])), _bf(u["s"])) for u in units]
    for u, o_inter, parts in zip(units, inter, intra):
        h = u["h"]
        u["o_ref"][:, h * dv:(h + 1) * dv] = _bf(o_inter + jnp.concatenate(parts, axis=0))
    for u in units:
        kd = u["k"] * jnp.exp(u["gl_row"] - u["g"])
        gl_col = jnp.sum(jnp.where(eye, jnp.broadcast_to(u["gl_row"], (dk, dk)), 0.0), axis=1, keepdims=True)
        state[u["z"], u["h"]] = u["s"] * jnp.exp(gl_col) + _dot_tn(_bf(kd), u["v_bf"])
    _store_state(state, sout_ref, tiles, step)


def _gla(q, k, v, lr, w_gate, gate_bias, s0, tiles):
    nt = q.shape[0]
    nh, dk, dv = C_HEADS, C_DK, C_DV
    c = tiles.c
    wg = jnp.zeros((2, LANES, nh * dk), F32)
    for z in range(2):
        wg = wg.at[z, z * GATE_RANK:(z + 1) * GATE_RANK].set(w_gate[z])
    in_specs = (_dir_specs(tiles, c, ((nh * dk, 0), (nh * dk, 0), (nh * dv, 0), (LANES, 0)))
                + [pl.BlockSpec((2, LANES, nh * dk), lambda i: (0, 0, 0)),
                   pl.BlockSpec((2, 1, nh * dk), lambda i: (0, 0, 0)),
                   pl.BlockSpec((1, 2, nh, dk, dv), lambda i: (tiles.latent_seq(i), 0, 0, 0, 0))])
    kern = functools.partial(_gla_kernel, tiles=tiles, nh=nh, dk=dk, dv=dv, rb=GLA_ROW_BLOCK)
    return pl.pallas_call(
        kern,
        out_shape=[jax.ShapeDtypeStruct((nt, nh * dv), BF16), jax.ShapeDtypeStruct((nt, nh * dv), BF16),
                   jax.ShapeDtypeStruct((tiles.n_p, 2, nh, dk, dv), F32)],
        grid=(tiles.n,),
        in_specs=in_specs,
        out_specs=[pl.BlockSpec((c, nh * dv), lambda i: (tiles.row_block(i, False), 0)),
                   pl.BlockSpec((c, nh * dv), lambda i: (tiles.row_block(i, True), 0)),
                   pl.BlockSpec((1, 2, nh, dk, dv), lambda i: (tiles.ctx_seq(i), 0, 0, 0, 0))],
        scratch_shapes=[pltpu.VMEM((2, nh, dk, dv), F32)],
        compiler_params=_cparams(("arbitrary",)),
        name="gla",
    )(q, k, v, lr, q, k, v, lr, wg, gate_bias.reshape(2, 1, nh * dk), s0)


def kernel(x_prompt, x_sample, state_delta, cache_k, cache_v, state_gla, c, c_ctx, norm_g, ada_w, ada_b,
           ffn_w_gu, ffn_w_down, even_w_in, even_conv, even_a_log, even_dt_bias, even_onorm, even_sink,
           even_w_out, odd_w_in, odd_w_gate, odd_gate_bias, odd_onorm, odd_w_out, final_g):
    n_p, t_p, d = x_prompt.shape
    n_s, t_s, _ = x_sample.shape
    depth = norm_g.shape[0]
    np_rows, ns_rows = n_p * t_p, n_s * t_s
    assert np_rows % t_s == 0
    assert t_p % CHUNK == 0 and t_s % CHUNK == 0 and t_s % GRID_W == 0
    rows = (np_rows, t_s)
    tiles = _Tiles(n_p, t_p, n_s, t_s, CHUNK)

    n_cond = 1 + n_s
    cond_rows = -(-n_cond // SUBLANES) * SUBLANES
    conds = jnp.concatenate([c_ctx[None, :], c, jnp.zeros((cond_rows - n_cond, d), F32)], axis=0)
    mods = _ada(conds, ada_w, ada_b)[:, :n_cond].reshape(depth, n_cond, N_MOD, d)

    w_gu_second = ffn_w_gu[:, 1:].astype(BF16)
    w_down_second = ffn_w_down[:, 1:].astype(BF16)

    xs = (x_prompt.reshape(np_rows, d), x_sample.reshape(ns_rows, d))
    new_delta, new_k, new_v, new_gla = [], [], [], []
    for l in range(depth):
        j = l // 2
        mod = mods[l]
        x = _ffn_half(xs, mod, 0, norm_g[l, 0], ffn_w_gu, ffn_w_down, (l, 0), rows)
        if l % 2 == 0:
            nh = A_HEADS
            o_qkv = 2 * nh * A_DK + nh * A_DV
            o_gate = o_qkv + nh * A_DV
            o_qb = o_gate + 4 * nh
            w_q, w_kv = B_Q_HEADS * HEAD_DIM, B_KV_HEADS * HEAD_DIM
            w = even_w_in[j]
            tail = jnp.concatenate([w[:, o_qb:], w[:, o_gate:o_qb], jnp.zeros((d, LANES - 4 * nh), F32)], axis=1)
            assert tail.shape[1] == w_q + 2 * w_kv + LANES and o_qkv % (nh * A_DV) == 0
            head = w[:, :o_gate]
            groups = ((head, 0, o_qkv), (head, o_qkv, nh * A_DV), (tail, 0, w_q), (tail, w_q, w_kv),
                      (tail, w_q + w_kv, w_kv), (tail, w_q + 2 * w_kv, LANES))
            dtypes = (F32, BF16, F32, F32, F32, F32)
            qkv, gate, qb, kb, vb, ba = _mixer_in(x, mod, norm_g[l, 1], groups, dtypes, rows)
            qkv_n = _conv_qkv(qkv, even_conv[j], tiles)
            o_f, o_b, st = _delta_rule(qkv_n, ba, even_a_log[j], even_dt_bias[j], state_delta[:, j], tiles)
            cos, sin_signed = _rope_tables(t_s)
            att_p = _ctx_attention(qb, kb, vb, even_sink[j], n_p, t_p)
            att_s = _lat_attention(qb, kb, vb,
                                   cache_k[:, j].reshape(n_s, -1, B_KV_HEADS * HEAD_DIM),
                                   cache_v[:, j].reshape(n_s, -1, B_KV_HEADS * HEAD_DIM),
                                   even_sink[j], cos, sin_signed, np_rows, n_s, t_s)
            mixer = (o_f, o_b, gate, even_onorm[j], even_w_out[j].astype(BF16), nh, A_DV, (att_p, att_s))
            new_delta.append(st)
            new_k.append(kb[:np_rows].reshape(n_p, t_p, B_KV_HEADS, HEAD_DIM))
            new_v.append(vb[:np_rows].reshape(n_p, t_p, B_KV_HEADS, HEAD_DIM))
        else:
            nh = C_HEADS
            o_v = 2 * nh * C_DK
            o_g = o_v + nh * C_DV
            o_lr = o_g + nh * C_DV
            w_lr = jnp.concatenate([odd_w_in[j][:, o_lr:], jnp.zeros((d, LANES - 2 * GATE_RANK), F32)], axis=1)
            head = odd_w_in[j][:, :o_lr]
            groups = ((head, 0, nh * C_DK), (head, nh * C_DK, nh * C_DK), (head, o_v, nh * C_DV),
                      (head, o_g, nh * C_DV), (w_lr, 0, LANES))
            dtypes = (F32, F32, BF16, BF16, F32)
            q, k, v, g_out, lr = _mixer_in(x, mod, norm_g[l, 1], groups, dtypes, rows)
            o_f, o_b, st = _gla(q, k, v, lr, odd_w_gate[j], odd_gate_bias[j], state_gla[:, j], tiles)
            mixer = (o_f, o_b, g_out, odd_onorm[j], odd_w_out[j].astype(BF16), nh, C_DV, None)
            new_gla.append(st)
        last = l == depth - 1
        xs = _ffn_half((x,), mod, 6, norm_g[l, 2], w_gu_second, w_down_second, (l, 0), rows, mixer=mixer,
                       final_g=final_g if last else None)
        if not last:
            xs = (xs,)

    y_prompt, y_sample = xs
    return (y_prompt.reshape(n_p, t_p, d), y_sample.reshape(n_s, t_s, d), jnp.stack(new_delta, axis=1),
            jnp.stack(new_k, axis=1), jnp.stack(new_v, axis=1), jnp.stack(new_gla, axis=1))
```

```python
import functools
import math

import jax
import jax.numpy as jnp
from jax import lax
from jax.experimental import pallas as pl
from jax.experimental.pallas import tpu as pltpu

F32 = jnp.float32
BF16 = jnp.bfloat16

EPS = 1e-6
N_MOD = 9
GRID_W = 64
HEAD_DIM = 128
A_HEADS = 4
A_DK = 128
A_DV = 128
SHORT_CONV = 5
B_Q_HEADS = 4
B_KV_HEADS = 2
WINDOW = 128
C_HEADS = 4
C_DK = 128
C_DV = 256
GATE_RANK = 16
GATE_NORM = 16.0
ROPE_THETA = 10000.0

LANES = 128
SUBLANES = 8
PROJ_ROW_TILE = 1024
PROJ_ROW_PARTS = 2
FFN_ROW_TILE = 512
FFN_CHUNKS = 11
CHUNK = 256
CTX_SEQS_PER_STEP = 4
ATTN_ROW_TILE = 256
GLA_ROW_BLOCK = 128
VMEM_LIMIT = 56 * 1024 * 1024


def _cparams(sem, vmem=VMEM_LIMIT):
    return pltpu.CompilerParams(dimension_semantics=sem, vmem_limit_bytes=vmem)


def _resident(block_shape, index_map):
    return pl.BlockSpec(block_shape, index_map, pipeline_mode=pl.Buffered(1))


def _dot(a, b):
    return jnp.dot(a, b, preferred_element_type=F32)


def _dot_nt(a, b):
    return lax.dot_general(a, b, (((1,), (1,)), ((), ())), preferred_element_type=F32)


def _dot_tn(a, b):
    return lax.dot_general(a, b, (((0,), (0,)), ((), ())), preferred_element_type=F32)


def _bf(x):
    return x.astype(BF16)


def _sigmoid(x):
    return 0.5 * jnp.tanh(0.5 * x) + 0.5


def _silu(x):
    return x * _sigmoid(x)


def _softplus(x):
    return jnp.maximum(x, 0.0) + jnp.log(1.0 + jnp.exp(-jnp.abs(x)))


def _rms(x):
    return x * lax.rsqrt(jnp.mean(x * x, axis=-1, keepdims=True) + EPS)


def _modnorm(x, g, shift, scale):
    return (_rms(x) * g) * (1.0 + scale) + shift


def _cumsum_rows(tri_bf, x):
    hi = _bf(x)
    lo = _bf(x - hi.astype(F32))
    return _dot(tri_bf, hi) + _dot(tri_bf, lo)


def _order_masks(ri, ci, reverse):
    if reverse:
        return ri <= ci, ri < ci
    return ri >= ci, ri > ci


class _Tiles:
    def __init__(self, n_p, t_p, n_s, t_s, c):
        self.n_p, self.n_s, self.c = n_p, n_s, c
        self.per_p, self.per_s = t_p // c, t_s // c
        self.np_tiles = n_p * self.per_p
        self.n = self.np_tiles + n_s * self.per_s

    def is_ctx(self, i):
        return i < self.np_tiles

    def seq(self, i):
        return jnp.where(i < self.np_tiles, i // self.per_p, self.n_p + (i - self.np_tiles) // self.per_s)

    def pos(self, i):
        return jnp.where(i < self.np_tiles, i % self.per_p, (i - self.np_tiles) % self.per_s)

    def length(self, i):
        return jnp.where(i < self.np_tiles, self.per_p, self.per_s)

    def row_block(self, i, reverse):
        return i + self.length(i) - 1 - 2 * self.pos(i) if reverse else i

    def ctx_seq(self, i):
        return jnp.minimum(self.seq(i), self.n_p - 1)

    def latent_seq(self, i):
        return jnp.maximum(self.seq(i) - self.n_p, 0)


def _ada_kernel(c_ref, w_ref, b_ref, o_ref):
    s = _bf(_silu(c_ref[...]))
    o_ref[0] = _dot(s, _bf(w_ref[0])) + b_ref[0]


def _ada(cond, ada_w, ada_b):
    depth, d, n = ada_w.shape
    rows = cond.shape[0]
    tn = n // 4
    return pl.pallas_call(
        _ada_kernel,
        out_shape=jax.ShapeDtypeStruct((depth, rows, n), F32),
        grid=(depth, n // tn),
        in_specs=[pl.BlockSpec((rows, d), lambda l, j: (0, 0)),
                  pl.BlockSpec((1, d, tn), lambda l, j: (l, 0, j)),
                  pl.BlockSpec((1, 1, tn), lambda l, j: (l, 0, j))],
        out_specs=pl.BlockSpec((1, rows, tn), lambda l, j: (l, 0, j)),
        compiler_params=_cparams(("parallel", "parallel")),
        name="ada",
    )(cond, ada_w, ada_b.reshape(depth, 1, n))


def _cond_index(n_prompt_rows, dec_seq, tm):
    npt = n_prompt_rows // tm

    def cond(i):
        return jnp.where(i < npt, 0, 1 + ((i - npt) * tm) // dec_seq)

    return cond


def _mixer_residual(x, gate_mod, of_ref, ob_ref, gate_ref, on_ref, w_ref, extra, nh, dv):
    od = of_ref[...].astype(F32) + ob_ref[...].astype(F32)
    gate = gate_ref[...].astype(F32)
    y = None if extra is None else _dot(extra, w_ref[nh * dv:, :])
    per = max(1, (2 * LANES) // dv)
    for h0 in range(0, nh, per):
        mix = [_bf((_rms(od[:, h * dv:(h + 1) * dv]) * on_ref[...]) * _silu(gate[:, h * dv:(h + 1) * dv]))
               for h in range(h0, min(h0 + per, nh))]
        part = _dot(jnp.concatenate(mix, axis=1), w_ref[h0 * dv:min(h0 + per, nh) * dv, :])
        y = part if y is None else y + part
    return x + gate_mod * y


def _ffn_kernel(*refs, i0, d_ff, n_chunks, npt, split_in, mixer, final):
    refs = list(refs)
    x_refs = [refs.pop(0) for _ in range(2 if split_in else 1)]
    mod_ref, g_ref, wgu_ref, wd_ref = (refs.pop(0) for _ in range(4))
    if mixer is not None:
        nh, dv, has_extra = mixer
        of_ref, ob_ref, gate_ref, on_ref, wout_ref = (refs.pop(0) for _ in range(5))
        extra_refs = [refs.pop(0) for _ in range(2 if has_extra else 0)]
    rest = refs
    i = pl.program_id(0)
    mod = mod_ref[0]
    ch = d_ff // n_chunks
    x = jnp.where(i < npt, x_refs[0][...], x_refs[1][...]) if split_in else x_refs[0][...]
    if mixer is not None:
        extra = jnp.where(i < npt, extra_refs[0][...], extra_refs[1][...]) if has_extra else None
        x = _mixer_residual(x, mod[i0 - 1:i0], of_ref, ob_ref, gate_ref, on_ref, wout_ref, extra, nh, dv)
    h = _modnorm(x, g_ref[...], mod[i0:i0 + 1], mod[i0 + 1:i0 + 2]).astype(wgu_ref.dtype)
    acts = []
    for c in range(n_chunks):
        gt = _dot(h, wgu_ref[:, c * ch:(c + 1) * ch])
        up = _dot(h, wgu_ref[:, d_ff + c * ch:d_ff + (c + 1) * ch])
        acts.append((_silu(gt) * up).astype(wd_ref.dtype))
    out = x + (0.5 * mod[i0 + 2:i0 + 3]) * _dot(jnp.concatenate(acts, axis=1), wd_ref[...])
    if final:
        fg_ref, yp_ref, ys_ref = rest
        out = _rms(out) * fg_ref[...]

        @pl.when(i < npt)
        def _():
            yp_ref[...] = out

        @pl.when(i >= npt)
        def _():
            ys_ref[...] = out
    else:
        rest[0][...] = out


def _ffn_half(xs, mod, i0, g, w_gu, w_down, widx, rows, mixer=None, final_g=None):
    split_in = len(xs) == 2
    d = xs[0].shape[1]
    nt = sum(x.shape[0] for x in xs)
    d_ff = w_down.shape[-2]
    tm = FFN_ROW_TILE
    np_rows = rows[0]
    assert np_rows % tm == 0 and rows[1] % tm == 0 and d_ff % (FFN_CHUNKS * LANES) == 0
    cond = _cond_index(*rows, tm)
    npt = np_rows // tm
    final = final_g is not None
    ctx_map = lambda i: (jnp.minimum(i, npt - 1), 0)
    lat_map = lambda i: (jnp.maximum(i - npt, 0), 0)
    mix_specs, mix_args, mix_cfg = [], [], None
    if mixer is not None:
        o_f, o_b, gate, onorm, w_out, nh, dv, extra = mixer
        assert i0 >= 1
        wa = nh * dv
        row = lambda i: (i, 0)
        mix_specs = [pl.BlockSpec((tm, wa), row), pl.BlockSpec((tm, wa), row), pl.BlockSpec((tm, wa), row),
                     pl.BlockSpec((1, dv), lambda i: (0, 0)), _resident(w_out.shape, lambda i: (0, 0))]
        mix_args = [o_f, o_b, gate, onorm.reshape(1, dv), w_out]
        if extra is not None:
            we = extra[0].shape[1]
            mix_specs += [pl.BlockSpec((tm, we), ctx_map), pl.BlockSpec((tm, we), lat_map)]
            mix_args += list(extra)
        mix_cfg = (nh, dv, extra is not None)
    kern = functools.partial(_ffn_kernel, i0=i0, d_ff=d_ff, n_chunks=FFN_CHUNKS, npt=npt, split_in=split_in,
                             mixer=mix_cfg, final=final)
    if split_in:
        x_specs = [pl.BlockSpec((tm, d), ctx_map), pl.BlockSpec((tm, d), lat_map)]
    else:
        x_specs = [pl.BlockSpec((tm, d), lambda i: (i, 0))]
    in_specs = x_specs + [pl.BlockSpec((1, N_MOD, d), lambda i: (cond(i), 0, 0)),
                          pl.BlockSpec((1, d), lambda i: (0, 0)),
                          _resident((None, None, d, 2 * d_ff), lambda i: widx + (0, 0)),
                          _resident((None, None, d_ff, d), lambda i: widx + (0, 0))]
    args = list(xs) + [mod, g.reshape(1, d), w_gu, w_down] + mix_args
    in_specs += mix_specs
    if final:
        in_specs.append(pl.BlockSpec((1, d), lambda i: (0, 0)))
        args.append(final_g.reshape(1, d))
        out_shape = [jax.ShapeDtypeStruct((np_rows, d), F32), jax.ShapeDtypeStruct((nt - np_rows, d), F32)]
        out_specs = [pl.BlockSpec((tm, d), ctx_map), pl.BlockSpec((tm, d), lat_map)]
    else:
        out_shape = jax.ShapeDtypeStruct((nt, d), F32)
        out_specs = pl.BlockSpec((tm, d), lambda i: (i, 0))
    return pl.pallas_call(
        kern,
        out_shape=out_shape,
        grid=(nt // tm,),
        in_specs=in_specs,
        out_specs=out_specs,
        compiler_params=_cparams(("arbitrary",)),
        name="ffn_half",
    )(*args)


def _proj_kernel(x_ref, mod_ref, g_ref, *refs, i0, parts):
    n_out = len(refs) // 2
    w_refs, o_refs = refs[:n_out], refs[n_out:]
    mod = mod_ref[0]
    rp = x_ref.shape[0] // parts
    for p in range(parts):
        rs = slice(p * rp, (p + 1) * rp)
        h = _modnorm(x_ref[rs, :], g_ref[...], mod[i0:i0 + 1], mod[i0 + 1:i0 + 2])
        for w_ref, o_ref in zip(w_refs, o_refs):
            o_ref[rs, :] = _dot(h, w_ref[...]).astype(o_ref.dtype)


def _mixer_in(x, mod, g, groups, dtypes, rows):
    nt, d = x.shape
    tm = PROJ_ROW_TILE
    assert rows[0] % tm == 0 and rows[1] % tm == 0
    cond = _cond_index(*rows, tm)
    kern = functools.partial(_proj_kernel, i0=3, parts=PROJ_ROW_PARTS)
    w_specs = []
    for w, layer, off, wd in groups:
        assert off % wd == 0 and off + wd <= w.shape[-1]
        if layer is None:
            w_specs.append(_resident((d, wd), functools.partial(lambda b, i: (0, b), off // wd)))
        else:
            w_specs.append(_resident((None, d, wd), functools.partial(lambda a, b, i: (a, 0, b), layer, off // wd)))
    return pl.pallas_call(
        kern,
        out_shape=[jax.ShapeDtypeStruct((nt, grp[3]), dt) for grp, dt in zip(groups, dtypes)],
        grid=(nt // tm,),
        in_specs=[pl.BlockSpec((tm, d), lambda i: (i, 0)),
                  pl.BlockSpec((1, N_MOD, d), lambda i: (cond(i), 0, 0)),
                  pl.BlockSpec((1, d), lambda i: (0, 0))] + w_specs,
        out_specs=[pl.BlockSpec((tm, grp[3]), lambda i: (i, 0)) for grp in groups],
        compiler_params=_cparams(("parallel",)),
        name="mixer_in",
    )(x, mod, g.reshape(1, d), *[grp[0] for grp in groups])


def _conv_kernel(prev_ref, x_ref, next_ref, w_ref, o_ref, *, tiles, dk):
    r = pl.program_id(0)
    c, width = x_ref.shape
    pad = (SHORT_CONV - 1) // 2
    has_prev = jnp.where(tiles.pos(r) > 0, 1.0, 0.0)
    has_next = jnp.where(tiles.pos(r) < tiles.length(r) - 1, 1.0, 0.0)
    ext = c + 2 * SUBLANES
    for hh in range(width // dk):
        part = hh // (width // (3 * dk))
        sl = slice(hh * dk, (hh + 1) * dk)
        xe = jnp.concatenate([prev_ref[:, sl] * has_prev, x_ref[:, sl], next_ref[:, sl] * has_next], axis=0)
        w = w_ref[:, sl]
        acc = None
        for j in range(SHORT_CONV):
            sh = pad - j
            xs = xe if sh == 0 else pltpu.roll(xe, sh % ext, axis=0)
            term = xs[SUBLANES:SUBLANES + c] * w[j:j + 1]
            acc = term if acc is None else acc + term
        y = _silu(acc)
        if part < 2:
            nrm = lax.rsqrt(jnp.sum(y * y, axis=-1, keepdims=True) + EPS)
            y = y * (nrm * (dk ** -0.5) if part == 0 else nrm)
        o_ref[:, sl] = y


def _conv_qkv(qkv, conv_w, tiles):
    nt, width = qkv.shape
    c = tiles.c
    per = c // SUBLANES
    n8 = nt // SUBLANES
    kern = functools.partial(_conv_kernel, tiles=tiles, dk=A_DK)
    return pl.pallas_call(
        kern,
        out_shape=jax.ShapeDtypeStruct((nt, width), F32),
        grid=(tiles.n,),
        in_specs=[pl.BlockSpec((SUBLANES, width), lambda r: (jnp.maximum(r * per - 1, 0), 0)),
                  pl.BlockSpec((c, width), lambda r: (r, 0)),
                  pl.BlockSpec((SUBLANES, width), lambda r: (jnp.minimum((r + 1) * per, n8 - 1), 0)),
                  pl.BlockSpec((SHORT_CONV, width), lambda r: (0, 0))],
        out_specs=pl.BlockSpec((c, width), lambda r: (r, 0)),
        compiler_params=_cparams(("parallel",)),
        name="conv_qkv",
    )(qkv, qkv, qkv, conv_w)


def _load_state(state, s0_ref, tiles, i):
    first = tiles.pos(i) == 0

    @pl.when(first & tiles.is_ctx(i))
    def _():
        state[...] = jnp.zeros(state.shape, F32)

    @pl.when(first & jnp.logical_not(tiles.is_ctx(i)))
    def _():
        state[...] = s0_ref[0]


def _store_state(state, out_ref, tiles, i):
    @pl.when((tiles.pos(i) == tiles.length(i) - 1) & tiles.is_ctx(i))
    def _():
        out_ref[0] = state[...]


def _dir_specs(tiles, c, cols):
    specs = []
    for reverse in (False, True):
        for width, col in cols:
            specs.append(pl.BlockSpec((c, width), functools.partial(
                lambda rev, cc, i: (tiles.row_block(i, rev), cc), reverse, col)))
    return specs


def _pair_dot(a, b):
    n = a.shape[0]
    a_bf, b_bf = _bf(a), _bf(b)
    z = jnp.zeros((n, n), BF16)
    b_diag = jnp.concatenate([jnp.concatenate([b_bf[:, :n], z], axis=1),
                              jnp.concatenate([z, b_bf[:, n:]], axis=1)], axis=0)
    return _dot(a_bf, b_diag)


def _unit_tri_solves(ls, rs, reverse_flags):
    c = ls[0].shape[0]
    n = c // 2
    ri = lax.broadcasted_iota(jnp.int32, (n, c), 0)
    ci = lax.broadcasted_iota(jnp.int32, (n, c), 1) & (n - 1)
    pairs = [jnp.concatenate([l[:n, :n], l[n:, n:]], axis=1) for l in ls]
    shift = SUBLANES.bit_length() - 1
    same = (ri >> shift) == (ci >> shift)
    ms = [jnp.where(same, -lp, 0.0) for lp in pairs]
    ps = [_pair_dot(m, m) for m in ms]
    ns = [m + p + _pair_dot(m, p) for m, p in zip(ms, ps)]
    ps = [_pair_dot(p, p) for p in ps]
    ns = [nv + p + _pair_dot(nv, p) for nv, p in zip(ns, ps)]
    while (1 << shift) < n:
        lvl = ((ri >> (shift + 1)) == (ci >> (shift + 1))) & ((ri >> shift) != (ci >> shift))
        cls = [jnp.where(lvl, lp, 0.0) for lp in pairs]
        ys = [cl + _pair_dot(cl, nv) for cl, nv in zip(cls, ns)]
        ns = [nv - (y + _pair_dot(nv, y)) for nv, y in zip(ns, ys)]
        shift += 1
    firsts, seconds = [], []
    for l, r, nv, rev in zip(ls, rs, ns, reverse_flags):
        if rev:
            firsts.append((r[n:], _bf(nv[:, n:])))
            seconds.append((r[:n], _bf(nv[:, :n]), _bf(l[:n, n:])))
        else:
            firsts.append((r[:n], _bf(nv[:, :n])))
            seconds.append((r[n:], _bf(nv[:, n:]), _bf(l[n:, :n])))
    xas = [ra + _dot(na, _bf(ra)) for ra, na in firsts]
    ts = [rb - _dot(lba, _bf(xa)) for (rb, _, lba), xa in zip(seconds, xas)]
    xbs = [t + _dot(nb, _bf(t)) for (_, nb, _), t in zip(seconds, ts)]
    return [jnp.concatenate([xb, xa] if rev else [xa, xb], axis=0) for xa, xb, rev in zip(xas, xbs, reverse_flags)]


def _delta_kernel(qf_ref, kf_ref, vf_ref, baf_ref, qb_ref, kb_ref, vb_ref, bab_ref, al_ref, dt_ref, s0_ref,
                  of_ref, ob_ref, sout_ref, state, *, tiles, nh, dk, dv):
    step = pl.program_id(0)
    _load_state(state, s0_ref, tiles, step)
    c = qf_ref.shape[0]
    ri = lax.broadcasted_iota(jnp.int32, (c, c), 0)
    ci = lax.broadcasted_iota(jnp.int32, (c, c), 1)
    units = []
    for z, (q_ref, k_ref, v_ref, ba_ref, o_ref) in enumerate(
            ((qf_ref, kf_ref, vf_ref, baf_ref, of_ref), (qb_ref, kb_ref, vb_ref, bab_ref, ob_ref))):
        reverse = z == 1
        incl, strict = _order_masks(ri, ci, reverse)
        ba = ba_ref[...]
        g_col = -jnp.exp(al_ref[...]) * _softplus(ba + dt_ref[...])
        beta_col = _sigmoid(ba)
        gc_col = _cumsum_rows(_bf(incl.astype(F32)), g_col)
        gc_row = gc_col.T
        last = 0 if reverse else c - 1
        for h in range(nh):
            cb = z * nh + h
            cg = 2 * nh + cb
            gcc = gc_col[:, cg:cg + 1]
            gcr = gc_row[cg:cg + 1, :]
            units.append(dict(
                z=z, h=h, o_ref=o_ref, strict=strict, gcc=gcc, gl=gcc[last:last + 1],
                beta=beta_col[:, cb:cb + 1], egc=jnp.exp(gcc),
                decay=jnp.where(incl, jnp.exp(gcc - gcr), 0.0),
                q=q_ref[:, h * dk:(h + 1) * dk], k=k_ref[:, h * dk:(h + 1) * dk], v=v_ref[:, h * dv:(h + 1) * dv]))
    for u in units:
        u["kb"] = u["k"] * u["beta"]
        u["k_bf"] = _bf(u["k"])
    kks = [_dot_nt(_bf(u["kb"]), u["k_bf"]) for u in units]
    qks = [_dot_nt(_bf(u["q"]), u["k_bf"]) * u["decay"] for u in units]
    rs = _unit_tri_solves([jnp.where(u["strict"], kk * u["decay"], 0.0) for u, kk in zip(units, kks)],
                          [jnp.concatenate([u["v"] * u["beta"], u["kb"] * u["egc"]], axis=1) for u in units],
                          [u["z"] == 1 for u in units])
    ss = [state[u["z"], u["h"]] for u in units]
    ss_bf = [_bf(s) for s in ss]
    v_news_bf = [_bf(r[:, :dv] - _dot(_bf(r[:, dv:]), sb)) for r, sb in zip(rs, ss_bf)]
    for u, sb, qk, vnb in zip(units, ss_bf, qks, v_news_bf):
        h = u["h"]
        u["o_ref"][:, h * dv:(h + 1) * dv] = _bf(_dot(_bf(u["q"] * u["egc"]), sb) + _dot(_bf(qk), vnb))
    for u, s, vnb in zip(units, ss, v_news_bf):
        kd = u["k"] * jnp.exp(u["gl"] - u["gcc"])
        state[u["z"], u["h"]] = s * jnp.exp(u["gl"]) + _dot_tn(_bf(kd), vnb)
    _store_state(state, sout_ref, tiles, step)


def _delta_rule(qkv, ba, a_log, dt_bias, s0, tiles):
    nt = qkv.shape[0]
    nh, dk, dv = A_HEADS, A_DK, A_DV
    c = tiles.c
    pad = LANES - 4 * nh
    al = jnp.concatenate([jnp.zeros((2 * nh,), F32), a_log.reshape(-1), jnp.zeros((pad,), F32)])
    dt = jnp.concatenate([jnp.zeros((2 * nh,), F32), dt_bias.reshape(-1), jnp.zeros((pad,), F32)])
    const = lambda i: (0, 0)
    in_specs = (_dir_specs(tiles, c, ((nh * dk, 0), (nh * dk, 1), (nh * dv, 2), (LANES, 0)))
                + [pl.BlockSpec((1, LANES), const), pl.BlockSpec((1, LANES), const),
                   pl.BlockSpec((1, 2, nh, dk, dv), lambda i: (tiles.latent_seq(i), 0, 0, 0, 0))])
    kern = functools.partial(_delta_kernel, tiles=tiles, nh=nh, dk=dk, dv=dv)
    return pl.pallas_call(
        kern,
        out_shape=[jax.ShapeDtypeStruct((nt, nh * dv), BF16), jax.ShapeDtypeStruct((nt, nh * dv), BF16),
                   jax.ShapeDtypeStruct((tiles.n_p, 2, nh, dk, dv), F32)],
        grid=(tiles.n,),
        in_specs=in_specs,
        out_specs=[pl.BlockSpec((c, nh * dv), lambda i: (tiles.row_block(i, False), 0)),
                   pl.BlockSpec((c, nh * dv), lambda i: (tiles.row_block(i, True), 0)),
                   pl.BlockSpec((1, 2, nh, dk, dv), lambda i: (tiles.ctx_seq(i), 0, 0, 0, 0))],
        scratch_shapes=[pltpu.VMEM((2, nh, dk, dv), F32)],
        compiler_params=_cparams(("arbitrary",)),
        name="delta_rule",
    )(qkv, qkv, qkv, ba, qkv, qkv, qkv, ba, al.reshape(1, LANES), dt.reshape(1, LANES), s0)


def _ctx_attn_kernel(sink_ref, q_ref, k_ref, v_ref, o_ref, *, n_kv, grp, dh, t):
    scale = dh ** -0.5
    seqs = [slice(b * t, (b + 1) * t) for b in range(q_ref.shape[0] // t)]
    units = [(rs, h) for rs in seqs for h in range(n_kv * grp)]
    ks = {(rs.start, hk): _bf(k_ref[rs, hk * dh:(hk + 1) * dh]) for rs in seqs for hk in range(n_kv)}
    vs = {(rs.start, hk): _bf(v_ref[rs, hk * dh:(hk + 1) * dh]) for rs in seqs for hk in range(n_kv)}
    sinks = [sink_ref[h] for _, h in units]
    ss = [_dot_nt(_bf(q_ref[rs, h * dh:(h + 1) * dh] * scale), ks[rs.start, h // grp]) for rs, h in units]
    ms = [jnp.maximum(jnp.max(s, axis=-1, keepdims=True), sk) for s, sk in zip(ss, sinks)]
    ps = [jnp.exp(s - m) for s, m in zip(ss, ms)]
    dens = [jnp.sum(p, axis=-1, keepdims=True) + jnp.exp(sk - m) for p, sk, m in zip(ps, sinks, ms)]
    for (rs, h), p, den in zip(units, ps, dens):
        o_ref[rs, h * dh:(h + 1) * dh] = _bf(_dot(_bf(p), vs[rs.start, h // grp]) / den)


def _ctx_attention(qb, kb, vb, sink, n_seq, t):
    wq = qb.shape[1]
    wk = kb.shape[1]
    per = math.gcd(n_seq, CTX_SEQS_PER_STEP)
    kern = functools.partial(_ctx_attn_kernel, n_kv=B_KV_HEADS, grp=B_Q_HEADS // B_KV_HEADS, dh=HEAD_DIM, t=t)
    return pl.pallas_call(
        kern,
        out_shape=jax.ShapeDtypeStruct((n_seq * t, wq), BF16),
        grid=(n_seq // per,),
        in_specs=[pl.BlockSpec(memory_space=pltpu.SMEM),
                  pl.BlockSpec((per * t, wq), lambda b: (b, 0)),
                  pl.BlockSpec((per * t, wk), lambda b: (b, 0)),
                  pl.BlockSpec((per * t, wk), lambda b: (b, 0))],
        out_specs=pl.BlockSpec((per * t, wq), lambda b: (b, 0)),
        compiler_params=_cparams(("parallel",)),
        name="ctx_attention",
    )(sink, qb, kb, vb)


def _rope(x, cos, sin_signed):
    lane = lax.broadcasted_iota(jnp.int32, x.shape, 1)
    quarter = HEAD_DIM // 4
    partner = jnp.where((lane % (2 * quarter)) < quarter,
                        pltpu.roll(x, HEAD_DIM - quarter, axis=1), pltpu.roll(x, quarter, axis=1))
    return x * cos + partner * sin_signed


def _lat_attn_kernel(sink_ref, q_ref, k_ref, v_ref, kc_ref, vc_ref, cq_ref, sq_ref, ck_ref, sk_ref,
                     o_ref, *, n_kv, grp, dh, window):
    scale = dh ** -0.5
    tq = q_ref.shape[0]
    t = k_ref.shape[0]
    span = min(t, tq + 2 * window)
    q0 = pl.program_id(1) * tq
    start = pl.multiple_of(jnp.clip(q0 - window, 0, t - span), math.gcd(tq, window))
    rows = pl.ds(start, span)
    qpos = q0 + lax.broadcasted_iota(jnp.int32, (tq, span), 0)
    kpos = start + lax.broadcasted_iota(jnp.int32, (tq, span), 1)
    valid = jnp.abs(qpos - kpos) <= window
    cq, sq = cq_ref[...], sq_ref[...]
    ck, sk_t = ck_ref[rows, :], sk_ref[rows, :]
    kv = []
    for hk in range(n_kv):
        hs = slice(hk * dh, (hk + 1) * dh)
        kv.append((_bf(_rope(k_ref[rows, hs], ck, sk_t)), _bf(v_ref[rows, hs]), _bf(kc_ref[0, :, hs]),
                   _bf(vc_ref[0, :, hs])))
    heads = range(n_kv * grp)
    sinks = [sink_ref[h] for h in heads]
    qs = [q_ref[:, h * dh:(h + 1) * dh] * scale for h in heads]
    s_locs = [jnp.where(valid, _dot_nt(_bf(_rope(q, cq, sq)), kv[h // grp][0]), -jnp.inf) for h, q in zip(heads, qs)]
    s_ctxs = [_dot_nt(_bf(q), kv[h // grp][2]) for h, q in zip(heads, qs)]
    ms = [jnp.maximum(jnp.maximum(jnp.max(sl, axis=-1, keepdims=True), jnp.max(sc, axis=-1, keepdims=True)), sk)
          for sl, sc, sk in zip(s_locs, s_ctxs, sinks)]
    p_locs = [jnp.exp(sl - m) for sl, m in zip(s_locs, ms)]
    p_ctxs = [jnp.exp(sc - m) for sc, m in zip(s_ctxs, ms)]
    dens = [jnp.sum(pl_, axis=-1, keepdims=True) + jnp.sum(pc, axis=-1, keepdims=True) + jnp.exp(sk - m)
            for pl_, pc, sk, m in zip(p_locs, p_ctxs, sinks, ms)]
    for h, pl_, pc, den in zip(heads, p_locs, p_ctxs, dens):
        o_ref[:, h * dh:(h + 1) * dh] = _bf((_dot(_bf(pl_), kv[h // grp][1]) + _dot(_bf(pc), kv[h // grp][3])) / den)


def _lat_attention(qb, kb, vb, k_ctx, v_ctx, sink, cos, sin_signed, row0, n_seq, t):
    wq = qb.shape[1]
    wk = kb.shape[1]
    tq = ATTN_ROW_TILE
    assert t % tq == 0 and row0 % t == 0
    nq = t // tq
    base_q = row0 // tq
    base_t = row0 // t
    past = k_ctx.shape[1]
    kern = functools.partial(_lat_attn_kernel, n_kv=B_KV_HEADS, grp=B_Q_HEADS // B_KV_HEADS, dh=HEAD_DIM,
                             window=WINDOW)
    return pl.pallas_call(
        kern,
        out_shape=jax.ShapeDtypeStruct((n_seq * t, wq), BF16),
        grid=(n_seq, nq),
        in_specs=[pl.BlockSpec(memory_space=pltpu.SMEM),
                  pl.BlockSpec((tq, wq), lambda b, i: (base_q + b * nq + i, 0)),
                  pl.BlockSpec((t, wk), lambda b, i: (base_t + b, 0)),
                  pl.BlockSpec((t, wk), lambda b, i: (base_t + b, 0)),
                  pl.BlockSpec((1, past, wk), lambda b, i: (b, 0, 0)),
                  pl.BlockSpec((1, past, wk), lambda b, i: (b, 0, 0)),
                  pl.BlockSpec((tq, HEAD_DIM), lambda b, i: (i, 0)),
                  pl.BlockSpec((tq, HEAD_DIM), lambda b, i: (i, 0)),
                  pl.BlockSpec((t, HEAD_DIM), lambda b, i: (0, 0)),
                  pl.BlockSpec((t, HEAD_DIM), lambda b, i: (0, 0))],
        out_specs=pl.BlockSpec((tq, wq), lambda b, i: (b * nq + i, 0)),
        compiler_params=_cparams(("parallel", "parallel")),
        name="latent_attention",
    )(sink, qb, kb, vb, k_ctx, v_ctx, cos, sin_signed, cos, sin_signed)


def _rope_tables(t):
    half = HEAD_DIM // 2
    quarter = half // 2
    pos = jnp.arange(t)
    row = (pos // GRID_W).astype(F32)
    col = (pos % GRID_W).astype(F32)
    inv = ROPE_THETA ** (-jnp.arange(quarter, dtype=F32) / quarter)
    ang_r = row[:, None] * inv[None, :]
    ang_c = col[:, None] * inv[None, :]
    cos = jnp.concatenate([jnp.cos(ang_r), jnp.cos(ang_r), jnp.cos(ang_c), jnp.cos(ang_c)], axis=-1)
    sin = jnp.concatenate([-jnp.sin(ang_r), jnp.sin(ang_r), -jnp.sin(ang_c), jnp.sin(ang_c)], axis=-1)
    return cos, sin


def _gla_kernel(qf_ref, kf_ref, vf_ref, lrf_ref, qb_ref, kb_ref, vb_ref, lrb_ref, wg_ref, bias_ref, s0_ref,
                of_ref, ob_ref, sout_ref, state, *, tiles, nh, dk, dv, rb):
    step = pl.program_id(0)
    _load_state(state, s0_ref, tiles, step)
    c = qf_ref.shape[0]
    ri = lax.broadcasted_iota(jnp.int32, (c, c), 0)
    ci = lax.broadcasted_iota(jnp.int32, (c, c), 1)
    eye = (lax.broadcasted_iota(jnp.int32, (dk, dk), 0) == lax.broadcasted_iota(jnp.int32, (dk, dk), 1))
    units = []
    for z, (q_ref, k_ref, v_ref, lr_ref, o_ref) in enumerate(
            ((qf_ref, kf_ref, vf_ref, lrf_ref, of_ref), (qb_ref, kb_ref, vb_ref, lrb_ref, ob_ref))):
        reverse = z == 1
        incl, _ = _order_masks(ri, ci, reverse)
        x = _dot(_bf(lr_ref[...]), _bf(wg_ref[z])) + bias_ref[z]
        gk = -_softplus(-x) * (1.0 / GATE_NORM)
        gcum = _cumsum_rows(_bf(incl.astype(F32)), gk)
        last = 0 if reverse else c - 1
        for h in range(nh):
            g = gcum[:, h * dk:(h + 1) * dk]
            units.append(dict(
                z=z, h=h, o_ref=o_ref, reverse=reverse, incl=incl, g=g, gl_row=g[last:last + 1],
                q=q_ref[:, h * dk:(h + 1) * dk] * (dk ** -0.5), k=k_ref[:, h * dk:(h + 1) * dk],
                v_bf=v_ref[:, h * dv:(h + 1) * dv], s=state[z, h]))
    intra = [[] for _ in units]
    for blk in range(c // rb):
        r0, r1 = blk * rb, (blk + 1) * rb
        scores, cols = [], []
        for u in units:
            g = u["g"]
            mid = r0 + rb // 2
            if u["reverse"]:
                c0, c1 = r0, c
                ref = g[mid:mid + 1]
            else:
                c0, c1 = 0, r1
                ref = g[mid - 1:mid]
            qe = u["q"][r0:r1] * jnp.exp(g[r0:r1] - ref)
            ke = u["k"][c0:c1] * jnp.exp(ref - g[c0:c1])
            scores.append(jnp.where(u["incl"][r0:r1, c0:c1], _dot_nt(_bf(qe), _bf(ke)), 0.0))
            cols.append((c0, c1))
        for parts, u, a, (c0, c1) in zip(intra, units, scores, cols):
            parts.append(_dot(_bf(a), u["v_bf"][c0:c1]))
    inter = [_dot(_bf(u["q"] * jnp.exp(u["g"])), _bf(u["s"])) for u in units]
    for u, o_inter, parts in zip(units, inter, intra):
        h = u["h"]
        u["o_ref"][:, h * dv:(h + 1) * dv] = _bf(o_inter + jnp.concatenate(parts, axis=0))
    for u in units:
        kd = u["k"] * jnp.exp(u["gl_row"] - u["g"])
        gl_col = jnp.sum(jnp.where(eye, jnp.broadcast_to(u["gl_row"], (dk, dk)), 0.0), axis=1, keepdims=True)
        state[u["z"], u["h"]] = u["s"] * jnp.exp(gl_col) + _dot_tn(_bf(kd), u["v_bf"])
    _store_state(state, sout_ref, tiles, step)


def _gla(q, k, v, lr, w_gate, gate_bias, s0, tiles):
    nt = q.shape[0]
    nh, dk, dv = C_HEADS, C_DK, C_DV
    c = tiles.c
    wg = jnp.zeros((2, LANES, nh * dk), F32)
    for z in range(2):
        wg = wg.at[z, z * GATE_RANK:(z + 1) * GATE_RANK].set(w_gate[z])
    in_specs = (_dir_specs(tiles, c, ((nh * dk, 0), (nh * dk, 0), (nh * dv, 0), (LANES, 0)))
                + [pl.BlockSpec((2, LANES, nh * dk), lambda i: (0, 0, 0)),
                   pl.BlockSpec((2, 1, nh * dk), lambda i: (0, 0, 0)),
                   pl.BlockSpec((1, 2, nh, dk, dv), lambda i: (tiles.latent_seq(i), 0, 0, 0, 0))])
    kern = functools.partial(_gla_kernel, tiles=tiles, nh=nh, dk=dk, dv=dv, rb=GLA_ROW_BLOCK)
    return pl.pallas_call(
        kern,
        out_shape=[jax.ShapeDtypeStruct((nt, nh * dv), BF16), jax.ShapeDtypeStruct((nt, nh * dv), BF16),
                   jax.ShapeDtypeStruct((tiles.n_p, 2, nh, dk, dv), F32)],
        grid=(tiles.n,),
        in_specs=in_specs,
        out_specs=[pl.BlockSpec((c, nh * dv), lambda i: (tiles.row_block(i, False), 0)),
                   pl.BlockSpec((c, nh * dv), lambda i: (tiles.row_block(i, True), 0)),
                   pl.BlockSpec((1, 2, nh, dk, dv), lambda i: (tiles.ctx_seq(i), 0, 0, 0, 0))],
        scratch_shapes=[pltpu.VMEM((2, nh, dk, dv), F32)],
        compiler_params=_cparams(("arbitrary",)),
        name="gla",
    )(q, k, v, lr, q, k, v, lr, wg, gate_bias.reshape(2, 1, nh * dk), s0)


def kernel(x_prompt, x_sample, state_delta, cache_k, cache_v, state_gla, c, c_ctx, norm_g, ada_w, ada_b,
           ffn_w_gu, ffn_w_down, even_w_in, even_conv, even_a_log, even_dt_bias, even_onorm, even_sink,
           even_w_out, odd_w_in, odd_w_gate, odd_gate_bias, odd_onorm, odd_w_out, final_g):
    n_p, t_p, d = x_prompt.shape
    n_s, t_s, _ = x_sample.shape
    depth = norm_g.shape[0]
    np_rows, ns_rows = n_p * t_p, n_s * t_s
    assert np_rows % t_s == 0
    assert t_p % CHUNK == 0 and t_s % CHUNK == 0 and t_s % GRID_W == 0
    rows = (np_rows, t_s)
    tiles = _Tiles(n_p, t_p, n_s, t_s, CHUNK)

    n_cond = 1 + n_s
    cond_rows = -(-n_cond // SUBLANES) * SUBLANES
    conds = jnp.concatenate([c_ctx[None, :], c, jnp.zeros((cond_rows - n_cond, d), F32)], axis=0)
    mods = _ada(conds, ada_w, ada_b)[:, :n_cond].reshape(depth, n_cond, N_MOD, d)

    w_gu_second = ffn_w_gu[:, 1:].astype(BF16)
    w_down_second = ffn_w_down[:, 1:].astype(BF16)

    xs = (x_prompt.reshape(np_rows, d), x_sample.reshape(ns_rows, d))
    new_delta, new_k, new_v, new_gla = [], [], [], []
    for l in range(depth):
        j = l // 2
        mod = mods[l]
        x = _ffn_half(xs, mod, 0, norm_g[l, 0], ffn_w_gu, ffn_w_down, (l, 0), rows)
        if l % 2 == 0:
            nh = A_HEADS
            o_qkv = 2 * nh * A_DK + nh * A_DV
            o_gate = o_qkv + nh * A_DV
            o_qb = o_gate + 4 * nh
            w_q, w_kv = B_Q_HEADS * HEAD_DIM, B_KV_HEADS * HEAD_DIM
            w = even_w_in[j]
            tail = jnp.concatenate([w[:, o_qb:], w[:, o_gate:o_qb], jnp.zeros((d, LANES - 4 * nh), F32)], axis=1)
            assert tail.shape[1] == w_q + 2 * w_kv + LANES and o_qkv % (nh * A_DV) == 0
            groups = ((even_w_in, j, 0, o_qkv), (even_w_in, j, o_qkv, nh * A_DV), (tail, None, 0, w_q),
                      (tail, None, w_q, w_kv), (tail, None, w_q + w_kv, w_kv), (tail, None, w_q + 2 * w_kv, LANES))
            dtypes = (F32, BF16, F32, F32, F32, F32)
            qkv, gate, qb, kb, vb, ba = _mixer_in(x, mod, norm_g[l, 1], groups, dtypes, rows)
            qkv_n = _conv_qkv(qkv, even_conv[j], tiles)
            o_f, o_b, st = _delta_rule(qkv_n, ba, even_a_log[j], even_dt_bias[j], state_delta[:, j], tiles)
            cos, sin_signed = _rope_tables(t_s)
            att_p = _ctx_attention(qb, kb, vb, even_sink[j], n_p, t_p)
            att_s = _lat_attention(qb, kb, vb,
                                   cache_k[:, j].reshape(n_s, -1, B_KV_HEADS * HEAD_DIM),
                                   cache_v[:, j].reshape(n_s, -1, B_KV_HEADS * HEAD_DIM),
                                   even_sink[j], cos, sin_signed, np_rows, n_s, t_s)
            mixer = (o_f, o_b, gate, even_onorm[j], even_w_out[j].astype(BF16), nh, A_DV, (att_p, att_s))
            new_delta.append(st)
            new_k.append(kb[:np_rows].reshape(n_p, t_p, B_KV_HEADS, HEAD_DIM))
            new_v.append(vb[:np_rows].reshape(n_p, t_p, B_KV_HEADS, HEAD_DIM))
        else:
            nh = C_HEADS
            o_v = 2 * nh * C_DK
            o_g = o_v + nh * C_DV
            o_lr = o_g + nh * C_DV
            w_lr = jnp.concatenate([odd_w_in[j][:, o_lr:], jnp.zeros((d, LANES - 2 * GATE_RANK), F32)], axis=1)
            groups = ((odd_w_in, j, 0, nh * C_DK), (odd_w_in, j, nh * C_DK, nh * C_DK), (odd_w_in, j, o_v, nh * C_DV),
                      (odd_w_in, j, o_g, nh * C_DV), (w_lr, None, 0, LANES))
            dtypes = (F32, F32, BF16, BF16, F32)
            q, k, v, g_out, lr = _mixer_in(x, mod, norm_g[l, 1], groups, dtypes, rows)
            o_f, o_b, st = _gla(q, k, v, lr, odd_w_gate[j], odd_gate_bias[j], state_gla[:, j], tiles)
            mixer = (o_f, o_b, g_out, odd_onorm[j], odd_w_out[j].astype(BF16), nh, C_DV, None)
            new_gla.append(st)
        last = l == depth - 1
        xs = _ffn_half((x,), mod, 6, norm_g[l, 2], w_gu_second, w_down_second, (l, 0), rows, mixer=mixer,
                       final_g=final_g if last else None)
        if not last:
            xs = (xs,)

    y_prompt, y_sample = xs
    return (y_prompt.reshape(n_p, t_p, d), y_sample.reshape(n_s, t_s, d), jnp.stack(new_delta, axis=1),
            jnp.stack(new_k, axis=1), jnp.stack(new_v, axis=1), jnp.stack(new_gla, axis=1))
```

```python
import functools
import math

import jax
import jax.numpy as jnp
from jax import lax
from jax.experimental import pallas as pl
from jax.experimental.pallas import tpu as pltpu

F32 = jnp.float32
BF16 = jnp.bfloat16

EPS = 1e-6
N_MOD = 9
GRID_W = 64
HEAD_DIM = 128
A_HEADS = 4
A_DK = 128
A_DV = 128
SHORT_CONV = 5
B_Q_HEADS = 4
B_KV_HEADS = 2
WINDOW = 128
C_HEADS = 4
C_DK = 128
C_DV = 256
GATE_RANK = 16
GATE_NORM = 16.0
ROPE_THETA = 10000.0

LANES = 128
SUBLANES = 8
PROJ_ROW_TILE = 1024
PROJ_ROW_PARTS = 2
FFN_ROW_TILE = 512
FFN_CHUNKS = 11
CHUNK = 256
CTX_SEQS_PER_STEP = 4
ATTN_ROW_TILE = 256
GLA_ROW_BLOCK = 128
VMEM_LIMIT = 56 * 1024 * 1024


def _cparams(sem, vmem=VMEM_LIMIT):
    return pltpu.CompilerParams(dimension_semantics=sem, vmem_limit_bytes=vmem)


def _resident(block_shape, index_map):
    return pl.BlockSpec(block_shape, index_map, pipeline_mode=pl.Buffered(1))


def _dot(a, b):
    return jnp.dot(a, b, preferred_element_type=F32)


def _dot_nt(a, b):
    return lax.dot_general(a, b, (((1,), (1,)), ((), ())), preferred_element_type=F32)


def _dot_tn(a, b):
    return lax.dot_general(a, b, (((0,), (0,)), ((), ())), preferred_element_type=F32)


def _bf(x):
    return x.astype(BF16)


def _sigmoid(x):
    return 0.5 * jnp.tanh(0.5 * x) + 0.5


def _silu(x):
    return x * _sigmoid(x)


def _softplus(x):
    return jnp.maximum(x, 0.0) + jnp.log(1.0 + jnp.exp(-jnp.abs(x)))


def _rms(x):
    return x * lax.rsqrt(jnp.mean(x * x, axis=-1, keepdims=True) + EPS)


def _modnorm(x, g, shift, scale):
    return (_rms(x) * g) * (1.0 + scale) + shift


def _cumsum_rows(tri_bf, x):
    hi = _bf(x)
    lo = _bf(x - hi.astype(F32))
    return _dot(tri_bf, hi) + _dot(tri_bf, lo)


def _order_masks(ri, ci, reverse):
    if reverse:
        return ri <= ci, ri < ci
    return ri >= ci, ri > ci


class _Tiles:
    def __init__(self, n_p, t_p, n_s, t_s, c):
        self.n_p, self.n_s, self.c = n_p, n_s, c
        self.per_p, self.per_s = t_p // c, t_s // c
        self.np_tiles = n_p * self.per_p
        self.n = self.np_tiles + n_s * self.per_s

    def is_ctx(self, i):
        return i < self.np_tiles

    def seq(self, i):
        return jnp.where(i < self.np_tiles, i // self.per_p, self.n_p + (i - self.np_tiles) // self.per_s)

    def pos(self, i):
        return jnp.where(i < self.np_tiles, i % self.per_p, (i - self.np_tiles) % self.per_s)

    def length(self, i):
        return jnp.where(i < self.np_tiles, self.per_p, self.per_s)

    def row_block(self, i, reverse):
        return i + self.length(i) - 1 - 2 * self.pos(i) if reverse else i

    def ctx_seq(self, i):
        return jnp.minimum(self.seq(i), self.n_p - 1)

    def latent_seq(self, i):
        return jnp.maximum(self.seq(i) - self.n_p, 0)


def _ada_kernel(c_ref, w_ref, b_ref, o_ref):
    s = _bf(_silu(c_ref[...]))
    o_ref[0] = _dot(s, _bf(w_ref[0])) + b_ref[0]


def _ada(cond, ada_w, ada_b):
    depth, d, n = ada_w.shape
    rows = cond.shape[0]
    tn = n // 4
    return pl.pallas_call(
        _ada_kernel,
        out_shape=jax.ShapeDtypeStruct((depth, rows, n), F32),
        grid=(depth, n // tn),
        in_specs=[pl.BlockSpec((rows, d), lambda l, j: (0, 0)),
                  pl.BlockSpec((1, d, tn), lambda l, j: (l, 0, j)),
                  pl.BlockSpec((1, 1, tn), lambda l, j: (l, 0, j))],
        out_specs=pl.BlockSpec((1, rows, tn), lambda l, j: (l, 0, j)),
        compiler_params=_cparams(("parallel", "parallel")),
        name="ada",
    )(cond, ada_w, ada_b.reshape(depth, 1, n))


def _cond_index(n_prompt_rows, dec_seq, tm):
    npt = n_prompt_rows // tm

    def cond(i):
        return jnp.where(i < npt, 0, 1 + ((i - npt) * tm) // dec_seq)

    return cond


def _mixer_residual(x, gate_mod, of_ref, ob_ref, gate_ref, on_ref, w_ref, extra, nh, dv):
    od = of_ref[...].astype(F32) + ob_ref[...].astype(F32)
    gate = gate_ref[...].astype(F32)
    y = None if extra is None else _dot(extra, w_ref[nh * dv:, :])
    per = max(1, (2 * LANES) // dv)
    for h0 in range(0, nh, per):
        mix = [_bf((_rms(od[:, h * dv:(h + 1) * dv]) * on_ref[...]) * _silu(gate[:, h * dv:(h + 1) * dv]))
               for h in range(h0, min(h0 + per, nh))]
        part = _dot(jnp.concatenate(mix, axis=1), w_ref[h0 * dv:min(h0 + per, nh) * dv, :])
        y = part if y is None else y + part
    return x + gate_mod * y


def _ffn_kernel(*refs, i0, d_ff, n_chunks, npt, split_in, mixer, final):
    refs = list(refs)
    x_refs = [refs.pop(0) for _ in range(2 if split_in else 1)]
    mod_ref, g_ref, wgu_ref, wd_ref = (refs.pop(0) for _ in range(4))
    if mixer is not None:
        nh, dv, has_extra = mixer
        of_ref, ob_ref, gate_ref, on_ref, wout_ref = (refs.pop(0) for _ in range(5))
        extra_refs = [refs.pop(0) for _ in range(2 if has_extra else 0)]
    rest = refs
    i = pl.program_id(0)
    mod = mod_ref[0]
    ch = d_ff // n_chunks
    x = jnp.where(i < npt, x_refs[0][...], x_refs[1][...]) if split_in else x_refs[0][...]
    if mixer is not None:
        extra = jnp.where(i < npt, extra_refs[0][...], extra_refs[1][...]) if has_extra else None
        x = _mixer_residual(x, mod[i0 - 1:i0], of_ref, ob_ref, gate_ref, on_ref, wout_ref, extra, nh, dv)
    h = _modnorm(x, g_ref[...], mod[i0:i0 + 1], mod[i0 + 1:i0 + 2]).astype(wgu_ref.dtype)
    acts = []
    for c in range(n_chunks):
        gt = _dot(h, wgu_ref[:, c * ch:(c + 1) * ch])
        up = _dot(h, wgu_ref[:, d_ff + c * ch:d_ff + (c + 1) * ch])
        acts.append((_silu(gt) * up).astype(wd_ref.dtype))
    out = x + (0.5 * mod[i0 + 2:i0 + 3]) * _dot(jnp.concatenate(acts, axis=1), wd_ref[...])
    if final:
        fg_ref, yp_ref, ys_ref = rest
        out = _rms(out) * fg_ref[...]

        @pl.when(i < npt)
        def _():
            yp_ref[...] = out

        @pl.when(i >= npt)
        def _():
            ys_ref[...] = out
    else:
        rest[0][...] = out


def _ffn_half(xs, mod, i0, g, w_gu, w_down, widx, rows, mixer=None, final_g=None):
    split_in = len(xs) == 2
    d = xs[0].shape[1]
    nt = sum(x.shape[0] for x in xs)
    d_ff = w_down.shape[-2]
    tm = FFN_ROW_TILE
    np_rows = rows[0]
    assert np_rows % tm == 0 and rows[1] % tm == 0 and d_ff % (FFN_CHUNKS * LANES) == 0
    cond = _cond_index(*rows, tm)
    npt = np_rows // tm
    final = final_g is not None
    ctx_map = lambda i: (jnp.minimum(i, npt - 1), 0)
    lat_map = lambda i: (jnp.maximum(i - npt, 0), 0)
    mix_specs, mix_args, mix_cfg = [], [], None
    if mixer is not None:
        o_f, o_b, gate, onorm, w_out, nh, dv, extra = mixer
        assert i0 >= 1
        wa = nh * dv
        row = lambda i: (i, 0)
        mix_specs = [pl.BlockSpec((tm, wa), row), pl.BlockSpec((tm, wa), row), pl.BlockSpec((tm, wa), row),
                     pl.BlockSpec((1, dv), lambda i: (0, 0)), _resident(w_out.shape, lambda i: (0, 0))]
        mix_args = [o_f, o_b, gate, onorm.reshape(1, dv), w_out]
        if extra is not None:
            we = extra[0].shape[1]
            mix_specs += [pl.BlockSpec((tm, we), ctx_map), pl.BlockSpec((tm, we), lat_map)]
            mix_args += list(extra)
        mix_cfg = (nh, dv, extra is not None)
    kern = functools.partial(_ffn_kernel, i0=i0, d_ff=d_ff, n_chunks=FFN_CHUNKS, npt=npt, split_in=split_in,
                             mixer=mix_cfg, final=final)
    if split_in:
        x_specs = [pl.BlockSpec((tm, d), ctx_map), pl.BlockSpec((tm, d), lat_map)]
    else:
        x_specs = [pl.BlockSpec((tm, d), lambda i: (i, 0))]
    in_specs = x_specs + [pl.BlockSpec((1, N_MOD, d), lambda i: (cond(i), 0, 0)),
                          pl.BlockSpec((1, d), lambda i: (0, 0)),
                          _resident((None, None, d, 2 * d_ff), lambda i: widx + (0, 0)),
                          _resident((None, None, d_ff, d), lambda i: widx + (0, 0))]
    args = list(xs) + [mod, g.reshape(1, d), w_gu, w_down] + mix_args
    in_specs += mix_specs
    if final:
        in_specs.append(pl.BlockSpec((1, d), lambda i: (0, 0)))
        args.append(final_g.reshape(1, d))
        out_shape = [jax.ShapeDtypeStruct((np_rows, d), F32), jax.ShapeDtypeStruct((nt - np_rows, d), F32)]
        out_specs = [pl.BlockSpec((tm, d), ctx_map), pl.BlockSpec((tm, d), lat_map)]
    else:
        out_shape = jax.ShapeDtypeStruct((nt, d), F32)
        out_specs = pl.BlockSpec((tm, d), lambda i: (i, 0))
    return pl.pallas_call(
        kern,
        out_shape=out_shape,
        grid=(nt // tm,),
        in_specs=in_specs,
        out_specs=out_specs,
        compiler_params=_cparams(("arbitrary",)),
        name="ffn_half",
    )(*args)


def _proj_kernel(x_ref, mod_ref, g_ref, *refs, i0, parts):
    n_out = len(refs) // 2
    w_refs, o_refs = refs[:n_out], refs[n_out:]
    mod = mod_ref[0]
    rp = x_ref.shape[0] // parts
    for p in range(parts):
        rs = slice(p * rp, (p + 1) * rp)
        h = _modnorm(x_ref[rs, :], g_ref[...], mod[i0:i0 + 1], mod[i0 + 1:i0 + 2])
        for w_ref, o_ref in zip(w_refs, o_refs):
            o_ref[rs, :] = _dot(h, w_ref[...]).astype(o_ref.dtype)


def _mixer_in(x, mod, g, groups, dtypes, rows):
    nt, d = x.shape
    tm = PROJ_ROW_TILE
    assert rows[0] % tm == 0 and rows[1] % tm == 0
    cond = _cond_index(*rows, tm)
    kern = functools.partial(_proj_kernel, i0=3, parts=PROJ_ROW_PARTS)
    w_specs = []
    for w, layer, off, wd in groups:
        assert off % wd == 0 and off + wd <= w.shape[-1]
        if layer is None:
            w_specs.append(_resident((d, wd), functools.partial(lambda b, i: (0, b), off // wd)))
        else:
            w_specs.append(_resident((None, d, wd), functools.partial(lambda a, b, i: (a, 0, b), layer, off // wd)))
    return pl.pallas_call(
        kern,
        out_shape=[jax.ShapeDtypeStruct((nt, grp[3]), dt) for grp, dt in zip(groups, dtypes)],
        grid=(nt // tm,),
        in_specs=[pl.BlockSpec((tm, d), lambda i: (i, 0)),
                  pl.BlockSpec((1, N_MOD, d), lambda i: (cond(i), 0, 0)),
                  pl.BlockSpec((1, d), lambda i: (0, 0))] + w_specs,
        out_specs=[pl.BlockSpec((tm, grp[3]), lambda i: (i, 0)) for grp in groups],
        compiler_params=_cparams(("parallel",)),
        name="mixer_in",
    )(x, mod, g.reshape(1, d), *[grp[0] for grp in groups])


def _conv_kernel(prev_ref, x_ref, next_ref, w_ref, o_ref, *, tiles, dk):
    r = pl.program_id(0)
    c, width = x_ref.shape
    pad = (SHORT_CONV - 1) // 2
    has_prev = jnp.where(tiles.pos(r) > 0, 1.0, 0.0)
    has_next = jnp.where(tiles.pos(r) < tiles.length(r) - 1, 1.0, 0.0)
    ext = c + 2 * SUBLANES
    for hh in range(width // dk):
        part = hh // (width // (3 * dk))
        sl = slice(hh * dk, (hh + 1) * dk)
        xe = jnp.concatenate([prev_ref[:, sl] * has_prev, x_ref[:, sl], next_ref[:, sl] * has_next], axis=0)
        w = w_ref[:, sl]
        acc = None
        for j in range(SHORT_CONV):
            sh = pad - j
            xs = xe if sh == 0 else pltpu.roll(xe, sh % ext, axis=0)
            term = xs[SUBLANES:SUBLANES + c] * w[j:j + 1]
            acc = term if acc is None else acc + term
        y = _silu(acc)
        if part < 2:
            nrm = lax.rsqrt(jnp.sum(y * y, axis=-1, keepdims=True) + EPS)
            y = y * (nrm * (dk ** -0.5) if part == 0 else nrm)
        o_ref[:, sl] = y


def _conv_qkv(qkv, conv_w, tiles):
    nt, width = qkv.shape
    c = tiles.c
    per = c // SUBLANES
    n8 = nt // SUBLANES
    kern = functools.partial(_conv_kernel, tiles=tiles, dk=A_DK)
    return pl.pallas_call(
        kern,
        out_shape=jax.ShapeDtypeStruct((nt, width), F32),
        grid=(tiles.n,),
        in_specs=[pl.BlockSpec((SUBLANES, width), lambda r: (jnp.maximum(r * per - 1, 0), 0)),
                  pl.BlockSpec((c, width), lambda r: (r, 0)),
                  pl.BlockSpec((SUBLANES, width), lambda r: (jnp.minimum((r + 1) * per, n8 - 1), 0)),
                  pl.BlockSpec((SHORT_CONV, width), lambda r: (0, 0))],
        out_specs=pl.BlockSpec((c, width), lambda r: (r, 0)),
        compiler_params=_cparams(("parallel",)),
        name="conv_qkv",
    )(qkv, qkv, qkv, conv_w)


def _load_state(state, s0_ref, tiles, i):
    first = tiles.pos(i) == 0

    @pl.when(first & tiles.is_ctx(i))
    def _():
        state[...] = jnp.zeros(state.shape, F32)

    @pl.when(first & jnp.logical_not(tiles.is_ctx(i)))
    def _():
        state[...] = s0_ref[0]


def _store_state(state, out_ref, tiles, i):
    @pl.when((tiles.pos(i) == tiles.length(i) - 1) & tiles.is_ctx(i))
    def _():
        out_ref[0] = state[...]


def _dir_specs(tiles, c, cols):
    specs = []
    for reverse in (False, True):
        for width, col in cols:
            specs.append(pl.BlockSpec((c, width), functools.partial(
                lambda rev, cc, i: (tiles.row_block(i, rev), cc), reverse, col)))
    return specs


def _pair_dot(a, b):
    n = a.shape[0]
    a_bf, b_bf = _bf(a), _bf(b)
    return jnp.concatenate([_dot(a_bf[:, :n], b_bf[:, :n]), _dot(a_bf[:, n:], b_bf[:, n:])], axis=1)


def _unit_tri_solves(ls, rs, reverse_flags):
    c = ls[0].shape[0]
    n = c // 2
    ri = lax.broadcasted_iota(jnp.int32, (n, c), 0)
    ci = lax.broadcasted_iota(jnp.int32, (n, c), 1) & (n - 1)
    pairs = [jnp.concatenate([l[:n, :n], l[n:, n:]], axis=1) for l in ls]
    shift = SUBLANES.bit_length() - 1
    same = (ri >> shift) == (ci >> shift)
    ms = [jnp.where(same, -lp, 0.0) for lp in pairs]
    ps = [_pair_dot(m, m) for m in ms]
    ns = [m + p + _pair_dot(m, p) for m, p in zip(ms, ps)]
    ps = [_pair_dot(p, p) for p in ps]
    ns = [nv + p + _pair_dot(nv, p) for nv, p in zip(ns, ps)]
    while (1 << shift) < n:
        lvl = ((ri >> (shift + 1)) == (ci >> (shift + 1))) & ((ri >> shift) != (ci >> shift))
        cls = [jnp.where(lvl, lp, 0.0) for lp in pairs]
        ys = [cl + _pair_dot(cl, nv) for cl, nv in zip(cls, ns)]
        ns = [nv - (y + _pair_dot(nv, y)) for nv, y in zip(ns, ys)]
        shift += 1
    firsts, seconds = [], []
    for l, r, nv, rev in zip(ls, rs, ns, reverse_flags):
        if rev:
            firsts.append((r[n:], _bf(nv[:, n:])))
            seconds.append((r[:n], _bf(nv[:, :n]), _bf(l[:n, n:])))
        else:
            firsts.append((r[:n], _bf(nv[:, :n])))
            seconds.append((r[n:], _bf(nv[:, n:]), _bf(l[n:, :n])))
    xas = [ra + _dot(na, _bf(ra)) for ra, na in firsts]
    ts = [rb - _dot(lba, _bf(xa)) for (rb, _, lba), xa in zip(seconds, xas)]
    xbs = [t + _dot(nb, _bf(t)) for (_, nb, _), t in zip(seconds, ts)]
    return [jnp.concatenate([xb, xa] if rev else [xa, xb], axis=0) for xa, xb, rev in zip(xas, xbs, reverse_flags)]


def _delta_kernel(qf_ref, kf_ref, vf_ref, baf_ref, qb_ref, kb_ref, vb_ref, bab_ref, al_ref, dt_ref, s0_ref,
                  of_ref, ob_ref, sout_ref, state, *, tiles, nh, dk, dv):
    step = pl.program_id(0)
    _load_state(state, s0_ref, tiles, step)
    c = qf_ref.shape[0]
    ri = lax.broadcasted_iota(jnp.int32, (c, c), 0)
    ci = lax.broadcasted_iota(jnp.int32, (c, c), 1)
    units = []
    for z, (q_ref, k_ref, v_ref, ba_ref, o_ref) in enumerate(
            ((qf_ref, kf_ref, vf_ref, baf_ref, of_ref), (qb_ref, kb_ref, vb_ref, bab_ref, ob_ref))):
        reverse = z == 1
        incl, strict = _order_masks(ri, ci, reverse)
        ba = ba_ref[...]
        g_col = -jnp.exp(al_ref[...]) * _softplus(ba + dt_ref[...])
        beta_col = _sigmoid(ba)
        gc_col = _cumsum_rows(_bf(incl.astype(F32)), g_col)
        gc_row = gc_col.T
        last = 0 if reverse else c - 1
        for h in range(nh):
            cb = z * nh + h
            cg = 2 * nh + cb
            gcc = gc_col[:, cg:cg + 1]
            gcr = gc_row[cg:cg + 1, :]
            units.append(dict(
                z=z, h=h, o_ref=o_ref, strict=strict, gcc=gcc, gl=gcc[last:last + 1],
                beta=beta_col[:, cb:cb + 1], egc=jnp.exp(gcc),
                decay=jnp.where(incl, jnp.exp(gcc - gcr), 0.0),
                q=q_ref[:, h * dk:(h + 1) * dk], k=k_ref[:, h * dk:(h + 1) * dk], v=v_ref[:, h * dv:(h + 1) * dv]))
    for u in units:
        u["kb"] = u["k"] * u["beta"]
        u["k_bf"] = _bf(u["k"])
    kks = [_dot_nt(_bf(u["kb"]), u["k_bf"]) for u in units]
    qks = [_dot_nt(_bf(u["q"]), u["k_bf"]) * u["decay"] for u in units]
    rs = _unit_tri_solves([jnp.where(u["strict"], kk * u["decay"], 0.0) for u, kk in zip(units, kks)],
                          [jnp.concatenate([u["v"] * u["beta"], u["kb"] * u["egc"]], axis=1) for u in units],
                          [u["z"] == 1 for u in units])
    ss = [state[u["z"], u["h"]] for u in units]
    ss_bf = [_bf(s) for s in ss]
    v_news_bf = [_bf(r[:, :dv] - _dot(_bf(r[:, dv:]), sb)) for r, sb in zip(rs, ss_bf)]
    for u, sb, qk, vnb in zip(units, ss_bf, qks, v_news_bf):
        h = u["h"]
        u["o_ref"][:, h * dv:(h + 1) * dv] = _bf(_dot(_bf(u["q"] * u["egc"]), sb) + _dot(_bf(qk), vnb))
    for u, s, vnb in zip(units, ss, v_news_bf):
        kd = u["k"] * jnp.exp(u["gl"] - u["gcc"])
        state[u["z"], u["h"]] = s * jnp.exp(u["gl"]) + _dot_tn(_bf(kd), vnb)
    _store_state(state, sout_ref, tiles, step)


def _delta_rule(qkv, ba, a_log, dt_bias, s0, tiles):
    nt = qkv.shape[0]
    nh, dk, dv = A_HEADS, A_DK, A_DV
    c = tiles.c
    pad = LANES - 4 * nh
    al = jnp.concatenate([jnp.zeros((2 * nh,), F32), a_log.reshape(-1), jnp.zeros((pad,), F32)])
    dt = jnp.concatenate([jnp.zeros((2 * nh,), F32), dt_bias.reshape(-1), jnp.zeros((pad,), F32)])
    const = lambda i: (0, 0)
    in_specs = (_dir_specs(tiles, c, ((nh * dk, 0), (nh * dk, 1), (nh * dv, 2), (LANES, 0)))
                + [pl.BlockSpec((1, LANES), const), pl.BlockSpec((1, LANES), const),
                   pl.BlockSpec((1, 2, nh, dk, dv), lambda i: (tiles.latent_seq(i), 0, 0, 0, 0))])
    kern = functools.partial(_delta_kernel, tiles=tiles, nh=nh, dk=dk, dv=dv)
    return pl.pallas_call(
        kern,
        out_shape=[jax.ShapeDtypeStruct((nt, nh * dv), BF16), jax.ShapeDtypeStruct((nt, nh * dv), BF16),
                   jax.ShapeDtypeStruct((tiles.n_p, 2, nh, dk, dv), F32)],
        grid=(tiles.n,),
        in_specs=in_specs,
        out_specs=[pl.BlockSpec((c, nh * dv), lambda i: (tiles.row_block(i, False), 0)),
                   pl.BlockSpec((c, nh * dv), lambda i: (tiles.row_block(i, True), 0)),
                   pl.BlockSpec((1, 2, nh, dk, dv), lambda i: (tiles.ctx_seq(i), 0, 0, 0, 0))],
        scratch_shapes=[pltpu.VMEM((2, nh, dk, dv), F32)],
        compiler_params=_cparams(("arbitrary",)),
        name="delta_rule",
    )(qkv, qkv, qkv, ba, qkv, qkv, qkv, ba, al.reshape(1, LANES), dt.reshape(1, LANES), s0)


def _ctx_attn_kernel(sink_ref, q_ref, k_ref, v_ref, o_ref, *, n_kv, grp, dh, t):
    scale = dh ** -0.5
    seqs = [slice(b * t, (b + 1) * t) for b in range(q_ref.shape[0] // t)]
    units = [(rs, h) for rs in seqs for h in range(n_kv * grp)]
    ks = {(rs.start, hk): _bf(k_ref[rs, hk * dh:(hk + 1) * dh]) for rs in seqs for hk in range(n_kv)}
    vs = {(rs.start, hk): _bf(v_ref[rs, hk * dh:(hk + 1) * dh]) for rs in seqs for hk in range(n_kv)}
    sinks = [sink_ref[h] for _, h in units]
    ss = [_dot_nt(_bf(q_ref[rs, h * dh:(h + 1) * dh] * scale), ks[rs.start, h // grp]) for rs, h in units]
    ms = [jnp.maximum(jnp.max(s, axis=-1, keepdims=True), sk) for s, sk in zip(ss, sinks)]
    ps = [jnp.exp(s - m) for s, m in zip(ss, ms)]
    dens = [jnp.sum(p, axis=-1, keepdims=True) + jnp.exp(sk - m) for p, sk, m in zip(ps, sinks, ms)]
    for (rs, h), p, den in zip(units, ps, dens):
        o_ref[rs, h * dh:(h + 1) * dh] = _bf(_dot(_bf(p), vs[rs.start, h // grp]) / den)


def _ctx_attention(qb, kb, vb, sink, n_seq, t):
    wq = qb.shape[1]
    wk = kb.shape[1]
    per = math.gcd(n_seq, CTX_SEQS_PER_STEP)
    kern = functools.partial(_ctx_attn_kernel, n_kv=B_KV_HEADS, grp=B_Q_HEADS // B_KV_HEADS, dh=HEAD_DIM, t=t)
    return pl.pallas_call(
        kern,
        out_shape=jax.ShapeDtypeStruct((n_seq * t, wq), BF16),
        grid=(n_seq // per,),
        in_specs=[pl.BlockSpec(memory_space=pltpu.SMEM),
                  pl.BlockSpec((per * t, wq), lambda b: (b, 0)),
                  pl.BlockSpec((per * t, wk), lambda b: (b, 0)),
                  pl.BlockSpec((per * t, wk), lambda b: (b, 0))],
        out_specs=pl.BlockSpec((per * t, wq), lambda b: (b, 0)),
        compiler_params=_cparams(("parallel",)),
        name="ctx_attention",
    )(sink, qb, kb, vb)


def _rope(x, cos, sin_signed):
    lane = lax.broadcasted_iota(jnp.int32, x.shape, 1)
    quarter = HEAD_DIM // 4
    partner = jnp.where((lane % (2 * quarter)) < quarter,
                        pltpu.roll(x, HEAD_DIM - quarter, axis=1), pltpu.roll(x, quarter, axis=1))
    return x * cos + partner * sin_signed


def _lat_attn_kernel(sink_ref, q_ref, k_ref, v_ref, kc_ref, vc_ref, cq_ref, sq_ref, ck_ref, sk_ref,
                     o_ref, *, n_kv, grp, dh, window):
    scale = dh ** -0.5
    tq = q_ref.shape[0]
    t = k_ref.shape[0]
    span = min(t, tq + 2 * window)
    q0 = pl.program_id(1) * tq
    start = pl.multiple_of(jnp.clip(q0 - window, 0, t - span), math.gcd(tq, window))
    rows = pl.ds(start, span)
    qpos = q0 + lax.broadcasted_iota(jnp.int32, (tq, span), 0)
    kpos = start + lax.broadcasted_iota(jnp.int32, (tq, span), 1)
    valid = jnp.abs(qpos - kpos) <= window
    cq, sq = cq_ref[...], sq_ref[...]
    ck, sk_t = ck_ref[rows, :], sk_ref[rows, :]
    kv = []
    for hk in range(n_kv):
        hs = slice(hk * dh, (hk + 1) * dh)
        kv.append((_bf(_rope(k_ref[rows, hs], ck, sk_t)), _bf(v_ref[rows, hs]), _bf(kc_ref[0, :, hs]),
                   _bf(vc_ref[0, :, hs])))
    heads = range(n_kv * grp)
    sinks = [sink_ref[h] for h in heads]
    qs = [q_ref[:, h * dh:(h + 1) * dh] * scale for h in heads]
    s_locs = [jnp.where(valid, _dot_nt(_bf(_rope(q, cq, sq)), kv[h // grp][0]), -jnp.inf) for h, q in zip(heads, qs)]
    s_ctxs = [_dot_nt(_bf(q), kv[h // grp][2]) for h, q in zip(heads, qs)]
    ms = [jnp.maximum(jnp.maximum(jnp.max(sl, axis=-1, keepdims=True), jnp.max(sc, axis=-1, keepdims=True)), sk)
          for sl, sc, sk in zip(s_locs, s_ctxs, sinks)]
    p_locs = [jnp.exp(sl - m) for sl, m in zip(s_locs, ms)]
    p_ctxs = [jnp.exp(sc - m) for sc, m in zip(s_ctxs, ms)]
    dens = [jnp.sum(pl_, axis=-1, keepdims=True) + jnp.sum(pc, axis=-1, keepdims=True) + jnp.exp(sk - m)
            for pl_, pc, sk, m in zip(p_locs, p_ctxs, sinks, ms)]
    for h, pl_, pc, den in zip(heads, p_locs, p_ctxs, dens):
        o_ref[:, h * dh:(h + 1) * dh] = _bf((_dot(_bf(pl_), kv[h // grp][1]) + _dot(_bf(pc), kv[h // grp][3])) / den)


def _lat_attention(qb, kb, vb, k_ctx, v_ctx, sink, cos, sin_signed, row0, n_seq, t):
    wq = qb.shape[1]
    wk = kb.shape[1]
    tq = ATTN_ROW_TILE
    assert t % tq == 0 and row0 % t == 0
    nq = t // tq
    base_q = row0 // tq
    base_t = row0 // t
    past = k_ctx.shape[1]
    kern = functools.partial(_lat_attn_kernel, n_kv=B_KV_HEADS, grp=B_Q_HEADS // B_KV_HEADS, dh=HEAD_DIM,
                             window=WINDOW)
    return pl.pallas_call(
        kern,
        out_shape=jax.ShapeDtypeStruct((n_seq * t, wq), BF16),
        grid=(n_seq, nq),
        in_specs=[pl.BlockSpec(memory_space=pltpu.SMEM),
                  pl.BlockSpec((tq, wq), lambda b, i: (base_q + b * nq + i, 0)),
                  pl.BlockSpec((t, wk), lambda b, i: (base_t + b, 0)),
                  pl.BlockSpec((t, wk), lambda b, i: (base_t + b, 0)),
                  pl.BlockSpec((1, past, wk), lambda b, i: (b, 0, 0)),
                  pl.BlockSpec((1, past, wk), lambda b, i: (b, 0, 0)),
                  pl.BlockSpec((tq, HEAD_DIM), lambda b, i: (i, 0)),
                  pl.BlockSpec((tq, HEAD_DIM), lambda b, i: (i, 0)),
                  pl.BlockSpec((t, HEAD_DIM), lambda b, i: (0, 0)),
                  pl.BlockSpec((t, HEAD_DIM), lambda b, i: (0, 0))],
        out_specs=pl.BlockSpec((tq, wq), lambda b, i: (b * nq + i, 0)),
        compiler_params=_cparams(("parallel", "parallel")),
        name="latent_attention",
    )(sink, qb, kb, vb, k_ctx, v_ctx, cos, sin_signed, cos, sin_signed)


def _rope_tables(t):
    half = HEAD_DIM // 2
    quarter = half // 2
    pos = jnp.arange(t)
    row = (pos // GRID_W).astype(F32)
    col = (pos % GRID_W).astype(F32)
    inv = ROPE_THETA ** (-jnp.arange(quarter, dtype=F32) / quarter)
    ang_r = row[:, None] * inv[None, :]
    ang_c = col[:, None] * inv[None, :]
    cos = jnp.concatenate([jnp.cos(ang_r), jnp.cos(ang_r), jnp.cos(ang_c), jnp.cos(ang_c)], axis=-1)
    sin = jnp.concatenate([-jnp.sin(ang_r), jnp.sin(ang_r), -jnp.sin(ang_c), jnp.sin(ang_c)], axis=-1)
    return cos, sin


def _gla_kernel(qf_ref, kf_ref, vf_ref, lrf_ref, qb_ref, kb_ref, vb_ref, lrb_ref, wg_ref, bias_ref, s0_ref,
                of_ref, ob_ref, sout_ref, state, *, tiles, nh, dk, dv, rb):
    step = pl.program_id(0)
    _load_state(state, s0_ref, tiles, step)
    c = qf_ref.shape[0]
    ri = lax.broadcasted_iota(jnp.int32, (c, c), 0)
    ci = lax.broadcasted_iota(jnp.int32, (c, c), 1)
    eye = (lax.broadcasted_iota(jnp.int32, (dk, dk), 0) == lax.broadcasted_iota(jnp.int32, (dk, dk), 1))
    units = []
    for z, (q_ref, k_ref, v_ref, lr_ref, o_ref) in enumerate(
            ((qf_ref, kf_ref, vf_ref, lrf_ref, of_ref), (qb_ref, kb_ref, vb_ref, lrb_ref, ob_ref))):
        reverse = z == 1
        incl, _ = _order_masks(ri, ci, reverse)
        x = _dot(_bf(lr_ref[...]), _bf(wg_ref[z])) + bias_ref[z]
        gk = -_softplus(-x) * (1.0 / GATE_NORM)
        gcum = _cumsum_rows(_bf(incl.astype(F32)), gk)
        last = 0 if reverse else c - 1
        for h in range(nh):
            g = gcum[:, h * dk:(h + 1) * dk]
            units.append(dict(
                z=z, h=h, o_ref=o_ref, reverse=reverse, incl=incl, g=g, gl_row=g[last:last + 1],
                q=q_ref[:, h * dk:(h + 1) * dk] * (dk ** -0.5), k=k_ref[:, h * dk:(h + 1) * dk],
                v_bf=v_ref[:, h * dv:(h + 1) * dv], s=state[z, h]))
    intra = [[] for _ in units]
    for blk in range(c // rb):
        r0, r1 = blk * rb, (blk + 1) * rb
        scores, cols = [], []
        for u in units:
            g = u["g"]
            mid = r0 + rb // 2
            if u["reverse"]:
                c0, c1 = r0, c
                ref = g[mid:mid + 1]
            else:
                c0, c1 = 0, r1
                ref = g[mid - 1:mid]
            qe = u["q"][r0:r1] * jnp.exp(g[r0:r1] - ref)
            ke = u["k"][c0:c1] * jnp.exp(ref - g[c0:c1])
            scores.append(jnp.where(u["incl"][r0:r1, c0:c1], _dot_nt(_bf(qe), _bf(ke)), 0.0))
            cols.append((c0, c1))
        for parts, u, a, (c0, c1) in zip(intra, units, scores, cols):
            parts.append(_dot(_bf(a), u["v_bf"][c0:c1]))
    inter = [_dot(_bf(u["q"] * jnp.exp(u["g"])), _bf(u["s"])) for u in units]
    for u, o_inter, parts in zip(units, inter, intra):
        h = u["h"]
        u["o_ref"][:, h * dv:(h + 1) * dv] = _bf(o_inter + jnp.concatenate(parts, axis=0))
    for u in units:
        kd = u["k"] * jnp.exp(u["gl_row"] - u["g"])
        gl_col = jnp.sum(jnp.where(eye, jnp.broadcast_to(u["gl_row"], (dk, dk)), 0.0), axis=1, keepdims=True)
        state[u["z"], u["h"]] = u["s"] * jnp.exp(gl_col) + _dot_tn(_bf(kd), u["v_bf"])
    _store_state(state, sout_ref, tiles, step)


def _gla(q, k, v, lr, w_gate, gate_bias, s0, tiles):
    nt = q.shape[0]
    nh, dk, dv = C_HEADS, C_DK, C_DV
    c = tiles.c
    wg = jnp.zeros((2, LANES, nh * dk), F32)
    for z in range(2):
        wg = wg.at[z, z * GATE_RANK:(z + 1) * GATE_RANK].set(w_gate[z])
    in_specs = (_dir_specs(tiles, c, ((nh * dk, 0), (nh * dk, 0), (nh * dv, 0), (LANES, 0)))
                + [pl.BlockSpec((2, LANES, nh * dk), lambda i: (0, 0, 0)),
                   pl.BlockSpec((2, 1, nh * dk), lambda i: (0, 0, 0)),
                   pl.BlockSpec((1, 2, nh, dk, dv), lambda i: (tiles.latent_seq(i), 0, 0, 0, 0))])
    kern = functools.partial(_gla_kernel, tiles=tiles, nh=nh, dk=dk, dv=dv, rb=GLA_ROW_BLOCK)
    return pl.pallas_call(
        kern,
        out_shape=[jax.ShapeDtypeStruct((nt, nh * dv), BF16), jax.ShapeDtypeStruct((nt, nh * dv), BF16),
                   jax.ShapeDtypeStruct((tiles.n_p, 2, nh, dk, dv), F32)],
        grid=(tiles.n,),
        in_specs=in_specs,
        out_specs=[pl.BlockSpec((c, nh * dv), lambda i: (tiles.row_block(i, False), 0)),
                   pl.BlockSpec((c, nh * dv), lambda i: (tiles.row_block(i, True), 0)),
                   pl.BlockSpec((1, 2, nh, dk, dv), lambda i: (tiles.ctx_seq(i), 0, 0, 0, 0))],
        scratch_shapes=[pltpu.VMEM((2, nh, dk, dv), F32)],
        compiler_params=_cparams(("arbitrary",)),
        name="gla",
    )(q, k, v, lr, q, k, v, lr, wg, gate_bias.reshape(2, 1, nh * dk), s0)


def kernel(x_prompt, x_sample, state_delta, cache_k, cache_v, state_gla, c, c_ctx, norm_g, ada_w, ada_b,
           ffn_w_gu, ffn_w_down, even_w_in, even_conv, even_a_log, even_dt_bias, even_onorm, even_sink,
           even_w_out, odd_w_in, odd_w_gate, odd_gate_bias, odd_onorm, odd_w_out, final_g):
    n_p, t_p, d = x_prompt.shape
    n_s, t_s, _ = x_sample.shape
    depth = norm_g.shape[0]
    np_rows, ns_rows = n_p * t_p, n_s * t_s
    assert np_rows % t_s == 0
    assert t_p % CHUNK == 0 and t_s % CHUNK == 0 and t_s % GRID_W == 0
    rows = (np_rows, t_s)
    tiles = _Tiles(n_p, t_p, n_s, t_s, CHUNK)

    n_cond = 1 + n_s
    cond_rows = -(-n_cond // SUBLANES) * SUBLANES
    conds = jnp.concatenate([c_ctx[None, :], c, jnp.zeros((cond_rows - n_cond, d), F32)], axis=0)
    mods = _ada(conds, ada_w, ada_b)[:, :n_cond].reshape(depth, n_cond, N_MOD, d)

    w_gu_second = ffn_w_gu[:, 1:].astype(BF16)
    w_down_second = ffn_w_down[:, 1:].astype(BF16)

    xs = (x_prompt.reshape(np_rows, d), x_sample.reshape(ns_rows, d))
    new_delta, new_k, new_v, new_gla = [], [], [], []
    for l in range(depth):
        j = l // 2
        mod = mods[l]
        x = _ffn_half(xs, mod, 0, norm_g[l, 0], ffn_w_gu, ffn_w_down, (l, 0), rows)
        if l % 2 == 0:
            nh = A_HEADS
            o_qkv = 2 * nh * A_DK + nh * A_DV
            o_gate = o_qkv + nh * A_DV
            o_qb = o_gate + 4 * nh
            w_q, w_kv = B_Q_HEADS * HEAD_DIM, B_KV_HEADS * HEAD_DIM
            w = even_w_in[j]
            tail = jnp.concatenate([w[:, o_qb:], w[:, o_gate:o_qb], jnp.zeros((d, LANES - 4 * nh), F32)], axis=1)
            assert tail.shape[1] == w_q + 2 * w_kv + LANES and o_qkv % (nh * A_DV) == 0
            groups = ((even_w_in, j, 0, o_qkv), (even_w_in, j, o_qkv, nh * A_DV), (tail, None, 0, w_q),
                      (tail, None, w_q, w_kv), (tail, None, w_q + w_kv, w_kv), (tail, None, w_q + 2 * w_kv, LANES))
            dtypes = (F32, BF16, F32, F32, F32, F32)
            qkv, gate, qb, kb, vb, ba = _mixer_in(x, mod, norm_g[l, 1], groups, dtypes, rows)
            qkv_n = _conv_qkv(qkv, even_conv[j], tiles)
            o_f, o_b, st = _delta_rule(qkv_n, ba, even_a_log[j], even_dt_bias[j], state_delta[:, j], tiles)
            cos, sin_signed = _rope_tables(t_s)
            att_p = _ctx_attention(qb, kb, vb, even_sink[j], n_p, t_p)
            att_s = _lat_attention(qb, kb, vb,
                                   cache_k[:, j].reshape(n_s, -1, B_KV_HEADS * HEAD_DIM),
                                   cache_v[:, j].reshape(n_s, -1, B_KV_HEADS * HEAD_DIM),
                                   even_sink[j], cos, sin_signed, np_rows, n_s, t_s)
            mixer = (o_f, o_b, gate, even_onorm[j], even_w_out[j].astype(BF16), nh, A_DV, (att_p, att_s))
            new_delta.append(st)
            new_k.append(kb[:np_rows].reshape(n_p, t_p, B_KV_HEADS, HEAD_DIM))
            new_v.append(vb[:np_rows].reshape(n_p, t_p, B_KV_HEADS, HEAD_DIM))
        else:
            nh = C_HEADS
            o_v = 2 * nh * C_DK
            o_g = o_v + nh * C_DV
            o_lr = o_g + nh * C_DV
            w_lr = jnp.concatenate([odd_w_in[j][:, o_lr:], jnp.zeros((d, LANES - 2 * GATE_RANK), F32)], axis=1)
            groups = ((odd_w_in, j, 0, nh * C_DK), (odd_w_in, j, nh * C_DK, nh * C_DK), (odd_w_in, j, o_v, nh * C_DV),
                      (odd_w_in, j, o_g, nh * C_DV), (w_lr, None, 0, LANES))
            dtypes = (F32, F32, BF16, BF16, F32)
            q, k, v, g_out, lr = _mixer_in(x, mod, norm_g[l, 1], groups, dtypes, rows)
            o_f, o_b, st = _gla(q, k, v, lr, odd_w_gate[j], odd_gate_bias[j], state_gla[:, j], tiles)
            mixer = (o_f, o_b, g_out, odd_onorm[j], odd_w_out[j].astype(BF16), nh, C_DV, None)
            new_gla.append(st)
        last = l == depth - 1
        xs = _ffn_half((x,), mod, 6, norm_g[l, 2], w_gu_second, w_down_second, (l, 0), rows, mixer=mixer,
                       final_g=final_g if last else None)
        if not last:
            xs = (xs,)

    y_prompt, y_sample = xs
    return (y_prompt.reshape(n_p, t_p, d), y_sample.reshape(n_s, t_s, d), jnp.stack(new_delta, axis=1),
            jnp.stack(new_k, axis=1), jnp.stack(new_v, axis=1), jnp.stack(new_gla, axis=1))
```

```python
import functools
import math

import jax
import jax.numpy as jnp
from jax import lax
from jax.experimental import pallas as pl
from jax.experimental.pallas import tpu as pltpu

F32 = jnp.float32
BF16 = jnp.bfloat16

EPS = 1e-6
N_MOD = 9
GRID_W = 64
HEAD_DIM = 128
A_HEADS = 4
A_DK = 128
A_DV = 128
SHORT_CONV = 5
B_Q_HEADS = 4
B_KV_HEADS = 2
WINDOW = 128
C_HEADS = 4
C_DK = 128
C_DV = 256
GATE_RANK = 16
GATE_NORM = 16.0
ROPE_THETA = 10000.0

LANES = 128
SUBLANES = 8
PROJ_ROW_TILE = 1024
PROJ_ROW_PARTS = 2
FFN_ROW_TILE = 512
FFN_CHUNKS = 11
CHUNK = 256
CTX_SEQS_PER_STEP = 4
ATTN_ROW_TILE = 256
GLA_ROW_BLOCK = 128
VMEM_LIMIT = 56 * 1024 * 1024


def _cparams(sem, vmem=VMEM_LIMIT):
    return pltpu.CompilerParams(dimension_semantics=sem, vmem_limit_bytes=vmem)


def _resident(block_shape, index_map):
    return pl.BlockSpec(block_shape, index_map, pipeline_mode=pl.Buffered(1))


def _dot(a, b):
    return jnp.dot(a, b, preferred_element_type=F32)


def _dot_nt(a, b):
    return lax.dot_general(a, b, (((1,), (1,)), ((), ())), preferred_element_type=F32)


def _dot_tn(a, b):
    return lax.dot_general(a, b, (((0,), (0,)), ((), ())), preferred_element_type=F32)


def _bf(x):
    return x.astype(BF16)


def _sigmoid(x):
    return 0.5 * jnp.tanh(0.5 * x) + 0.5


def _silu(x):
    return x * _sigmoid(x)


def _softplus(x):
    return jnp.maximum(x, 0.0) + jnp.log(1.0 + jnp.exp(-jnp.abs(x)))


def _rms(x):
    return x * lax.rsqrt(jnp.mean(x * x, axis=-1, keepdims=True) + EPS)


def _modnorm(x, g, shift, scale):
    return (_rms(x) * g) * (1.0 + scale) + shift


def _cumsum_rows(tri_bf, x):
    hi = _bf(x)
    lo = _bf(x - hi.astype(F32))
    return _dot(tri_bf, hi) + _dot(tri_bf, lo)


def _order_masks(ri, ci, reverse):
    if reverse:
        return ri <= ci, ri < ci
    return ri >= ci, ri > ci


class _Tiles:
    def __init__(self, n_p, t_p, n_s, t_s, c):
        self.n_p, self.n_s, self.c = n_p, n_s, c
        self.per_p, self.per_s = t_p // c, t_s // c
        self.np_tiles = n_p * self.per_p
        self.n = self.np_tiles + n_s * self.per_s

    def is_ctx(self, i):
        return i < self.np_tiles

    def seq(self, i):
        return jnp.where(i < self.np_tiles, i // self.per_p, self.n_p + (i - self.np_tiles) // self.per_s)

    def pos(self, i):
        return jnp.where(i < self.np_tiles, i % self.per_p, (i - self.np_tiles) % self.per_s)

    def length(self, i):
        return jnp.where(i < self.np_tiles, self.per_p, self.per_s)

    def row_block(self, i, reverse):
        return i + self.length(i) - 1 - 2 * self.pos(i) if reverse else i

    def ctx_seq(self, i):
        return jnp.minimum(self.seq(i), self.n_p - 1)

    def latent_seq(self, i):
        return jnp.maximum(self.seq(i) - self.n_p, 0)


def _ada_kernel(c_ref, w_ref, b_ref, o_ref):
    s = _bf(_silu(c_ref[...]))
    o_ref[0] = _dot(s, _bf(w_ref[0])) + b_ref[0]


def _ada(cond, ada_w, ada_b):
    depth, d, n = ada_w.shape
    rows = cond.shape[0]
    tn = n // 4
    return pl.pallas_call(
        _ada_kernel,
        out_shape=jax.ShapeDtypeStruct((depth, rows, n), F32),
        grid=(depth, n // tn),
        in_specs=[pl.BlockSpec((rows, d), lambda l, j: (0, 0)),
                  pl.BlockSpec((1, d, tn), lambda l, j: (l, 0, j)),
                  pl.BlockSpec((1, 1, tn), lambda l, j: (l, 0, j))],
        out_specs=pl.BlockSpec((1, rows, tn), lambda l, j: (l, 0, j)),
        compiler_params=_cparams(("parallel", "parallel")),
        name="ada",
    )(cond, ada_w, ada_b.reshape(depth, 1, n))


def _cond_index(n_prompt_rows, dec_seq, tm):
    npt = n_prompt_rows // tm

    def cond(i):
        return jnp.where(i < npt, 0, 1 + ((i - npt) * tm) // dec_seq)

    return cond


def _mixer_residual(x, gate_mod, of_ref, ob_ref, gate_ref, on_ref, w_ref, extra, nh, dv):
    od = of_ref[...].astype(F32) + ob_ref[...].astype(F32)
    gate = gate_ref[...].astype(F32)
    y = None if extra is None else _dot(extra, w_ref[nh * dv:, :])
    per = max(1, (2 * LANES) // dv)
    for h0 in range(0, nh, per):
        mix = [_bf((_rms(od[:, h * dv:(h + 1) * dv]) * on_ref[...]) * _silu(gate[:, h * dv:(h + 1) * dv]))
               for h in range(h0, min(h0 + per, nh))]
        part = _dot(jnp.concatenate(mix, axis=1), w_ref[h0 * dv:min(h0 + per, nh) * dv, :])
        y = part if y is None else y + part
    return x + gate_mod * y


def _stream_cast(src_slices, stages, sem, dst_slices):
    def copy(k):
        return pltpu.make_async_copy(src_slices[k], stages[k].at[k % 2], sem.at[k % 2])

    copy(0).start()
    for k in range(len(src_slices)):
        if k + 1 < len(src_slices):
            copy(k + 1).start()
        copy(k).wait()
        dst_ref, idx = dst_slices[k]
        dst_ref[idx] = stages[k][k % 2].astype(BF16)


def _ffn_kernel(*refs, i0, d_ff, n_chunks, npt, split_in, mixer, final, stream):
    refs = list(refs)
    x_refs = [refs.pop(0) for _ in range(2 if split_in else 1)]
    mod_ref, g_ref, wgu_ref, wd_ref = (refs.pop(0) for _ in range(4))
    if mixer is not None:
        nh, dv, has_extra = mixer
        of_ref, ob_ref, gate_ref, on_ref, wout_ref = (refs.pop(0) for _ in range(5))
        extra_refs = [refs.pop(0) for _ in range(2 if has_extra else 0)]
    i = pl.program_id(0)
    if stream is not None:
        wgu_hbm, wd_hbm = wgu_ref, wd_ref
        wgu_ref, wd_ref, stage_gu, stage_d, sem = refs[-5:]
        refs = refs[:-5]
        cw, rw = stage_gu.shape[2], stage_d.shape[1]

        @pl.when(i == 0)
        def _():
            n_gu, n_d = wgu_ref.shape[1] // cw, wd_ref.shape[0] // rw
            srcs = ([wgu_hbm.at[stream[0], stream[1], :, pl.ds(c * cw, cw)] for c in range(n_gu)]
                    + [wd_hbm.at[stream[0], stream[1], pl.ds(c * rw, rw), :] for c in range(n_d)])
            dsts = ([(wgu_ref, (slice(None), slice(c * cw, (c + 1) * cw))) for c in range(n_gu)]
                    + [(wd_ref, (slice(c * rw, (c + 1) * rw), slice(None))) for c in range(n_d)])
            _stream_cast(srcs, [stage_gu] * n_gu + [stage_d] * n_d, sem, dsts)
    rest = refs
    mod = mod_ref[0]
    ch = d_ff // n_chunks
    x = jnp.where(i < npt, x_refs[0][...], x_refs[1][...]) if split_in else x_refs[0][...]
    if mixer is not None:
        extra = jnp.where(i < npt, extra_refs[0][...], extra_refs[1][...]) if has_extra else None
        x = _mixer_residual(x, mod[i0 - 1:i0], of_ref, ob_ref, gate_ref, on_ref, wout_ref, extra, nh, dv)
    h = _modnorm(x, g_ref[...], mod[i0:i0 + 1], mod[i0 + 1:i0 + 2]).astype(wgu_ref.dtype)
    acts = []
    for c in range(n_chunks):
        gt = _dot(h, wgu_ref[:, c * ch:(c + 1) * ch])
        up = _dot(h, wgu_ref[:, d_ff + c * ch:d_ff + (c + 1) * ch])
        acts.append((_silu(gt) * up).astype(wd_ref.dtype))
    out = x + (0.5 * mod[i0 + 2:i0 + 3]) * _dot(jnp.concatenate(acts, axis=1), wd_ref[...])
    if final:
        fg_ref, yp_ref, ys_ref = rest
        out = _rms(out) * fg_ref[...]

        @pl.when(i < npt)
        def _():
            yp_ref[...] = out

        @pl.when(i >= npt)
        def _():
            ys_ref[...] = out
    else:
        rest[0][...] = out


def _ffn_half(xs, mod, i0, g, w_gu, w_down, widx, rows, mixer=None, final_g=None):
    split_in = len(xs) == 2
    d = xs[0].shape[1]
    nt = sum(x.shape[0] for x in xs)
    d_ff = w_down.shape[-2]
    tm = FFN_ROW_TILE
    np_rows = rows[0]
    assert np_rows % tm == 0 and rows[1] % tm == 0 and d_ff % (FFN_CHUNKS * LANES) == 0
    cond = _cond_index(*rows, tm)
    npt = np_rows // tm
    final = final_g is not None
    ctx_map = lambda i: (jnp.minimum(i, npt - 1), 0)
    lat_map = lambda i: (jnp.maximum(i - npt, 0), 0)
    mix_specs, mix_args, mix_cfg = [], [], None
    if mixer is not None:
        o_f, o_b, gate, onorm, w_out, nh, dv, extra = mixer
        assert i0 >= 1
        wa = nh * dv
        row = lambda i: (i, 0)
        mix_specs = [pl.BlockSpec((tm, wa), row), pl.BlockSpec((tm, wa), row), pl.BlockSpec((tm, wa), row),
                     pl.BlockSpec((1, dv), lambda i: (0, 0)), _resident(w_out.shape, lambda i: (0, 0))]
        mix_args = [o_f, o_b, gate, onorm.reshape(1, dv), w_out]
        if extra is not None:
            we = extra[0].shape[1]
            mix_specs += [pl.BlockSpec((tm, we), ctx_map), pl.BlockSpec((tm, we), lat_map)]
            mix_args += list(extra)
        mix_cfg = (nh, dv, extra is not None)
    stream = widx if mixer is not None else None
    kern = functools.partial(_ffn_kernel, i0=i0, d_ff=d_ff, n_chunks=FFN_CHUNKS, npt=npt, split_in=split_in,
                             mixer=mix_cfg, final=final, stream=stream)
    if split_in:
        x_specs = [pl.BlockSpec((tm, d), ctx_map), pl.BlockSpec((tm, d), lat_map)]
    else:
        x_specs = [pl.BlockSpec((tm, d), lambda i: (i, 0))]
    scratch = []
    if stream is None:
        w_specs = [_resident((None, None, d, 2 * d_ff), lambda i: widx + (0, 0)),
                   _resident((None, None, d_ff, d), lambda i: widx + (0, 0))]
    else:
        w_specs = [pl.BlockSpec(memory_space=pl.ANY), pl.BlockSpec(memory_space=pl.ANY)]
        piece = d_ff // FFN_CHUNKS
        scratch = [pltpu.VMEM((d, 2 * d_ff), BF16), pltpu.VMEM((d_ff, d), BF16), pltpu.VMEM((2, d, piece), F32),
                   pltpu.VMEM((2, piece, d), F32), pltpu.SemaphoreType.DMA((2,))]
    in_specs = x_specs + [pl.BlockSpec((1, N_MOD, d), lambda i: (cond(i), 0, 0)),
                          pl.BlockSpec((1, d), lambda i: (0, 0))] + w_specs
    args = list(xs) + [mod, g.reshape(1, d), w_gu, w_down] + mix_args
    in_specs += mix_specs
    if final:
        in_specs.append(pl.BlockSpec((1, d), lambda i: (0, 0)))
        args.append(final_g.reshape(1, d))
        out_shape = [jax.ShapeDtypeStruct((np_rows, d), F32), jax.ShapeDtypeStruct((nt - np_rows, d), F32)]
        out_specs = [pl.BlockSpec((tm, d), ctx_map), pl.BlockSpec((tm, d), lat_map)]
    else:
        out_shape = jax.ShapeDtypeStruct((nt, d), F32)
        out_specs = pl.BlockSpec((tm, d), lambda i: (i, 0))
    return pl.pallas_call(
        kern,
        out_shape=out_shape,
        grid=(nt // tm,),
        in_specs=in_specs,
        out_specs=out_specs,
        scratch_shapes=scratch,
        compiler_params=_cparams(("arbitrary",)),
        name="ffn_half",
    )(*args)


def _proj_kernel(x_ref, mod_ref, g_ref, *refs, i0, parts):
    n_out = len(refs) // 2
    w_refs, o_refs = refs[:n_out], refs[n_out:]
    mod = mod_ref[0]
    rp = x_ref.shape[0] // parts
    for p in range(parts):
        rs = slice(p * rp, (p + 1) * rp)
        h = _modnorm(x_ref[rs, :], g_ref[...], mod[i0:i0 + 1], mod[i0 + 1:i0 + 2])
        for w_ref, o_ref in zip(w_refs, o_refs):
            o_ref[rs, :] = _dot(h, w_ref[...]).astype(o_ref.dtype)


def _mixer_in(x, mod, g, groups, dtypes, rows):
    nt, d = x.shape
    tm = PROJ_ROW_TILE
    assert rows[0] % tm == 0 and rows[1] % tm == 0
    cond = _cond_index(*rows, tm)
    kern = functools.partial(_proj_kernel, i0=3, parts=PROJ_ROW_PARTS)
    w_specs = []
    for w, layer, off, wd in groups:
        assert off % wd == 0 and off + wd <= w.shape[-1]
        if layer is None:
            w_specs.append(_resident((d, wd), functools.partial(lambda b, i: (0, b), off // wd)))
        else:
            w_specs.append(_resident((None, d, wd), functools.partial(lambda a, b, i: (a, 0, b), layer, off // wd)))
    return pl.pallas_call(
        kern,
        out_shape=[jax.ShapeDtypeStruct((nt, grp[3]), dt) for grp, dt in zip(groups, dtypes)],
        grid=(nt // tm,),
        in_specs=[pl.BlockSpec((tm, d), lambda i: (i, 0)),
                  pl.BlockSpec((1, N_MOD, d), lambda i: (cond(i), 0, 0)),
                  pl.BlockSpec((1, d), lambda i: (0, 0))] + w_specs,
        out_specs=[pl.BlockSpec((tm, grp[3]), lambda i: (i, 0)) for grp in groups],
        compiler_params=_cparams(("parallel",)),
        name="mixer_in",
    )(x, mod, g.reshape(1, d), *[grp[0] for grp in groups])


def _conv_kernel(prev_ref, x_ref, next_ref, w_ref, o_ref, *, tiles, dk):
    r = pl.program_id(0)
    c, width = x_ref.shape
    pad = (SHORT_CONV - 1) // 2
    has_prev = jnp.where(tiles.pos(r) > 0, 1.0, 0.0)
    has_next = jnp.where(tiles.pos(r) < tiles.length(r) - 1, 1.0, 0.0)
    ext = c + 2 * SUBLANES
    for hh in range(width // dk):
        part = hh // (width // (3 * dk))
        sl = slice(hh * dk, (hh + 1) * dk)
        xe = jnp.concatenate([prev_ref[:, sl] * has_prev, x_ref[:, sl], next_ref[:, sl] * has_next], axis=0)
        w = w_ref[:, sl]
        acc = None
        for j in range(SHORT_CONV):
            sh = pad - j
            xs = xe if sh == 0 else pltpu.roll(xe, sh % ext, axis=0)
            term = xs[SUBLANES:SUBLANES + c] * w[j:j + 1]
            acc = term if acc is None else acc + term
        y = _silu(acc)
        if part < 2:
            nrm = lax.rsqrt(jnp.sum(y * y, axis=-1, keepdims=True) + EPS)
            y = y * (nrm * (dk ** -0.5) if part == 0 else nrm)
        o_ref[:, sl] = y


def _conv_qkv(qkv, conv_w, tiles):
    nt, width = qkv.shape
    c = tiles.c
    per = c // SUBLANES
    n8 = nt // SUBLANES
    kern = functools.partial(_conv_kernel, tiles=tiles, dk=A_DK)
    return pl.pallas_call(
        kern,
        out_shape=jax.ShapeDtypeStruct((nt, width), F32),
        grid=(tiles.n,),
        in_specs=[pl.BlockSpec((SUBLANES, width), lambda r: (jnp.maximum(r * per - 1, 0), 0)),
                  pl.BlockSpec((c, width), lambda r: (r, 0)),
                  pl.BlockSpec((SUBLANES, width), lambda r: (jnp.minimum((r + 1) * per, n8 - 1), 0)),
                  pl.BlockSpec((SHORT_CONV, width), lambda r: (0, 0))],
        out_specs=pl.BlockSpec((c, width), lambda r: (r, 0)),
        compiler_params=_cparams(("parallel",)),
        name="conv_qkv",
    )(qkv, qkv, qkv, conv_w)


def _load_state(state, s0_ref, tiles, i):
    first = tiles.pos(i) == 0

    @pl.when(first & tiles.is_ctx(i))
    def _():
        state[...] = jnp.zeros(state.shape, F32)

    @pl.when(first & jnp.logical_not(tiles.is_ctx(i)))
    def _():
        state[...] = s0_ref[0]


def _store_state(state, out_ref, tiles, i):
    @pl.when((tiles.pos(i) == tiles.length(i) - 1) & tiles.is_ctx(i))
    def _():
        out_ref[0] = state[...]


def _dir_specs(tiles, c, cols):
    specs = []
    for reverse in (False, True):
        for width, col in cols:
            specs.append(pl.BlockSpec((c, width), functools.partial(
                lambda rev, cc, i: (tiles.row_block(i, rev), cc), reverse, col)))
    return specs


def _pair_dot(a, b):
    n = a.shape[0]
    a_bf, b_bf = _bf(a), _bf(b)
    return jnp.concatenate([_dot(a_bf[:, :n], b_bf[:, :n]), _dot(a_bf[:, n:], b_bf[:, n:])], axis=1)


def _unit_tri_solves(ls, rs, reverse_flags):
    c = ls[0].shape[0]
    n = c // 2
    ri = lax.broadcasted_iota(jnp.int32, (n, c), 0)
    ci = lax.broadcasted_iota(jnp.int32, (n, c), 1) & (n - 1)
    pairs = [jnp.concatenate([l[:n, :n], l[n:, n:]], axis=1) for l in ls]
    shift = SUBLANES.bit_length() - 1
    same = (ri >> shift) == (ci >> shift)
    ms = [jnp.where(same, -lp, 0.0) for lp in pairs]
    ps = [_pair_dot(m, m) for m in ms]
    ns = [m + p + _pair_dot(m, p) for m, p in zip(ms, ps)]
    ps = [_pair_dot(p, p) for p in ps]
    ns = [nv + p + _pair_dot(nv, p) for nv, p in zip(ns, ps)]
    while (1 << shift) < n:
        lvl = ((ri >> (shift + 1)) == (ci >> (shift + 1))) & ((ri >> shift) != (ci >> shift))
        cls = [jnp.where(lvl, lp, 0.0) for lp in pairs]
        ys = [cl + _pair_dot(cl, nv) for cl, nv in zip(cls, ns)]
        ns = [nv - (y + _pair_dot(nv, y)) for nv, y in zip(ns, ys)]
        shift += 1
    firsts, seconds = [], []
    for l, r, nv, rev in zip(ls, rs, ns, reverse_flags):
        if rev:
            firsts.append((r[n:], _bf(nv[:, n:])))
            seconds.append((r[:n], _bf(nv[:, :n]), _bf(l[:n, n:])))
        else:
            firsts.append((r[:n], _bf(nv[:, :n])))
            seconds.append((r[n:], _bf(nv[:, n:]), _bf(l[n:, :n])))
    xas = [ra + _dot(na, _bf(ra)) for ra, na in firsts]
    ts = [rb - _dot(lba, _bf(xa)) for (rb, _, lba), xa in zip(seconds, xas)]
    xbs = [t + _dot(nb, _bf(t)) for (_, nb, _), t in zip(seconds, ts)]
    return [jnp.concatenate([xb, xa] if rev else [xa, xb], axis=0) for xa, xb, rev in zip(xas, xbs, reverse_flags)]


def _delta_kernel(qf_ref, kf_ref, vf_ref, baf_ref, qb_ref, kb_ref, vb_ref, bab_ref, al_ref, dt_ref, s0_ref,
                  of_ref, ob_ref, sout_ref, state, *, tiles, nh, dk, dv):
    step = pl.program_id(0)
    _load_state(state, s0_ref, tiles, step)
    c = qf_ref.shape[0]
    ri = lax.broadcasted_iota(jnp.int32, (c, c), 0)
    ci = lax.broadcasted_iota(jnp.int32, (c, c), 1)
    units = []
    for z, (q_ref, k_ref, v_ref, ba_ref, o_ref) in enumerate(
            ((qf_ref, kf_ref, vf_ref, baf_ref, of_ref), (qb_ref, kb_ref, vb_ref, bab_ref, ob_ref))):
        reverse = z == 1
        incl, strict = _order_masks(ri, ci, reverse)
        ba = ba_ref[...]
        g_col = -jnp.exp(al_ref[...]) * _softplus(ba + dt_ref[...])
        beta_col = _sigmoid(ba)
        gc_col = _cumsum_rows(_bf(incl.astype(F32)), g_col)
        gc_row = gc_col.T
        last = 0 if reverse else c - 1
        for h in range(nh):
            cb = z * nh + h
            cg = 2 * nh + cb
            gcc = gc_col[:, cg:cg + 1]
            gcr = gc_row[cg:cg + 1, :]
            units.append(dict(
                z=z, h=h, o_ref=o_ref, strict=strict, gcc=gcc, gl=gcc[last:last + 1],
                beta=beta_col[:, cb:cb + 1], egc=jnp.exp(gcc),
                decay=jnp.where(incl, jnp.exp(gcc - gcr), 0.0),
                q=q_ref[:, h * dk:(h + 1) * dk], k=k_ref[:, h * dk:(h + 1) * dk], v=v_ref[:, h * dv:(h + 1) * dv]))
    for u in units:
        u["kb"] = u["k"] * u["beta"]
        u["k_bf"] = _bf(u["k"])
    kks = [_dot_nt(_bf(u["kb"]), u["k_bf"]) for u in units]
    qks = [_dot_nt(_bf(u["q"]), u["k_bf"]) * u["decay"] for u in units]
    rs = _unit_tri_solves([jnp.where(u["strict"], kk * u["decay"], 0.0) for u, kk in zip(units, kks)],
                          [jnp.concatenate([u["v"] * u["beta"], u["kb"] * u["egc"]], axis=1) for u in units],
                          [u["z"] == 1 for u in units])
    ss = [state[u["z"], u["h"]] for u in units]
    ss_bf = [_bf(s) for s in ss]
    v_news_bf = [_bf(r[:, :dv] - _dot(_bf(r[:, dv:]), sb)) for r, sb in zip(rs, ss_bf)]
    for u, sb, qk, vnb in zip(units, ss_bf, qks, v_news_bf):
        h = u["h"]
        u["o_ref"][:, h * dv:(h + 1) * dv] = _bf(_dot(_bf(u["q"] * u["egc"]), sb) + _dot(_bf(qk), vnb))
    for u, s, vnb in zip(units, ss, v_news_bf):
        kd = u["k"] * jnp.exp(u["gl"] - u["gcc"])
        state[u["z"], u["h"]] = s * jnp.exp(u["gl"]) + _dot_tn(_bf(kd), vnb)
    _store_state(state, sout_ref, tiles, step)


def _delta_rule(qkv, ba, a_log, dt_bias, s0, tiles):
    nt = qkv.shape[0]
    nh, dk, dv = A_HEADS, A_DK, A_DV
    c = tiles.c
    pad = LANES - 4 * nh
    al = jnp.concatenate([jnp.zeros((2 * nh,), F32), a_log.reshape(-1), jnp.zeros((pad,), F32)])
    dt = jnp.concatenate([jnp.zeros((2 * nh,), F32), dt_bias.reshape(-1), jnp.zeros((pad,), F32)])
    const = lambda i: (0, 0)
    in_specs = (_dir_specs(tiles, c, ((nh * dk, 0), (nh * dk, 1), (nh * dv, 2), (LANES, 0)))
                + [pl.BlockSpec((1, LANES), const), pl.BlockSpec((1, LANES), const),
                   pl.BlockSpec((1, 2, nh, dk, dv), lambda i: (tiles.latent_seq(i), 0, 0, 0, 0))])
    kern = functools.partial(_delta_kernel, tiles=tiles, nh=nh, dk=dk, dv=dv)
    return pl.pallas_call(
        kern,
        out_shape=[jax.ShapeDtypeStruct((nt, nh * dv), BF16), jax.ShapeDtypeStruct((nt, nh * dv), BF16),
                   jax.ShapeDtypeStruct((tiles.n_p, 2, nh, dk, dv), F32)],
        grid=(tiles.n,),
        in_specs=in_specs,
        out_specs=[pl.BlockSpec((c, nh * dv), lambda i: (tiles.row_block(i, False), 0)),
                   pl.BlockSpec((c, nh * dv), lambda i: (tiles.row_block(i, True), 0)),
                   pl.BlockSpec((1, 2, nh, dk, dv), lambda i: (tiles.ctx_seq(i), 0, 0, 0, 0))],
        scratch_shapes=[pltpu.VMEM((2, nh, dk, dv), F32)],
        compiler_params=_cparams(("arbitrary",)),
        name="delta_rule",
    )(qkv, qkv, qkv, ba, qkv, qkv, qkv, ba, al.reshape(1, LANES), dt.reshape(1, LANES), s0)


def _ctx_attn_kernel(sink_ref, q_ref, k_ref, v_ref, o_ref, *, n_kv, grp, dh, t):
    scale = dh ** -0.5
    seqs = [slice(b * t, (b + 1) * t) for b in range(q_ref.shape[0] // t)]
    units = [(rs, h) for rs in seqs for h in range(n_kv * grp)]
    ks = {(rs.start, hk): _bf(k_ref[rs, hk * dh:(hk + 1) * dh]) for rs in seqs for hk in range(n_kv)}
    vs = {(rs.start, hk): _bf(v_ref[rs, hk * dh:(hk + 1) * dh]) for rs in seqs for hk in range(n_kv)}
    sinks = [sink_ref[h] for _, h in units]
    ss = [_dot_nt(_bf(q_ref[rs, h * dh:(h + 1) * dh] * scale), ks[rs.start, h // grp]) for rs, h in units]
    ms = [jnp.maximum(jnp.max(s, axis=-1, keepdims=True), sk) for s, sk in zip(ss, sinks)]
    ps = [jnp.exp(s - m) for s, m in zip(ss, ms)]
    dens = [jnp.sum(p, axis=-1, keepdims=True) + jnp.exp(sk - m) for p, sk, m in zip(ps, sinks, ms)]
    for (rs, h), p, den in zip(units, ps, dens):
        o_ref[rs, h * dh:(h + 1) * dh] = _bf(_dot(_bf(p), vs[rs.start, h // grp]) / den)


def _ctx_attention(qb, kb, vb, sink, n_seq, t):
    wq = qb.shape[1]
    wk = kb.shape[1]
    per = math.gcd(n_seq, CTX_SEQS_PER_STEP)
    kern = functools.partial(_ctx_attn_kernel, n_kv=B_KV_HEADS, grp=B_Q_HEADS // B_KV_HEADS, dh=HEAD_DIM, t=t)
    return pl.pallas_call(
        kern,
        out_shape=jax.ShapeDtypeStruct((n_seq * t, wq), BF16),
        grid=(n_seq // per,),
        in_specs=[pl.BlockSpec(memory_space=pltpu.SMEM),
                  pl.BlockSpec((per * t, wq), lambda b: (b, 0)),
                  pl.BlockSpec((per * t, wk), lambda b: (b, 0)),
                  pl.BlockSpec((per * t, wk), lambda b: (b, 0))],
        out_specs=pl.BlockSpec((per * t, wq), lambda b: (b, 0)),
        compiler_params=_cparams(("parallel",)),
        name="ctx_attention",
    )(sink, qb, kb, vb)


def _rope(x, cos, sin_signed):
    lane = lax.broadcasted_iota(jnp.int32, x.shape, 1)
    quarter = HEAD_DIM // 4
    partner = jnp.where((lane % (2 * quarter)) < quarter,
                        pltpu.roll(x, HEAD_DIM - quarter, axis=1), pltpu.roll(x, quarter, axis=1))
    return x * cos + partner * sin_signed


def _lat_attn_kernel(sink_ref, q_ref, k_ref, v_ref, kc_ref, vc_ref, cq_ref, sq_ref, ck_ref, sk_ref,
                     o_ref, *, n_kv, grp, dh, window):
    scale = dh ** -0.5
    tq = q_ref.shape[0]
    t = k_ref.shape[0]
    span = min(t, tq + 2 * window)
    q0 = pl.program_id(1) * tq
    start = pl.multiple_of(jnp.clip(q0 - window, 0, t - span), math.gcd(tq, window))
    rows = pl.ds(start, span)
    qpos = q0 + lax.broadcasted_iota(jnp.int32, (tq, span), 0)
    kpos = start + lax.broadcasted_iota(jnp.int32, (tq, span), 1)
    valid = jnp.abs(qpos - kpos) <= window
    cq, sq = cq_ref[...], sq_ref[...]
    ck, sk_t = ck_ref[rows, :], sk_ref[rows, :]
    kv = []
    for hk in range(n_kv):
        hs = slice(hk * dh, (hk + 1) * dh)
        kv.append((_bf(_rope(k_ref[rows, hs], ck, sk_t)), _bf(v_ref[rows, hs]), _bf(kc_ref[0, :, hs]),
                   _bf(vc_ref[0, :, hs])))
    heads = range(n_kv * grp)
    sinks = [sink_ref[h] for h in heads]
    qs = [q_ref[:, h * dh:(h + 1) * dh] * scale for h in heads]
    s_locs = [jnp.where(valid, _dot_nt(_bf(_rope(q, cq, sq)), kv[h // grp][0]), -jnp.inf) for h, q in zip(heads, qs)]
    s_ctxs = [_dot_nt(_bf(q), kv[h // grp][2]) for h, q in zip(heads, qs)]
    ms = [jnp.maximum(jnp.maximum(jnp.max(sl, axis=-1, keepdims=True), jnp.max(sc, axis=-1, keepdims=True)), sk)
          for sl, sc, sk in zip(s_locs, s_ctxs, sinks)]
    p_locs = [jnp.exp(sl - m) for sl, m in zip(s_locs, ms)]
    p_ctxs = [jnp.exp(sc - m) for sc, m in zip(s_ctxs, ms)]
    dens = [jnp.sum(pl_, axis=-1, keepdims=True) + jnp.sum(pc, axis=-1, keepdims=True) + jnp.exp(sk - m)
            for pl_, pc, sk, m in zip(p_locs, p_ctxs, sinks, ms)]
    for h, pl_, pc, den in zip(heads, p_locs, p_ctxs, dens):
        o_ref[:, h * dh:(h + 1) * dh] = _bf((_dot(_bf(pl_), kv[h // grp][1]) + _dot(_bf(pc), kv[h // grp][3])) / den)


def _lat_attention(qb, kb, vb, k_ctx, v_ctx, sink, cos, sin_signed, row0, n_seq, t):
    wq = qb.shape[1]
    wk = kb.shape[1]
    tq = ATTN_ROW_TILE
    assert t % tq == 0 and row0 % t == 0
    nq = t // tq
    base_q = row0 // tq
    base_t = row0 // t
    past = k_ctx.shape[1]
    kern = functools.partial(_lat_attn_kernel, n_kv=B_KV_HEADS, grp=B_Q_HEADS // B_KV_HEADS, dh=HEAD_DIM,
                             window=WINDOW)
    return pl.pallas_call(
        kern,
        out_shape=jax.ShapeDtypeStruct((n_seq * t, wq), BF16),
        grid=(n_seq, nq),
        in_specs=[pl.BlockSpec(memory_space=pltpu.SMEM),
                  pl.BlockSpec((tq, wq), lambda b, i: (base_q + b * nq + i, 0)),
                  pl.BlockSpec((t, wk), lambda b, i: (base_t + b, 0)),
                  pl.BlockSpec((t, wk), lambda b, i: (base_t + b, 0)),
                  pl.BlockSpec((1, past, wk), lambda b, i: (b, 0, 0)),
                  pl.BlockSpec((1, past, wk), lambda b, i: (b, 0, 0)),
                  pl.BlockSpec((tq, HEAD_DIM), lambda b, i: (i, 0)),
                  pl.BlockSpec((tq, HEAD_DIM), lambda b, i: (i, 0)),
                  pl.BlockSpec((t, HEAD_DIM), lambda b, i: (0, 0)),
                  pl.BlockSpec((t, HEAD_DIM), lambda b, i: (0, 0))],
        out_specs=pl.BlockSpec((tq, wq), lambda b, i: (b * nq + i, 0)),
        compiler_params=_cparams(("parallel", "parallel")),
        name="latent_attention",
    )(sink, qb, kb, vb, k_ctx, v_ctx, cos, sin_signed, cos, sin_signed)


def _rope_tables(t):
    half = HEAD_DIM // 2
    quarter = half // 2
    pos = jnp.arange(t)
    row = (pos // GRID_W).astype(F32)
    col = (pos % GRID_W).astype(F32)
    inv = ROPE_THETA ** (-jnp.arange(quarter, dtype=F32) / quarter)
    ang_r = row[:, None] * inv[None, :]
    ang_c = col[:, None] * inv[None, :]
    cos = jnp.concatenate([jnp.cos(ang_r), jnp.cos(ang_r), jnp.cos(ang_c), jnp.cos(ang_c)], axis=-1)
    sin = jnp.concatenate([-jnp.sin(ang_r), jnp.sin(ang_r), -jnp.sin(ang_c), jnp.sin(ang_c)], axis=-1)
    return cos, sin


def _gla_kernel(qf_ref, kf_ref, vf_ref, lrf_ref, qb_ref, kb_ref, vb_ref, lrb_ref, wg_ref, bias_ref, s0_ref,
                of_ref, ob_ref, sout_ref, state, *, tiles, nh, dk, dv, rb):
    step = pl.program_id(0)
    _load_state(state, s0_ref, tiles, step)
    c = qf_ref.shape[0]
    ri = lax.broadcasted_iota(jnp.int32, (c, c), 0)
    ci = lax.broadcasted_iota(jnp.int32, (c, c), 1)
    eye = (lax.broadcasted_iota(jnp.int32, (dk, dk), 0) == lax.broadcasted_iota(jnp.int32, (dk, dk), 1))
    units = []
    for z, (q_ref, k_ref, v_ref, lr_ref, o_ref) in enumerate(
            ((qf_ref, kf_ref, vf_ref, lrf_ref, of_ref), (qb_ref, kb_ref, vb_ref, lrb_ref, ob_ref))):
        reverse = z == 1
        incl, _ = _order_masks(ri, ci, reverse)
        x = _dot(_bf(lr_ref[...]), _bf(wg_ref[z])) + bias_ref[z]
        gk = -_softplus(-x) * (1.0 / GATE_NORM)
        gcum = _cumsum_rows(_bf(incl.astype(F32)), gk)
        last = 0 if reverse else c - 1
        for h in range(nh):
            g = gcum[:, h * dk:(h + 1) * dk]
            units.append(dict(
                z=z, h=h, o_ref=o_ref, reverse=reverse, incl=incl, g=g, gl_row=g[last:last + 1],
                q=q_ref[:, h * dk:(h + 1) * dk] * (dk ** -0.5), k=k_ref[:, h * dk:(h + 1) * dk],
                v_bf=v_ref[:, h * dv:(h + 1) * dv], s=state[z, h]))
    intra = [[] for _ in units]
    for blk in range(c // rb):
        r0, r1 = blk * rb, (blk + 1) * rb
        scores, cols = [], []
        for u in units:
            g = u["g"]
            mid = r0 + rb // 2
            if u["reverse"]:
                c0, c1 = r0, c
                ref = g[mid:mid + 1]
            else:
                c0, c1 = 0, r1
                ref = g[mid - 1:mid]
            qe = u["q"][r0:r1] * jnp.exp(g[r0:r1] - ref)
            ke = u["k"][c0:c1] * jnp.exp(ref - g[c0:c1])
            scores.append(jnp.where(u["incl"][r0:r1, c0:c1], _dot_nt(_bf(qe), _bf(ke)), 0.0))
            cols.append((c0, c1))
        for parts, u, a, (c0, c1) in zip(intra, units, scores, cols):
            parts.append(_dot(_bf(a), u["v_bf"][c0:c1]))
    inter = [_dot(_bf(u["q"] * jnp.exp(u["g"])), _bf(u["s"])) for u in units]
    for u, o_inter, parts in zip(units, inter, intra):
        h = u["h"]
        u["o_ref"][:, h * dv:(h + 1) * dv] = _bf(o_inter + jnp.concatenate(parts, axis=0))
    for u in units:
        kd = u["k"] * jnp.exp(u["gl_row"] - u["g"])
        gl_col = jnp.sum(jnp.where(eye, jnp.broadcast_to(u["gl_row"], (dk, dk)), 0.0), axis=1, keepdims=True)
        state[u["z"], u["h"]] = u["s"] * jnp.exp(gl_col) + _dot_tn(_bf(kd), u["v_bf"])
    _store_state(state, sout_ref, tiles, step)


def _gla(q, k, v, lr, w_gate, gate_bias, s0, tiles):
    nt = q.shape[0]
    nh, dk, dv = C_HEADS, C_DK, C_DV
    c = tiles.c
    wg = jnp.zeros((2, LANES, nh * dk), F32)
    for z in range(2):
        wg = wg.at[z, z * GATE_RANK:(z + 1) * GATE_RANK].set(w_gate[z])
    in_specs = (_dir_specs(tiles, c, ((nh * dk, 0), (nh * dk, 0), (nh * dv, 0), (LANES, 0)))
                + [pl.BlockSpec((2, LANES, nh * dk), lambda i: (0, 0, 0)),
                   pl.BlockSpec((2, 1, nh * dk), lambda i: (0, 0, 0)),
                   pl.BlockSpec((1, 2, nh, dk, dv), lambda i: (tiles.latent_seq(i), 0, 0, 0, 0))])
    kern = functools.partial(_gla_kernel, tiles=tiles, nh=nh, dk=dk, dv=dv, rb=GLA_ROW_BLOCK)
    return pl.pallas_call(
        kern,
        out_shape=[jax.ShapeDtypeStruct((nt, nh * dv), BF16), jax.ShapeDtypeStruct((nt, nh * dv), BF16),
                   jax.ShapeDtypeStruct((tiles.n_p, 2, nh, dk, dv), F32)],
        grid=(tiles.n,),
        in_specs=in_specs,
        out_specs=[pl.BlockSpec((c, nh * dv), lambda i: (tiles.row_block(i, False), 0)),
                   pl.BlockSpec((c, nh * dv), lambda i: (tiles.row_block(i, True), 0)),
                   pl.BlockSpec((1, 2, nh, dk, dv), lambda i: (tiles.ctx_seq(i), 0, 0, 0, 0))],
        scratch_shapes=[pltpu.VMEM((2, nh, dk, dv), F32)],
        compiler_params=_cparams(("arbitrary",)),
        name="gla",
    )(q, k, v, lr, q, k, v, lr, wg, gate_bias.reshape(2, 1, nh * dk), s0)


def kernel(x_prompt, x_sample, state_delta, cache_k, cache_v, state_gla, c, c_ctx, norm_g, ada_w, ada_b,
           ffn_w_gu, ffn_w_down, even_w_in, even_conv, even_a_log, even_dt_bias, even_onorm, even_sink,
           even_w_out, odd_w_in, odd_w_gate, odd_gate_bias, odd_onorm, odd_w_out, final_g):
    n_p, t_p, d = x_prompt.shape
    n_s, t_s, _ = x_sample.shape
    depth = norm_g.shape[0]
    np_rows, ns_rows = n_p * t_p, n_s * t_s
    assert np_rows % t_s == 0
    assert t_p % CHUNK == 0 and t_s % CHUNK == 0 and t_s % GRID_W == 0
    rows = (np_rows, t_s)
    tiles = _Tiles(n_p, t_p, n_s, t_s, CHUNK)

    n_cond = 1 + n_s
    cond_rows = -(-n_cond // SUBLANES) * SUBLANES
    conds = jnp.concatenate([c_ctx[None, :], c, jnp.zeros((cond_rows - n_cond, d), F32)], axis=0)
    mods = _ada(conds, ada_w, ada_b)[:, :n_cond].reshape(depth, n_cond, N_MOD, d)


    xs = (x_prompt.reshape(np_rows, d), x_sample.reshape(ns_rows, d))
    new_delta, new_k, new_v, new_gla = [], [], [], []
    for l in range(depth):
        j = l // 2
        mod = mods[l]
        x = _ffn_half(xs, mod, 0, norm_g[l, 0], ffn_w_gu, ffn_w_down, (l, 0), rows)
        if l % 2 == 0:
            nh = A_HEADS
            o_qkv = 2 * nh * A_DK + nh * A_DV
            o_gate = o_qkv + nh * A_DV
            o_qb = o_gate + 4 * nh
            w_q, w_kv = B_Q_HEADS * HEAD_DIM, B_KV_HEADS * HEAD_DIM
            w = even_w_in[j]
            tail = jnp.concatenate([w[:, o_qb:], w[:, o_gate:o_qb], jnp.zeros((d, LANES - 4 * nh), F32)], axis=1)
            assert tail.shape[1] == w_q + 2 * w_kv + LANES and o_qkv % (nh * A_DV) == 0
            groups = ((even_w_in, j, 0, o_qkv), (even_w_in, j, o_qkv, nh * A_DV), (tail, None, 0, w_q),
                      (tail, None, w_q, w_kv), (tail, None, w_q + w_kv, w_kv), (tail, None, w_q + 2 * w_kv, LANES))
            dtypes = (F32, BF16, F32, F32, F32, F32)
            qkv, gate, qb, kb, vb, ba = _mixer_in(x, mod, norm_g[l, 1], groups, dtypes, rows)
            qkv_n = _conv_qkv(qkv, even_conv[j], tiles)
            o_f, o_b, st = _delta_rule(qkv_n, ba, even_a_log[j], even_dt_bias[j], state_delta[:, j], tiles)
            cos, sin_signed = _rope_tables(t_s)
            att_p = _ctx_attention(qb, kb, vb, even_sink[j], n_p, t_p)
            att_s = _lat_attention(qb, kb, vb,
                                   cache_k[:, j].reshape(n_s, -1, B_KV_HEADS * HEAD_DIM),
                                   cache_v[:, j].reshape(n_s, -1, B_KV_HEADS * HEAD_DIM),
                                   even_sink[j], cos, sin_signed, np_rows, n_s, t_s)
            mixer = (o_f, o_b, gate, even_onorm[j], even_w_out[j].astype(BF16), nh, A_DV, (att_p, att_s))
            new_delta.append(st)
            new_k.append(kb[:np_rows].reshape(n_p, t_p, B_KV_HEADS, HEAD_DIM))
            new_v.append(vb[:np_rows].reshape(n_p, t_p, B_KV_HEADS, HEAD_DIM))
        else:
            nh = C_HEADS
            o_v = 2 * nh * C_DK
            o_g = o_v + nh * C_DV
            o_lr = o_g + nh * C_DV
            w_lr = jnp.concatenate([odd_w_in[j][:, o_lr:], jnp.zeros((d, LANES - 2 * GATE_RANK), F32)], axis=1)
            groups = ((odd_w_in, j, 0, nh * C_DK), (odd_w_in, j, nh * C_DK, nh * C_DK), (odd_w_in, j, o_v, nh * C_DV),
                      (odd_w_in, j, o_g, nh * C_DV), (w_lr, None, 0, LANES))
            dtypes = (F32, F32, BF16, BF16, F32)
            q, k, v, g_out, lr = _mixer_in(x, mod, norm_g[l, 1], groups, dtypes, rows)
            o_f, o_b, st = _gla(q, k, v, lr, odd_w_gate[j], odd_gate_bias[j], state_gla[:, j], tiles)
            mixer = (o_f, o_b, g_out, odd_onorm[j], odd_w_out[j].astype(BF16), nh, C_DV, None)
            new_gla.append(st)
        last = l == depth - 1
        xs = _ffn_half((x,), mod, 6, norm_g[l, 2], ffn_w_gu, ffn_w_down, (l, 1), rows, mixer=mixer,
                       final_g=final_g if last else None)
        if not last:
            xs = (xs,)

    y_prompt, y_sample = xs
    return (y_prompt.reshape(n_p, t_p, d), y_sample.reshape(n_s, t_s, d), jnp.stack(new_delta, axis=1),
            jnp.stack(new_k, axis=1), jnp.stack(new_v, axis=1), jnp.stack(new_gla, axis=1))
```

```python
import functools
import math

import jax
import jax.numpy as jnp
from jax import lax
from jax.experimental import pallas as pl
from jax.experimental.pallas import tpu as pltpu

F32 = jnp.float32
BF16 = jnp.bfloat16

EPS = 1e-6
N_MOD = 9
GRID_W = 64
HEAD_DIM = 128
A_HEADS = 4
A_DK = 128
A_DV = 128
SHORT_CONV = 5
B_Q_HEADS = 4
B_KV_HEADS = 2
WINDOW = 128
C_HEADS = 4
C_DK = 128
C_DV = 256
GATE_RANK = 16
GATE_NORM = 16.0
ROPE_THETA = 10000.0

LANES = 128
SUBLANES = 8
PROJ_ROW_TILE = 1024
PROJ_ROW_PARTS = 2
FFN_ROW_TILE = 512
FFN_CHUNKS = 11
CHUNK = 256
CTX_SEQS_PER_STEP = 4
ATTN_ROW_TILE = 256
GLA_ROW_BLOCK = 128
VMEM_LIMIT = 56 * 1024 * 1024


def _cparams(sem, vmem=VMEM_LIMIT):
    return pltpu.CompilerParams(dimension_semantics=sem, vmem_limit_bytes=vmem)


def _resident(block_shape, index_map):
    return pl.BlockSpec(block_shape, index_map, pipeline_mode=pl.Buffered(1))


def _dot(a, b):
    return jnp.dot(a, b, preferred_element_type=F32)


def _dot_nt(a, b):
    return lax.dot_general(a, b, (((1,), (1,)), ((), ())), preferred_element_type=F32)


def _dot_tn(a, b):
    return lax.dot_general(a, b, (((0,), (0,)), ((), ())), preferred_element_type=F32)


def _bf(x):
    return x.astype(BF16)


def _sigmoid(x):
    return 0.5 * jnp.tanh(0.5 * x) + 0.5


def _silu(x):
    return x * _sigmoid(x)


def _softplus(x):
    return jnp.maximum(x, 0.0) + jnp.log(1.0 + jnp.exp(-jnp.abs(x)))


def _rms(x):
    return x * lax.rsqrt(jnp.mean(x * x, axis=-1, keepdims=True) + EPS)


def _modnorm(x, g, shift, scale):
    return (_rms(x) * g) * (1.0 + scale) + shift


def _cumsum_rows(tri_bf, x):
    hi = _bf(x)
    lo = _bf(x - hi.astype(F32))
    return _dot(tri_bf, hi) + _dot(tri_bf, lo)


def _cumsum_rows_halves(tri_bf, x, reverse):
    n = x.shape[0] // 2
    top = _cumsum_rows(tri_bf[:n, :n], x[:n])
    bot = _cumsum_rows(tri_bf[n:, n:], x[n:])
    if reverse:
        top = top + jnp.sum(x[n:], axis=0, keepdims=True)
    else:
        bot = bot + jnp.sum(x[:n], axis=0, keepdims=True)
    return jnp.concatenate([top, bot], axis=0)


def _order_masks(ri, ci, reverse):
    if reverse:
        return ri <= ci, ri < ci
    return ri >= ci, ri > ci


class _Tiles:
    def __init__(self, n_p, t_p, n_s, t_s, c):
        self.n_p, self.n_s, self.c = n_p, n_s, c
        self.per_p, self.per_s = t_p // c, t_s // c
        self.np_tiles = n_p * self.per_p
        self.n = self.np_tiles + n_s * self.per_s

    def is_ctx(self, i):
        return i < self.np_tiles

    def seq(self, i):
        return jnp.where(i < self.np_tiles, i // self.per_p, self.n_p + (i - self.np_tiles) // self.per_s)

    def pos(self, i):
        return jnp.where(i < self.np_tiles, i % self.per_p, (i - self.np_tiles) % self.per_s)

    def length(self, i):
        return jnp.where(i < self.np_tiles, self.per_p, self.per_s)

    def row_block(self, i, reverse):
        return i + self.length(i) - 1 - 2 * self.pos(i) if reverse else i

    def ctx_seq(self, i):
        return jnp.minimum(self.seq(i), self.n_p - 1)

    def latent_seq(self, i):
        return jnp.maximum(self.seq(i) - self.n_p, 0)


def _ada_kernel(c_ref, w_ref, b_ref, o_ref):
    s = _bf(_silu(c_ref[...]))
    o_ref[0] = _dot(s, _bf(w_ref[0])) + b_ref[0]


def _ada(cond, ada_w, ada_b):
    depth, d, n = ada_w.shape
    rows = cond.shape[0]
    tn = n // 4
    return pl.pallas_call(
        _ada_kernel,
        out_shape=jax.ShapeDtypeStruct((depth, rows, n), F32),
        grid=(depth, n // tn),
        in_specs=[pl.BlockSpec((rows, d), lambda l, j: (0, 0)),
                  pl.BlockSpec((1, d, tn), lambda l, j: (l, 0, j)),
                  pl.BlockSpec((1, 1, tn), lambda l, j: (l, 0, j))],
        out_specs=pl.BlockSpec((1, rows, tn), lambda l, j: (l, 0, j)),
        compiler_params=_cparams(("parallel", "parallel")),
        name="ada",
    )(cond, ada_w, ada_b.reshape(depth, 1, n))


def _cond_index(n_prompt_rows, dec_seq, tm):
    npt = n_prompt_rows // tm

    def cond(i):
        return jnp.where(i < npt, 0, 1 + ((i - npt) * tm) // dec_seq)

    return cond


def _mixer_residual(x, gate_mod, of_ref, ob_ref, gate_ref, on_ref, w_ref, extra, nh, dv):
    od = of_ref[...].astype(F32) + ob_ref[...].astype(F32)
    gate = gate_ref[...].astype(F32)
    y = None if extra is None else _dot(extra, w_ref[nh * dv:, :])
    per = max(1, (2 * LANES) // dv)
    for h0 in range(0, nh, per):
        mix = [_bf((_rms(od[:, h * dv:(h + 1) * dv]) * on_ref[...]) * _silu(gate[:, h * dv:(h + 1) * dv]))
               for h in range(h0, min(h0 + per, nh))]
        part = _dot(jnp.concatenate(mix, axis=1), w_ref[h0 * dv:min(h0 + per, nh) * dv, :])
        y = part if y is None else y + part
    return x + gate_mod * y


def _ffn_kernel(*refs, i0, d_ff, n_chunks, npt, split_in, mixer, final):
    refs = list(refs)
    x_refs = [refs.pop(0) for _ in range(2 if split_in else 1)]
    mod_ref, g_ref, wgu_ref, wd_ref = (refs.pop(0) for _ in range(4))
    if mixer is not None:
        nh, dv, has_extra = mixer
        of_ref, ob_ref, gate_ref, on_ref, wout_ref = (refs.pop(0) for _ in range(5))
        extra_refs = [refs.pop(0) for _ in range(2 if has_extra else 0)]
    rest = refs
    i = pl.program_id(0)
    mod = mod_ref[0]
    ch = d_ff // n_chunks
    x = jnp.where(i < npt, x_refs[0][...], x_refs[1][...]) if split_in else x_refs[0][...]
    if mixer is not None:
        extra = jnp.where(i < npt, extra_refs[0][...], extra_refs[1][...]) if has_extra else None
        x = _mixer_residual(x, mod[i0 - 1:i0], of_ref, ob_ref, gate_ref, on_ref, wout_ref, extra, nh, dv)
    h = _modnorm(x, g_ref[...], mod[i0:i0 + 1], mod[i0 + 1:i0 + 2]).astype(wgu_ref.dtype)
    acts = []
    for c in range(n_chunks):
        gt = _dot(h, wgu_ref[:, c * ch:(c + 1) * ch])
        up = _dot(h, wgu_ref[:, d_ff + c * ch:d_ff + (c + 1) * ch])
        acts.append((_silu(gt) * up).astype(wd_ref.dtype))
    out = x + (0.5 * mod[i0 + 2:i0 + 3]) * _dot(jnp.concatenate(acts, axis=1), wd_ref[...])
    if final:
        fg_ref, yp_ref, ys_ref = rest
        out = _rms(out) * fg_ref[...]

        @pl.when(i < npt)
        def _():
            yp_ref[...] = out

        @pl.when(i >= npt)
        def _():
            ys_ref[...] = out
    else:
        rest[0][...] = out


def _ffn_half(xs, mod, i0, g, w_gu, w_down, widx, rows, mixer=None, final_g=None):
    split_in = len(xs) == 2
    d = xs[0].shape[1]
    nt = sum(x.shape[0] for x in xs)
    d_ff = w_down.shape[-2]
    tm = FFN_ROW_TILE
    np_rows = rows[0]
    assert np_rows % tm == 0 and rows[1] % tm == 0 and d_ff % (FFN_CHUNKS * LANES) == 0
    cond = _cond_index(*rows, tm)
    npt = np_rows // tm
    final = final_g is not None
    ctx_map = lambda i: (jnp.minimum(i, npt - 1), 0)
    lat_map = lambda i: (jnp.maximum(i - npt, 0), 0)
    mix_specs, mix_args, mix_cfg = [], [], None
    if mixer is not None:
        o_f, o_b, gate, onorm, w_out, nh, dv, extra = mixer
        assert i0 >= 1
        wa = nh * dv
        row = lambda i: (i, 0)
        mix_specs = [pl.BlockSpec((tm, wa), row), pl.BlockSpec((tm, wa), row), pl.BlockSpec((tm, wa), row),
                     pl.BlockSpec((1, dv), lambda i: (0, 0)), _resident(w_out.shape, lambda i: (0, 0))]
        mix_args = [o_f, o_b, gate, onorm.reshape(1, dv), w_out]
        if extra is not None:
            we = extra[0].shape[1]
            mix_specs += [pl.BlockSpec((tm, we), ctx_map), pl.BlockSpec((tm, we), lat_map)]
            mix_args += list(extra)
        mix_cfg = (nh, dv, extra is not None)
    kern = functools.partial(_ffn_kernel, i0=i0, d_ff=d_ff, n_chunks=FFN_CHUNKS, npt=npt, split_in=split_in,
                             mixer=mix_cfg, final=final)
    if split_in:
        x_specs = [pl.BlockSpec((tm, d), ctx_map), pl.BlockSpec((tm, d), lat_map)]
    else:
        x_specs = [pl.BlockSpec((tm, d), lambda i: (i, 0))]
    in_specs = x_specs + [pl.BlockSpec((1, N_MOD, d), lambda i: (cond(i), 0, 0)),
                          pl.BlockSpec((1, d), lambda i: (0, 0)),
                          _resident((None, None, d, 2 * d_ff), lambda i: widx + (0, 0)),
                          _resident((None, None, d_ff, d), lambda i: widx + (0, 0))]
    args = list(xs) + [mod, g.reshape(1, d), w_gu, w_down] + mix_args
    in_specs += mix_specs
    if final:
        in_specs.append(pl.BlockSpec((1, d), lambda i: (0, 0)))
        args.append(final_g.reshape(1, d))
        out_shape = [jax.ShapeDtypeStruct((np_rows, d), F32), jax.ShapeDtypeStruct((nt - np_rows, d), F32)]
        out_specs = [pl.BlockSpec((tm, d), ctx_map), pl.BlockSpec((tm, d), lat_map)]
    else:
        out_shape = jax.ShapeDtypeStruct((nt, d), F32)
        out_specs = pl.BlockSpec((tm, d), lambda i: (i, 0))
    return pl.pallas_call(
        kern,
        out_shape=out_shape,
        grid=(nt // tm,),
        in_specs=in_specs,
        out_specs=out_specs,
        compiler_params=_cparams(("arbitrary",)),
        name="ffn_half",
    )(*args)


def _proj_kernel(x_ref, mod_ref, g_ref, *refs, i0, parts):
    n_out = len(refs) // 2
    w_refs, o_refs = refs[:n_out], refs[n_out:]
    mod = mod_ref[0]
    rp = x_ref.shape[0] // parts
    for p in range(parts):
        rs = slice(p * rp, (p + 1) * rp)
        h = _modnorm(x_ref[rs, :], g_ref[...], mod[i0:i0 + 1], mod[i0 + 1:i0 + 2])
        for w_ref, o_ref in zip(w_refs, o_refs):
            o_ref[rs, :] = _dot(h, w_ref[...]).astype(o_ref.dtype)


def _mixer_in(x, mod, g, groups, dtypes, rows):
    nt, d = x.shape
    tm = PROJ_ROW_TILE
    assert rows[0] % tm == 0 and rows[1] % tm == 0
    cond = _cond_index(*rows, tm)
    kern = functools.partial(_proj_kernel, i0=3, parts=PROJ_ROW_PARTS)
    w_specs = []
    for w, layer, off, wd in groups:
        assert off % wd == 0 and off + wd <= w.shape[-1]
        if layer is None:
            w_specs.append(_resident((d, wd), functools.partial(lambda b, i: (0, b), off // wd)))
        else:
            w_specs.append(_resident((None, d, wd), functools.partial(lambda a, b, i: (a, 0, b), layer, off // wd)))
    return pl.pallas_call(
        kern,
        out_shape=[jax.ShapeDtypeStruct((nt, grp[3]), dt) for grp, dt in zip(groups, dtypes)],
        grid=(nt // tm,),
        in_specs=[pl.BlockSpec((tm, d), lambda i: (i, 0)),
                  pl.BlockSpec((1, N_MOD, d), lambda i: (cond(i), 0, 0)),
                  pl.BlockSpec((1, d), lambda i: (0, 0))] + w_specs,
        out_specs=[pl.BlockSpec((tm, grp[3]), lambda i: (i, 0)) for grp in groups],
        compiler_params=_cparams(("parallel",)),
        name="mixer_in",
    )(x, mod, g.reshape(1, d), *[grp[0] for grp in groups])


def _conv_kernel(prev_ref, x_ref, next_ref, w_ref, o_ref, *, tiles, dk):
    r = pl.program_id(0)
    c, width = x_ref.shape
    pad = (SHORT_CONV - 1) // 2
    has_prev = jnp.where(tiles.pos(r) > 0, 1.0, 0.0)
    has_next = jnp.where(tiles.pos(r) < tiles.length(r) - 1, 1.0, 0.0)
    ext = c + 2 * SUBLANES
    for hh in range(width // dk):
        part = hh // (width // (3 * dk))
        sl = slice(hh * dk, (hh + 1) * dk)
        xe = jnp.concatenate([prev_ref[:, sl] * has_prev, x_ref[:, sl], next_ref[:, sl] * has_next], axis=0)
        w = w_ref[:, sl]
        acc = None
        for j in range(SHORT_CONV):
            sh = pad - j
            xs = xe if sh == 0 else pltpu.roll(xe, sh % ext, axis=0)
            term = xs[SUBLANES:SUBLANES + c] * w[j:j + 1]
            acc = term if acc is None else acc + term
        y = _silu(acc)
        if part < 2:
            nrm = lax.rsqrt(jnp.sum(y * y, axis=-1, keepdims=True) + EPS)
            y = y * (nrm * (dk ** -0.5) if part == 0 else nrm)
        o_ref[:, sl] = y


def _conv_qkv(qkv, conv_w, tiles):
    nt, width = qkv.shape
    c = tiles.c
    per = c // SUBLANES
    n8 = nt // SUBLANES
    kern = functools.partial(_conv_kernel, tiles=tiles, dk=A_DK)
    return pl.pallas_call(
        kern,
        out_shape=jax.ShapeDtypeStruct((nt, width), F32),
        grid=(tiles.n,),
        in_specs=[pl.BlockSpec((SUBLANES, width), lambda r: (jnp.maximum(r * per - 1, 0), 0)),
                  pl.BlockSpec((c, width), lambda r: (r, 0)),
                  pl.BlockSpec((SUBLANES, width), lambda r: (jnp.minimum((r + 1) * per, n8 - 1), 0)),
                  pl.BlockSpec((SHORT_CONV, width), lambda r: (0, 0))],
        out_specs=pl.BlockSpec((c, width), lambda r: (r, 0)),
        compiler_params=_cparams(("parallel",)),
        name="conv_qkv",
    )(qkv, qkv, qkv, conv_w)


def _load_state(state, s0_ref, tiles, i):
    first = tiles.pos(i) == 0

    @pl.when(first & tiles.is_ctx(i))
    def _():
        state[...] = jnp.zeros(state.shape, F32)

    @pl.when(first & jnp.logical_not(tiles.is_ctx(i)))
    def _():
        state[...] = s0_ref[0]


def _store_state(state, out_ref, tiles, i):
    @pl.when((tiles.pos(i) == tiles.length(i) - 1) & tiles.is_ctx(i))
    def _():
        out_ref[0] = state[...]


def _dir_specs(tiles, c, cols):
    specs = []
    for reverse in (False, True):
        for width, col in cols:
            specs.append(pl.BlockSpec((c, width), functools.partial(
                lambda rev, cc, i: (tiles.row_block(i, rev), cc), reverse, col)))
    return specs


def _pair_dot(a, b):
    n = a.shape[0]
    a_bf, b_bf = _bf(a), _bf(b)
    return jnp.concatenate([_dot(a_bf[:, :n], b_bf[:, :n]), _dot(a_bf[:, n:], b_bf[:, n:])], axis=1)


def _unit_tri_solves(ls, rs, reverse_flags):
    c = ls[0].shape[0]
    n = c // 2
    ri = lax.broadcasted_iota(jnp.int32, (n, c), 0)
    ci = lax.broadcasted_iota(jnp.int32, (n, c), 1) & (n - 1)
    pairs = [jnp.concatenate([l[:n, :n], l[n:, n:]], axis=1) for l in ls]
    shift = SUBLANES.bit_length() - 1
    same = (ri >> shift) == (ci >> shift)
    ms = [jnp.where(same, -lp, 0.0) for lp in pairs]
    ps = [_pair_dot(m, m) for m in ms]
    ns = [m + p + _pair_dot(m, p) for m, p in zip(ms, ps)]
    ps = [_pair_dot(p, p) for p in ps]
    ns = [nv + p + _pair_dot(nv, p) for nv, p in zip(ns, ps)]
    while (1 << shift) < n:
        lvl = ((ri >> (shift + 1)) == (ci >> (shift + 1))) & ((ri >> shift) != (ci >> shift))
        cls = [jnp.where(lvl, lp, 0.0) for lp in pairs]
        ys = [cl + _pair_dot(cl, nv) for cl, nv in zip(cls, ns)]
        ns = [nv - (y + _pair_dot(nv, y)) for nv, y in zip(ns, ys)]
        shift += 1
    firsts, seconds = [], []
    for l, r, nv, rev in zip(ls, rs, ns, reverse_flags):
        if rev:
            firsts.append((r[n:], _bf(nv[:, n:])))
            seconds.append((r[:n], _bf(nv[:, :n]), _bf(l[:n, n:])))
        else:
            firsts.append((r[:n], _bf(nv[:, :n])))
            seconds.append((r[n:], _bf(nv[:, n:]), _bf(l[n:, :n])))
    xas = [ra + _dot(na, _bf(ra)) for ra, na in firsts]
    ts = [rb - _dot(lba, _bf(xa)) for (rb, _, lba), xa in zip(seconds, xas)]
    xbs = [t + _dot(nb, _bf(t)) for (_, nb, _), t in zip(seconds, ts)]
    return [jnp.concatenate([xb, xa] if rev else [xa, xb], axis=0) for xa, xb, rev in zip(xas, xbs, reverse_flags)]


def _delta_kernel(qf_ref, kf_ref, vf_ref, baf_ref, qb_ref, kb_ref, vb_ref, bab_ref, al_ref, dt_ref, s0_ref,
                  of_ref, ob_ref, sout_ref, state, *, tiles, nh, dk, dv):
    step = pl.program_id(0)
    _load_state(state, s0_ref, tiles, step)
    c = qf_ref.shape[0]
    ri = lax.broadcasted_iota(jnp.int32, (c, c), 0)
    ci = lax.broadcasted_iota(jnp.int32, (c, c), 1)
    units = []
    for z, (q_ref, k_ref, v_ref, ba_ref, o_ref) in enumerate(
            ((qf_ref, kf_ref, vf_ref, baf_ref, of_ref), (qb_ref, kb_ref, vb_ref, bab_ref, ob_ref))):
        reverse = z == 1
        incl, strict = _order_masks(ri, ci, reverse)
        ba = ba_ref[...]
        g_col = -jnp.exp(al_ref[...]) * _softplus(ba + dt_ref[...])
        beta_col = _sigmoid(ba)
        gc_col = _cumsum_rows(_bf(incl.astype(F32)), g_col)
        gc_row = gc_col.T
        last = 0 if reverse else c - 1
        for h in range(nh):
            cb = z * nh + h
            cg = 2 * nh + cb
            gcc = gc_col[:, cg:cg + 1]
            gcr = gc_row[cg:cg + 1, :]
            units.append(dict(
                z=z, h=h, o_ref=o_ref, strict=strict, gcc=gcc, gl=gcc[last:last + 1],
                beta=beta_col[:, cb:cb + 1], egc=jnp.exp(gcc),
                decay=jnp.where(incl, jnp.exp(gcc - gcr), 0.0),
                q=q_ref[:, h * dk:(h + 1) * dk], k=k_ref[:, h * dk:(h + 1) * dk], v=v_ref[:, h * dv:(h + 1) * dv]))
    for u in units:
        u["kb"] = u["k"] * u["beta"]
        u["k_bf"] = _bf(u["k"])
    kks = [_dot_nt(_bf(u["kb"]), u["k_bf"]) for u in units]
    qks = [_dot_nt(_bf(u["q"]), u["k_bf"]) * u["decay"] for u in units]
    rs = _unit_tri_solves([jnp.where(u["strict"], kk * u["decay"], 0.0) for u, kk in zip(units, kks)],
                          [jnp.concatenate([u["v"] * u["beta"], u["kb"] * u["egc"]], axis=1) for u in units],
                          [u["z"] == 1 for u in units])
    ss = [state[u["z"], u["h"]] for u in units]
    ss_bf = [_bf(s) for s in ss]
    v_news_bf = [_bf(r[:, :dv] - _dot(_bf(r[:, dv:]), sb)) for r, sb in zip(rs, ss_bf)]
    for u, sb, qk, vnb in zip(units, ss_bf, qks, v_news_bf):
        h = u["h"]
        u["o_ref"][:, h * dv:(h + 1) * dv] = _bf(_dot(_bf(u["q"] * u["egc"]), sb) + _dot(_bf(qk), vnb))
    for u, s, vnb in zip(units, ss, v_news_bf):
        kd = u["k"] * jnp.exp(u["gl"] - u["gcc"])
        state[u["z"], u["h"]] = s * jnp.exp(u["gl"]) + _dot_tn(_bf(kd), vnb)
    _store_state(state, sout_ref, tiles, step)


def _delta_rule(qkv, ba, a_log, dt_bias, s0, tiles):
    nt = qkv.shape[0]
    nh, dk, dv = A_HEADS, A_DK, A_DV
    c = tiles.c
    pad = LANES - 4 * nh
    al = jnp.concatenate([jnp.zeros((2 * nh,), F32), a_log.reshape(-1), jnp.zeros((pad,), F32)])
    dt = jnp.concatenate([jnp.zeros((2 * nh,), F32), dt_bias.reshape(-1), jnp.zeros((pad,), F32)])
    const = lambda i: (0, 0)
    in_specs = (_dir_specs(tiles, c, ((nh * dk, 0), (nh * dk, 1), (nh * dv, 2), (LANES, 0)))
                + [pl.BlockSpec((1, LANES), const), pl.BlockSpec((1, LANES), const),
                   pl.BlockSpec((1, 2, nh, dk, dv), lambda i: (tiles.latent_seq(i), 0, 0, 0, 0))])
    kern = functools.partial(_delta_kernel, tiles=tiles, nh=nh, dk=dk, dv=dv)
    return pl.pallas_call(
        kern,
        out_shape=[jax.ShapeDtypeStruct((nt, nh * dv), BF16), jax.ShapeDtypeStruct((nt, nh * dv), BF16),
                   jax.ShapeDtypeStruct((tiles.n_p, 2, nh, dk, dv), F32)],
        grid=(tiles.n,),
        in_specs=in_specs,
        out_specs=[pl.BlockSpec((c, nh * dv), lambda i: (tiles.row_block(i, False), 0)),
                   pl.BlockSpec((c, nh * dv), lambda i: (tiles.row_block(i, True), 0)),
                   pl.BlockSpec((1, 2, nh, dk, dv), lambda i: (tiles.ctx_seq(i), 0, 0, 0, 0))],
        scratch_shapes=[pltpu.VMEM((2, nh, dk, dv), F32)],
        compiler_params=_cparams(("arbitrary",)),
        name="delta_rule",
    )(qkv, qkv, qkv, ba, qkv, qkv, qkv, ba, al.reshape(1, LANES), dt.reshape(1, LANES), s0)


def _ctx_attn_kernel(sink_ref, q_ref, k_ref, v_ref, o_ref, *, n_kv, grp, dh, t):
    scale = dh ** -0.5
    seqs = [slice(b * t, (b + 1) * t) for b in range(q_ref.shape[0] // t)]
    units = [(rs, h) for rs in seqs for h in range(n_kv * grp)]
    ks = {(rs.start, hk): _bf(k_ref[rs, hk * dh:(hk + 1) * dh]) for rs in seqs for hk in range(n_kv)}
    vs = {(rs.start, hk): _bf(v_ref[rs, hk * dh:(hk + 1) * dh]) for rs in seqs for hk in range(n_kv)}
    sinks = [sink_ref[h] for _, h in units]
    ss = [_dot_nt(_bf(q_ref[rs, h * dh:(h + 1) * dh] * scale), ks[rs.start, h // grp]) for rs, h in units]
    ms = [jnp.maximum(jnp.max(s, axis=-1, keepdims=True), sk) for s, sk in zip(ss, sinks)]
    ps = [jnp.exp(s - m) for s, m in zip(ss, ms)]
    dens = [jnp.sum(p, axis=-1, keepdims=True) + jnp.exp(sk - m) for p, sk, m in zip(ps, sinks, ms)]
    for (rs, h), p, den in zip(units, ps, dens):
        o_ref[rs, h * dh:(h + 1) * dh] = _bf(_dot(_bf(p), vs[rs.start, h // grp]) / den)


def _ctx_attention(qb, kb, vb, sink, n_seq, t):
    wq = qb.shape[1]
    wk = kb.shape[1]
    per = math.gcd(n_seq, CTX_SEQS_PER_STEP)
    kern = functools.partial(_ctx_attn_kernel, n_kv=B_KV_HEADS, grp=B_Q_HEADS // B_KV_HEADS, dh=HEAD_DIM, t=t)
    return pl.pallas_call(
        kern,
        out_shape=jax.ShapeDtypeStruct((n_seq * t, wq), BF16),
        grid=(n_seq // per,),
        in_specs=[pl.BlockSpec(memory_space=pltpu.SMEM),
                  pl.BlockSpec((per * t, wq), lambda b: (b, 0)),
                  pl.BlockSpec((per * t, wk), lambda b: (b, 0)),
                  pl.BlockSpec((per * t, wk), lambda b: (b, 0))],
        out_specs=pl.BlockSpec((per * t, wq), lambda b: (b, 0)),
        compiler_params=_cparams(("parallel",)),
        name="ctx_attention",
    )(sink, qb, kb, vb)


def _rope(x, cos, sin_signed):
    lane = lax.broadcasted_iota(jnp.int32, x.shape, 1)
    quarter = HEAD_DIM // 4
    partner = jnp.where((lane % (2 * quarter)) < quarter,
                        pltpu.roll(x, HEAD_DIM - quarter, axis=1), pltpu.roll(x, quarter, axis=1))
    return x * cos + partner * sin_signed


def _lat_attn_kernel(sink_ref, q_ref, k_ref, v_ref, kc_ref, vc_ref, cq_ref, sq_ref, ck_ref, sk_ref,
                     o_ref, *, n_kv, grp, dh, window):
    scale = dh ** -0.5
    tq = q_ref.shape[0]
    t = k_ref.shape[0]
    span = min(t, tq + 2 * window)
    q0 = pl.program_id(1) * tq
    start = pl.multiple_of(jnp.clip(q0 - window, 0, t - span), math.gcd(tq, window))
    rows = pl.ds(start, span)
    qpos = q0 + lax.broadcasted_iota(jnp.int32, (tq, span), 0)
    kpos = start + lax.broadcasted_iota(jnp.int32, (tq, span), 1)
    valid = jnp.abs(qpos - kpos) <= window
    cq, sq = cq_ref[...], sq_ref[...]
    ck, sk_t = ck_ref[rows, :], sk_ref[rows, :]
    kv = []
    for hk in range(n_kv):
        hs = slice(hk * dh, (hk + 1) * dh)
        kv.append((_bf(_rope(k_ref[rows, hs], ck, sk_t)), _bf(v_ref[rows, hs]), _bf(kc_ref[0, :, hs]),
                   _bf(vc_ref[0, :, hs])))
    heads = range(n_kv * grp)
    sinks = [sink_ref[h] for h in heads]
    qs = [q_ref[:, h * dh:(h + 1) * dh] * scale for h in heads]
    s_locs = [jnp.where(valid, _dot_nt(_bf(_rope(q, cq, sq)), kv[h // grp][0]), -jnp.inf) for h, q in zip(heads, qs)]
    s_ctxs = [_dot_nt(_bf(q), kv[h // grp][2]) for h, q in zip(heads, qs)]
    ms = [jnp.maximum(jnp.maximum(jnp.max(sl, axis=-1, keepdims=True), jnp.max(sc, axis=-1, keepdims=True)), sk)
          for sl, sc, sk in zip(s_locs, s_ctxs, sinks)]
    p_locs = [jnp.exp(sl - m) for sl, m in zip(s_locs, ms)]
    p_ctxs = [jnp.exp(sc - m) for sc, m in zip(s_ctxs, ms)]
    dens = [jnp.sum(pl_, axis=-1, keepdims=True) + jnp.sum(pc, axis=-1, keepdims=True) + jnp.exp(sk - m)
            for pl_, pc, sk, m in zip(p_locs, p_ctxs, sinks, ms)]
    for h, pl_, pc, den in zip(heads, p_locs, p_ctxs, dens):
        o_ref[:, h * dh:(h + 1) * dh] = _bf((_dot(_bf(pl_), kv[h // grp][1]) + _dot(_bf(pc), kv[h // grp][3])) / den)


def _lat_attention(qb, kb, vb, k_ctx, v_ctx, sink, cos, sin_signed, row0, n_seq, t):
    wq = qb.shape[1]
    wk = kb.shape[1]
    tq = ATTN_ROW_TILE
    assert t % tq == 0 and row0 % t == 0
    nq = t // tq
    base_q = row0 // tq
    base_t = row0 // t
    past = k_ctx.shape[1]
    kern = functools.partial(_lat_attn_kernel, n_kv=B_KV_HEADS, grp=B_Q_HEADS // B_KV_HEADS, dh=HEAD_DIM,
                             window=WINDOW)
    return pl.pallas_call(
        kern,
        out_shape=jax.ShapeDtypeStruct((n_seq * t, wq), BF16),
        grid=(n_seq, nq),
        in_specs=[pl.BlockSpec(memory_space=pltpu.SMEM),
                  pl.BlockSpec((tq, wq), lambda b, i: (base_q + b * nq + i, 0)),
                  pl.BlockSpec((t, wk), lambda b, i: (base_t + b, 0)),
                  pl.BlockSpec((t, wk), lambda b, i: (base_t + b, 0)),
                  pl.BlockSpec((1, past, wk), lambda b, i: (b, 0, 0)),
                  pl.BlockSpec((1, past, wk), lambda b, i: (b, 0, 0)),
                  pl.BlockSpec((tq, HEAD_DIM), lambda b, i: (i, 0)),
                  pl.BlockSpec((tq, HEAD_DIM), lambda b, i: (i, 0)),
                  pl.BlockSpec((t, HEAD_DIM), lambda b, i: (0, 0)),
                  pl.BlockSpec((t, HEAD_DIM), lambda b, i: (0, 0))],
        out_specs=pl.BlockSpec((tq, wq), lambda b, i: (b * nq + i, 0)),
        compiler_params=_cparams(("parallel", "parallel")),
        name="latent_attention",
    )(sink, qb, kb, vb, k_ctx, v_ctx, cos, sin_signed, cos, sin_signed)


def _rope_tables(t):
    half = HEAD_DIM // 2
    quarter = half // 2
    pos = jnp.arange(t)
    row = (pos // GRID_W).astype(F32)
    col = (pos % GRID_W).astype(F32)
    inv = ROPE_THETA ** (-jnp.arange(quarter, dtype=F32) / quarter)
    ang_r = row[:, None] * inv[None, :]
    ang_c = col[:, None] * inv[None, :]
    cos = jnp.concatenate([jnp.cos(ang_r), jnp.cos(ang_r), jnp.cos(ang_c), jnp.cos(ang_c)], axis=-1)
    sin = jnp.concatenate([-jnp.sin(ang_r), jnp.sin(ang_r), -jnp.sin(ang_c), jnp.sin(ang_c)], axis=-1)
    return cos, sin


def _gla_kernel(qf_ref, kf_ref, vf_ref, lrf_ref, qb_ref, kb_ref, vb_ref, lrb_ref, wg_ref, bias_ref, s0_ref,
                of_ref, ob_ref, sout_ref, state, *, tiles, nh, dk, dv, rb):
    step = pl.program_id(0)
    _load_state(state, s0_ref, tiles, step)
    c = qf_ref.shape[0]
    ri = lax.broadcasted_iota(jnp.int32, (c, c), 0)
    ci = lax.broadcasted_iota(jnp.int32, (c, c), 1)
    eye = (lax.broadcasted_iota(jnp.int32, (dk, dk), 0) == lax.broadcasted_iota(jnp.int32, (dk, dk), 1))
    units = []
    for z, (q_ref, k_ref, v_ref, lr_ref, o_ref) in enumerate(
            ((qf_ref, kf_ref, vf_ref, lrf_ref, of_ref), (qb_ref, kb_ref, vb_ref, lrb_ref, ob_ref))):
        reverse = z == 1
        incl, _ = _order_masks(ri, ci, reverse)
        x = _dot(_bf(lr_ref[...]), _bf(wg_ref[z])) + bias_ref[z]
        gk = -_softplus(-x) * (1.0 / GATE_NORM)
        gcum = _cumsum_rows_halves(_bf(incl.astype(F32)), gk, reverse)
        last = 0 if reverse else c - 1
        for h in range(nh):
            g = gcum[:, h * dk:(h + 1) * dk]
            units.append(dict(
                z=z, h=h, o_ref=o_ref, reverse=reverse, incl=incl, g=g, gl_row=g[last:last + 1],
                q=q_ref[:, h * dk:(h + 1) * dk] * (dk ** -0.5), k=k_ref[:, h * dk:(h + 1) * dk],
                v_bf=v_ref[:, h * dv:(h + 1) * dv], s=state[z, h]))
    intra = [[] for _ in units]
    for blk in range(c // rb):
        r0, r1 = blk * rb, (blk + 1) * rb
        scores, cols = [], []
        for u in units:
            g = u["g"]
            mid = r0 + rb // 2
            if u["reverse"]:
                c0, c1 = r0, c
                ref = g[mid:mid + 1]
            else:
                c0, c1 = 0, r1
                ref = g[mid - 1:mid]
            qe = u["q"][r0:r1] * jnp.exp(g[r0:r1] - ref)
            ke = u["k"][c0:c1] * jnp.exp(ref - g[c0:c1])
            scores.append(jnp.where(u["incl"][r0:r1, c0:c1], _dot_nt(_bf(qe), _bf(ke)), 0.0))
            cols.append((c0, c1))
        for parts, u, a, (c0, c1) in zip(intra, units, scores, cols):
            parts.append(_dot(_bf(a), u["v_bf"][c0:c1]))
    inter = [_dot(_bf(u["q"] * jnp.exp(u["g"])), _bf(u["s"])) for u in units]
    for u, o_inter, parts in zip(units, inter, intra):
        h = u["h"]
        u["o_ref"][:, h * dv:(h + 1) * dv] = _bf(o_inter + jnp.concatenate(parts, axis=0))
    for u in units:
        kd = u["k"] * jnp.exp(u["gl_row"] - u["g"])
        gl_col = jnp.sum(jnp.where(eye, jnp.broadcast_to(u["gl_row"], (dk, dk)), 0.0), axis=1, keepdims=True)
        state[u["z"], u["h"]] = u["s"] * jnp.exp(gl_col) + _dot_tn(_bf(kd), u["v_bf"])
    _store_state(state, sout_ref, tiles, step)


def _gla(q, k, v, lr, w_gate, gate_bias, s0, tiles):
    nt = q.shape[0]
    nh, dk, dv = C_HEADS, C_DK, C_DV
    c = tiles.c
    wg = jnp.zeros((2, LANES, nh * dk), F32)
    for z in range(2):
        wg = wg.at[z, z * GATE_RANK:(z + 1) * GATE_RANK].set(w_gate[z])
    in_specs = (_dir_specs(tiles, c, ((nh * dk, 0), (nh * dk, 0), (nh * dv, 0), (LANES, 0)))
                + [pl.BlockSpec((2, LANES, nh * dk), lambda i: (0, 0, 0)),
                   pl.BlockSpec((2, 1, nh * dk), lambda i: (0, 0, 0)),
                   pl.BlockSpec((1, 2, nh, dk, dv), lambda i: (tiles.latent_seq(i), 0, 0, 0, 0))])
    kern = functools.partial(_gla_kernel, tiles=tiles, nh=nh, dk=dk, dv=dv, rb=GLA_ROW_BLOCK)
    return pl.pallas_call(
        kern,
        out_shape=[jax.ShapeDtypeStruct((nt, nh * dv), BF16), jax.ShapeDtypeStruct((nt, nh * dv), BF16),
                   jax.ShapeDtypeStruct((tiles.n_p, 2, nh, dk, dv), F32)],
        grid=(tiles.n,),
        in_specs=in_specs,
        out_specs=[pl.BlockSpec((c, nh * dv), lambda i: (tiles.row_block(i, False), 0)),
                   pl.BlockSpec((c, nh * dv), lambda i: (tiles.row_block(i, True), 0)),
                   pl.BlockSpec((1, 2, nh, dk, dv), lambda i: (tiles.ctx_seq(i), 0, 0, 0, 0))],
        scratch_shapes=[pltpu.VMEM((2, nh, dk, dv), F32)],
        compiler_params=_cparams(("arbitrary",)),
        name="gla",
    )(q, k, v, lr, q, k, v, lr, wg, gate_bias.reshape(2, 1, nh * dk), s0)


def kernel(x_prompt, x_sample, state_delta, cache_k, cache_v, state_gla, c, c_ctx, norm_g, ada_w, ada_b,
           ffn_w_gu, ffn_w_down, even_w_in, even_conv, even_a_log, even_dt_bias, even_onorm, even_sink,
           even_w_out, odd_w_in, odd_w_gate, odd_gate_bias, odd_onorm, odd_w_out, final_g):
    n_p, t_p, d = x_prompt.shape
    n_s, t_s, _ = x_sample.shape
    depth = norm_g.shape[0]
    np_rows, ns_rows = n_p * t_p, n_s * t_s
    assert np_rows % t_s == 0
    assert t_p % CHUNK == 0 and t_s % CHUNK == 0 and t_s % GRID_W == 0
    rows = (np_rows, t_s)
    tiles = _Tiles(n_p, t_p, n_s, t_s, CHUNK)

    n_cond = 1 + n_s
    cond_rows = -(-n_cond // SUBLANES) * SUBLANES
    conds = jnp.concatenate([c_ctx[None, :], c, jnp.zeros((cond_rows - n_cond, d), F32)], axis=0)
    mods = _ada(conds, ada_w, ada_b)[:, :n_cond].reshape(depth, n_cond, N_MOD, d)

    w_gu_second = ffn_w_gu[:, 1:].astype(BF16)
    w_down_second = ffn_w_down[:, 1:].astype(BF16)

    xs = (x_prompt.reshape(np_rows, d), x_sample.reshape(ns_rows, d))
    new_delta, new_k, new_v, new_gla = [], [], [], []
    for l in range(depth):
        j = l // 2
        mod = mods[l]
        x = _ffn_half(xs, mod, 0, norm_g[l, 0], ffn_w_gu, ffn_w_down, (l, 0), rows)
        if l % 2 == 0:
            nh = A_HEADS
            o_qkv = 2 * nh * A_DK + nh * A_DV
            o_gate = o_qkv + nh * A_DV
            o_qb = o_gate + 4 * nh
            w_q, w_kv = B_Q_HEADS * HEAD_DIM, B_KV_HEADS * HEAD_DIM
            w = even_w_in[j]
            tail = jnp.concatenate([w[:, o_qb:], w[:, o_gate:o_qb], jnp.zeros((d, LANES - 4 * nh), F32)], axis=1)
            assert tail.shape[1] == w_q + 2 * w_kv + LANES and o_qkv % (nh * A_DV) == 0
            groups = ((even_w_in, j, 0, o_qkv), (even_w_in, j, o_qkv, nh * A_DV), (tail, None, 0, w_q),
                      (tail, None, w_q, w_kv), (tail, None, w_q + w_kv, w_kv), (tail, None, w_q + 2 * w_kv, LANES))
            dtypes = (F32, BF16, F32, F32, F32, F32)
            qkv, gate, qb, kb, vb, ba = _mixer_in(x, mod, norm_g[l, 1], groups, dtypes, rows)
            qkv_n = _conv_qkv(qkv, even_conv[j], tiles)
            o_f, o_b, st = _delta_rule(qkv_n, ba, even_a_log[j], even_dt_bias[j], state_delta[:, j], tiles)
            cos, sin_signed = _rope_tables(t_s)
            att_p = _ctx_attention(qb, kb, vb, even_sink[j], n_p, t_p)
            att_s = _lat_attention(qb, kb, vb,
                                   cache_k[:, j].reshape(n_s, -1, B_KV_HEADS * HEAD_DIM),
                                   cache_v[:, j].reshape(n_s, -1, B_KV_HEADS * HEAD_DIM),
                                   even_sink[j], cos, sin_signed, np_rows, n_s, t_s)
            mixer = (o_f, o_b, gate, even_onorm[j], even_w_out[j].astype(BF16), nh, A_DV, (att_p, att_s))
            new_delta.append(st)
            new_k.append(kb[:np_rows].reshape(n_p, t_p, B_KV_HEADS, HEAD_DIM))
            new_v.append(vb[:np_rows].reshape(n_p, t_p, B_KV_HEADS, HEAD_DIM))
        else:
            nh = C_HEADS
            o_v = 2 * nh * C_DK
            o_g = o_v + nh * C_DV
            o_lr = o_g + nh * C_DV
            w_lr = jnp.concatenate([odd_w_in[j][:, o_lr:], jnp.zeros((d, LANES - 2 * GATE_RANK), F32)], axis=1)
            groups = ((odd_w_in, j, 0, nh * C_DK), (odd_w_in, j, nh * C_DK, nh * C_DK), (odd_w_in, j, o_v, nh * C_DV),
                      (odd_w_in, j, o_g, nh * C_DV), (w_lr, None, 0, LANES))
            dtypes = (F32, F32, BF16, BF16, F32)
            q, k, v, g_out, lr = _mixer_in(x, mod, norm_g[l, 1], groups, dtypes, rows)
            o_f, o_b, st = _gla(q, k, v, lr, odd_w_gate[j], odd_gate_bias[j], state_gla[:, j], tiles)
            mixer = (o_f, o_b, g_out, odd_onorm[j], odd_w_out[j].astype(BF16), nh, C_DV, None)
            new_gla.append(st)
        last = l == depth - 1
        xs = _ffn_half((x,), mod, 6, norm_g[l, 2], w_gu_second, w_down_second, (l, 0), rows, mixer=mixer,
                       final_g=final_g if last else None)
        if not last:
            xs = (xs,)

    y_prompt, y_sample = xs
    return (y_prompt.reshape(n_p, t_p, d), y_sample.reshape(n_s, t_s, d), jnp.stack(new_delta, axis=1),
            jnp.stack(new_k, axis=1), jnp.stack(new_v, axis=1), jnp.stack(new_gla, axis=1))
```
